```python
import jax, jax.numpy as jnp
from jax import lax
import numpy as np

D_MODEL = 2048
BATCH = 2
SEQ = 16384
DEPTH = 4

HEAD_DIM = 128
Q_BLOCK = 128
RMS_EPS = 1e-6
NEG_INF = -1e30

MLA_HEADS = 6
MLA_Q_LORA = 512
MLA_KV_LORA = 512
MLA_NOPE = 128
MLA_ROPE = 64
MLA_V = 128
ROPE_BASE = 10000.0

DIL_PAIRS = ((128, 1), (512, 4), (2048, 16))
DIL_HEADS_PER_GROUP = 2
DIL_HEADS = DIL_HEADS_PER_GROUP * len(DIL_PAIRS)

NSA_HEADS = 4
NSA_CMP_LEN = 32
NSA_CMP_STRIDE = 16
NSA_SEL_LEN = 64
NSA_TOPK = 16
NSA_WINDOW = 512
NSA_FORCED_SCORE = 100.0

N_ALIBI = DIL_HEADS + NSA_HEADS

D_FF = 5504

IN_SPLITS = (MLA_Q_LORA, MLA_KV_LORA, MLA_ROPE,
             DIL_HEADS * HEAD_DIM, DIL_HEADS * HEAD_DIM, DIL_HEADS * HEAD_DIM,
             NSA_HEADS * HEAD_DIM,
             HEAD_DIM, HEAD_DIM, HEAD_DIM, HEAD_DIM, HEAD_DIM, HEAD_DIM,
             NSA_HEADS * 3)
IN_COLS = sum(IN_SPLITS)
IN_SPLIT_POINTS = [int(c) for c in np.cumsum(IN_SPLITS)[:-1]]
MIX_OUT = MLA_HEADS * MLA_V + DIL_HEADS * HEAD_DIM + NSA_HEADS * HEAD_DIM

kernel_name = 'hybrid_mla_dilated_nsa_macaron'


def rms_norm(x, g):
    xf = x.astype(jnp.float32)
    y = xf * lax.rsqrt(jnp.mean(xf * xf, axis=-1, keepdims=True) + RMS_EPS)
    return (y * g.astype(jnp.float32)).astype(x.dtype)


def swiglu_ffn(x, w_in, w_out):
    gate, up = jnp.split(x @ w_in, 2, axis=-1)
    return (jax.nn.silu(gate) * up) @ w_out


def alibi_slopes():
    return 2.0 ** (-8.0 * jnp.arange(1, N_ALIBI + 1, dtype=jnp.float32) / N_ALIBI)


def apply_rope(x, pos):
    half = x.shape[-1] // 2
    inv_freq = ROPE_BASE ** (-jnp.arange(half, dtype=jnp.float32) / half)
    ang = pos.astype(jnp.float32)[:, None] * inv_freq[None, :]
    cos, sin = jnp.cos(ang)[:, None, :], jnp.sin(ang)[:, None, :]
    xf = x.astype(jnp.float32)
    x1, x2 = xf[..., :half], xf[..., half:]
    return jnp.concatenate([x1 * cos - x2 * sin, x2 * cos + x1 * sin], axis=-1).astype(x.dtype)


def masked_softmax(s, mask):
    s = jnp.where(mask, s.astype(jnp.float32), NEG_INF)
    p = jax.nn.softmax(s, axis=-1)
    return jnp.where(mask, p, 0.0)


def mla_attention(q_lat, kv_lat, k_rope, q_norm, w_uq, kv_norm, w_ukv):
    B, S, _ = q_lat.shape
    pos = jnp.arange(S)
    q = (rms_norm(q_lat, q_norm) @ w_uq).reshape(B, S, MLA_HEADS, MLA_NOPE + MLA_ROPE)
    q = jnp.concatenate([q[..., :MLA_NOPE], apply_rope(q[..., MLA_NOPE:], pos)], axis=-1)
    kv = (rms_norm(kv_lat, kv_norm) @ w_ukv).reshape(B, S, MLA_HEADS, MLA_NOPE + MLA_V)
    k_pe = jnp.broadcast_to(apply_rope(k_rope[:, :, None, :], pos), (B, S, MLA_HEADS, MLA_ROPE))
    k = jnp.concatenate([kv[..., :MLA_NOPE], k_pe], axis=-1)
    v = kv[..., MLA_NOPE:]
    scale = (MLA_NOPE + MLA_ROPE) ** -0.5
    nb = S // Q_BLOCK
    q_blocks = q.reshape(B, nb, Q_BLOCK, MLA_HEADS, -1).transpose(1, 0, 2, 3, 4)

    def block(args):
        qb, i = args
        t = i * Q_BLOCK + jnp.arange(Q_BLOCK)
        s = jnp.einsum('bqhd,bkhd->bhqk', qb, k).astype(jnp.float32) * scale
        p = masked_softmax(s, pos[None, :] <= t[:, None])
        return jnp.einsum('bhqk,bkhd->bqhd', p.astype(v.dtype), v)

    o = lax.map(block, (q_blocks, jnp.arange(nb)))
    return o.transpose(1, 0, 2, 3, 4).reshape(B, S, MLA_HEADS * MLA_V)


def dilated_group(q, k, v, window, dilation, slopes):
    B, S, Hg, Dh = q.shape
    span = window // dilation
    L = S // dilation
    nb = -(-L // Q_BLOCK)
    Lp = nb * Q_BLOCK
    Z = B * dilation

    def strided(x):
        x = x.reshape(B, L, dilation, Hg, Dh).transpose(0, 2, 1, 3, 4).reshape(Z, L, Hg, Dh)
        return jnp.pad(x, ((0, 0), (0, Lp - L), (0, 0), (0, 0)))

    def band_keys(x):
        xb = jnp.pad(x, ((0, 0), (Q_BLOCK, 0), (0, 0), (0, 0))).reshape(Z, nb + 1, Q_BLOCK, Hg, Dh)
        return jnp.concatenate([xb[:, :-1], xb[:, 1:]], axis=2)

    qb = strided(q).reshape(Z, nb, Q_BLOCK, Hg, Dh)
    kb, vb = band_keys(strided(k)), band_keys(strided(v))
    a = jnp.arange(Q_BLOCK)[:, None]
    c = jnp.arange(2 * Q_BLOCK)[None, :]
    j = Q_BLOCK + a - c
    kpos = (jnp.arange(nb)[:, None, None] - 1) * Q_BLOCK + c[None]
    mask = ((j >= 0) & (j <= span))[None] & (kpos >= 0)
    mask = mask[:, None]
    s = jnp.einsum('znqhd,znkhd->znhqk', qb, kb).astype(jnp.float32) * (Dh ** -0.5)
    s = s - slopes[:, None, None] * (j * dilation).astype(jnp.float32)
    s = jnp.where(mask, s, NEG_INF)
    m = jnp.max(s, axis=-1, keepdims=True)
    e = jnp.where(mask, jnp.exp(s - m), 0.0)
    den = jnp.sum(e, axis=-1, keepdims=True)
    o = jnp.einsum('znhqk,znkhd->znqhd', (e / den).astype(vb.dtype), vb)
    lse = (m + jnp.log(den))[..., 0]
    o = o.reshape(Z, Lp, Hg, Dh)[:, :L].reshape(B, dilation, L, Hg, Dh)
    o = o.transpose(0, 2, 1, 3, 4).reshape(B, S, Hg, Dh)
    lse = lse.transpose(0, 1, 3, 2).reshape(Z, Lp, Hg)[:, :L].reshape(B, dilation, L, Hg)
    lse = lse.transpose(0, 2, 1, 3).reshape(B, S, Hg)
    return o, lse


def dilated_mixture(q, k, v, slopes):
    B, S, _, Dh = q.shape
    outs, lses = [], []
    for g, (window, dilation) in enumerate(DIL_PAIRS):
        sl = slice(g * DIL_HEADS_PER_GROUP, (g + 1) * DIL_HEADS_PER_GROUP)
        o, lse = dilated_group(q[:, :, sl], k[:, :, sl], v[:, :, sl], window, dilation, slopes[sl])
        outs.append(o)
        lses.append(lse)
    alpha = jax.nn.softmax(jnp.stack(lses, axis=0), axis=0)
    o = jnp.stack(outs, axis=0) * alpha[..., None].astype(outs[0].dtype)
    return o.transpose(1, 2, 0, 3, 4).reshape(B, S, DIL_HEADS * Dh)


def nsa_compress(x, pos_emb, w1, w2):
    B, S, Dh = x.shape
    nc = (S - NSA_CMP_LEN) // NSA_CMP_STRIDE + 1
    idx = jnp.arange(nc)[:, None] * NSA_CMP_STRIDE + jnp.arange(NSA_CMP_LEN)[None, :]
    blocks = x[:, idx] + pos_emb
    h = jax.nn.silu(blocks.reshape(B, nc, NSA_CMP_LEN * Dh) @ w1)
    return h @ w2


def nsa_attention(q, k_cmp, v_cmp, k_slc, v_slc, k_win, v_win, gate_logits,
                  cmp_pos, phi_k1, phi_k2, phi_v1, phi_v2, slopes):
    B, S, H, Dh = q.shape
    kc = nsa_compress(k_cmp, cmp_pos, phi_k1, phi_k2)
    vc = nsa_compress(v_cmp, cmp_pos, phi_v1, phi_v2)
    nc = kc.shape[1]
    ns = S // NSA_SEL_LEN
    topk = min(NSA_TOPK, ns)
    c_start = jnp.arange(nc) * NSA_CMP_STRIDE
    c_end = c_start + NSA_CMP_LEN - 1
    c_centre = c_start.astype(jnp.float32) + 0.5 * (NSA_CMP_LEN - 1)
    j_idx = jnp.arange(ns)
    cmp_to_sel = ((c_start[:, None] < (j_idx[None, :] + 1) * NSA_SEL_LEN)
                  & (c_end[:, None] >= j_idx[None, :] * NSA_SEL_LEN)).astype(jnp.float32)
    pad = ((0, 0), (NSA_WINDOW, 0), (0, 0))
    k_win_p, v_win_p = jnp.pad(k_win, pad), jnp.pad(v_win, pad)
    gates = jax.nn.sigmoid(gate_logits.astype(jnp.float32)).reshape(B, S, H, 3)
    nb = S // Q_BLOCK
    q_blocks = q.reshape(B, nb, Q_BLOCK, H, Dh).transpose(1, 0, 2, 3, 4)
    g_blocks = gates.reshape(B, nb, Q_BLOCK, H, 3).transpose(1, 0, 2, 3, 4)
    slope = slopes[:, None, None]
    scale = Dh ** -0.5
    gather = jax.vmap(lambda seq, idx: seq[idx])

    def block(args):
        qb, gb, i = args
        t = i * Q_BLOCK + jnp.arange(Q_BLOCK)
        tf = t.astype(jnp.float32)
        s = jnp.einsum('bqhd,bcd->bhqc', qb, kc).astype(jnp.float32) * scale
        s = s - slope * (tf[:, None] - c_centre[None, :])
        p_cmp = masked_softmax(s, c_end[None, :] <= t[:, None])
        o_cmp = jnp.einsum('bhqc,bcd->bqhd', p_cmp.astype(vc.dtype), vc)
        score = jnp.einsum('bhqc,cj->bqj', p_cmp, cmp_to_sel)
        cur = (t // NSA_SEL_LEN)[:, None]
        jj = j_idx[None, :]
        forced = (jj == 0) | (jj == cur) | (jj == cur - 1)
        score = jnp.where(jj > cur, -1.0, jnp.where(forced, NSA_FORCED_SCORE, score))
        _, sel = lax.top_k(score, topk)
        kpos = (sel[..., None] * NSA_SEL_LEN + jnp.arange(NSA_SEL_LEN)).reshape(B, Q_BLOCK, topk * NSA_SEL_LEN)
        kg, vg = gather(k_slc, kpos), gather(v_slc, kpos)
        dist = t[None, :, None] - kpos
        s = jnp.einsum('bqhd,bqnd->bhqn', qb, kg).astype(jnp.float32) * scale
        s = s - slope * dist[:, None].astype(jnp.float32)
        p = masked_softmax(s, (dist >= 0)[:, None])
        o_slc = jnp.einsum('bhqn,bqnd->bqhd', p.astype(vg.dtype), vg)
        kw = lax.dynamic_slice_in_dim(k_win_p, i * Q_BLOCK, NSA_WINDOW + Q_BLOCK, axis=1)
        vw = lax.dynamic_slice_in_dim(v_win_p, i * Q_BLOCK, NSA_WINDOW + Q_BLOCK, axis=1)
        wpos = i * Q_BLOCK - NSA_WINDOW + jnp.arange(NSA_WINDOW + Q_BLOCK)
        dist = t[:, None] - wpos[None, :]
        s = jnp.einsum('bqhd,bkd->bhqk', qb, kw).astype(jnp.float32) * scale
        s = s - slope * dist.astype(jnp.float32)
        p = masked_softmax(s, (dist >= 0) & (dist < NSA_WINDOW) & (wpos[None, :] >= 0))
        o_win = jnp.einsum('bhqk,bkd->bqhd', p.astype(vw.dtype), vw)
        g = gb.astype(o_cmp.dtype)
        return g[..., 0:1] * o_cmp + g[..., 1:2] * o_slc + g[..., 2:3] * o_win

    o = lax.map(block, (q_blocks, g_blocks, jnp.arange(nb)))
    return o.transpose(1, 0, 2, 3, 4).reshape(B, S, H * Dh)


def setup_inputs(seed: int = 0) -> dict:
    key = jax.random.key(seed)
    ks = jax.random.split(key, 20)

    def w(k, shape, fan_in):
        return jax.random.normal(k, shape, jnp.float32) * (fan_in ** -0.5)

    def gain(k, n):
        return 1.0 + 0.02 * jax.random.normal(k, (DEPTH, n), jnp.float32)

    return {
        'x': jax.random.normal(ks[0], (BATCH, SEQ, D_MODEL), jnp.float32),
        'ffn1_norm': gain(ks[1], D_MODEL),
        'ffn1_w_in': w(ks[2], (DEPTH, D_MODEL, 2 * D_FF), D_MODEL),
        'ffn1_w_out': w(ks[3], (DEPTH, D_FF, D_MODEL), D_FF),
        'mix_norm': gain(ks[4], D_MODEL),
        'w_mix_in': w(ks[5], (DEPTH, D_MODEL, IN_COLS), D_MODEL),
        'mla_q_norm': gain(ks[6], MLA_Q_LORA),
        'mla_w_uq': w(ks[7], (DEPTH, MLA_Q_LORA, MLA_HEADS * (MLA_NOPE + MLA_ROPE)), MLA_Q_LORA),
        'mla_kv_norm': gain(ks[8], MLA_KV_LORA),
        'mla_w_ukv': w(ks[9], (DEPTH, MLA_KV_LORA, MLA_HEADS * (MLA_NOPE + MLA_V)), MLA_KV_LORA),
        'nsa_cmp_pos': 0.1 * jax.random.normal(ks[10], (DEPTH, NSA_CMP_LEN, HEAD_DIM), jnp.float32),
        'nsa_phi_k1': w(ks[11], (DEPTH, NSA_CMP_LEN * HEAD_DIM, HEAD_DIM), NSA_CMP_LEN * HEAD_DIM),
        'nsa_phi_k2': w(ks[12], (DEPTH, HEAD_DIM, HEAD_DIM), HEAD_DIM),
        'nsa_phi_v1': w(ks[13], (DEPTH, NSA_CMP_LEN * HEAD_DIM, HEAD_DIM), NSA_CMP_LEN * HEAD_DIM),
        'nsa_phi_v2': w(ks[14], (DEPTH, HEAD_DIM, HEAD_DIM), HEAD_DIM),
        'w_mix_out': w(ks[15], (DEPTH, MIX_OUT, D_MODEL), MIX_OUT),
        'ffn2_norm': gain(ks[16], D_MODEL),
        'ffn2_w_in': w(ks[17], (DEPTH, D_MODEL, 2 * D_FF), D_MODEL),
        'ffn2_w_out': w(ks[18], (DEPTH, D_FF, D_MODEL), D_FF),
        'final_norm': 1.0 + 0.02 * jax.random.normal(ks[19], (D_MODEL,), jnp.float32),
    }


def reference(x, ffn1_norm, ffn1_w_in, ffn1_w_out, mix_norm, w_mix_in, mla_q_norm, mla_w_uq,
              mla_kv_norm, mla_w_ukv, nsa_cmp_pos, nsa_phi_k1, nsa_phi_k2, nsa_phi_v1, nsa_phi_v2,
              w_mix_out, ffn2_norm, ffn2_w_in, ffn2_w_out, final_norm):
    B, S, _ = x.shape
    slopes = alibi_slopes()
    dil_slopes, nsa_slopes = slopes[:DIL_HEADS], slopes[DIL_HEADS:]
    for l in range(DEPTH):
        x = x + 0.5 * swiglu_ffn(rms_norm(x, ffn1_norm[l]), ffn1_w_in[l], ffn1_w_out[l])
        h = rms_norm(x, mix_norm[l]) @ w_mix_in[l]
        (q_lat, kv_lat, k_rope, dq, dk, dv, nq,
         nkc, nvc, nks, nvs, nkw, nvw, ng) = jnp.split(h, IN_SPLIT_POINTS, axis=-1)
        o_mla = mla_attention(q_lat, kv_lat, k_rope, mla_q_norm[l], mla_w_uq[l],
                              mla_kv_norm[l], mla_w_ukv[l])
        o_dil = dilated_mixture(dq.reshape(B, S, DIL_HEADS, HEAD_DIM),
                                dk.reshape(B, S, DIL_HEADS, HEAD_DIM),
                                dv.reshape(B, S, DIL_HEADS, HEAD_DIM), dil_slopes)
        o_nsa = nsa_attention(nq.reshape(B, S, NSA_HEADS, HEAD_DIM), nkc, nvc, nks, nvs, nkw, nvw, ng,
                              nsa_cmp_pos[l], nsa_phi_k1[l], nsa_phi_k2[l], nsa_phi_v1[l], nsa_phi_v2[l],
                              nsa_slopes)
        x = x + jnp.concatenate([o_mla, o_dil, o_nsa], axis=-1) @ w_mix_out[l]
        x = x + 0.5 * swiglu_ffn(rms_norm(x, ffn2_norm[l]), ffn2_w_in[l], ffn2_w_out[l])
    return rms_norm(x, final_norm)
```

```python
import functools

import numpy as np
import jax
import jax.numpy as jnp
from jax import lax
from jax.experimental import pallas as pl
from jax.experimental.pallas import tpu as pltpu

F32 = jnp.float32
MXU_DTYPE = jnp.bfloat16
VMEM_LIMIT_BYTES = 56 * 1024 * 1024
LANES = 128

HEAD_DIM = 128
RMS_EPS = 1e-6
NEG_INF = -1e30

MLA_HEADS = 6
MLA_LORA = 512
MLA_NOPE = 128
MLA_ROPE = 64
MLA_V = 128
MLA_QK_PAD = 256
ROPE_BASE = 10000.0

DIL_PAIRS = ((128, 1), (512, 4), (2048, 16))
DIL_HEADS = 6
DIL_BLOCK = 128

NSA_HEADS = 4
NSA_CMP_LEN = 32
NSA_CMP_STRIDE = 16
NSA_SEL_LEN = 64
NSA_TOPK = 16
NSA_WINDOW = 512
NSA_FORCED_SCORE = 100.0
NSA_Q_BLOCK = 128

N_ALIBI = DIL_HEADS + NSA_HEADS
ALIBI_SLOPES = tuple(float(2.0 ** (-8.0 * i / N_ALIBI)) for i in range(1, N_ALIBI + 1))

COL_QLAT = 0
COL_KVLAT = 512
COL_NSA_Q = 1024
COL_NSA_KV = 1536
COL_KROPE = 2048
COL_DIL = 2304
H_COLS = 4608
H_TILE = 1536
AUX_COLS = 384


def _cparams(sem):
    return pltpu.CompilerParams(dimension_semantics=sem, vmem_limit_bytes=VMEM_LIMIT_BYTES)


def _rms(x, g):
    ms = jnp.mean(x * x, axis=-1, keepdims=True)
    return (x * lax.rsqrt(ms + RMS_EPS)) * g


def _dot(a, b):
    return jnp.dot(a, b, preferred_element_type=F32)


def _shr(x, pow2):
    return lax.shift_right_logical(x, int(pow2).bit_length() - 1)


def _dot_t(a, b):
    return lax.dot_general(a, b, (((1,), (1,)), ((), ())), preferred_element_type=F32)


def _ffn_kernel(x_ref, g_ref, wgu_ref, wo_ref, *rest, final):
    if final:
        gf_ref, o_ref, xn_ref = rest
    else:
        o_ref, xn_ref = rest
    j = pl.program_id(1)

    @pl.when(j == 0)
    def _():
        x = x_ref[...]
        xn_ref[...] = _rms(x, g_ref[...]).astype(xn_ref.dtype)
        o_ref[...] = x

    xn = xn_ref[...]
    gate = _dot(xn, wgu_ref[0])
    up = _dot(xn, wgu_ref[1])
    h = (0.5 * gate) * jax.nn.sigmoid(gate) * up
    o_ref[...] += _dot(h.astype(MXU_DTYPE), wo_ref[...])

    if final:
        @pl.when(j == pl.num_programs(1) - 1)
        def _():
            o_ref[...] = _rms(o_ref[...], gf_ref[...])


def _ffn(x2, g, wgu, wo, gf=None, *, tm=512, tf=512):
    n, d = x2.shape
    dffp = wo.shape[0]
    final = gf is not None
    in_specs = [
        pl.BlockSpec((tm, d), lambda i, j: (i, 0)),
        pl.BlockSpec((1, d), lambda i, j: (0, 0)),
        pl.BlockSpec((2, d, tf), lambda i, j: (0, 0, j)),
        pl.BlockSpec((tf, d), lambda i, j: (j, 0)),
    ]
    args = [x2, g.reshape(1, d), wgu, wo]
    if final:
        in_specs.append(pl.BlockSpec((1, d), lambda i, j: (0, 0)))
        args.append(gf.reshape(1, d))
    return pl.pallas_call(
        functools.partial(_ffn_kernel, final=final),
        grid=(n // tm, dffp // tf),
        in_specs=in_specs,
        out_specs=pl.BlockSpec((tm, d), lambda i, j: (i, 0)),
        out_shape=jax.ShapeDtypeStruct((n, d), F32),
        scratch_shapes=[pltpu.VMEM((tm, d), MXU_DTYPE)],
        compiler_params=_cparams(("parallel", "arbitrary")),
        name="ffn_final" if final else "ffn",
    )(*args)


def _prep_ffn(w_in, w_out, tf=512):
    d, two_dff = w_in.shape
    dff = two_dff // 2
    dffp = -(-dff // tf) * tf
    wgu = w_in.reshape(d, 2, dff).transpose(1, 0, 2)
    wgu = jnp.pad(wgu, ((0, 0), (0, 0), (0, dffp - dff))).astype(MXU_DTYPE)
    wo = jnp.pad(w_out, ((0, dffp - dff), (0, 0))).astype(MXU_DTYPE)
    return wgu, wo


def _mixin_kernel(x_ref, g_ref, w_ref, wa_ref, h_ref, a_ref, xn_ref):
    j = pl.program_id(1)

    @pl.when(j == 0)
    def _():
        xn = _rms(x_ref[...], g_ref[...]).astype(xn_ref.dtype)
        xn_ref[...] = xn
        a_ref[...] = _dot(xn, wa_ref[...])

    h_ref[...] = _dot(xn_ref[...], w_ref[...]).astype(h_ref.dtype)


def _mixin(x2, g, w, wa, *, tm=512):
    n, d = x2.shape
    return pl.pallas_call(
        _mixin_kernel,
        grid=(n // tm, H_COLS // H_TILE),
        in_specs=[
            pl.BlockSpec((tm, d), lambda i, j: (i, 0)),
            pl.BlockSpec((1, d), lambda i, j: (0, 0)),
            pl.BlockSpec((d, H_TILE), lambda i, j: (0, j)),
            pl.BlockSpec((d, AUX_COLS), lambda i, j: (0, 0)),
        ],
        out_specs=[
            pl.BlockSpec((tm, H_TILE), lambda i, j: (i, j)),
            pl.BlockSpec((tm, AUX_COLS), lambda i, j: (i, 0)),
        ],
        out_shape=[
            jax.ShapeDtypeStruct((n, H_COLS), MXU_DTYPE),
            jax.ShapeDtypeStruct((n, AUX_COLS), F32),
        ],
        scratch_shapes=[pltpu.VMEM((tm, d), MXU_DTYPE)],
        compiler_params=_cparams(("parallel", "arbitrary")),
        name="mix_in",
    )(x2, g.reshape(1, d), w, wa)


def _rot_half_cols(w):
    half = w.shape[1] // 2
    return jnp.concatenate([-w[:, half:], w[:, :half]], axis=1)


def _prep_mixin(w):
    d = w.shape[0]
    o = 0
    parts = {}
    for name, width in (("q_lat", 512), ("kv_lat", 512), ("k_rope", 64), ("dq", 768), ("dk", 768),
                        ("dv", 768), ("nq", 512), ("nkc", 128), ("nvc", 128), ("nks", 128),
                        ("nvs", 128), ("nkw", 128), ("nvw", 128), ("ng", 12)):
        parts[name] = w[:, o:o + width]
        o += width
    z64 = jnp.zeros((d, 64), w.dtype)
    kr = parts["k_rope"]
    wh = jnp.concatenate([
        parts["q_lat"], parts["kv_lat"], parts["nq"],
        parts["nks"], parts["nvs"], parts["nkw"], parts["nvw"],
        kr, z64, _rot_half_cols(kr), z64,
        parts["dq"], parts["dk"], parts["dv"]], axis=1).astype(MXU_DTYPE)
    wa = jnp.concatenate([parts["nkc"], parts["nvc"], parts["ng"],
                          jnp.zeros((d, LANES - 12), w.dtype)], axis=1).astype(MXU_DTYPE)
    return wh, wa


def _mla_proj_kernel(lat_ref, kr_ref, qn_ref, kvn_ref, wq_ref, wk_ref, wv_ref, invf_ref,
                     q_ref, k_ref, v_ref, *, seq, tm):
    i = pl.program_id(0)
    pos0 = lax.rem(i * tm, seq)
    pos = (pos0 + lax.broadcasted_iota(jnp.int32, (tm, 1), 0)).astype(F32)
    ang = pos * invf_ref[...]
    cos = jnp.cos(ang)
    sin = jnp.sin(ang)
    scale = (MLA_NOPE + MLA_ROPE) ** -0.5

    lat = lat_ref[...].astype(F32)
    qn = _rms(lat[:, :MLA_LORA], qn_ref[...]).astype(MXU_DTYPE)
    kvn = _rms(lat[:, MLA_LORA:], kvn_ref[...]).astype(MXU_DTYPE)
    qm = _dot(qn, wq_ref[...])
    rot0 = MLA_HEADS * MLA_QK_PAD
    for h in range(MLA_HEADS):
        c = h * MLA_QK_PAD
        nope = qm[:, c:c + LANES]
        pe = qm[:, c + LANES:c + 2 * LANES] * cos + qm[:, rot0 + h * LANES:rot0 + (h + 1) * LANES] * sin
        q_ref[:, c:c + LANES] = (nope * scale).astype(q_ref.dtype)
        q_ref[:, c + LANES:c + 2 * LANES] = (pe * scale).astype(q_ref.dtype)

    kr = kr_ref[...].astype(F32)
    kpe = (kr[:, :LANES] * cos + kr[:, LANES:] * sin).astype(k_ref.dtype)
    kn = _dot(kvn, wk_ref[...])
    for h in range(MLA_HEADS):
        c = h * MLA_QK_PAD
        k_ref[:, c:c + LANES] = kn[:, h * LANES:(h + 1) * LANES].astype(k_ref.dtype)
        k_ref[:, c + LANES:c + 2 * LANES] = kpe
    v_ref[...] = _dot(kvn, wv_ref[...]).astype(v_ref.dtype)


def _mla_proj(h2, qn, kvn, wq, wk, wv, invf, *, seq, tm=512):
    n = h2.shape[0]
    qk_cols = MLA_HEADS * MLA_QK_PAD
    v_cols = MLA_HEADS * MLA_V
    full = lambda a: pl.BlockSpec(a.shape, lambda i: (0,) * a.ndim)
    return pl.pallas_call(
        functools.partial(_mla_proj_kernel, seq=seq, tm=tm),
        grid=(n // tm,),
        in_specs=[
            pl.BlockSpec((tm, 2 * MLA_LORA), lambda i: (i, 0)),
            pl.BlockSpec((tm, 2 * LANES), lambda i: (i, COL_KROPE // (2 * LANES))),
            full(qn), full(kvn), full(wq), full(wk), full(wv), full(invf),
        ],
        out_specs=[
            pl.BlockSpec((tm, qk_cols), lambda i: (i, 0)),
            pl.BlockSpec((tm, qk_cols), lambda i: (i, 0)),
            pl.BlockSpec((tm, v_cols), lambda i: (i, 0)),
        ],
        out_shape=[
            jax.ShapeDtypeStruct((n, qk_cols), MXU_DTYPE),
            jax.ShapeDtypeStruct((n, qk_cols), MXU_DTYPE),
            jax.ShapeDtypeStruct((n, v_cols), MXU_DTYPE),
        ],
        compiler_params=_cparams(("parallel",)),
        name="mla_proj",
    )(h2, h2, qn, kvn, wq, wk, wv, invf)


def _prep_mla(w_uq, w_ukv):
    r = MLA_LORA
    wq3 = w_uq.reshape(r, MLA_HEADS, MLA_NOPE + MLA_ROPE)
    nope, pe = wq3[..., :MLA_NOPE], wq3[..., MLA_NOPE:]
    z = jnp.zeros((r, MLA_HEADS, 64), w_uq.dtype)
    main = jnp.concatenate([nope, pe, z], axis=-1).reshape(r, MLA_HEADS * MLA_QK_PAD)
    half = MLA_ROPE // 2
    pe_rot = jnp.concatenate([-pe[..., half:], pe[..., :half]], axis=-1)
    rot = jnp.concatenate([pe_rot, z], axis=-1).reshape(r, MLA_HEADS * LANES)
    wq = jnp.concatenate([main, rot], axis=1).astype(MXU_DTYPE)
    wkv3 = w_ukv.reshape(r, MLA_HEADS, MLA_NOPE + MLA_V)
    wk = wkv3[..., :MLA_NOPE].reshape(r, MLA_HEADS * MLA_NOPE).astype(MXU_DTYPE)
    wv = wkv3[..., MLA_NOPE:].reshape(r, MLA_HEADS * MLA_V).astype(MXU_DTYPE)
    return wq, wk, wv


def _rope_inv_freq_row():
    half = MLA_ROPE // 2
    f = ROPE_BASE ** (-jnp.arange(half, dtype=F32) / half)
    return jnp.concatenate([f, f, jnp.zeros((LANES - MLA_ROPE,), F32)]).reshape(1, LANES)


def _softmax_step(s, mask, v, m_scr, l_scr, acc_scr):
    m_prev = m_scr[...]
    m_new = jnp.maximum(m_prev, jnp.max(s, axis=1, keepdims=True))
    e = jnp.exp(s - m_new)
    if mask is not None:
        e = jnp.where(mask, e, 0.0)
    alpha = jnp.exp(m_prev - m_new)
    l_scr[...] = alpha * l_scr[...] + jnp.sum(e, axis=1, keepdims=True)
    acc_scr[...] = alpha * acc_scr[...] + _dot(e.astype(MXU_DTYPE), v)
    m_scr[...] = m_new


def _flash_kernel(qt_ref, kt_ref, q_ref, k_ref, v_ref, o_ref, m_scr, l_scr, acc_scr, *, tq, tk):
    step = pl.program_id(2)
    qi = qt_ref[step]
    ki = kt_ref[step]

    @pl.when(ki == 0)
    def _():
        m_scr[...] = jnp.full(m_scr.shape, NEG_INF, F32)
        l_scr[...] = jnp.zeros(l_scr.shape, F32)
        acc_scr[...] = jnp.zeros(acc_scr.shape, F32)

    s = _dot_t(q_ref[0], k_ref[0])
    crosses_diagonal = (ki + 1) * tk - 1 > qi * tq

    @pl.when(crosses_diagonal)
    def _():
        qpos = qi * tq + lax.broadcasted_iota(jnp.int32, (tq, 1), 0)
        kpos = ki * tk + lax.broadcasted_iota(jnp.int32, (1, tk), 1)
        mask = kpos <= qpos
        _softmax_step(jnp.where(mask, s, NEG_INF), mask, v_ref[0], m_scr, l_scr, acc_scr)

    @pl.when(jnp.logical_not(crosses_diagonal))
    def _():
        _softmax_step(s, None, v_ref[0], m_scr, l_scr, acc_scr)

    @pl.when(ki == ((qi + 1) * tq - 1) // tk)
    def _():
        o_ref[0] = (acc_scr[...] / l_scr[...]).astype(o_ref.dtype)


def _mla_flash(q3, k3, v3, *, tq=512, tk=512):
    b, s, _ = q3.shape
    nq = s // tq
    pairs = [(qi, ki) for qi in range(nq) for ki in range(((qi + 1) * tq - 1) // tk + 1)]
    qt = jnp.asarray(np.array([p[0] for p in pairs], np.int32))
    kt = jnp.asarray(np.array([p[1] for p in pairs], np.int32))
    grid_spec = pltpu.PrefetchScalarGridSpec(
        num_scalar_prefetch=2,
        grid=(b, MLA_HEADS, len(pairs)),
        in_specs=[
            pl.BlockSpec((1, tq, MLA_QK_PAD), lambda bi, h, st, qt, kt: (bi, qt[st], h)),
            pl.BlockSpec((1, tk, MLA_QK_PAD), lambda bi, h, st, qt, kt: (bi, kt[st], h)),
            pl.BlockSpec((1, tk, MLA_V), lambda bi, h, st, qt, kt: (bi, kt[st], h)),
        ],
        out_specs=pl.BlockSpec((1, tq, MLA_V), lambda bi, h, st, qt, kt: (bi, qt[st], h)),
        scratch_shapes=[pltpu.VMEM((tq, 1), F32), pltpu.VMEM((tq, 1), F32), pltpu.VMEM((tq, MLA_V), F32)],
    )
    return pl.pallas_call(
        functools.partial(_flash_kernel, tq=tq, tk=tk),
        grid_spec=grid_spec,
        out_shape=jax.ShapeDtypeStruct((b, s, MLA_HEADS * MLA_V), MXU_DTYPE),
        compiler_params=_cparams(("parallel", "parallel", "arbitrary")),
        name="mla_flash",
    )(qt, kt, q3, k3, v3)


def _dil_kernel(q_ref, kc_ref, kp_ref, vc_ref, vp_ref, o_ref, lse_ref, *, dilation, span, slopes, tl):
    n = pl.program_id(2)
    scale = HEAD_DIM ** -0.5
    blk = DIL_BLOCK
    a = lax.broadcasted_iota(jnp.int32, (blk, 1), 0)
    c = lax.broadcasted_iota(jnp.int32, (1, 2 * blk), 1)
    j = blk + a - c
    in_band = (j >= 0) & (j <= span)
    first_valid = in_band & ((c >= blk) | (n > 0))
    dist = (j * dilation).astype(F32)
    for hg in range(2):
        cols = slice(hg * LANES, (hg + 1) * LANES)
        bias = slopes[hg] * dist
        for sb in range(tl // blk):
            rows = slice(sb * blk, (sb + 1) * blk)
            q = (q_ref[0, rows, cols].astype(F32) * scale).astype(MXU_DTYPE)
            if sb == 0:
                kprev, vprev, valid = kp_ref[0, :, cols], vp_ref[0, :, cols], first_valid
            else:
                prev = slice((sb - 1) * blk, sb * blk)
                kprev, vprev, valid = kc_ref[0, prev, cols], vc_ref[0, prev, cols], in_band
            keys = jnp.concatenate([kprev, kc_ref[0, rows, cols]], axis=0)
            vals = jnp.concatenate([vprev, vc_ref[0, rows, cols]], axis=0)
            s = jnp.where(valid, _dot_t(q, keys) - bias, NEG_INF)
            m = jnp.max(s, axis=1, keepdims=True)
            e = jnp.where(valid, jnp.exp(s - m), 0.0)
            den = jnp.sum(e, axis=1, keepdims=True)
            o_ref[0, rows, cols] = _dot((e / den).astype(MXU_DTYPE), vals)
            lse_ref[0, rows, cols] = jnp.broadcast_to(m + jnp.log(den), (blk, LANES))


def _dilated_group(h3, g, *, tl=512):
    b, s, _ = h3.shape
    window, dilation = DIL_PAIRS[g]
    span = window // dilation
    seq_l = s // dilation
    tl = min(tl, seq_l)
    per_blk = tl // DIL_BLOCK
    hv = h3.reshape(b, seq_l, dilation * H_COLS)
    w = 2 * LANES
    stride = H_COLS // w
    qc, kc, vc = ((COL_DIL + t * DIL_HEADS * HEAD_DIM) // w + g for t in range(3))
    cur = lambda col: pl.BlockSpec((1, tl, w), lambda bi, r, n: (bi, n, r * stride + col))
    prev = lambda col: pl.BlockSpec(
        (1, DIL_BLOCK, w), lambda bi, r, n: (bi, jnp.maximum(n * per_blk - 1, 0), r * stride + col))
    out_spec = pl.BlockSpec((1, tl, w), lambda bi, r, n: (bi, n, r))
    out_sds = jax.ShapeDtypeStruct((b, seq_l, dilation * w), F32)
    o, lse = pl.pallas_call(
        functools.partial(_dil_kernel, dilation=dilation, span=span,
                          slopes=ALIBI_SLOPES[2 * g:2 * g + 2], tl=tl),
        grid=(b, dilation, seq_l // tl),
        in_specs=[cur(qc), cur(kc), prev(kc), cur(vc), prev(vc)],
        out_specs=[out_spec, out_spec],
        out_shape=[out_sds, out_sds],
        compiler_params=_cparams(("parallel", "parallel", "parallel")),
        name=f"dilated_g{g}",
    )(hv, hv, hv, hv, hv)
    return o.reshape(b, s, w), lse.reshape(b, s, w)


def _nsa_cmp_kernel(x_ref, pos_ref, wa_ref, wb_ref, w2k_ref, w2v_ref, kc_ref, vc_ref):
    x = x_ref[0]
    xa = (x + pos_ref[0:1, :]).astype(MXU_DTYPE)
    xb = (x + pos_ref[1:2, :]).astype(MXU_DTYPE)
    first = _dot(xa, wa_ref[...])
    second = _dot(xb, wb_ref[...])
    second_next = pltpu.roll(second, x.shape[0] - 1, 0)
    pre = first + second_next
    hid = (pre * jax.nn.sigmoid(pre)).astype(MXU_DTYPE)
    kc_ref[0] = _dot(hid[:, :LANES], w2k_ref[...]).astype(kc_ref.dtype)
    vc_ref[0] = _dot(hid[:, LANES:], w2v_ref[...]).astype(vc_ref.dtype)


def _nsa_compress(aux3, pos2, wa, wb, w2k, w2v):
    b, s, _ = aux3.shape
    nchunk = s // NSA_CMP_STRIDE
    xk = aux3[:, :, :2 * LANES].reshape(b, nchunk, NSA_CMP_STRIDE * 2 * LANES)
    full = lambda a: pl.BlockSpec(a.shape, lambda bi: (0,) * a.ndim)
    out_spec = pl.BlockSpec((1, nchunk, LANES), lambda bi: (bi, 0, 0))
    out_sds = jax.ShapeDtypeStruct((b, nchunk, LANES), MXU_DTYPE)
    return pl.pallas_call(
        _nsa_cmp_kernel,
        grid=(b,),
        in_specs=[pl.BlockSpec((1, nchunk, xk.shape[2]), lambda bi: (bi, 0, 0)),
                  full(pos2), full(wa), full(wb), full(w2k), full(w2v)],
        out_specs=[out_spec, out_spec],
        out_shape=[out_sds, out_sds],
        compiler_params=_cparams(("parallel",)),
        name="nsa_cmp",
    )(xk, pos2, wa, wb, w2k, w2v)


def _prep_nsa_cmp(cmp_pos, phi_k1, phi_k2, phi_v1, phi_v2):
    half = NSA_CMP_LEN // 2
    pk = cmp_pos.reshape(2, half, HEAD_DIM)
    pos2 = jnp.concatenate([pk, pk], axis=-1).reshape(2, half * 2 * LANES)

    def halves(w1, is_v):
        w = w1.reshape(2, half, HEAD_DIM, HEAD_DIM)
        z = jnp.zeros_like(w)
        w = jnp.concatenate([z, w] if is_v else [w, z], axis=2)
        return w.reshape(2, half * 2 * LANES, HEAD_DIM)

    wk, wv = halves(phi_k1, False), halves(phi_v1, True)
    wa = jnp.concatenate([wk[0], wv[0]], axis=1).astype(MXU_DTYPE)
    wb = jnp.concatenate([wk[1], wv[1]], axis=1).astype(MXU_DTYPE)
    return pos2, wa, wb, phi_k2.astype(MXU_DTYPE), phi_v2.astype(MXU_DTYPE)


def _masked_softmax_rows(s, mask):
    s = jnp.where(mask, s, NEG_INF)
    m = jnp.max(s, axis=1, keepdims=True)
    e = jnp.where(mask, jnp.exp(s - m), 0.0)
    den = jnp.sum(e, axis=1, keepdims=True)
    return e / jnp.where(den > 0.0, den, 1.0)


def _split_dot(p, w):
    if MXU_DTYPE == jnp.float32:
        return _dot(p, w)
    hi = p.astype(MXU_DTYPE)
    r = p - hi.astype(F32)
    mid = r.astype(MXU_DTYPE)
    lo = (r - mid.astype(F32)).astype(MXU_DTYPE)
    return _dot(hi, w) + _dot(mid, w) + _dot(lo, w)


def _nsa_kernel(q_ref, kv_ref, kc_ref, vc_ref, c2s_ref, g_ref, o_ref, m_scr, l_scr, acc_scr,
                *, tq, tk, topk):
    i = pl.program_id(1)
    t0 = i * tq
    nh = NSA_HEADS
    rows = nh * tq
    ncp = kc_ref.shape[1]
    ns = c2s_ref.shape[1]
    scale = HEAD_DIM ** -0.5

    q = q_ref[0]
    q4 = jnp.concatenate([q[:, h * LANES:(h + 1) * LANES] for h in range(nh)], axis=0)
    q4 = (q4.astype(F32) * scale).astype(MXU_DTYPE)
    row = lax.broadcasted_iota(jnp.int32, (rows, 1), 0)
    head = _shr(row, tq)
    tpos = (t0 + (row & (tq - 1))).astype(F32)
    slope = jnp.full((rows, 1), ALIBI_SLOPES[DIL_HEADS + nh - 1], F32)
    for h in range(nh - 1):
        slope = jnp.where(head == h, ALIBI_SLOPES[DIL_HEADS + h], slope)

    cidx = lax.broadcasted_iota(jnp.int32, (1, ncp), 1).astype(F32)
    c_end = cidx * NSA_CMP_STRIDE + (NSA_CMP_LEN - 1)
    c_centre = cidx * NSA_CMP_STRIDE + 0.5 * (NSA_CMP_LEN - 1)
    s = _dot_t(q4, kc_ref[0]) - slope * (tpos - c_centre)
    p_cmp = _masked_softmax_rows(s, c_end <= tpos)
    o_cmp = _dot(p_cmp.astype(MXU_DTYPE), vc_ref[0])

    p_sum = p_cmp[0:tq]
    for h in range(1, nh):
        p_sum = p_sum + p_cmp[h * tq:(h + 1) * tq]
    score = _split_dot(p_sum, c2s_ref[...])
    t1 = t0 + lax.broadcasted_iota(jnp.int32, (tq, 1), 0)
    cur = _shr(t1, NSA_SEL_LEN).astype(F32)
    jj = lax.broadcasted_iota(jnp.int32, (1, ns), 1).astype(F32)
    forced = (jj == 0.0) | (jj == cur) | (jj == cur - 1.0)
    score = jnp.where(jj > cur, -1.0, jnp.where(forced, NSA_FORCED_SCORE, score))
    sel = jnp.zeros((tq, ns), F32)
    for _ in range(topk):
        best = jnp.max(score, axis=1, keepdims=True)
        idx = jnp.min(jnp.where(score == best, jj, float(ns)), axis=1, keepdims=True)
        hit = jj == idx
        sel = jnp.where(hit, 1.0, sel)
        score = jnp.where(hit, -2.0, score)
    sel = sel.astype(MXU_DTYPE)

    m_scr[...] = jnp.full(m_scr.shape, NEG_INF, F32)
    l_scr[...] = jnp.zeros(l_scr.shape, F32)
    acc_scr[...] = jnp.zeros(acc_scr.shape, F32)
    blk_row = lax.broadcasted_iota(jnp.int32, (ns, 1), 0)
    key_col = lax.broadcasted_iota(jnp.int32, (1, tk), 1)

    def slc_tile(kt, carry):
        k0 = pl.multiple_of(kt * tk, tk)
        ks = kv_ref[0, pl.ds(k0, tk), 0:LANES]
        vs = kv_ref[0, pl.ds(k0, tk), LANES:2 * LANES]
        dist = tpos - (k0 + key_col).astype(F32)
        expand = jnp.where(_shr(k0 + key_col, NSA_SEL_LEN) == blk_row, 1.0, 0.0).astype(MXU_DTYPE)
        chosen = _dot(sel, expand)
        chosen = jnp.concatenate([chosen] * nh, axis=0)
        mask = (chosen > 0.5) & (dist >= 0.0)
        s = jnp.where(mask, _dot_t(q4, ks) - slope * dist, NEG_INF)
        _softmax_step(s, mask, vs, m_scr, l_scr, acc_scr)
        return carry

    lax.fori_loop(0, (t0 + tq - 1) // tk + 1, slc_tile, 0)
    o_slc = acc_scr[...] / l_scr[...]

    wlen = NSA_WINDOW + tq
    ws = pl.multiple_of(jnp.maximum(t0 - NSA_WINDOW, 0), tq)
    kw = kv_ref[0, pl.ds(ws, wlen), 2 * LANES:3 * LANES]
    vw = kv_ref[0, pl.ds(ws, wlen), 3 * LANES:4 * LANES]
    wpos = (ws + lax.broadcasted_iota(jnp.int32, (1, wlen), 1)).astype(F32)
    dist = tpos - wpos
    s = _dot_t(q4, kw) - slope * dist
    p_win = _masked_softmax_rows(s, (dist >= 0.0) & (dist < float(NSA_WINDOW)))
    o_win = _dot(p_win.astype(MXU_DTYPE), vw)

    gates = jax.nn.sigmoid(g_ref[0])
    for h in range(nh):
        r = slice(h * tq, (h + 1) * tq)
        o = (gates[:, 3 * h:3 * h + 1] * o_cmp[r] + gates[:, 3 * h + 1:3 * h + 2] * o_slc[r]
             + gates[:, 3 * h + 2:3 * h + 3] * o_win[r])
        o_ref[0, :, h * LANES:(h + 1) * LANES] = o.astype(o_ref.dtype)


def _nsa(h3, aux3, kc, vc, c2s, *, tq=NSA_Q_BLOCK, tk=512):
    b, s, _ = h3.shape
    ncp = kc.shape[1]
    ns = c2s.shape[1]
    tk = min(tk, s)
    w = NSA_HEADS * HEAD_DIM
    return pl.pallas_call(
        functools.partial(_nsa_kernel, tq=tq, tk=tk, topk=min(NSA_TOPK, ns)),
        grid=(b, s // tq),
        in_specs=[
            pl.BlockSpec((1, tq, w), lambda bi, i: (bi, i, COL_NSA_Q // w)),
            pl.BlockSpec((1, s, w), lambda bi, i: (bi, 0, COL_NSA_KV // w)),
            pl.BlockSpec((1, ncp, LANES), lambda bi, i: (bi, 0, 0)),
            pl.BlockSpec((1, ncp, LANES), lambda bi, i: (bi, 0, 0)),
            pl.BlockSpec((ncp, ns), lambda bi, i: (0, 0)),
            pl.BlockSpec((1, tq, LANES), lambda bi, i: (bi, i, 2)),
        ],
        out_specs=pl.BlockSpec((1, tq, w), lambda bi, i: (bi, i, 0)),
        out_shape=jax.ShapeDtypeStruct((b, s, w), MXU_DTYPE),
        scratch_shapes=[pltpu.VMEM((NSA_HEADS * tq, 1), F32), pltpu.VMEM((NSA_HEADS * tq, 1), F32),
                        pltpu.VMEM((NSA_HEADS * tq, HEAD_DIM), F32)],
        compiler_params=_cparams(("parallel", "arbitrary")),
        name="nsa",
    )(h3, h3, kc, vc, c2s, aux3)


def _cmp_to_sel(ncp, ns):
    nc = ncp - 1
    c = np.arange(ncp)[:, None]
    j = np.arange(ns)[None, :]
    start = c * NSA_CMP_STRIDE
    m = (start < (j + 1) * NSA_SEL_LEN) & (start + NSA_CMP_LEN - 1 >= j * NSA_SEL_LEN) & (c < nc)
    return jnp.asarray(m.astype(np.float32)).astype(MXU_DTYPE)


def _mixout_kernel(x_ref, mla_ref, d0_ref, d1_ref, d2_ref, l0_ref, l1_ref, l2_ref, nsa_ref, w_ref, o_ref):
    o_dil = (d0_ref[...], d1_ref[...], d2_ref[...])
    lse = (l0_ref[...], l1_ref[...], l2_ref[...])
    top = jnp.maximum(jnp.maximum(lse[0], lse[1]), lse[2])
    e = [jnp.exp(l - top) for l in lse]
    inv = 1.0 / (e[0] + e[1] + e[2])
    acc = x_ref[...] + _dot(mla_ref[...], w_ref[0:768, :])
    for g in range(3):
        mixed = (o_dil[g] * (e[g] * inv)).astype(MXU_DTYPE)
        acc += _dot(mixed, w_ref[768 + g * 256:768 + (g + 1) * 256, :])
    acc += _dot(nsa_ref[...], w_ref[1536:2048, :])
    o_ref[...] = acc


def _mixout(x2, o_mla, o_dil, lse_dil, o_nsa, w, *, tm=512):
    n, d = x2.shape
    row = lambda width: pl.BlockSpec((tm, width), lambda i: (i, 0))
    return pl.pallas_call(
        _mixout_kernel,
        grid=(n // tm,),
        in_specs=[row(d), row(o_mla.shape[1])] + [row(256)] * 6 + [row(o_nsa.shape[1]),
                  pl.BlockSpec(w.shape, lambda i: (0, 0))],
        out_specs=row(d),
        out_shape=jax.ShapeDtypeStruct((n, d), F32),
        compiler_params=_cparams(("parallel",)),
        name="mix_out",
    )(x2, o_mla, *o_dil, *lse_dil, o_nsa, w)


def kernel(x, ffn1_norm, ffn1_w_in, ffn1_w_out, mix_norm, w_mix_in, mla_q_norm, mla_w_uq, mla_kv_norm,
           mla_w_ukv, nsa_cmp_pos, nsa_phi_k1, nsa_phi_k2, nsa_phi_v1, nsa_phi_v2, w_mix_out, ffn2_norm,
           ffn2_w_in, ffn2_w_out, final_norm):
    b, s, d = x.shape
    depth = ffn1_w_in.shape[0]
    n = b * s
    ncp = s // NSA_CMP_STRIDE
    ns = s // NSA_SEL_LEN
    invf = _rope_inv_freq_row()
    c2s = _cmp_to_sel(ncp, ns)
    x2 = x.reshape(n, d)
    for l in range(depth):
        x2 = _ffn(x2, ffn1_norm[l], *_prep_ffn(ffn1_w_in[l], ffn1_w_out[l]))

        h2, aux2 = _mixin(x2, mix_norm[l], *_prep_mixin(w_mix_in[l]))
        h3 = h2.reshape(b, s, H_COLS)
        aux3 = aux2.reshape(b, s, AUX_COLS)

        q, k, v = _mla_proj(h2, mla_q_norm[l].reshape(1, -1), mla_kv_norm[l].reshape(1, -1),
                            *_prep_mla(mla_w_uq[l], mla_w_ukv[l]), invf, seq=s)
        o_mla = _mla_flash(q.reshape(b, s, -1), k.reshape(b, s, -1), v.reshape(b, s, -1))

        dil = [_dilated_group(h3, g) for g in range(len(DIL_PAIRS))]

        kc, vc = _nsa_compress(aux3, *_prep_nsa_cmp(nsa_cmp_pos[l], nsa_phi_k1[l], nsa_phi_k2[l],
                                                     nsa_phi_v1[l], nsa_phi_v2[l]))
        o_nsa = _nsa(h3, aux3, kc, vc, c2s)

        x2 = _mixout(x2, o_mla.reshape(n, -1), [o.reshape(n, -1) for o, _ in dil],
                     [e.reshape(n, -1) for _, e in dil], o_nsa.reshape(n, -1),
                     w_mix_out[l].astype(MXU_DTYPE))

        gf = final_norm if l == depth - 1 else None
        x2 = _ffn(x2, ffn2_norm[l], *_prep_ffn(ffn2_w_in[l], ffn2_w_out[l]), gf)
    return x2.reshape(b, s, d)
```

```python
import functools
import math

import numpy as np
import jax
import jax.numpy as jnp
from jax import lax
from jax.experimental import pallas as pl
from jax.experimental.pallas import tpu as pltpu

F32 = jnp.float32
MXU_DTYPE = jnp.bfloat16
VMEM_LIMIT_BYTES = 56 * 1024 * 1024
LANES = 128
LOG2E = math.log2(math.e)

HEAD_DIM = 128
RMS_EPS = 1e-6
NEG_INF = -1e30

MLA_HEADS = 6
MLA_LORA = 512
MLA_NOPE = 128
MLA_ROPE = 64
MLA_V = 128
MLA_QK_PAD = 256
ROPE_BASE = 10000.0

DIL_PAIRS = ((128, 1), (512, 4), (2048, 16))
DIL_HEADS = 6
DIL_BLOCK = 128

NSA_HEADS = 4
NSA_CMP_LEN = 32
NSA_CMP_STRIDE = 16
NSA_SEL_LEN = 64
NSA_TOPK = 16
NSA_WINDOW = 512
NSA_FORCED_SCORE = 100.0
NSA_Q_BLOCK = 128
POS_SPLIT = 128

N_ALIBI = DIL_HEADS + NSA_HEADS
ALIBI_SLOPES = tuple(float(2.0 ** (-8.0 * i / N_ALIBI)) for i in range(1, N_ALIBI + 1))

COL_QLAT = 0
COL_KVLAT = 512
COL_NSA_Q = 1024
COL_KROPE = 1536
COL_NSA_K = 1792
COL_DIL = 2048
H_COLS = 4352
H_TILE = 2176
AUX_COLS = 384


def _cparams(sem):
    return pltpu.CompilerParams(dimension_semantics=sem, vmem_limit_bytes=VMEM_LIMIT_BYTES)


def _rms(x, g):
    ms = jnp.mean(x * x, axis=-1, keepdims=True)
    return (x * lax.rsqrt(ms + RMS_EPS)) * g


def _dot(a, b):
    return jnp.dot(a, b, preferred_element_type=F32)


def _shr(x, pow2):
    return lax.shift_right_logical(x, int(pow2).bit_length() - 1)


def _dot_t(a, b):
    return lax.dot_general(a, b, (((1,), (1,)), ((), ())), preferred_element_type=F32)


def _resident(shape, index_map):
    return pl.BlockSpec(shape, index_map, pipeline_mode=pl.Buffered(1))


def _ffn_kernel(x_ref, g_ref, wgu_ref, wo_ref, *rest, final):
    if final:
        gf_ref, o_ref, xn_ref = rest
    else:
        o_ref, xn_ref = rest
    j = pl.program_id(1)

    @pl.when(j == 0)
    def _():
        x = x_ref[...]
        xn_ref[...] = _rms(x, g_ref[...]).astype(xn_ref.dtype)
        o_ref[...] = x

    xn = xn_ref[...]
    gate = _dot(xn, wgu_ref[0])
    up = _dot(xn, wgu_ref[1])
    h = (0.5 * gate) * jax.nn.sigmoid(gate) * up
    o_ref[...] += _dot(h.astype(MXU_DTYPE), wo_ref[...])

    if final:
        @pl.when(j == pl.num_programs(1) - 1)
        def _():
            o_ref[...] = _rms(o_ref[...], gf_ref[...])


def _ffn(x2, g, wgu, wo, gf=None, *, tm=512, tf=512):
    n, d = x2.shape
    dffp = wo.shape[0]
    final = gf is not None
    in_specs = [
        pl.BlockSpec((tm, d), lambda i, j: (i, 0)),
        pl.BlockSpec((1, d), lambda i, j: (0, 0)),
        pl.BlockSpec((2, d, tf), lambda i, j: (0, 0, j)),
        pl.BlockSpec((tf, d), lambda i, j: (j, 0)),
    ]
    args = [x2, g.reshape(1, d), wgu, wo]
    if final:
        in_specs.append(pl.BlockSpec((1, d), lambda i, j: (0, 0)))
        args.append(gf.reshape(1, d))
    return pl.pallas_call(
        functools.partial(_ffn_kernel, final=final),
        grid=(n // tm, dffp // tf),
        in_specs=in_specs,
        out_specs=pl.BlockSpec((tm, d), lambda i, j: (i, 0)),
        out_shape=jax.ShapeDtypeStruct((n, d), F32),
        scratch_shapes=[pltpu.VMEM((tm, d), MXU_DTYPE)],
        compiler_params=_cparams(("parallel", "arbitrary")),
        name="ffn_final" if final else "ffn",
    )(*args)


def _prep_ffn(w_in, w_out, tf=512):
    d, two_dff = w_in.shape
    dff = two_dff // 2
    dffp = -(-dff // tf) * tf
    wgu = w_in.reshape(d, 2, dff).transpose(1, 0, 2)
    wgu = jnp.pad(wgu, ((0, 0), (0, 0), (0, dffp - dff))).astype(MXU_DTYPE)
    wo = jnp.pad(w_out, ((0, dffp - dff), (0, 0))).astype(MXU_DTYPE)
    return wgu, wo


def _mixin_kernel(x_ref, g_ref, w_ref, wa_ref, wvt_ref, h_ref, a_ref, vt_ref, xn_ref):
    j = pl.program_id(1)

    @pl.when(j == 0)
    def _():
        xn = _rms(x_ref[...], g_ref[...]).astype(xn_ref.dtype)
        xn_ref[...] = xn
        a_ref[...] = _dot(xn, wa_ref[...])
        vt_ref[...] = _dot_t(wvt_ref[...], xn).astype(vt_ref.dtype)

    h_ref[...] = _dot(xn_ref[...], w_ref[...]).astype(h_ref.dtype)


def _mixin(x2, g, w, wa, wvt, *, tm=512):
    n, d = x2.shape
    return pl.pallas_call(
        _mixin_kernel,
        grid=(n // tm, H_COLS // H_TILE),
        in_specs=[
            pl.BlockSpec((tm, d), lambda i, j: (i, 0)),
            pl.BlockSpec((1, d), lambda i, j: (0, 0)),
            pl.BlockSpec((d, H_TILE), lambda i, j: (0, j)),
            pl.BlockSpec((d, AUX_COLS), lambda i, j: (0, 0)),
            pl.BlockSpec((2 * LANES, d), lambda i, j: (0, 0)),
        ],
        out_specs=[
            pl.BlockSpec((tm, H_TILE), lambda i, j: (i, j)),
            pl.BlockSpec((tm, AUX_COLS), lambda i, j: (i, 0)),
            pl.BlockSpec((2 * LANES, tm), lambda i, j: (0, i)),
        ],
        out_shape=[
            jax.ShapeDtypeStruct((n, H_COLS), MXU_DTYPE),
            jax.ShapeDtypeStruct((n, AUX_COLS), F32),
            jax.ShapeDtypeStruct((2 * LANES, n), MXU_DTYPE),
        ],
        scratch_shapes=[pltpu.VMEM((tm, d), MXU_DTYPE)],
        compiler_params=_cparams(("parallel", "arbitrary")),
        name="mix_in",
    )(x2, g.reshape(1, d), w, wa, wvt)


def _rot_half_cols(w):
    half = w.shape[1] // 2
    return jnp.concatenate([-w[:, half:], w[:, :half]], axis=1)


def _prep_mixin(w):
    d = w.shape[0]
    o = 0
    parts = {}
    for name, width in (("q_lat", 512), ("kv_lat", 512), ("k_rope", 64), ("dq", 768), ("dk", 768),
                        ("dv", 768), ("nq", 512), ("nkc", 128), ("nvc", 128), ("nks", 128),
                        ("nvs", 128), ("nkw", 128), ("nvw", 128), ("ng", 12)):
        parts[name] = w[:, o:o + width]
        o += width
    z64 = jnp.zeros((d, 64), w.dtype)
    kr = parts["k_rope"]
    wh = jnp.concatenate([
        parts["q_lat"], parts["kv_lat"], parts["nq"],
        kr, z64, _rot_half_cols(kr), z64,
        parts["nks"], parts["nkw"],
        parts["dq"], parts["dk"], parts["dv"]], axis=1).astype(MXU_DTYPE)
    wa = jnp.concatenate([parts["nkc"], parts["nvc"], parts["ng"],
                          jnp.zeros((d, LANES - 12), w.dtype)], axis=1).astype(MXU_DTYPE)
    wvt = jnp.concatenate([parts["nvs"], parts["nvw"]], axis=1).T.astype(MXU_DTYPE)
    return wh, wa, wvt


def _mla_proj_kernel(lat_ref, kr_ref, qn_ref, kvn_ref, wq_ref, wk_ref, wvt_ref, invf_ref,
                     q_ref, k_ref, vt_ref, *, seq, tm):
    i = pl.program_id(0)
    pos0 = lax.rem(i * tm, seq)
    pos = (pos0 + lax.broadcasted_iota(jnp.int32, (tm, 1), 0)).astype(F32)
    ang = pos * invf_ref[...]
    cos = jnp.cos(ang)
    sin = jnp.sin(ang)
    scale = (MLA_NOPE + MLA_ROPE) ** -0.5 * LOG2E

    lat = lat_ref[...].astype(F32)
    qn = _rms(lat[:, :MLA_LORA], qn_ref[...]).astype(MXU_DTYPE)
    kvn = _rms(lat[:, MLA_LORA:], kvn_ref[...]).astype(MXU_DTYPE)
    qm = _dot(qn, wq_ref[...])
    rot0 = MLA_HEADS * MLA_QK_PAD
    for h in range(MLA_HEADS):
        c = h * MLA_QK_PAD
        nope = qm[:, c:c + LANES]
        pe = qm[:, c + LANES:c + 2 * LANES] * cos + qm[:, rot0 + h * LANES:rot0 + (h + 1) * LANES] * sin
        q_ref[:, c:c + LANES] = (nope * scale).astype(q_ref.dtype)
        q_ref[:, c + LANES:c + 2 * LANES] = (pe * scale).astype(q_ref.dtype)

    kr = kr_ref[...].astype(F32)
    kpe = (kr[:, :LANES] * cos + kr[:, LANES:] * sin).astype(k_ref.dtype)
    kn = _dot(kvn, wk_ref[...])
    for h in range(MLA_HEADS):
        c = h * MLA_QK_PAD
        k_ref[:, c:c + LANES] = kn[:, h * LANES:(h + 1) * LANES].astype(k_ref.dtype)
        k_ref[:, c + LANES:c + 2 * LANES] = kpe
    vt_ref[...] = _dot_t(wvt_ref[...], kvn).astype(vt_ref.dtype)


def _mla_proj(h2, qn, kvn, wq, wk, wvt, invf, *, seq, tm=512):
    n = h2.shape[0]
    qk_cols = MLA_HEADS * MLA_QK_PAD
    v_rows = MLA_HEADS * MLA_V
    full = lambda a: pl.BlockSpec(a.shape, lambda i: (0,) * a.ndim)
    return pl.pallas_call(
        functools.partial(_mla_proj_kernel, seq=seq, tm=tm),
        grid=(n // tm,),
        in_specs=[
            pl.BlockSpec((tm, 2 * MLA_LORA), lambda i: (i, 0)),
            pl.BlockSpec((tm, 2 * LANES), lambda i: (i, COL_KROPE // (2 * LANES))),
            full(qn), full(kvn), full(wq), full(wk), full(wvt), full(invf),
        ],
        out_specs=[
            pl.BlockSpec((tm, qk_cols), lambda i: (i, 0)),
            pl.BlockSpec((tm, qk_cols), lambda i: (i, 0)),
            pl.BlockSpec((v_rows, tm), lambda i: (0, i)),
        ],
        out_shape=[
            jax.ShapeDtypeStruct((n, qk_cols), MXU_DTYPE),
            jax.ShapeDtypeStruct((n, qk_cols), MXU_DTYPE),
            jax.ShapeDtypeStruct((v_rows, n), MXU_DTYPE),
        ],
        compiler_params=_cparams(("parallel",)),
        name="mla_proj",
    )(h2, h2, qn, kvn, wq, wk, wvt, invf)


def _prep_mla(w_uq, w_ukv):
    r = MLA_LORA
    wq3 = w_uq.reshape(r, MLA_HEADS, MLA_NOPE + MLA_ROPE)
    nope, pe = wq3[..., :MLA_NOPE], wq3[..., MLA_NOPE:]
    z = jnp.zeros((r, MLA_HEADS, 64), w_uq.dtype)
    main = jnp.concatenate([nope, pe, z], axis=-1).reshape(r, MLA_HEADS * MLA_QK_PAD)
    half = MLA_ROPE // 2
    pe_rot = jnp.concatenate([-pe[..., half:], pe[..., :half]], axis=-1)
    rot = jnp.concatenate([pe_rot, z], axis=-1).reshape(r, MLA_HEADS * LANES)
    wq = jnp.concatenate([main, rot], axis=1).astype(MXU_DTYPE)
    wkv3 = w_ukv.reshape(r, MLA_HEADS, MLA_NOPE + MLA_V)
    wk = wkv3[..., :MLA_NOPE].reshape(r, MLA_HEADS * MLA_NOPE).astype(MXU_DTYPE)
    wvt = wkv3[..., MLA_NOPE:].reshape(r, MLA_HEADS * MLA_V).T.astype(MXU_DTYPE)
    return wq, wk, wvt


def _rope_inv_freq_row():
    half = MLA_ROPE // 2
    f = ROPE_BASE ** (-jnp.arange(half, dtype=F32) / half)
    return jnp.concatenate([f, f, jnp.zeros((LANES - MLA_ROPE,), F32)]).reshape(1, LANES)


def _softmax_step_t(s_t, v_t, m_ref, l_ref, acc_ref):
    m_prev = m_ref[...]
    m_new = jnp.maximum(m_prev, jnp.max(s_t, axis=0, keepdims=True))
    e = jnp.exp2(s_t - m_new)
    alpha = jnp.exp2(m_prev - m_new)
    l_ref[...] = alpha * l_ref[...] + jnp.sum(e, axis=0, keepdims=True)
    acc_ref[...] = alpha * acc_ref[...] + _dot(v_t, e.astype(MXU_DTYPE))
    m_ref[...] = m_new


def _init_softmax_state(m_scr, l_scr, acc_scr):
    m_scr[...] = jnp.full(m_scr.shape, NEG_INF, F32)
    l_scr[...] = jnp.zeros(l_scr.shape, F32)
    acc_scr[...] = jnp.zeros(acc_scr.shape, F32)


def _flash_kernel(qt_ref, kt_ref, q_ref, k_ref, vt_ref, o_ref, m_scr, l_scr, acc_scr, *, t, sub):
    step = pl.program_id(2)
    qi = qt_ref[step]
    ki = kt_ref[step]
    nsub = t // sub

    @pl.when(ki == 0)
    def _():
        _init_softmax_state(m_scr, l_scr, acc_scr)

    def piece(kp, qp, masked):
        ks = slice(kp * sub, (kp + 1) * sub)
        qs = slice(qp * sub, (qp + 1) * sub)
        s_t = _dot_t(k_ref[0, ks, :], q_ref[0, qs, :])
        if masked:
            kpos = lax.broadcasted_iota(jnp.int32, (sub, 1), 0)
            qpos = lax.broadcasted_iota(jnp.int32, (1, sub), 1)
            s_t = jnp.where(kpos <= qpos, s_t, NEG_INF)
        _softmax_step_t(s_t, vt_ref[:, ks], m_scr.at[:, qs], l_scr.at[:, qs], acc_scr.at[:, qs])

    @pl.when(ki < qi)
    def _():
        for kp in range(nsub):
            for qp in range(nsub):
                piece(kp, qp, False)

    @pl.when(ki == qi)
    def _():
        for kp in range(nsub):
            for qp in range(kp, nsub):
                piece(kp, qp, kp == qp)
        o_t = acc_scr[...] * (1.0 / l_scr[...])
        o_ref[0] = o_t.T.astype(o_ref.dtype)


def _mla_flash(q3, k3, vt, *, t=1024, sub=512):
    b, s, _ = q3.shape
    t = min(t, s)
    sub = min(sub, t)
    nq = s // t
    pairs = [(qi, ki) for qi in range(nq) for ki in range(qi + 1)]
    qt = jnp.asarray(np.array([p[0] for p in pairs], np.int32))
    kt = jnp.asarray(np.array([p[1] for p in pairs], np.int32))
    grid_spec = pltpu.PrefetchScalarGridSpec(
        num_scalar_prefetch=2,
        grid=(b, MLA_HEADS, len(pairs)),
        in_specs=[
            pl.BlockSpec((1, t, MLA_QK_PAD), lambda bi, h, st, qt, kt: (bi, qt[st], h)),
            pl.BlockSpec((1, t, MLA_QK_PAD), lambda bi, h, st, qt, kt: (bi, kt[st], h)),
            pl.BlockSpec((MLA_V, t), lambda bi, h, st, qt, kt: (h, bi * nq + kt[st])),
        ],
        out_specs=pl.BlockSpec((1, t, MLA_V), lambda bi, h, st, qt, kt: (bi, qt[st], h)),
        scratch_shapes=[pltpu.VMEM((1, t), F32), pltpu.VMEM((1, t), F32), pltpu.VMEM((MLA_V, t), F32)],
    )
    return pl.pallas_call(
        functools.partial(_flash_kernel, t=t, sub=sub),
        grid_spec=grid_spec,
        out_shape=jax.ShapeDtypeStruct((b, s, MLA_HEADS * MLA_V), MXU_DTYPE),
        compiler_params=_cparams(("parallel", "parallel", "arbitrary")),
        name="mla_flash",
    )(qt, kt, q3, k3, vt)


def _dil_kernel(q_ref, kc_ref, kp_ref, vc_ref, vp_ref, o_ref, lse_ref, *, dilation, span, slopes, tl):
    n = pl.program_id(2)
    scale = HEAD_DIM ** -0.5
    blk = DIL_BLOCK
    a = lax.broadcasted_iota(jnp.int32, (blk, 1), 0)
    c = lax.broadcasted_iota(jnp.int32, (1, 2 * blk), 1)
    j = blk + a - c
    in_band = (j >= 0) & (j <= span)
    first_valid = in_band & ((c >= blk) | (n > 0))
    dist = (j * dilation).astype(F32)
    for hg in range(2):
        cols = slice(hg * LANES, (hg + 1) * LANES)
        bias = slopes[hg] * dist
        for sb in range(tl // blk):
            rows = slice(sb * blk, (sb + 1) * blk)
            q = (q_ref[0, rows, cols].astype(F32) * scale).astype(MXU_DTYPE)
            if sb == 0:
                kprev, vprev, valid = kp_ref[0, :, cols], vp_ref[0, :, cols], first_valid
            else:
                prev = slice((sb - 1) * blk, sb * blk)
                kprev, vprev, valid = kc_ref[0, prev, cols], vc_ref[0, prev, cols], in_band
            keys = jnp.concatenate([kprev, kc_ref[0, rows, cols]], axis=0)
            vals = jnp.concatenate([vprev, vc_ref[0, rows, cols]], axis=0)
            s = jnp.where(valid, _dot_t(q, keys) - bias, NEG_INF)
            m = jnp.max(s, axis=1, keepdims=True)
            e = jnp.where(valid, jnp.exp(s - m), 0.0)
            den = jnp.sum(e, axis=1, keepdims=True)
            o_ref[0, rows, cols] = _dot((e / den).astype(MXU_DTYPE), vals)
            lse_ref[0, rows, cols] = jnp.broadcast_to(m + jnp.log(den), (blk, LANES))


def _dilated_group(h3, g, *, tl=512):
    b, s, _ = h3.shape
    window, dilation = DIL_PAIRS[g]
    span = window // dilation
    seq_l = s // dilation
    tl = min(tl, seq_l)
    per_blk = tl // DIL_BLOCK
    hv = h3.reshape(b, seq_l, dilation * H_COLS)
    w = 2 * LANES
    stride = H_COLS // w
    qc, kc, vc = ((COL_DIL + t * DIL_HEADS * HEAD_DIM) // w + g for t in range(3))
    cur = lambda col: pl.BlockSpec((1, tl, w), lambda bi, r, n: (bi, n, r * stride + col))
    prev = lambda col: pl.BlockSpec(
        (1, DIL_BLOCK, w), lambda bi, r, n: (bi, jnp.maximum(n * per_blk - 1, 0), r * stride + col))
    out_spec = pl.BlockSpec((1, tl, w), lambda bi, r, n: (bi, n, r))
    out_sds = jax.ShapeDtypeStruct((b, seq_l, dilation * w), F32)
    o, lse = pl.pallas_call(
        functools.partial(_dil_kernel, dilation=dilation, span=span,
                          slopes=ALIBI_SLOPES[2 * g:2 * g + 2], tl=tl),
        grid=(b, dilation, seq_l // tl),
        in_specs=[cur(qc), cur(kc), prev(kc), cur(vc), prev(vc)],
        out_specs=[out_spec, out_spec],
        out_shape=[out_sds, out_sds],
        compiler_params=_cparams(("parallel", "parallel", "parallel")),
        name=f"dilated_g{g}",
    )(hv, hv, hv, hv, hv)
    return o.reshape(b, s, w), lse.reshape(b, s, w)


def _nsa_cmp_kernel(x_ref, pos_ref, wa_ref, wb_ref, w2k_ref, w2vt_ref, kc_ref, vct_ref):
    x = x_ref[0]
    xa = (x + pos_ref[0:1, :]).astype(MXU_DTYPE)
    xb = (x + pos_ref[1:2, :]).astype(MXU_DTYPE)
    first = _dot(xa, wa_ref[...])
    second = _dot(xb, wb_ref[...])
    second_next = pltpu.roll(second, x.shape[0] - 1, 0)
    pre = first + second_next
    hid = (pre * jax.nn.sigmoid(pre)).astype(MXU_DTYPE)
    kc_ref[0] = _dot(hid[:, :LANES], w2k_ref[...]).astype(kc_ref.dtype)
    vct_ref[0] = _dot_t(w2vt_ref[...], hid[:, LANES:]).astype(vct_ref.dtype)


def _nsa_compress(aux3, pos2, wa, wb, w2k, w2vt):
    b, s, _ = aux3.shape
    nchunk = s // NSA_CMP_STRIDE
    xk = aux3[:, :, :2 * LANES].reshape(b, nchunk, NSA_CMP_STRIDE * 2 * LANES)
    full = lambda a: pl.BlockSpec(a.shape, lambda bi: (0,) * a.ndim)
    return pl.pallas_call(
        _nsa_cmp_kernel,
        grid=(b,),
        in_specs=[pl.BlockSpec((1, nchunk, xk.shape[2]), lambda bi: (bi, 0, 0)),
                  full(pos2), full(wa), full(wb), full(w2k), full(w2vt)],
        out_specs=[pl.BlockSpec((1, nchunk, LANES), lambda bi: (bi, 0, 0)),
                   pl.BlockSpec((1, LANES, nchunk), lambda bi: (bi, 0, 0))],
        out_shape=[jax.ShapeDtypeStruct((b, nchunk, LANES), MXU_DTYPE),
                   jax.ShapeDtypeStruct((b, LANES, nchunk), MXU_DTYPE)],
        compiler_params=_cparams(("parallel",)),
        name="nsa_cmp",
    )(xk, pos2, wa, wb, w2k, w2vt)


def _prep_nsa_cmp(cmp_pos, phi_k1, phi_k2, phi_v1, phi_v2):
    half = NSA_CMP_LEN // 2
    pk = cmp_pos.reshape(2, half, HEAD_DIM)
    pos2 = jnp.concatenate([pk, pk], axis=-1).reshape(2, half * 2 * LANES)

    def halves(w1, is_v):
        w = w1.reshape(2, half, HEAD_DIM, HEAD_DIM)
        z = jnp.zeros_like(w)
        w = jnp.concatenate([z, w] if is_v else [w, z], axis=2)
        return w.reshape(2, half * 2 * LANES, HEAD_DIM)

    wk, wv = halves(phi_k1, False), halves(phi_v1, True)
    wa = jnp.concatenate([wk[0], wv[0]], axis=1).astype(MXU_DTYPE)
    wb = jnp.concatenate([wk[1], wv[1]], axis=1).astype(MXU_DTYPE)
    return pos2, wa, wb, phi_k2.astype(MXU_DTYPE), phi_v2.T.astype(MXU_DTYPE)


def _split3(x):
    hi = x.astype(MXU_DTYPE)
    r = x - hi.astype(F32)
    mid = r.astype(MXU_DTYPE)
    lo = (r - mid.astype(F32)).astype(MXU_DTYPE)
    return hi, mid, lo


def _pos_features(pos):
    hi = jnp.floor(pos / POS_SPLIT) * POS_SPLIT
    lo = pos - hi
    cols = jnp.stack([hi, hi, hi, lo, lo, lo], axis=1)
    return jnp.pad(cols, ((0, 0), (0, LANES - 6))).astype(MXU_DTYPE)


def _slope_features(tq):
    sig = jnp.asarray([s * LOG2E for s in ALIBI_SLOPES[DIL_HEADS:]], F32)
    pieces = jnp.stack(_split3(sig), axis=1)
    rows = jnp.concatenate([pieces, pieces], axis=1)
    rows = jnp.pad(rows, ((0, 0), (0, LANES - 6)))
    return jnp.repeat(rows, tq, axis=0)


def _masked_softmax_t(s_t, mask):
    s_t = jnp.where(mask, s_t, NEG_INF)
    m = jnp.max(s_t, axis=0, keepdims=True)
    e = jnp.where(mask, jnp.exp2(s_t - m), 0.0)
    den = jnp.sum(e, axis=0, keepdims=True)
    return e * (1.0 / jnp.where(den > 0.0, den, 1.0))


def _nsa_kernel(q_ref, k_ref, vt_ref, kc_ref, vct_ref, c2st_ref, pf_ref, cpf_ref, sf_ref, g_ref,
                o_ref, m_scr, l_scr, acc_scr, bias_scr, any_scr, *, tq, tk, topk):
    i = pl.program_id(1)
    t0 = i * tq
    nh = NSA_HEADS
    cols = nh * tq
    ncp = kc_ref.shape[1]
    ns = c2st_ref.shape[0]
    blocks_per_tile = tk // NSA_SEL_LEN

    q = q_ref[0]
    q4 = jnp.concatenate([q[:, h * LANES:(h + 1) * LANES] for h in range(nh)], axis=0)
    q4 = (q4.astype(F32) * (HEAD_DIM ** -0.5 * LOG2E)).astype(MXU_DTYPE)
    q4 = jnp.concatenate([q4, sf_ref[...]], axis=1)
    col = lax.broadcasted_iota(jnp.int32, (1, cols), 1)
    tpos = (t0 + (col & (tq - 1))).astype(F32)

    cidx = lax.broadcasted_iota(jnp.int32, (ncp, 1), 0).astype(F32)
    c_end = cidx * NSA_CMP_STRIDE + (NSA_CMP_LEN - 1)
    kc = jnp.concatenate([kc_ref[0], cpf_ref[...]], axis=1)
    p_cmp = _masked_softmax_t(_dot_t(kc, q4), c_end <= tpos)
    o_cmp = _dot(vct_ref[0], p_cmp.astype(MXU_DTYPE))

    p_sum = p_cmp[:, 0:tq]
    for h in range(1, nh):
        p_sum = p_sum + p_cmp[:, h * tq:(h + 1) * tq]
    c2st = c2st_ref[...]
    if MXU_DTYPE == jnp.float32:
        score = _dot(c2st, p_sum)
    else:
        score = sum(_dot(c2st, piece) for piece in _split3(p_sum))
    t1 = t0 + lax.broadcasted_iota(jnp.int32, (1, tq), 1)
    cur = _shr(t1, NSA_SEL_LEN).astype(F32)
    jj = lax.broadcasted_iota(jnp.int32, (ns, 1), 0).astype(F32)
    forced = (jj == 0.0) | (jj == cur) | (jj == cur - 1.0)
    score = jnp.where(jj > cur, -1.0, jnp.where(forced, NSA_FORCED_SCORE, score))
    bias = jnp.full((ns, tq), NEG_INF, F32)
    for _ in range(topk):
        best = jnp.max(score, axis=0, keepdims=True)
        idx = jnp.min(jnp.where(score == best, jj, float(ns)), axis=0, keepdims=True)
        hit = jj == idx
        bias = jnp.where(hit, 0.0, bias)
        score = jnp.where(hit, -2.0, score)
    bias_scr[...] = bias
    any_scr[...] = jnp.broadcast_to(jnp.max(bias, axis=1, keepdims=True), (ns, LANES))

    _init_softmax_state(m_scr, l_scr, acc_scr)

    def slc_scores(k0, kt):
        keys = jnp.concatenate([k_ref[0, pl.ds(k0, tk), 0:LANES], pf_ref[pl.ds(k0, tk), :]], axis=1)
        s_t = _dot_t(keys, q4)
        b0 = pl.multiple_of(kt * blocks_per_tile, blocks_per_tile)
        rows = [jnp.broadcast_to(bias_scr[pl.ds(b0 + r, 1), :], (NSA_SEL_LEN, tq))
                for r in range(blocks_per_tile)]
        sel_bias = jnp.concatenate(rows, axis=0)
        return s_t + jnp.concatenate([sel_bias] * nh, axis=1)

    def slc_tile(kt, carry):
        b0 = pl.multiple_of(kt * blocks_per_tile, blocks_per_tile)
        touched = jnp.max(any_scr[pl.ds(b0, blocks_per_tile), :]) == 0.0

        @pl.when(touched)
        def _():
            k0 = pl.multiple_of(kt * tk, tk)
            _softmax_step_t(slc_scores(k0, kt), vt_ref[0:LANES, pl.ds(k0, tk)], m_scr, l_scr, acc_scr)
        return carry

    last = t0 // tk
    lax.fori_loop(0, last, slc_tile, 0)
    k0 = pl.multiple_of(last * tk, tk)
    kpos = (k0 + lax.broadcasted_iota(jnp.int32, (tk, 1), 0)).astype(F32)
    s_t = jnp.where(kpos <= tpos, slc_scores(k0, last), NEG_INF)
    _softmax_step_t(s_t, vt_ref[0:LANES, pl.ds(k0, tk)], m_scr, l_scr, acc_scr)
    o_slc = acc_scr[...] * (1.0 / l_scr[...])

    wlen = NSA_WINDOW + tq
    ws = pl.multiple_of(jnp.maximum(t0 - NSA_WINDOW, 0), tq)
    keys = jnp.concatenate([k_ref[0, pl.ds(ws, wlen), LANES:2 * LANES], pf_ref[pl.ds(ws, wlen), :]], axis=1)
    wpos = (ws + lax.broadcasted_iota(jnp.int32, (wlen, 1), 0)).astype(F32)
    dist = tpos - wpos
    p_win = _masked_softmax_t(_dot_t(keys, q4), (dist >= 0.0) & (dist < float(NSA_WINDOW)))
    o_win = _dot(vt_ref[LANES:2 * LANES, pl.ds(ws, wlen)], p_win.astype(MXU_DTYPE))

    gates = jax.nn.sigmoid(g_ref[0]).T
    for h in range(nh):
        c = slice(h * tq, (h + 1) * tq)
        o = (gates[3 * h:3 * h + 1, :] * o_cmp[:, c] + gates[3 * h + 1:3 * h + 2, :] * o_slc[:, c]
             + gates[3 * h + 2:3 * h + 3, :] * o_win[:, c])
        o_ref[0, :, h * LANES:(h + 1) * LANES] = o.T.astype(o_ref.dtype)


def _nsa(h3, aux3, vt, kc, vct, *, tq=NSA_Q_BLOCK, tk=512):
    b, s, _ = h3.shape
    ncp = kc.shape[1]
    ns = s // NSA_SEL_LEN
    tk = min(tk, s)
    w = NSA_HEADS * HEAD_DIM
    c2st = _cmp_to_sel_t(ncp, ns)
    pf = _pos_features(jnp.arange(s, dtype=F32))
    cpf = _pos_features(jnp.arange(ncp, dtype=F32) * NSA_CMP_STRIDE + 0.5 * (NSA_CMP_LEN - 1))
    sf = _slope_features(tq).astype(MXU_DTYPE)
    const = lambda a: _resident(a.shape, lambda bi, i: (0,) * a.ndim)
    return pl.pallas_call(
        functools.partial(_nsa_kernel, tq=tq, tk=tk, topk=min(NSA_TOPK, ns)),
        grid=(b, s // tq),
        in_specs=[
            pl.BlockSpec((1, tq, w), lambda bi, i: (bi, i, COL_NSA_Q // w)),
            _resident((1, s, 2 * LANES), lambda bi, i: (bi, 0, COL_NSA_K // (2 * LANES))),
            _resident((2 * LANES, s), lambda bi, i: (0, bi)),
            _resident((1, ncp, LANES), lambda bi, i: (bi, 0, 0)),
            _resident((1, LANES, ncp), lambda bi, i: (bi, 0, 0)),
            const(c2st), const(pf), const(cpf), const(sf),
            pl.BlockSpec((1, tq, LANES), lambda bi, i: (bi, i, 2)),
        ],
        out_specs=pl.BlockSpec((1, tq, w), lambda bi, i: (bi, i, 0)),
        out_shape=jax.ShapeDtypeStruct((b, s, w), MXU_DTYPE),
        scratch_shapes=[pltpu.VMEM((1, NSA_HEADS * tq), F32), pltpu.VMEM((1, NSA_HEADS * tq), F32),
                        pltpu.VMEM((HEAD_DIM, NSA_HEADS * tq), F32),
                        pltpu.VMEM((ns, tq), F32), pltpu.VMEM((ns, LANES), F32)],
        compiler_params=_cparams(("parallel", "arbitrary")),
        name="nsa",
    )(h3, h3, vt, kc, vct, c2st, pf, cpf, sf, aux3)


def _cmp_to_sel_t(ncp, ns):
    nc = ncp - 1
    c = np.arange(ncp)[None, :]
    j = np.arange(ns)[:, None]
    start = c * NSA_CMP_STRIDE
    m = (start < (j + 1) * NSA_SEL_LEN) & (start + NSA_CMP_LEN - 1 >= j * NSA_SEL_LEN) & (c < nc)
    return jnp.asarray(m.astype(np.float32)).astype(MXU_DTYPE)


def _mixout_kernel(x_ref, mla_ref, d0_ref, d1_ref, d2_ref, l0_ref, l1_ref, l2_ref, nsa_ref, w_ref, o_ref):
    o_dil = (d0_ref[...], d1_ref[...], d2_ref[...])
    lse = (l0_ref[...], l1_ref[...], l2_ref[...])
    top = jnp.maximum(jnp.maximum(lse[0], lse[1]), lse[2])
    e = [jnp.exp(l - top) for l in lse]
    inv = 1.0 / (e[0] + e[1] + e[2])
    acc = x_ref[...] + _dot(mla_ref[...], w_ref[0:768, :])
    for g in range(3):
        mixed = (o_dil[g] * (e[g] * inv)).astype(MXU_DTYPE)
        acc += _dot(mixed, w_ref[768 + g * 256:768 + (g + 1) * 256, :])
    acc += _dot(nsa_ref[...], w_ref[1536:2048, :])
    o_ref[...] = acc


def _mixout(x2, o_mla, o_dil, lse_dil, o_nsa, w, *, tm=512):
    n, d = x2.shape
    row = lambda width: pl.BlockSpec((tm, width), lambda i: (i, 0))
    return pl.pallas_call(
        _mixout_kernel,
        grid=(n // tm,),
        in_specs=[row(d), row(o_mla.shape[1])] + [row(256)] * 6 + [row(o_nsa.shape[1]),
                  pl.BlockSpec(w.shape, lambda i: (0, 0))],
        out_specs=row(d),
        out_shape=jax.ShapeDtypeStruct((n, d), F32),
        compiler_params=_cparams(("parallel",)),
        name="mix_out",
    )(x2, o_mla, *o_dil, *lse_dil, o_nsa, w)


def kernel(x, ffn1_norm, ffn1_w_in, ffn1_w_out, mix_norm, w_mix_in, mla_q_norm, mla_w_uq, mla_kv_norm,
           mla_w_ukv, nsa_cmp_pos, nsa_phi_k1, nsa_phi_k2, nsa_phi_v1, nsa_phi_v2, w_mix_out, ffn2_norm,
           ffn2_w_in, ffn2_w_out, final_norm):
    b, s, d = x.shape
    depth = ffn1_w_in.shape[0]
    n = b * s
    invf = _rope_inv_freq_row()
    x2 = x.reshape(n, d)
    for l in range(depth):
        x2 = _ffn(x2, ffn1_norm[l], *_prep_ffn(ffn1_w_in[l], ffn1_w_out[l]))

        h2, aux2, nsa_vt = _mixin(x2, mix_norm[l], *_prep_mixin(w_mix_in[l]))
        h3 = h2.reshape(b, s, H_COLS)
        aux3 = aux2.reshape(b, s, AUX_COLS)

        q, k, vt = _mla_proj(h2, mla_q_norm[l].reshape(1, -1), mla_kv_norm[l].reshape(1, -1),
                             *_prep_mla(mla_w_uq[l], mla_w_ukv[l]), invf, seq=s)
        o_mla = _mla_flash(q.reshape(b, s, -1), k.reshape(b, s, -1), vt)

        dil = [_dilated_group(h3, g) for g in range(len(DIL_PAIRS))]

        kc, vct = _nsa_compress(aux3, *_prep_nsa_cmp(nsa_cmp_pos[l], nsa_phi_k1[l], nsa_phi_k2[l],
                                                      nsa_phi_v1[l], nsa_phi_v2[l]))
        o_nsa = _nsa(h3, aux3, nsa_vt, kc, vct)

        x2 = _mixout(x2, o_mla.reshape(n, -1), [o.reshape(n, -1) for o, _ in dil],
                     [e.reshape(n, -1) for _, e in dil], o_nsa.reshape(n, -1),
                     w_mix_out[l].astype(MXU_DTYPE))

        gf = final_norm if l == depth - 1 else None
        x2 = _ffn(x2, ffn2_norm[l], *_prep_ffn(ffn2_w_in[l], ffn2_w_out[l]), gf)
    return x2.reshape(b, s, d)
```

```python
import functools
import math

import numpy as np
import jax
import jax.numpy as jnp
from jax import lax
from jax.experimental import pallas as pl
from jax.experimental.pallas import tpu as pltpu

F32 = jnp.float32
MXU_DTYPE = jnp.bfloat16
VMEM_LIMIT_BYTES = 56 * 1024 * 1024
LANES = 128
LOG2E = math.log2(math.e)
FFN_TILE = 512

HEAD_DIM = 128
RMS_EPS = 1e-6
NEG_INF = -1e30

MLA_HEADS = 6
MLA_LORA = 512
MLA_NOPE = 128
MLA_ROPE = 64
MLA_V = 128
MLA_QK_PAD = 256
ROPE_BASE = 10000.0

DIL_PAIRS = ((128, 1), (512, 4), (2048, 16))
DIL_HEADS = 6
DIL_BLOCK = 128

NSA_HEADS = 4
NSA_CMP_LEN = 32
NSA_CMP_STRIDE = 16
NSA_SEL_LEN = 64
NSA_TOPK = 16
NSA_WINDOW = 512
NSA_FORCED_SCORE = 100.0
NSA_Q_BLOCK = 256
POS_SPLIT = 128

N_ALIBI = DIL_HEADS + NSA_HEADS
ALIBI_SLOPES = tuple(float(2.0 ** (-8.0 * i / N_ALIBI)) for i in range(1, N_ALIBI + 1))

COL_QLAT = 0
COL_KVLAT = 512
COL_NSA_Q = 1024
COL_KROPE = 1536
COL_NSA_K = 1792
H_COLS = 2048
DIL_COLS = 2304
AUX_COLS = 384


def _cparams(sem):
    return pltpu.CompilerParams(dimension_semantics=sem, vmem_limit_bytes=VMEM_LIMIT_BYTES)


def _rms(x, g):
    ms = jnp.mean(x * x, axis=-1, keepdims=True)
    return (x * lax.rsqrt(ms + RMS_EPS)) * g


def _dot(a, b):
    return jnp.dot(a, b, preferred_element_type=F32)


def _shr(x, pow2):
    return lax.shift_right_logical(x, int(pow2).bit_length() - 1)


def _dot_t(a, b):
    return lax.dot_general(a, b, (((1,), (1,)), ((), ())), preferred_element_type=F32)


def _resident(shape, index_map):
    return pl.BlockSpec(shape, index_map, pipeline_mode=pl.Buffered(1))


def _ffn_kernel(x_ref, g_ref, wg_ref, wu_ref, wo_ref, *rest, final):
    if final:
        gf_ref, o_ref, xn_ref = rest
    else:
        o_ref, xn_ref = rest
    j = pl.program_id(1)

    @pl.when(j == 0)
    def _():
        x = x_ref[...]
        xn_ref[...] = _rms(x, g_ref[...]).astype(xn_ref.dtype)
        o_ref[...] = x

    xn = xn_ref[...]
    gate = _dot(xn, wg_ref[...])
    up = _dot(xn, wu_ref[...])
    h = (0.5 * gate) * jax.nn.sigmoid(gate) * up
    o_ref[...] += _dot(h.astype(MXU_DTYPE), wo_ref[...])

    if final:
        @pl.when(j == pl.num_programs(1) - 1)
        def _():
            o_ref[...] = _rms(o_ref[...], gf_ref[...])


def _ffn(x2, g, w_in, wo, gf=None, *, tm=1024, tf=FFN_TILE):
    n, d = x2.shape
    dffp = wo.shape[0]
    nff = dffp // tf
    tm = min(tm, n)
    final = gf is not None
    in_specs = [
        _resident((tm, d), lambda i, j: (i, 0)),
        pl.BlockSpec((1, d), lambda i, j: (0, 0)),
        pl.BlockSpec((d, tf), lambda i, j: (0, j)),
        pl.BlockSpec((d, tf), lambda i, j: (0, nff + j)),
        pl.BlockSpec((tf, d), lambda i, j: (j, 0)),
    ]
    args = [x2, g.reshape(1, d), w_in, w_in, wo]
    if final:
        in_specs.append(pl.BlockSpec((1, d), lambda i, j: (0, 0)))
        args.append(gf.reshape(1, d))
    return pl.pallas_call(
        functools.partial(_ffn_kernel, final=final),
        grid=(n // tm, nff),
        in_specs=in_specs,
        out_specs=pl.BlockSpec((tm, d), lambda i, j: (i, 0)),
        out_shape=jax.ShapeDtypeStruct((n, d), F32),
        scratch_shapes=[pltpu.VMEM((tm, d), MXU_DTYPE)],
        compiler_params=_cparams(("parallel", "arbitrary")),
        name="ffn_final" if final else "ffn",
    )(*args)


def _prep_ffn(w_in, w_out, tf=FFN_TILE):
    d, two_dff = w_in.shape
    dff = two_dff // 2
    dffp = -(-dff // tf) * tf
    w = jnp.pad(w_in.reshape(d, 2, dff), ((0, 0), (0, 0), (0, dffp - dff)))
    w = w.reshape(d, 2 * dffp).astype(MXU_DTYPE)
    wo = jnp.pad(w_out, ((0, dffp - dff), (0, 0))).astype(MXU_DTYPE)
    return w, wo


def _mixin_kernel(x_ref, g_ref, wm_ref, wd_ref, wa_ref, wvt_ref, hm_ref, hd_ref, a_ref, vt_ref):
    xn = _rms(x_ref[...], g_ref[...]).astype(MXU_DTYPE)
    hm_ref[...] = _dot(xn, wm_ref[...]).astype(hm_ref.dtype)
    hd_ref[...] = _dot(xn, wd_ref[...]).astype(hd_ref.dtype)
    a_ref[...] = _dot(xn, wa_ref[...])
    vt_ref[...] = _dot_t(wvt_ref[...], xn).astype(vt_ref.dtype)


def _mixin(x2, g, wm, wd, wa, wvt, *, tm=512):
    n, d = x2.shape
    weight = lambda a: _resident(a.shape, lambda i: (0, 0))
    return pl.pallas_call(
        _mixin_kernel,
        grid=(n // tm,),
        in_specs=[
            pl.BlockSpec((tm, d), lambda i: (i, 0)),
            pl.BlockSpec((1, d), lambda i: (0, 0)),
            weight(wm), weight(wd), weight(wa), weight(wvt),
        ],
        out_specs=[
            pl.BlockSpec((tm, H_COLS), lambda i: (i, 0)),
            pl.BlockSpec((tm, DIL_COLS), lambda i: (i, 0)),
            pl.BlockSpec((tm, AUX_COLS), lambda i: (i, 0)),
            pl.BlockSpec((2 * LANES, tm), lambda i: (0, i)),
        ],
        out_shape=[
            jax.ShapeDtypeStruct((n, H_COLS), MXU_DTYPE),
            jax.ShapeDtypeStruct((n, DIL_COLS), MXU_DTYPE),
            jax.ShapeDtypeStruct((n, AUX_COLS), F32),
            jax.ShapeDtypeStruct((2 * LANES, n), MXU_DTYPE),
        ],
        compiler_params=_cparams(("parallel",)),
        name="mix_in",
    )(x2, g.reshape(1, d), wm, wd, wa, wvt)


def _rot_half_cols(w):
    half = w.shape[1] // 2
    return jnp.concatenate([-w[:, half:], w[:, :half]], axis=1)


def _prep_mixin(w):
    d = w.shape[0]
    o = 0
    parts = {}
    for name, width in (("q_lat", 512), ("kv_lat", 512), ("k_rope", 64), ("dq", 768), ("dk", 768),
                        ("dv", 768), ("nq", 512), ("nkc", 128), ("nvc", 128), ("nks", 128),
                        ("nvs", 128), ("nkw", 128), ("nvw", 128), ("ng", 12)):
        parts[name] = w[:, o:o + width]
        o += width
    z64 = jnp.zeros((d, 64), w.dtype)
    kr = parts["k_rope"]
    wm = jnp.concatenate([
        parts["q_lat"], parts["kv_lat"], parts["nq"],
        kr, z64, _rot_half_cols(kr), z64,
        parts["nks"], parts["nkw"]], axis=1).astype(MXU_DTYPE)
    wd = jnp.concatenate([parts["dq"], parts["dk"], parts["dv"]], axis=1).astype(MXU_DTYPE)
    wa = jnp.concatenate([parts["nkc"], parts["nvc"], parts["ng"],
                          jnp.zeros((d, LANES - 12), w.dtype)], axis=1).astype(MXU_DTYPE)
    wvt = jnp.concatenate([parts["nvs"], parts["nvw"]], axis=1).T.astype(MXU_DTYPE)
    return wm, wd, wa, wvt


def _mla_proj_kernel(lat_ref, kr_ref, qn_ref, kvn_ref, wq_ref, wk_ref, wvt_ref, invf_ref,
                     q_ref, k_ref, vt_ref, *, seq, tm):
    i = pl.program_id(0)
    pos0 = lax.rem(i * tm, seq)
    pos = (pos0 + lax.broadcasted_iota(jnp.int32, (tm, 1), 0)).astype(F32)
    ang = pos * invf_ref[...]
    cos = jnp.cos(ang)
    sin = jnp.sin(ang)
    scale = (MLA_NOPE + MLA_ROPE) ** -0.5 * LOG2E

    lat = lat_ref[...].astype(F32)
    qn = _rms(lat[:, :MLA_LORA], qn_ref[...]).astype(MXU_DTYPE)
    kvn = _rms(lat[:, MLA_LORA:], kvn_ref[...]).astype(MXU_DTYPE)
    qm = _dot(qn, wq_ref[...])
    rot0 = MLA_HEADS * MLA_QK_PAD
    for h in range(MLA_HEADS):
        c = h * MLA_QK_PAD
        nope = qm[:, c:c + LANES]
        pe = qm[:, c + LANES:c + 2 * LANES] * cos + qm[:, rot0 + h * LANES:rot0 + (h + 1) * LANES] * sin
        q_ref[:, c:c + LANES] = (nope * scale).astype(q_ref.dtype)
        q_ref[:, c + LANES:c + 2 * LANES] = (pe * scale).astype(q_ref.dtype)

    kr = kr_ref[...].astype(F32)
    kpe = (kr[:, :LANES] * cos + kr[:, LANES:] * sin).astype(k_ref.dtype)
    kn = _dot(kvn, wk_ref[...])
    for h in range(MLA_HEADS):
        c = h * MLA_QK_PAD
        k_ref[:, c:c + LANES] = kn[:, h * LANES:(h + 1) * LANES].astype(k_ref.dtype)
        k_ref[:, c + LANES:c + 2 * LANES] = kpe
    vt_ref[...] = _dot_t(wvt_ref[...], kvn).astype(vt_ref.dtype)


def _mla_proj(h2, qn, kvn, wq, wk, wvt, invf, *, seq, tm=512):
    n = h2.shape[0]
    qk_cols = MLA_HEADS * MLA_QK_PAD
    v_rows = MLA_HEADS * MLA_V
    full = lambda a: pl.BlockSpec(a.shape, lambda i: (0,) * a.ndim)
    return pl.pallas_call(
        functools.partial(_mla_proj_kernel, seq=seq, tm=tm),
        grid=(n // tm,),
        in_specs=[
            pl.BlockSpec((tm, 2 * MLA_LORA), lambda i: (i, 0)),
            pl.BlockSpec((tm, 2 * LANES), lambda i: (i, COL_KROPE // (2 * LANES))),
            full(qn), full(kvn), full(wq), full(wk), full(wvt), full(invf),
        ],
        out_specs=[
            pl.BlockSpec((tm, qk_cols), lambda i: (i, 0)),
            pl.BlockSpec((tm, qk_cols), lambda i: (i, 0)),
            pl.BlockSpec((v_rows, tm), lambda i: (0, i)),
        ],
        out_shape=[
            jax.ShapeDtypeStruct((n, qk_cols), MXU_DTYPE),
            jax.ShapeDtypeStruct((n, qk_cols), MXU_DTYPE),
            jax.ShapeDtypeStruct((v_rows, n), MXU_DTYPE),
        ],
        compiler_params=_cparams(("parallel",)),
        name="mla_proj",
    )(h2, h2, qn, kvn, wq, wk, wvt, invf)


def _prep_mla(w_uq, w_ukv):
    r = MLA_LORA
    wq3 = w_uq.reshape(r, MLA_HEADS, MLA_NOPE + MLA_ROPE)
    nope, pe = wq3[..., :MLA_NOPE], wq3[..., MLA_NOPE:]
    z = jnp.zeros((r, MLA_HEADS, 64), w_uq.dtype)
    main = jnp.concatenate([nope, pe, z], axis=-1).reshape(r, MLA_HEADS * MLA_QK_PAD)
    half = MLA_ROPE // 2
    pe_rot = jnp.concatenate([-pe[..., half:], pe[..., :half]], axis=-1)
    rot = jnp.concatenate([pe_rot, z], axis=-1).reshape(r, MLA_HEADS * LANES)
    wq = jnp.concatenate([main, rot], axis=1).astype(MXU_DTYPE)
    wkv3 = w_ukv.reshape(r, MLA_HEADS, MLA_NOPE + MLA_V)
    wk = wkv3[..., :MLA_NOPE].reshape(r, MLA_HEADS * MLA_NOPE).astype(MXU_DTYPE)
    wvt = wkv3[..., MLA_NOPE:].reshape(r, MLA_HEADS * MLA_V).T.astype(MXU_DTYPE)
    return wq, wk, wvt


def _rope_inv_freq_row():
    half = MLA_ROPE // 2
    f = ROPE_BASE ** (-jnp.arange(half, dtype=F32) / half)
    return jnp.concatenate([f, f, jnp.zeros((LANES - MLA_ROPE,), F32)]).reshape(1, LANES)


def _softmax_step_t(s_t, v_t, m_ref, l_ref, acc_ref):
    m_prev = m_ref[...]
    m_new = jnp.maximum(m_prev, jnp.max(s_t, axis=0, keepdims=True))
    e = jnp.exp2(s_t - m_new)
    alpha = jnp.exp2(m_prev - m_new)
    l_ref[...] = alpha * l_ref[...] + jnp.sum(e, axis=0, keepdims=True)
    acc_ref[...] = alpha * acc_ref[...] + _dot(v_t, e.astype(MXU_DTYPE))
    m_ref[...] = m_new


def _init_softmax_state(m_scr, l_scr, acc_scr):
    m_scr[...] = jnp.full(m_scr.shape, NEG_INF, F32)
    l_scr[...] = jnp.zeros(l_scr.shape, F32)
    acc_scr[...] = jnp.zeros(acc_scr.shape, F32)


def _flash_kernel(qt_ref, kt_ref, q_ref, k_ref, vt_ref, o_ref, m_scr, l_scr, acc_scr, *, t, ksub, qsub):
    step = pl.program_id(2)
    qi = qt_ref[step]
    ki = kt_ref[step]

    @pl.when(ki == 0)
    def _():
        _init_softmax_state(m_scr, l_scr, acc_scr)

    def piece(kp, qp, diagonal):
        k0, q0 = kp * ksub, qp * qsub
        ks = slice(k0, k0 + ksub)
        qs = slice(q0, q0 + qsub)
        s_t = _dot_t(k_ref[0, ks, :], q_ref[0, qs, :])
        if diagonal and k0 + ksub - 1 > q0:
            kpos = k0 + lax.broadcasted_iota(jnp.int32, (ksub, 1), 0)
            qpos = q0 + lax.broadcasted_iota(jnp.int32, (1, qsub), 1)
            s_t = jnp.where(kpos <= qpos, s_t, NEG_INF)
        _softmax_step_t(s_t, vt_ref[:, ks], m_scr.at[:, qs], l_scr.at[:, qs], acc_scr.at[:, qs])

    @pl.when(ki < qi)
    def _():
        for kp in range(t // ksub):
            for qp in range(t // qsub):
                piece(kp, qp, False)

    @pl.when(ki == qi)
    def _():
        for kp in range(t // ksub):
            for qp in range(t // qsub):
                if kp * ksub <= (qp + 1) * qsub - 1:
                    piece(kp, qp, True)
        o_t = acc_scr[...] * (1.0 / l_scr[...])
        o_ref[0] = o_t.T.astype(o_ref.dtype)


def _mla_flash(q3, k3, vt, *, t=2048, ksub=1024, qsub=1024):
    b, s, _ = q3.shape
    t = min(t, s)
    ksub, qsub = min(ksub, t), min(qsub, t)
    nq = s // t
    pairs = [(qi, ki) for qi in range(nq) for ki in range(qi + 1)]
    qt = jnp.asarray(np.array([p[0] for p in pairs], np.int32))
    kt = jnp.asarray(np.array([p[1] for p in pairs], np.int32))
    grid_spec = pltpu.PrefetchScalarGridSpec(
        num_scalar_prefetch=2,
        grid=(b, MLA_HEADS, len(pairs)),
        in_specs=[
            pl.BlockSpec((1, t, MLA_QK_PAD), lambda bi, h, st, qt, kt: (bi, qt[st], h)),
            pl.BlockSpec((1, t, MLA_QK_PAD), lambda bi, h, st, qt, kt: (bi, kt[st], h)),
            pl.BlockSpec((MLA_V, t), lambda bi, h, st, qt, kt: (h, bi * nq + kt[st])),
        ],
        out_specs=pl.BlockSpec((1, t, MLA_V), lambda bi, h, st, qt, kt: (bi, qt[st], h)),
        scratch_shapes=[pltpu.VMEM((1, t), F32), pltpu.VMEM((1, t), F32), pltpu.VMEM((MLA_V, t), F32)],
    )
    return pl.pallas_call(
        functools.partial(_flash_kernel, t=t, ksub=ksub, qsub=qsub),
        grid_spec=grid_spec,
        out_shape=jax.ShapeDtypeStruct((b, s, MLA_HEADS * MLA_V), MXU_DTYPE),
        compiler_params=_cparams(("parallel", "parallel", "arbitrary")),
        name="mla_flash",
    )(qt, kt, q3, k3, vt)


def _dil_kernel(q_ref, kc_ref, kp_ref, vc_ref, vp_ref, o_ref, lse_ref, *, dilation, span, slopes, tl):
    n = pl.program_id(2)
    scale = HEAD_DIM ** -0.5
    blk = DIL_BLOCK
    a = lax.broadcasted_iota(jnp.int32, (blk, 1), 0)
    c = lax.broadcasted_iota(jnp.int32, (1, 2 * blk), 1)
    j = blk + a - c
    in_band = (j >= 0) & (j <= span)
    first_valid = in_band & ((c >= blk) | (n > 0))
    dist = (j * dilation).astype(F32)
    for hg in range(2):
        cols = slice(hg * LANES, (hg + 1) * LANES)
        bias = slopes[hg] * dist
        for sb in range(tl // blk):
            rows = slice(sb * blk, (sb + 1) * blk)
            q = (q_ref[0, rows, cols].astype(F32) * scale).astype(MXU_DTYPE)
            if sb == 0:
                kprev, vprev, valid = kp_ref[0, :, cols], vp_ref[0, :, cols], first_valid
            else:
                prev = slice((sb - 1) * blk, sb * blk)
                kprev, vprev, valid = kc_ref[0, prev, cols], vc_ref[0, prev, cols], in_band
            keys = jnp.concatenate([kprev, kc_ref[0, rows, cols]], axis=0)
            vals = jnp.concatenate([vprev, vc_ref[0, rows, cols]], axis=0)
            s = jnp.where(valid, _dot_t(q, keys) - bias, NEG_INF)
            m = jnp.max(s, axis=1, keepdims=True)
            e = jnp.where(valid, jnp.exp(s - m), 0.0)
            den = jnp.sum(e, axis=1, keepdims=True)
            o_ref[0, rows, cols] = _dot((e / den).astype(MXU_DTYPE), vals)
            lse_ref[0, rows, cols] = jnp.broadcast_to(m + jnp.log(den), (blk, LANES))


def _dilated_group(hd3, g, *, tl=512):
    b, s, _ = hd3.shape
    window, dilation = DIL_PAIRS[g]
    span = window // dilation
    seq_l = s // dilation
    tl = min(tl, seq_l)
    per_blk = tl // DIL_BLOCK
    w = 2 * LANES
    group_cols = [t * DIL_HEADS * HEAD_DIM + g * w for t in range(3)]
    if dilation == 1:
        hv = hd3
        qc, kc, vc = (c // w for c in group_cols)
    else:
        hv = jnp.concatenate([hd3[:, :, c:c + w] for c in group_cols], axis=2)
        hv = hv.reshape(b, seq_l, dilation * 3 * w)
        qc, kc, vc = 0, 1, 2
    stride = hv.shape[2] // (dilation * w)
    cur = lambda col: pl.BlockSpec((1, tl, w), lambda bi, r, n: (bi, n, r * stride + col))
    prev = lambda col: pl.BlockSpec(
        (1, DIL_BLOCK, w), lambda bi, r, n: (bi, jnp.maximum(n * per_blk - 1, 0), r * stride + col))
    out_spec = pl.BlockSpec((1, tl, w), lambda bi, r, n: (bi, n, r))
    out_sds = jax.ShapeDtypeStruct((b, seq_l, dilation * w), F32)
    o, lse = pl.pallas_call(
        functools.partial(_dil_kernel, dilation=dilation, span=span,
                          slopes=ALIBI_SLOPES[2 * g:2 * g + 2], tl=tl),
        grid=(b, dilation, seq_l // tl),
        in_specs=[cur(qc), cur(kc), prev(kc), cur(vc), prev(vc)],
        out_specs=[out_spec, out_spec],
        out_shape=[out_sds, out_sds],
        compiler_params=_cparams(("parallel", "parallel", "parallel")),
        name=f"dilated_g{g}",
    )(hv, hv, hv, hv, hv)
    return o.reshape(b, s, w), lse.reshape(b, s, w)


def _nsa_cmp_kernel(x_ref, pos_ref, wa_ref, wb_ref, w2k_ref, w2vt_ref, kc_ref, vct_ref):
    x = x_ref[0]
    xa = (x + pos_ref[0:1, :]).astype(MXU_DTYPE)
    xb = (x + pos_ref[1:2, :]).astype(MXU_DTYPE)
    first = _dot(xa, wa_ref[...])
    second = _dot(xb, wb_ref[...])
    second_next = pltpu.roll(second, x.shape[0] - 1, 0)
    pre = first + second_next
    hid = (pre * jax.nn.sigmoid(pre)).astype(MXU_DTYPE)
    kc_ref[0] = _dot(hid[:, :LANES], w2k_ref[...]).astype(kc_ref.dtype)
    vct_ref[0] = _dot_t(w2vt_ref[...], hid[:, LANES:]).astype(vct_ref.dtype)


def _nsa_compress(aux3, pos2, wa, wb, w2k, w2vt):
    b, s, _ = aux3.shape
    nchunk = s // NSA_CMP_STRIDE
    xk = aux3[:, :, :2 * LANES].reshape(b, nchunk, NSA_CMP_STRIDE * 2 * LANES)
    full = lambda a: pl.BlockSpec(a.shape, lambda bi: (0,) * a.ndim)
    return pl.pallas_call(
        _nsa_cmp_kernel,
        grid=(b,),
        in_specs=[pl.BlockSpec((1, nchunk, xk.shape[2]), lambda bi: (bi, 0, 0)),
                  full(pos2), full(wa), full(wb), full(w2k), full(w2vt)],
        out_specs=[pl.BlockSpec((1, nchunk, LANES), lambda bi: (bi, 0, 0)),
                   pl.BlockSpec((1, LANES, nchunk), lambda bi: (bi, 0, 0))],
        out_shape=[jax.ShapeDtypeStruct((b, nchunk, LANES), MXU_DTYPE),
                   jax.ShapeDtypeStruct((b, LANES, nchunk), MXU_DTYPE)],
        compiler_params=_cparams(("parallel",)),
        name="nsa_cmp",
    )(xk, pos2, wa, wb, w2k, w2vt)


def _prep_nsa_cmp(cmp_pos, phi_k1, phi_k2, phi_v1, phi_v2):
    half = NSA_CMP_LEN // 2
    pk = cmp_pos.reshape(2, half, HEAD_DIM)
    pos2 = jnp.concatenate([pk, pk], axis=-1).reshape(2, half * 2 * LANES)

    def halves(w1, is_v):
        w = w1.reshape(2, half, HEAD_DIM, HEAD_DIM)
        z = jnp.zeros_like(w)
        w = jnp.concatenate([z, w] if is_v else [w, z], axis=2)
        return w.reshape(2, half * 2 * LANES, HEAD_DIM)

    wk, wv = halves(phi_k1, False), halves(phi_v1, True)
    wa = jnp.concatenate([wk[0], wv[0]], axis=1).astype(MXU_DTYPE)
    wb = jnp.concatenate([wk[1], wv[1]], axis=1).astype(MXU_DTYPE)
    return pos2, wa, wb, phi_k2.astype(MXU_DTYPE), phi_v2.T.astype(MXU_DTYPE)


def _split3(x):
    hi = x.astype(MXU_DTYPE)
    r = x - hi.astype(F32)
    mid = r.astype(MXU_DTYPE)
    lo = (r - mid.astype(F32)).astype(MXU_DTYPE)
    return hi, mid, lo


def _pos_features(pos):
    hi = jnp.floor(pos / POS_SPLIT) * POS_SPLIT
    lo = pos - hi
    cols = jnp.stack([hi, hi, hi, lo, lo, lo], axis=1)
    return jnp.pad(cols, ((0, 0), (0, LANES - 6))).astype(MXU_DTYPE)


def _slope_features(tq):
    sig = jnp.asarray([s * LOG2E for s in ALIBI_SLOPES[DIL_HEADS:]], F32)
    pieces = jnp.stack(_split3(sig), axis=1)
    rows = jnp.concatenate([pieces, pieces], axis=1)
    rows = jnp.pad(rows, ((0, 0), (0, LANES - 6)))
    return jnp.repeat(rows, tq, axis=0)


def _masked_softmax_t(s_t, mask):
    s_t = jnp.where(mask, s_t, NEG_INF)
    m = jnp.max(s_t, axis=0, keepdims=True)
    e = jnp.exp2(s_t - m)
    den = jnp.sum(e, axis=0, keepdims=True)
    return e * jnp.where(m > 0.5 * NEG_INF, 1.0 / den, 0.0)


def _nsa_kernel(q_ref, k_ref, vt_ref, kc_ref, vct_ref, c2st_ref, pf_ref, cpf_ref, sf_ref, g_ref,
                o_ref, m_scr, l_scr, acc_scr, bias_scr, any_scr, ocmp_scr, score_scr,
                *, tq, tk, topk, cmp_chunk):
    i = pl.program_id(1)
    t0 = i * tq
    nh = NSA_HEADS
    cols = nh * tq
    ncp = kc_ref.shape[1]
    ns = c2st_ref.shape[0]
    blocks_per_tile = tk // NSA_SEL_LEN

    q = q_ref[0]
    q4 = jnp.concatenate([q[:, h * LANES:(h + 1) * LANES] for h in range(nh)], axis=0)
    q4 = (q4.astype(F32) * (HEAD_DIM ** -0.5 * LOG2E)).astype(MXU_DTYPE)
    q4 = jnp.concatenate([q4, sf_ref[...]], axis=1)
    col = lax.broadcasted_iota(jnp.int32, (1, cols), 1)
    tpos = (t0 + (col & (tq - 1))).astype(F32)

    def cmp_part(rows):
        cidx = lax.broadcasted_iota(jnp.int32, (rows, 1), 0).astype(F32)
        c_end = cidx * NSA_CMP_STRIDE + (NSA_CMP_LEN - 1)
        kc = jnp.concatenate([kc_ref[0, 0:rows, :], cpf_ref[0:rows, :]], axis=1)
        p_cmp = _masked_softmax_t(_dot_t(kc, q4), c_end <= tpos)
        ocmp_scr[...] = _dot(vct_ref[0, :, 0:rows], p_cmp.astype(MXU_DTYPE))
        p_sum = p_cmp[:, 0:tq]
        for h in range(1, nh):
            p_sum = p_sum + p_cmp[:, h * tq:(h + 1) * tq]
        c2st = c2st_ref[:, 0:rows]
        if MXU_DTYPE == jnp.float32:
            score_scr[...] = _dot(c2st, p_sum)
        else:
            score_scr[...] = sum(_dot(c2st, piece) for piece in _split3(p_sum))

    n_variants = -(-ncp // cmp_chunk)
    n_ending = (t0 + tq) // NSA_CMP_STRIDE - 1
    need = jnp.clip((n_ending + cmp_chunk - 1) // cmp_chunk, 1, n_variants)
    for v in range(1, n_variants + 1):
        pl.when(need == v)(functools.partial(cmp_part, min(v * cmp_chunk, ncp)))
    o_cmp = ocmp_scr[...]
    score = score_scr[...]

    t1 = t0 + lax.broadcasted_iota(jnp.int32, (1, tq), 1)
    cur = _shr(t1, NSA_SEL_LEN).astype(F32)
    jj = lax.broadcasted_iota(jnp.int32, (ns, 1), 0).astype(F32)
    forced = (jj == 0.0) | (jj == cur) | (jj == cur - 1.0)
    score = jnp.where(jj > cur, -1.0, jnp.where(forced, NSA_FORCED_SCORE, score))
    bias = jnp.full((ns, tq), NEG_INF, F32)
    for _ in range(topk):
        best = jnp.max(score, axis=0, keepdims=True)
        idx = jnp.min(jnp.where(score == best, jj, float(ns)), axis=0, keepdims=True)
        hit = jj == idx
        bias = jnp.where(hit, 0.0, bias)
        score = jnp.where(hit, -2.0, score)
    bias_scr[...] = bias
    any_scr[...] = jnp.broadcast_to(jnp.max(bias, axis=1, keepdims=True), (ns, LANES))

    _init_softmax_state(m_scr, l_scr, acc_scr)

    def slc_scores(k0, kt):
        keys = jnp.concatenate([k_ref[0, pl.ds(k0, tk), 0:LANES], pf_ref[pl.ds(k0, tk), :]], axis=1)
        s_t = _dot_t(keys, q4)
        b0 = pl.multiple_of(kt * blocks_per_tile, blocks_per_tile)
        rows = [jnp.broadcast_to(bias_scr[pl.ds(b0 + r, 1), :], (NSA_SEL_LEN, tq))
                for r in range(blocks_per_tile)]
        sel_bias = jnp.concatenate(rows, axis=0)
        return s_t + jnp.concatenate([sel_bias] * nh, axis=1)

    def slc_tile(kt, carry):
        b0 = pl.multiple_of(kt * blocks_per_tile, blocks_per_tile)
        touched = jnp.max(any_scr[pl.ds(b0, blocks_per_tile), :]) == 0.0

        @pl.when(touched)
        def _():
            k0 = pl.multiple_of(kt * tk, tk)
            _softmax_step_t(slc_scores(k0, kt), vt_ref[0:LANES, pl.ds(k0, tk)], m_scr, l_scr, acc_scr)
        return carry

    last = t0 // tk
    lax.fori_loop(0, last, slc_tile, 0)
    k0 = pl.multiple_of(last * tk, tk)
    kpos = (k0 + lax.broadcasted_iota(jnp.int32, (tk, 1), 0)).astype(F32)
    s_t = jnp.where(kpos <= tpos, slc_scores(k0, last), NEG_INF)
    _softmax_step_t(s_t, vt_ref[0:LANES, pl.ds(k0, tk)], m_scr, l_scr, acc_scr)
    o_slc = acc_scr[...] * (1.0 / l_scr[...])

    wlen = NSA_WINDOW + tq
    ws = pl.multiple_of(jnp.maximum(t0 - NSA_WINDOW, 0), tq)
    keys = jnp.concatenate([k_ref[0, pl.ds(ws, wlen), LANES:2 * LANES], pf_ref[pl.ds(ws, wlen), :]], axis=1)
    wpos = (ws + lax.broadcasted_iota(jnp.int32, (wlen, 1), 0)).astype(F32)
    dist = tpos - wpos
    p_win = _masked_softmax_t(_dot_t(keys, q4), (dist >= 0.0) & (dist < float(NSA_WINDOW)))
    o_win = _dot(vt_ref[LANES:2 * LANES, pl.ds(ws, wlen)], p_win.astype(MXU_DTYPE))

    gates = jax.nn.sigmoid(g_ref[0]).T
    for h in range(nh):
        c = slice(h * tq, (h + 1) * tq)
        o = (gates[3 * h:3 * h + 1, :] * o_cmp[:, c] + gates[3 * h + 1:3 * h + 2, :] * o_slc[:, c]
             + gates[3 * h + 2:3 * h + 3, :] * o_win[:, c])
        o_ref[0, :, h * LANES:(h + 1) * LANES] = o.T.astype(o_ref.dtype)


def _nsa(h3, aux3, vt, kc, vct, *, tq=NSA_Q_BLOCK, tk=512):
    b, s, _ = h3.shape
    ncp = kc.shape[1]
    ns = s // NSA_SEL_LEN
    tk = min(tk, s)
    w = NSA_HEADS * HEAD_DIM
    c2st = _cmp_to_sel_t(ncp, ns)
    pf = _pos_features(jnp.arange(s, dtype=F32))
    cpf = _pos_features(jnp.arange(ncp, dtype=F32) * NSA_CMP_STRIDE + 0.5 * (NSA_CMP_LEN - 1))
    sf = _slope_features(tq).astype(MXU_DTYPE)
    const = lambda a: _resident(a.shape, lambda bi, i: (0,) * a.ndim)
    return pl.pallas_call(
        functools.partial(_nsa_kernel, tq=tq, tk=tk, topk=min(NSA_TOPK, ns), cmp_chunk=min(256, ncp)),
        grid=(b, s // tq),
        in_specs=[
            pl.BlockSpec((1, tq, w), lambda bi, i: (bi, i, COL_NSA_Q // w)),
            _resident((1, s, 2 * LANES), lambda bi, i: (bi, 0, COL_NSA_K // (2 * LANES))),
            _resident((2 * LANES, s), lambda bi, i: (0, bi)),
            _resident((1, ncp, LANES), lambda bi, i: (bi, 0, 0)),
            _resident((1, LANES, ncp), lambda bi, i: (bi, 0, 0)),
            const(c2st), const(pf), const(cpf), const(sf),
            pl.BlockSpec((1, tq, LANES), lambda bi, i: (bi, i, 2)),
        ],
        out_specs=pl.BlockSpec((1, tq, w), lambda bi, i: (bi, i, 0)),
        out_shape=jax.ShapeDtypeStruct((b, s, w), MXU_DTYPE),
        scratch_shapes=[pltpu.VMEM((1, NSA_HEADS * tq), F32), pltpu.VMEM((1, NSA_HEADS * tq), F32),
                        pltpu.VMEM((HEAD_DIM, NSA_HEADS * tq), F32),
                        pltpu.VMEM((ns, tq), F32), pltpu.VMEM((ns, LANES), F32),
                        pltpu.VMEM((HEAD_DIM, NSA_HEADS * tq), F32), pltpu.VMEM((ns, tq), F32)],
        compiler_params=_cparams(("parallel", "arbitrary")),
        name="nsa",
    )(h3, h3, vt, kc, vct, c2st, pf, cpf, sf, aux3)


def _cmp_to_sel_t(ncp, ns):
    nc = ncp - 1
    c = np.arange(ncp)[None, :]
    j = np.arange(ns)[:, None]
    start = c * NSA_CMP_STRIDE
    m = (start < (j + 1) * NSA_SEL_LEN) & (start + NSA_CMP_LEN - 1 >= j * NSA_SEL_LEN) & (c < nc)
    return jnp.asarray(m.astype(np.float32)).astype(MXU_DTYPE)


def _mixout_kernel(x_ref, mla_ref, d0_ref, d1_ref, d2_ref, l0_ref, l1_ref, l2_ref, nsa_ref, w_ref, o_ref):
    o_dil = (d0_ref[...], d1_ref[...], d2_ref[...])
    lse = (l0_ref[...], l1_ref[...], l2_ref[...])
    top = jnp.maximum(jnp.maximum(lse[0], lse[1]), lse[2])
    e = [jnp.exp(l - top) for l in lse]
    inv = 1.0 / (e[0] + e[1] + e[2])
    acc = x_ref[...] + _dot(mla_ref[...], w_ref[0:768, :])
    for g in range(3):
        mixed = (o_dil[g] * (e[g] * inv)).astype(MXU_DTYPE)
        acc += _dot(mixed, w_ref[768 + g * 256:768 + (g + 1) * 256, :])
    acc += _dot(nsa_ref[...], w_ref[1536:2048, :])
    o_ref[...] = acc


def _mixout(x2, o_mla, o_dil, lse_dil, o_nsa, w, *, tm=512):
    n, d = x2.shape
    row = lambda width: pl.BlockSpec((tm, width), lambda i: (i, 0))
    return pl.pallas_call(
        _mixout_kernel,
        grid=(n // tm,),
        in_specs=[row(d), row(o_mla.shape[1])] + [row(256)] * 6 + [row(o_nsa.shape[1]),
                  pl.BlockSpec(w.shape, lambda i: (0, 0))],
        out_specs=row(d),
        out_shape=jax.ShapeDtypeStruct((n, d), F32),
        compiler_params=_cparams(("parallel",)),
        name="mix_out",
    )(x2, o_mla, *o_dil, *lse_dil, o_nsa, w)


def kernel(x, ffn1_norm, ffn1_w_in, ffn1_w_out, mix_norm, w_mix_in, mla_q_norm, mla_w_uq, mla_kv_norm,
           mla_w_ukv, nsa_cmp_pos, nsa_phi_k1, nsa_phi_k2, nsa_phi_v1, nsa_phi_v2, w_mix_out, ffn2_norm,
           ffn2_w_in, ffn2_w_out, final_norm):
    b, s, d = x.shape
    depth = ffn1_w_in.shape[0]
    n = b * s
    invf = _rope_inv_freq_row()
    x2 = x.reshape(n, d)
    for l in range(depth):
        x2 = _ffn(x2, ffn1_norm[l], *_prep_ffn(ffn1_w_in[l], ffn1_w_out[l]))

        h2, hd2, aux2, nsa_vt = _mixin(x2, mix_norm[l], *_prep_mixin(w_mix_in[l]))
        h3 = h2.reshape(b, s, H_COLS)
        hd3 = hd2.reshape(b, s, DIL_COLS)
        aux3 = aux2.reshape(b, s, AUX_COLS)

        q, k, vt = _mla_proj(h2, mla_q_norm[l].reshape(1, -1), mla_kv_norm[l].reshape(1, -1),
                             *_prep_mla(mla_w_uq[l], mla_w_ukv[l]), invf, seq=s)
        o_mla = _mla_flash(q.reshape(b, s, -1), k.reshape(b, s, -1), vt)

        dil = [_dilated_group(hd3, g) for g in range(len(DIL_PAIRS))]

        kc, vct = _nsa_compress(aux3, *_prep_nsa_cmp(nsa_cmp_pos[l], nsa_phi_k1[l], nsa_phi_k2[l],
                                                      nsa_phi_v1[l], nsa_phi_v2[l]))
        o_nsa = _nsa(h3, aux3, nsa_vt, kc, vct)

        x2 = _mixout(x2, o_mla.reshape(n, -1), [o.reshape(n, -1) for o, _ in dil],
                     [e.reshape(n, -1) for _, e in dil], o_nsa.reshape(n, -1),
                     w_mix_out[l].astype(MXU_DTYPE))

        gf = final_norm if l == depth - 1 else None
        x2 = _ffn(x2, ffn2_norm[l], *_prep_ffn(ffn2_w_in[l], ffn2_w_out[l]), gf)
    return x2.reshape(b, s, d)
```

```python
import functools
import math

import numpy as np
import jax
import jax.numpy as jnp
from jax import lax
from jax.experimental import pallas as pl
from jax.experimental.pallas import tpu as pltpu

F32 = jnp.float32
MXU_DTYPE = jnp.bfloat16
VMEM_LIMIT_BYTES = 56 * 1024 * 1024
LANES = 128
LOG2E = math.log2(math.e)
FFN_TILE = 512
DEN_ROWS = 16
LAZY_MAX_SLACK = 16.0

HEAD_DIM = 128
RMS_EPS = 1e-6
NEG_INF = -1e30

MLA_HEADS = 6
MLA_LORA = 512
MLA_NOPE = 128
MLA_ROPE = 64
MLA_V = 128
MLA_QK_PAD = 256
ROPE_BASE = 10000.0

DIL_PAIRS = ((128, 1), (512, 4), (2048, 16))
DIL_HEADS = 6
DIL_BLOCK = 128

NSA_HEADS = 4
NSA_CMP_LEN = 32
NSA_CMP_STRIDE = 16
NSA_SEL_LEN = 64
NSA_TOPK = 16
NSA_WINDOW = 512
NSA_FORCED_SCORE = 100.0
NSA_Q_BLOCK = 256
POS_SPLIT = 128

N_ALIBI = DIL_HEADS + NSA_HEADS
ALIBI_SLOPES = tuple(float(2.0 ** (-8.0 * i / N_ALIBI)) for i in range(1, N_ALIBI + 1))

COL_QLAT = 0
COL_KVLAT = 512
COL_NSA_Q = 1024
COL_KROPE = 1536
COL_NSA_K = 1792
H_COLS = 2048
DIL_COLS = 2304
AUX_COLS = 384


def _cparams(sem):
    return pltpu.CompilerParams(dimension_semantics=sem, vmem_limit_bytes=VMEM_LIMIT_BYTES)


def _rms(x, g):
    ms = jnp.mean(x * x, axis=-1, keepdims=True)
    return (x * lax.rsqrt(ms + RMS_EPS)) * g


def _dot(a, b):
    return jnp.dot(a, b, preferred_element_type=F32)


def _shr(x, pow2):
    return lax.shift_right_logical(x, int(pow2).bit_length() - 1)


def _dot_t(a, b):
    return lax.dot_general(a, b, (((1,), (1,)), ((), ())), preferred_element_type=F32)


def _resident(shape, index_map):
    return pl.BlockSpec(shape, index_map, pipeline_mode=pl.Buffered(1))


def _ffn_kernel(x_ref, g_ref, wg_ref, wu_ref, wo_ref, *rest, final):
    if final:
        gf_ref, o_ref, xn_ref = rest
    else:
        o_ref, xn_ref = rest
    j = pl.program_id(1)

    @pl.when(j == 0)
    def _():
        x = x_ref[...]
        xn_ref[...] = _rms(x, g_ref[...]).astype(xn_ref.dtype)
        o_ref[...] = x

    xn = xn_ref[...]
    gate = _dot(xn, wg_ref[...])
    up = _dot(xn, wu_ref[...])
    h = (0.5 * gate) * jax.nn.sigmoid(gate) * up
    o_ref[...] += _dot(h.astype(MXU_DTYPE), wo_ref[...])

    if final:
        @pl.when(j == pl.num_programs(1) - 1)
        def _():
            o_ref[...] = _rms(o_ref[...], gf_ref[...])


def _ffn(x2, g, w_in, wo, gf=None, *, tm=1024, tf=FFN_TILE):
    n, d = x2.shape
    dffp = wo.shape[0]
    nff = dffp // tf
    tm = min(tm, n)
    final = gf is not None
    in_specs = [
        _resident((tm, d), lambda i, j: (i, 0)),
        pl.BlockSpec((1, d), lambda i, j: (0, 0)),
        pl.BlockSpec((d, tf), lambda i, j: (0, j)),
        pl.BlockSpec((d, tf), lambda i, j: (0, nff + j)),
        pl.BlockSpec((tf, d), lambda i, j: (j, 0)),
    ]
    args = [x2, g.reshape(1, d), w_in, w_in, wo]
    if final:
        in_specs.append(pl.BlockSpec((1, d), lambda i, j: (0, 0)))
        args.append(gf.reshape(1, d))
    return pl.pallas_call(
        functools.partial(_ffn_kernel, final=final),
        grid=(n // tm, nff),
        in_specs=in_specs,
        out_specs=pl.BlockSpec((tm, d), lambda i, j: (i, 0)),
        out_shape=jax.ShapeDtypeStruct((n, d), F32),
        scratch_shapes=[pltpu.VMEM((tm, d), MXU_DTYPE)],
        compiler_params=_cparams(("parallel", "arbitrary")),
        name="ffn_final" if final else "ffn",
    )(*args)


def _prep_ffn(w_in, w_out, tf=FFN_TILE):
    d, two_dff = w_in.shape
    dff = two_dff // 2
    dffp = -(-dff // tf) * tf
    z = jnp.zeros((d, dffp - dff), MXU_DTYPE)
    w = jnp.concatenate([w_in[:, :dff].astype(MXU_DTYPE), z, w_in[:, dff:].astype(MXU_DTYPE), z], axis=1)
    wo = jnp.pad(w_out, ((0, dffp - dff), (0, 0))).astype(MXU_DTYPE)
    return w, wo


def _mixin_kernel(x_ref, g_ref, wm_ref, wd_ref, wa_ref, wvt_ref, hm_ref, hd_ref, a_ref, vt_ref):
    xn = _rms(x_ref[...], g_ref[...]).astype(MXU_DTYPE)
    hm_ref[...] = _dot(xn, wm_ref[...]).astype(hm_ref.dtype)
    hd_ref[...] = _dot(xn, wd_ref[...]).astype(hd_ref.dtype)
    a_ref[...] = _dot(xn, wa_ref[...])
    vt_ref[...] = _dot_t(wvt_ref[...], xn).astype(vt_ref.dtype)


def _mixin(x2, g, wm, wd, wa, wvt, *, tm=512):
    n, d = x2.shape
    weight = lambda a: _resident(a.shape, lambda i: (0, 0))
    return pl.pallas_call(
        _mixin_kernel,
        grid=(n // tm,),
        in_specs=[
            pl.BlockSpec((tm, d), lambda i: (i, 0)),
            pl.BlockSpec((1, d), lambda i: (0, 0)),
            weight(wm), weight(wd), weight(wa), weight(wvt),
        ],
        out_specs=[
            pl.BlockSpec((tm, H_COLS), lambda i: (i, 0)),
            pl.BlockSpec((tm, DIL_COLS), lambda i: (i, 0)),
            pl.BlockSpec((tm, AUX_COLS), lambda i: (i, 0)),
            pl.BlockSpec((2 * LANES, tm), lambda i: (0, i)),
        ],
        out_shape=[
            jax.ShapeDtypeStruct((n, H_COLS), MXU_DTYPE),
            jax.ShapeDtypeStruct((n, DIL_COLS), MXU_DTYPE),
            jax.ShapeDtypeStruct((n, AUX_COLS), F32),
            jax.ShapeDtypeStruct((2 * LANES, n), MXU_DTYPE),
        ],
        compiler_params=_cparams(("parallel",)),
        name="mix_in",
    )(x2, g.reshape(1, d), wm, wd, wa, wvt)


def _rot_half_cols(w):
    half = w.shape[1] // 2
    return jnp.concatenate([-w[:, half:], w[:, :half]], axis=1)


def _prep_mixin(w):
    d = w.shape[0]
    o = 0
    parts = {}
    for name, width in (("q_lat", 512), ("kv_lat", 512), ("k_rope", 64), ("dq", 768), ("dk", 768),
                        ("dv", 768), ("nq", 512), ("nkc", 128), ("nvc", 128), ("nks", 128),
                        ("nvs", 128), ("nkw", 128), ("nvw", 128), ("ng", 12)):
        parts[name] = w[:, o:o + width]
        o += width
    z64 = jnp.zeros((d, 64), w.dtype)
    kr = parts["k_rope"]
    wm = jnp.concatenate([
        parts["q_lat"], parts["kv_lat"], parts["nq"],
        kr, z64, _rot_half_cols(kr), z64,
        parts["nks"], parts["nkw"]], axis=1).astype(MXU_DTYPE)
    wd = jnp.concatenate([parts["dq"], parts["dk"], parts["dv"]], axis=1).astype(MXU_DTYPE)
    wa = jnp.concatenate([parts["nkc"], parts["nvc"], parts["ng"],
                          jnp.zeros((d, LANES - 12), w.dtype)], axis=1).astype(MXU_DTYPE)
    wvt = jnp.concatenate([parts["nvs"], parts["nvw"]], axis=1).T.astype(MXU_DTYPE)
    return wm, wd, wa, wvt


def _mla_proj_kernel(lat_ref, kr_ref, qn_ref, kvn_ref, wq_ref, wk_ref, wvt_ref, invf_ref,
                     q_ref, k_ref, vt_ref, *, seq, tm):
    i = pl.program_id(0)
    pos0 = lax.rem(i * tm, seq)
    pos = (pos0 + lax.broadcasted_iota(jnp.int32, (tm, 1), 0)).astype(F32)
    ang = pos * invf_ref[...]
    cos = jnp.cos(ang)
    sin = jnp.sin(ang)
    scale = (MLA_NOPE + MLA_ROPE) ** -0.5 * LOG2E

    lat = lat_ref[...].astype(F32)
    qn = _rms(lat[:, :MLA_LORA], qn_ref[...]).astype(MXU_DTYPE)
    kvn = _rms(lat[:, MLA_LORA:], kvn_ref[...]).astype(MXU_DTYPE)
    qm = _dot(qn, wq_ref[...])
    rot0 = MLA_HEADS * MLA_QK_PAD
    for h in range(MLA_HEADS):
        c = h * MLA_QK_PAD
        nope = qm[:, c:c + LANES]
        pe = qm[:, c + LANES:c + 2 * LANES] * cos + qm[:, rot0 + h * LANES:rot0 + (h + 1) * LANES] * sin
        q_ref[:, c:c + LANES] = (nope * scale).astype(q_ref.dtype)
        q_ref[:, c + LANES:c + 2 * LANES] = (pe * scale).astype(q_ref.dtype)

    kr = kr_ref[...].astype(F32)
    kpe = (kr[:, :LANES] * cos + kr[:, LANES:] * sin).astype(k_ref.dtype)
    kn = _dot(kvn, wk_ref[...])
    for h in range(MLA_HEADS):
        c = h * MLA_QK_PAD
        k_ref[:, c:c + LANES] = kn[:, h * LANES:(h + 1) * LANES].astype(k_ref.dtype)
        k_ref[:, c + LANES:c + 2 * LANES] = kpe
    vt_ref[...] = _dot_t(wvt_ref[...], kvn).astype(vt_ref.dtype)


def _mla_proj(h2, qn, kvn, wq, wk, wvt, invf, *, seq, tm=512):
    n = h2.shape[0]
    qk_cols = MLA_HEADS * MLA_QK_PAD
    v_rows = MLA_HEADS * MLA_V
    full = lambda a: pl.BlockSpec(a.shape, lambda i: (0,) * a.ndim)
    return pl.pallas_call(
        functools.partial(_mla_proj_kernel, seq=seq, tm=tm),
        grid=(n // tm,),
        in_specs=[
            pl.BlockSpec((tm, 2 * MLA_LORA), lambda i: (i, 0)),
            pl.BlockSpec((tm, 2 * LANES), lambda i: (i, COL_KROPE // (2 * LANES))),
            full(qn), full(kvn), full(wq), full(wk), full(wvt), full(invf),
        ],
        out_specs=[
            pl.BlockSpec((tm, qk_cols), lambda i: (i, 0)),
            pl.BlockSpec((tm, qk_cols), lambda i: (i, 0)),
            pl.BlockSpec((v_rows, tm), lambda i: (0, i)),
        ],
        out_shape=[
            jax.ShapeDtypeStruct((n, qk_cols), MXU_DTYPE),
            jax.ShapeDtypeStruct((n, qk_cols), MXU_DTYPE),
            jax.ShapeDtypeStruct((v_rows, n), MXU_DTYPE),
        ],
        compiler_params=_cparams(("parallel",)),
        name="mla_proj",
    )(h2, h2, qn, kvn, wq, wk, wvt, invf)


def _prep_mla(w_uq, w_ukv):
    r = MLA_LORA
    wq3 = w_uq.reshape(r, MLA_HEADS, MLA_NOPE + MLA_ROPE)
    nope, pe = wq3[..., :MLA_NOPE], wq3[..., MLA_NOPE:]
    z = jnp.zeros((r, MLA_HEADS, 64), w_uq.dtype)
    main = jnp.concatenate([nope, pe, z], axis=-1).reshape(r, MLA_HEADS * MLA_QK_PAD)
    half = MLA_ROPE // 2
    pe_rot = jnp.concatenate([-pe[..., half:], pe[..., :half]], axis=-1)
    rot = jnp.concatenate([pe_rot, z], axis=-1).reshape(r, MLA_HEADS * LANES)
    wq = jnp.concatenate([main, rot], axis=1).astype(MXU_DTYPE)
    wkv3 = w_ukv.reshape(r, MLA_HEADS, MLA_NOPE + MLA_V)
    wk = wkv3[..., :MLA_NOPE].reshape(r, MLA_HEADS * MLA_NOPE).astype(MXU_DTYPE)
    wvt = wkv3[..., MLA_NOPE:].reshape(r, MLA_HEADS * MLA_V).T.astype(MXU_DTYPE)
    return wq, wk, wvt


def _rope_inv_freq_row():
    half = MLA_ROPE // 2
    f = ROPE_BASE ** (-jnp.arange(half, dtype=F32) / half)
    return jnp.concatenate([f, f, jnp.zeros((LANES - MLA_ROPE,), F32)]).reshape(1, LANES)


def _with_ones_rows(v_t):
    return jnp.concatenate([v_t, jnp.ones((DEN_ROWS, v_t.shape[1]), v_t.dtype)], axis=0)


def _softmax_piece_t(scores, v_aug, m_ref, acc_ref):
    m_old = m_ref[...]
    s_t = scores()
    pv = _dot(v_aug, jnp.exp2(s_t - m_old).astype(MXU_DTYPE))
    fits = jnp.max(jnp.max(s_t, axis=0, keepdims=True) - m_old) <= LAZY_MAX_SLACK

    @pl.when(fits)
    def _():
        acc_ref[...] += pv

    @pl.when(jnp.logical_not(fits))
    def _():
        s_again = scores()
        m_new = jnp.maximum(m_old, jnp.max(s_again, axis=0, keepdims=True))
        weights = jnp.exp2(s_again - m_new).astype(MXU_DTYPE)
        acc_ref[...] = jnp.exp2(m_old - m_new) * acc_ref[...] + _dot(v_aug, weights)
        m_ref[...] = m_new


def _init_softmax_state(m_scr, acc_scr):
    m_scr[...] = jnp.full(m_scr.shape, NEG_INF, F32)
    acc_scr[...] = jnp.zeros(acc_scr.shape, F32)


def _softmax_result(acc_scr, dv):
    return acc_scr[0:dv, :] * (1.0 / acc_scr[dv:dv + 1, :])


def _flash_kernel(qt_ref, kt_ref, q_ref, k_ref, vt_ref, o_ref, m_scr, acc_scr, *, t, ksub, qsub):
    step = pl.program_id(2)
    qi = qt_ref[step]
    ki = kt_ref[step]

    @pl.when(ki == 0)
    def _():
        _init_softmax_state(m_scr, acc_scr)

    def piece(kp, qp, diagonal):
        k0, q0 = kp * ksub, qp * qsub
        ks = slice(k0, k0 + ksub)
        qs = slice(q0, q0 + qsub)

        def scores():
            s_t = _dot_t(k_ref[0, ks, :], q_ref[0, qs, :])
            if diagonal and k0 + ksub - 1 > q0:
                kpos = k0 + lax.broadcasted_iota(jnp.int32, (ksub, 1), 0)
                qpos = q0 + lax.broadcasted_iota(jnp.int32, (1, qsub), 1)
                s_t = jnp.where(kpos <= qpos, s_t, NEG_INF)
            return s_t

        _softmax_piece_t(scores, _with_ones_rows(vt_ref[:, ks]), m_scr.at[:, qs], acc_scr.at[:, qs])

    @pl.when(ki < qi)
    def _():
        for kp in range(t // ksub):
            for qp in range(t // qsub):
                piece(kp, qp, False)

    @pl.when(ki == qi)
    def _():
        for kp in range(t // ksub):
            for qp in range(t // qsub):
                if kp * ksub <= (qp + 1) * qsub - 1:
                    piece(kp, qp, True)
        o_ref[0] = _softmax_result(acc_scr, MLA_V).T.astype(o_ref.dtype)


def _mla_flash(q3, k3, vt, *, t=2048, ksub=1024, qsub=1024):
    b, s, _ = q3.shape
    t = min(t, s)
    ksub, qsub = min(ksub, t), min(qsub, t)
    nq = s // t
    pairs = [(qi, ki) for qi in range(nq) for ki in range(qi + 1)]
    qt = jnp.asarray(np.array([p[0] for p in pairs], np.int32))
    kt = jnp.asarray(np.array([p[1] for p in pairs], np.int32))
    grid_spec = pltpu.PrefetchScalarGridSpec(
        num_scalar_prefetch=2,
        grid=(b, MLA_HEADS, len(pairs)),
        in_specs=[
            pl.BlockSpec((1, t, MLA_QK_PAD), lambda bi, h, st, qt, kt: (bi, qt[st], h)),
            pl.BlockSpec((1, t, MLA_QK_PAD), lambda bi, h, st, qt, kt: (bi, kt[st], h)),
            pl.BlockSpec((MLA_V, t), lambda bi, h, st, qt, kt: (h, bi * nq + kt[st])),
        ],
        out_specs=pl.BlockSpec((1, t, MLA_V), lambda bi, h, st, qt, kt: (bi, qt[st], h)),
        scratch_shapes=[pltpu.VMEM((1, t), F32), pltpu.VMEM((MLA_V + DEN_ROWS, t), F32)],
    )
    return pl.pallas_call(
        functools.partial(_flash_kernel, t=t, ksub=ksub, qsub=qsub),
        grid_spec=grid_spec,
        out_shape=jax.ShapeDtypeStruct((b, s, MLA_HEADS * MLA_V), MXU_DTYPE),
        compiler_params=_cparams(("parallel", "parallel", "arbitrary")),
        name="mla_flash",
    )(qt, kt, q3, k3, vt)


def _dil_kernel(q_ref, kc_ref, kp_ref, vc_ref, vp_ref, o_ref, lse_ref, *, dilation, span, slopes, tl):
    n = pl.program_id(2)
    scale = HEAD_DIM ** -0.5
    blk = DIL_BLOCK
    a = lax.broadcasted_iota(jnp.int32, (blk, 1), 0)
    c = lax.broadcasted_iota(jnp.int32, (1, 2 * blk), 1)
    j = blk + a - c
    in_band = (j >= 0) & (j <= span)
    first_valid = in_band & ((c >= blk) | (n > 0))
    dist = (j * dilation).astype(F32)
    for hg in range(2):
        cols = slice(hg * LANES, (hg + 1) * LANES)
        bias = slopes[hg] * dist
        for sb in range(tl // blk):
            rows = slice(sb * blk, (sb + 1) * blk)
            q = (q_ref[0, rows, cols].astype(F32) * scale).astype(MXU_DTYPE)
            if sb == 0:
                kprev, vprev, valid = kp_ref[0, :, cols], vp_ref[0, :, cols], first_valid
            else:
                prev = slice((sb - 1) * blk, sb * blk)
                kprev, vprev, valid = kc_ref[0, prev, cols], vc_ref[0, prev, cols], in_band
            keys = jnp.concatenate([kprev, kc_ref[0, rows, cols]], axis=0)
            vals = jnp.concatenate([vprev, vc_ref[0, rows, cols]], axis=0)
            s = jnp.where(valid, _dot_t(q, keys) - bias, NEG_INF)
            m = jnp.max(s, axis=1, keepdims=True)
            e = jnp.where(valid, jnp.exp(s - m), 0.0)
            den = jnp.sum(e, axis=1, keepdims=True)
            o_ref[0, rows, cols] = _dot((e / den).astype(MXU_DTYPE), vals)
            lse_ref[0, rows, cols] = jnp.broadcast_to(m + jnp.log(den), (blk, LANES))


def _dilated_group(hd3, g, *, tl=512):
    b, s, _ = hd3.shape
    window, dilation = DIL_PAIRS[g]
    span = window // dilation
    seq_l = s // dilation
    tl = min(tl, seq_l)
    per_blk = tl // DIL_BLOCK
    w = 2 * LANES
    group_cols = [t * DIL_HEADS * HEAD_DIM + g * w for t in range(3)]
    if dilation == 1:
        hv = hd3
        qc, kc, vc = (c // w for c in group_cols)
    else:
        hv = jnp.concatenate([hd3[:, :, c:c + w] for c in group_cols], axis=2)
        hv = hv.reshape(b, seq_l, dilation * 3 * w)
        qc, kc, vc = 0, 1, 2
    stride = hv.shape[2] // (dilation * w)
    cur = lambda col: pl.BlockSpec((1, tl, w), lambda bi, r, n: (bi, n, r * stride + col))
    prev = lambda col: pl.BlockSpec(
        (1, DIL_BLOCK, w), lambda bi, r, n: (bi, jnp.maximum(n * per_blk - 1, 0), r * stride + col))
    out_spec = pl.BlockSpec((1, tl, w), lambda bi, r, n: (bi, n, r))
    out_sds = jax.ShapeDtypeStruct((b, seq_l, dilation * w), F32)
    o, lse = pl.pallas_call(
        functools.partial(_dil_kernel, dilation=dilation, span=span,
                          slopes=ALIBI_SLOPES[2 * g:2 * g + 2], tl=tl),
        grid=(b, dilation, seq_l // tl),
        in_specs=[cur(qc), cur(kc), prev(kc), cur(vc), prev(vc)],
        out_specs=[out_spec, out_spec],
        out_shape=[out_sds, out_sds],
        compiler_params=_cparams(("parallel", "parallel", "parallel")),
        name=f"dilated_g{g}",
    )(hv, hv, hv, hv, hv)
    return o.reshape(b, s, w), lse.reshape(b, s, w)


def _nsa_cmp_kernel(x_ref, pos_ref, wa_ref, wb_ref, w2k_ref, w2vt_ref, kc_ref, vct_ref):
    x = x_ref[0]
    xa = (x + pos_ref[0:1, :]).astype(MXU_DTYPE)
    xb = (x + pos_ref[1:2, :]).astype(MXU_DTYPE)
    first = _dot(xa, wa_ref[...])
    second = _dot(xb, wb_ref[...])
    second_next = pltpu.roll(second, x.shape[0] - 1, 0)
    pre = first + second_next
    hid = (pre * jax.nn.sigmoid(pre)).astype(MXU_DTYPE)
    kc_ref[0] = _dot(hid[:, :LANES], w2k_ref[...]).astype(kc_ref.dtype)
    vct_ref[0] = _dot_t(w2vt_ref[...], hid[:, LANES:]).astype(vct_ref.dtype)


def _nsa_compress(aux3, pos2, wa, wb, w2k, w2vt):
    b, s, _ = aux3.shape
    nchunk = s // NSA_CMP_STRIDE
    xk = aux3[:, :, :2 * LANES].reshape(b, nchunk, NSA_CMP_STRIDE * 2 * LANES)
    full = lambda a: pl.BlockSpec(a.shape, lambda bi: (0,) * a.ndim)
    return pl.pallas_call(
        _nsa_cmp_kernel,
        grid=(b,),
        in_specs=[pl.BlockSpec((1, nchunk, xk.shape[2]), lambda bi: (bi, 0, 0)),
                  full(pos2), full(wa), full(wb), full(w2k), full(w2vt)],
        out_specs=[pl.BlockSpec((1, nchunk, LANES), lambda bi: (bi, 0, 0)),
                   pl.BlockSpec((1, LANES, nchunk), lambda bi: (bi, 0, 0))],
        out_shape=[jax.ShapeDtypeStruct((b, nchunk, LANES), MXU_DTYPE),
                   jax.ShapeDtypeStruct((b, LANES, nchunk), MXU_DTYPE)],
        compiler_params=_cparams(("parallel",)),
        name="nsa_cmp",
    )(xk, pos2, wa, wb, w2k, w2vt)


def _prep_nsa_cmp(cmp_pos, phi_k1, phi_k2, phi_v1, phi_v2):
    half = NSA_CMP_LEN // 2
    pk = cmp_pos.reshape(2, half, HEAD_DIM)
    pos2 = jnp.concatenate([pk, pk], axis=-1).reshape(2, half * 2 * LANES)

    def halves(w1, is_v):
        w = w1.reshape(2, half, HEAD_DIM, HEAD_DIM)
        z = jnp.zeros_like(w)
        w = jnp.concatenate([z, w] if is_v else [w, z], axis=2)
        return w.reshape(2, half * 2 * LANES, HEAD_DIM)

    wk, wv = halves(phi_k1, False), halves(phi_v1, True)
    wa = jnp.concatenate([wk[0], wv[0]], axis=1).astype(MXU_DTYPE)
    wb = jnp.concatenate([wk[1], wv[1]], axis=1).astype(MXU_DTYPE)
    return pos2, wa, wb, phi_k2.astype(MXU_DTYPE), phi_v2.T.astype(MXU_DTYPE)


def _split3(x):
    hi = x.astype(MXU_DTYPE)
    r = x - hi.astype(F32)
    mid = r.astype(MXU_DTYPE)
    lo = (r - mid.astype(F32)).astype(MXU_DTYPE)
    return hi, mid, lo


def _pos_features(pos):
    hi = jnp.floor(pos / POS_SPLIT) * POS_SPLIT
    lo = pos - hi
    cols = jnp.stack([hi, hi, hi, lo, lo, lo], axis=1)
    return jnp.pad(cols, ((0, 0), (0, LANES - 6))).astype(MXU_DTYPE)


def _slope_features(tq):
    sig = jnp.asarray([s * LOG2E for s in ALIBI_SLOPES[DIL_HEADS:]], F32)
    pieces = jnp.stack(_split3(sig), axis=1)
    rows = jnp.concatenate([pieces, pieces], axis=1)
    rows = jnp.pad(rows, ((0, 0), (0, LANES - 6)))
    return jnp.repeat(rows, tq, axis=0)


def _masked_softmax_t(s_t, mask):
    s_t = jnp.where(mask, s_t, NEG_INF)
    m = jnp.max(s_t, axis=0, keepdims=True)
    e = jnp.exp2(s_t - m)
    den = jnp.sum(e, axis=0, keepdims=True)
    return e * jnp.where(m > 0.5 * NEG_INF, 1.0 / den, 0.0)


def _nsa_kernel(q_ref, k_ref, vt_ref, kc_ref, vct_ref, c2st_ref, pf_ref, cpf_ref, sf_ref, g_ref,
                o_ref, m_scr, acc_scr, bias_scr, any_scr, ocmp_scr, score_scr,
                *, tq, tk, topk, cmp_chunk):
    i = pl.program_id(1)
    t0 = i * tq
    nh = NSA_HEADS
    cols = nh * tq
    ncp = kc_ref.shape[1]
    ns = c2st_ref.shape[0]
    blocks_per_tile = tk // NSA_SEL_LEN

    q = q_ref[0]
    q4 = jnp.concatenate([q[:, h * LANES:(h + 1) * LANES] for h in range(nh)], axis=0)
    q4 = (q4.astype(F32) * (HEAD_DIM ** -0.5 * LOG2E)).astype(MXU_DTYPE)
    q4 = jnp.concatenate([q4, sf_ref[...]], axis=1)
    col = lax.broadcasted_iota(jnp.int32, (1, cols), 1)
    tpos = (t0 + (col & (tq - 1))).astype(F32)

    def cmp_part(rows):
        cidx = lax.broadcasted_iota(jnp.int32, (rows, 1), 0).astype(F32)
        c_end = cidx * NSA_CMP_STRIDE + (NSA_CMP_LEN - 1)
        kc = jnp.concatenate([kc_ref[0, 0:rows, :], cpf_ref[0:rows, :]], axis=1)
        p_cmp = _masked_softmax_t(_dot_t(kc, q4), c_end <= tpos)
        ocmp_scr[...] = _dot(vct_ref[0, :, 0:rows], p_cmp.astype(MXU_DTYPE))
        p_sum = p_cmp[:, 0:tq]
        for h in range(1, nh):
            p_sum = p_sum + p_cmp[:, h * tq:(h + 1) * tq]
        c2st = c2st_ref[:, 0:rows]
        if MXU_DTYPE == jnp.float32:
            score_scr[...] = _dot(c2st, p_sum)
        else:
            score_scr[...] = sum(_dot(c2st, piece) for piece in _split3(p_sum))

    n_variants = -(-ncp // cmp_chunk)
    n_ending = (t0 + tq) // NSA_CMP_STRIDE - 1
    need = jnp.clip((n_ending + cmp_chunk - 1) // cmp_chunk, 1, n_variants)
    for v in range(1, n_variants + 1):
        pl.when(need == v)(functools.partial(cmp_part, min(v * cmp_chunk, ncp)))
    o_cmp = ocmp_scr[...]
    score = score_scr[...]

    t1 = t0 + lax.broadcasted_iota(jnp.int32, (1, tq), 1)
    cur = _shr(t1, NSA_SEL_LEN).astype(F32)
    jj = lax.broadcasted_iota(jnp.int32, (ns, 1), 0).astype(F32)
    forced = (jj == 0.0) | (jj == cur) | (jj == cur - 1.0)
    score = jnp.where(jj > cur, -1.0, jnp.where(forced, NSA_FORCED_SCORE, score))
    bias = jnp.full((ns, tq), NEG_INF, F32)
    for _ in range(topk):
        best = jnp.max(score, axis=0, keepdims=True)
        idx = jnp.min(jnp.where(score == best, jj, float(ns)), axis=0, keepdims=True)
        hit = jj == idx
        bias = jnp.where(hit, 0.0, bias)
        score = jnp.where(hit, -2.0, score)
    bias_scr[...] = bias
    any_scr[...] = jnp.broadcast_to(jnp.max(bias, axis=1, keepdims=True), (ns, LANES))

    _init_softmax_state(m_scr, acc_scr)

    def slc_update(kt, causal):
        k0 = pl.multiple_of(kt * tk, tk)
        b0 = pl.multiple_of(kt * blocks_per_tile, blocks_per_tile)

        def scores():
            keys = jnp.concatenate([k_ref[0, pl.ds(k0, tk), 0:LANES], pf_ref[pl.ds(k0, tk), :]], axis=1)
            s_t = _dot_t(keys, q4)
            rows = [jnp.broadcast_to(bias_scr[pl.ds(b0 + r, 1), :], (NSA_SEL_LEN, tq))
                    for r in range(blocks_per_tile)]
            sel_bias = jnp.concatenate(rows, axis=0)
            s_t = s_t + jnp.concatenate([sel_bias] * nh, axis=1)
            if causal:
                kpos = (k0 + lax.broadcasted_iota(jnp.int32, (tk, 1), 0)).astype(F32)
                s_t = jnp.where(kpos <= tpos, s_t, NEG_INF)
            return s_t

        _softmax_piece_t(scores, _with_ones_rows(vt_ref[0:LANES, pl.ds(k0, tk)]), m_scr, acc_scr)

    def slc_tile(kt, carry):
        b0 = pl.multiple_of(kt * blocks_per_tile, blocks_per_tile)
        touched = jnp.max(any_scr[pl.ds(b0, blocks_per_tile), :]) == 0.0
        pl.when(touched)(functools.partial(slc_update, kt, False))
        return carry

    last = t0 // tk
    lax.fori_loop(0, last, slc_tile, 0)
    slc_update(last, True)
    o_slc = _softmax_result(acc_scr, HEAD_DIM)

    wlen = NSA_WINDOW + tq
    ws = pl.multiple_of(jnp.maximum(t0 - NSA_WINDOW, 0), tq)
    keys = jnp.concatenate([k_ref[0, pl.ds(ws, wlen), LANES:2 * LANES], pf_ref[pl.ds(ws, wlen), :]], axis=1)
    wpos = (ws + lax.broadcasted_iota(jnp.int32, (wlen, 1), 0)).astype(F32)
    dist = tpos - wpos
    p_win = _masked_softmax_t(_dot_t(keys, q4), (dist >= 0.0) & (dist < float(NSA_WINDOW)))
    o_win = _dot(vt_ref[LANES:2 * LANES, pl.ds(ws, wlen)], p_win.astype(MXU_DTYPE))

    gates = jax.nn.sigmoid(g_ref[0]).T
    for h in range(nh):
        c = slice(h * tq, (h + 1) * tq)
        o = (gates[3 * h:3 * h + 1, :] * o_cmp[:, c] + gates[3 * h + 1:3 * h + 2, :] * o_slc[:, c]
             + gates[3 * h + 2:3 * h + 3, :] * o_win[:, c])
        o_ref[0, :, h * LANES:(h + 1) * LANES] = o.T.astype(o_ref.dtype)


def _nsa(h3, aux3, vt, kc, vct, *, tq=NSA_Q_BLOCK, tk=512):
    b, s, _ = h3.shape
    ncp = kc.shape[1]
    ns = s // NSA_SEL_LEN
    tk = min(tk, s)
    w = NSA_HEADS * HEAD_DIM
    c2st = _cmp_to_sel_t(ncp, ns)
    pf = _pos_features(jnp.arange(s, dtype=F32))
    cpf = _pos_features(jnp.arange(ncp, dtype=F32) * NSA_CMP_STRIDE + 0.5 * (NSA_CMP_LEN - 1))
    sf = _slope_features(tq).astype(MXU_DTYPE)
    const = lambda a: _resident(a.shape, lambda bi, i: (0,) * a.ndim)
    return pl.pallas_call(
        functools.partial(_nsa_kernel, tq=tq, tk=tk, topk=min(NSA_TOPK, ns), cmp_chunk=min(256, ncp)),
        grid=(b, s // tq),
        in_specs=[
            pl.BlockSpec((1, tq, w), lambda bi, i: (bi, i, COL_NSA_Q // w)),
            _resident((1, s, 2 * LANES), lambda bi, i: (bi, 0, COL_NSA_K // (2 * LANES))),
            _resident((2 * LANES, s), lambda bi, i: (0, bi)),
            _resident((1, ncp, LANES), lambda bi, i: (bi, 0, 0)),
            _resident((1, LANES, ncp), lambda bi, i: (bi, 0, 0)),
            const(c2st), const(pf), const(cpf), const(sf),
            pl.BlockSpec((1, tq, LANES), lambda bi, i: (bi, i, 2)),
        ],
        out_specs=pl.BlockSpec((1, tq, w), lambda bi, i: (bi, i, 0)),
        out_shape=jax.ShapeDtypeStruct((b, s, w), MXU_DTYPE),
        scratch_shapes=[pltpu.VMEM((1, NSA_HEADS * tq), F32),
                        pltpu.VMEM((HEAD_DIM + DEN_ROWS, NSA_HEADS * tq), F32),
                        pltpu.VMEM((ns, tq), F32), pltpu.VMEM((ns, LANES), F32),
                        pltpu.VMEM((HEAD_DIM, NSA_HEADS * tq), F32), pltpu.VMEM((ns, tq), F32)],
        compiler_params=_cparams(("parallel", "arbitrary")),
        name="nsa",
    )(h3, h3, vt, kc, vct, c2st, pf, cpf, sf, aux3)


def _cmp_to_sel_t(ncp, ns):
    nc = ncp - 1
    c = np.arange(ncp)[None, :]
    j = np.arange(ns)[:, None]
    start = c * NSA_CMP_STRIDE
    m = (start < (j + 1) * NSA_SEL_LEN) & (start + NSA_CMP_LEN - 1 >= j * NSA_SEL_LEN) & (c < nc)
    return jnp.asarray(m.astype(np.float32)).astype(MXU_DTYPE)


def _mixout_kernel(x_ref, mla_ref, d0_ref, d1_ref, d2_ref, l0_ref, l1_ref, l2_ref, nsa_ref, w_ref, o_ref):
    o_dil = (d0_ref[...], d1_ref[...], d2_ref[...])
    lse = (l0_ref[...], l1_ref[...], l2_ref[...])
    top = jnp.maximum(jnp.maximum(lse[0], lse[1]), lse[2])
    e = [jnp.exp(l - top) for l in lse]
    inv = 1.0 / (e[0] + e[1] + e[2])
    acc = x_ref[...] + _dot(mla_ref[...], w_ref[0:768, :])
    for g in range(3):
        mixed = (o_dil[g] * (e[g] * inv)).astype(MXU_DTYPE)
        acc += _dot(mixed, w_ref[768 + g * 256:768 + (g + 1) * 256, :])
    acc += _dot(nsa_ref[...], w_ref[1536:2048, :])
    o_ref[...] = acc


def _mixout(x2, o_mla, o_dil, lse_dil, o_nsa, w, *, tm=512):
    n, d = x2.shape
    row = lambda width: pl.BlockSpec((tm, width), lambda i: (i, 0))
    return pl.pallas_call(
        _mixout_kernel,
        grid=(n // tm,),
        in_specs=[row(d), row(o_mla.shape[1])] + [row(256)] * 6 + [row(o_nsa.shape[1]),
                  pl.BlockSpec(w.shape, lambda i: (0, 0))],
        out_specs=row(d),
        out_shape=jax.ShapeDtypeStruct((n, d), F32),
        compiler_params=_cparams(("parallel",)),
        name="mix_out",
    )(x2, o_mla, *o_dil, *lse_dil, o_nsa, w)


def kernel(x, ffn1_norm, ffn1_w_in, ffn1_w_out, mix_norm, w_mix_in, mla_q_norm, mla_w_uq, mla_kv_norm,
           mla_w_ukv, nsa_cmp_pos, nsa_phi_k1, nsa_phi_k2, nsa_phi_v1, nsa_phi_v2, w_mix_out, ffn2_norm,
           ffn2_w_in, ffn2_w_out, final_norm):
    b, s, d = x.shape
    depth = ffn1_w_in.shape[0]
    n = b * s
    invf = _rope_inv_freq_row()
    x2 = x.reshape(n, d)
    for l in range(depth):
        x2 = _ffn(x2, ffn1_norm[l], *_prep_ffn(ffn1_w_in[l], ffn1_w_out[l]))

        h2, hd2, aux2, nsa_vt = _mixin(x2, mix_norm[l], *_prep_mixin(w_mix_in[l]))
        h3 = h2.reshape(b, s, H_COLS)
        hd3 = hd2.reshape(b, s, DIL_COLS)
        aux3 = aux2.reshape(b, s, AUX_COLS)

        q, k, vt = _mla_proj(h2, mla_q_norm[l].reshape(1, -1), mla_kv_norm[l].reshape(1, -1),
                             *_prep_mla(mla_w_uq[l], mla_w_ukv[l]), invf, seq=s)
        o_mla = _mla_flash(q.reshape(b, s, -1), k.reshape(b, s, -1), vt)

        dil = [_dilated_group(hd3, g) for g in range(len(DIL_PAIRS))]

        kc, vct = _nsa_compress(aux3, *_prep_nsa_cmp(nsa_cmp_pos[l], nsa_phi_k1[l], nsa_phi_k2[l],
                                                      nsa_phi_v1[l], nsa_phi_v2[l]))
        o_nsa = _nsa(h3, aux3, nsa_vt, kc, vct)

        x2 = _mixout(x2, o_mla.reshape(n, -1), [o.reshape(n, -1) for o, _ in dil],
                     [e.reshape(n, -1) for _, e in dil], o_nsa.reshape(n, -1),
                     w_mix_out[l].astype(MXU_DTYPE))

        gf = final_norm if l == depth - 1 else None
        x2 = _ffn(x2, ffn2_norm[l], *_prep_ffn(ffn2_w_in[l], ffn2_w_out[l]), gf)
    return x2.reshape(b, s, d)
```

```python
import functools
import math

import numpy as np
import jax
import jax.numpy as jnp
from jax import lax
from jax.experimental import pallas as pl
from jax.experimental.pallas import tpu as pltpu

F32 = jnp.float32
MXU_DTYPE = jnp.bfloat16
VMEM_LIMIT_BYTES = 56 * 1024 * 1024
LANES = 128
LOG2E = math.log2(math.e)
FFN_TILE = 512
DEN_ROWS = 16
LAZY_MAX_SLACK = 16.0

HEAD_DIM = 128
RMS_EPS = 1e-6
NEG_INF = -1e30

MLA_HEADS = 6
MLA_LORA = 512
MLA_NOPE = 128
MLA_ROPE = 64
MLA_V = 128
MLA_QK_PAD = 256
ROPE_BASE = 10000.0

DIL_PAIRS = ((128, 1), (512, 4), (2048, 16))
DIL_HEADS = 6
DIL_BLOCK = 128

NSA_HEADS = 4
NSA_CMP_LEN = 32
NSA_CMP_STRIDE = 16
NSA_SEL_LEN = 64
NSA_TOPK = 16
NSA_WINDOW = 512
NSA_FORCED_SCORE = 100.0
NSA_Q_BLOCK = 256
POS_SPLIT = 128

N_ALIBI = DIL_HEADS + NSA_HEADS
ALIBI_SLOPES = tuple(float(2.0 ** (-8.0 * i / N_ALIBI)) for i in range(1, N_ALIBI + 1))

COL_QLAT = 0
COL_KVLAT = 512
COL_NSA_Q = 1024
COL_KROPE = 1536
COL_NSA_K = 1792
H_COLS = 2048
DIL_COLS = 2304
AUX_COLS = 384


def _cparams(sem):
    return pltpu.CompilerParams(dimension_semantics=sem, vmem_limit_bytes=VMEM_LIMIT_BYTES)


def _rms(x, g):
    ms = jnp.mean(x * x, axis=-1, keepdims=True)
    return (x * lax.rsqrt(ms + RMS_EPS)) * g


def _dot(a, b):
    return jnp.dot(a, b, preferred_element_type=F32)


def _shr(x, pow2):
    return lax.shift_right_logical(x, int(pow2).bit_length() - 1)


def _dot_t(a, b):
    return lax.dot_general(a, b, (((1,), (1,)), ((), ())), preferred_element_type=F32)


def _resident(shape, index_map):
    return pl.BlockSpec(shape, index_map, pipeline_mode=pl.Buffered(1))


def _ffn_kernel(x_ref, g_ref, wg_ref, wu_ref, wo_ref, *rest, final):
    if final:
        gf_ref, o_ref, xn_ref = rest
    else:
        o_ref, xn_ref = rest
    j = pl.program_id(1)

    @pl.when(j == 0)
    def _():
        x = x_ref[...]
        xn_ref[...] = _rms(x, g_ref[...]).astype(xn_ref.dtype)
        o_ref[...] = x

    xn = xn_ref[...]
    gate = _dot(xn, wg_ref[...])
    up = _dot(xn, wu_ref[...])
    h = (0.5 * gate) * jax.nn.sigmoid(gate) * up
    o_ref[...] += _dot(h.astype(MXU_DTYPE), wo_ref[...])

    if final:
        @pl.when(j == pl.num_programs(1) - 1)
        def _():
            o_ref[...] = _rms(o_ref[...], gf_ref[...])


def _ffn(x2, g, w_in, wo, gf=None, *, tm=1024, tf=FFN_TILE):
    n, d = x2.shape
    dffp = wo.shape[0]
    nff = dffp // tf
    tm = min(tm, n)
    final = gf is not None
    in_specs = [
        _resident((tm, d), lambda i, j: (i, 0)),
        pl.BlockSpec((1, d), lambda i, j: (0, 0)),
        pl.BlockSpec((d, tf), lambda i, j: (0, j)),
        pl.BlockSpec((d, tf), lambda i, j: (0, nff + j)),
        pl.BlockSpec((tf, d), lambda i, j: (j, 0)),
    ]
    args = [x2, g.reshape(1, d), w_in, w_in, wo]
    if final:
        in_specs.append(pl.BlockSpec((1, d), lambda i, j: (0, 0)))
        args.append(gf.reshape(1, d))
    return pl.pallas_call(
        functools.partial(_ffn_kernel, final=final),
        grid=(n // tm, nff),
        in_specs=in_specs,
        out_specs=pl.BlockSpec((tm, d), lambda i, j: (i, 0)),
        out_shape=jax.ShapeDtypeStruct((n, d), F32),
        scratch_shapes=[pltpu.VMEM((tm, d), MXU_DTYPE)],
        compiler_params=_cparams(("parallel", "arbitrary")),
        name="ffn_final" if final else "ffn",
    )(*args)


def _prep_ffn(w_in, w_out, tf=FFN_TILE):
    d, two_dff = w_in.shape
    dff = two_dff // 2
    dffp = -(-dff // tf) * tf
    z = jnp.zeros((d, dffp - dff), MXU_DTYPE)
    w = jnp.concatenate([w_in[:, :dff].astype(MXU_DTYPE), z, w_in[:, dff:].astype(MXU_DTYPE), z], axis=1)
    wo = jnp.pad(w_out, ((0, dffp - dff), (0, 0))).astype(MXU_DTYPE)
    return w, wo


def _mixin_kernel(x_ref, g_ref, wm_ref, wd_ref, wa_ref, wvt_ref, hm_ref, hd_ref, a_ref, vt_ref):
    xn = _rms(x_ref[...], g_ref[...]).astype(MXU_DTYPE)
    hm_ref[...] = _dot(xn, wm_ref[...]).astype(hm_ref.dtype)
    hd_ref[...] = _dot(xn, wd_ref[...]).astype(hd_ref.dtype)
    a_ref[...] = _dot(xn, wa_ref[...])
    vt_ref[...] = _dot_t(wvt_ref[...], xn).astype(vt_ref.dtype)


def _mixin(x2, g, wm, wd, wa, wvt, *, tm=512):
    n, d = x2.shape
    weight = lambda a: _resident(a.shape, lambda i: (0, 0))
    return pl.pallas_call(
        _mixin_kernel,
        grid=(n // tm,),
        in_specs=[
            pl.BlockSpec((tm, d), lambda i: (i, 0)),
            pl.BlockSpec((1, d), lambda i: (0, 0)),
            weight(wm), weight(wd), weight(wa), weight(wvt),
        ],
        out_specs=[
            pl.BlockSpec((tm, H_COLS), lambda i: (i, 0)),
            pl.BlockSpec((tm, DIL_COLS), lambda i: (i, 0)),
            pl.BlockSpec((tm, AUX_COLS), lambda i: (i, 0)),
            pl.BlockSpec((2 * LANES, tm), lambda i: (0, i)),
        ],
        out_shape=[
            jax.ShapeDtypeStruct((n, H_COLS), MXU_DTYPE),
            jax.ShapeDtypeStruct((n, DIL_COLS), MXU_DTYPE),
            jax.ShapeDtypeStruct((n, AUX_COLS), F32),
            jax.ShapeDtypeStruct((2 * LANES, n), MXU_DTYPE),
        ],
        compiler_params=_cparams(("parallel",)),
        name="mix_in",
    )(x2, g.reshape(1, d), wm, wd, wa, wvt)


def _rot_half_cols(w):
    half = w.shape[1] // 2
    return jnp.concatenate([-w[:, half:], w[:, :half]], axis=1)


def _prep_mixin(w):
    d = w.shape[0]
    o = 0
    parts = {}
    for name, width in (("q_lat", 512), ("kv_lat", 512), ("k_rope", 64), ("dq", 768), ("dk", 768),
                        ("dv", 768), ("nq", 512), ("nkc", 128), ("nvc", 128), ("nks", 128),
                        ("nvs", 128), ("nkw", 128), ("nvw", 128), ("ng", 12)):
        parts[name] = w[:, o:o + width]
        o += width
    z64 = jnp.zeros((d, 64), w.dtype)
    kr = parts["k_rope"]
    wm = jnp.concatenate([
        parts["q_lat"], parts["kv_lat"], parts["nq"],
        kr, z64, _rot_half_cols(kr), z64,
        parts["nks"], parts["nkw"]], axis=1).astype(MXU_DTYPE)
    wd = jnp.concatenate([parts["dq"], parts["dk"], parts["dv"]], axis=1).astype(MXU_DTYPE)
    wa = jnp.concatenate([parts["nkc"], parts["nvc"], parts["ng"],
                          jnp.zeros((d, LANES - 12), w.dtype)], axis=1).astype(MXU_DTYPE)
    wvt = jnp.concatenate([parts["nvs"], parts["nvw"]], axis=1).T.astype(MXU_DTYPE)
    return wm, wd, wa, wvt


def _mla_proj_kernel(lat_ref, kr_ref, qn_ref, kvn_ref, wq_ref, wk_ref, wvt_ref, invf_ref,
                     q_ref, k_ref, vt_ref, *, seq, tm):
    i = pl.program_id(0)
    pos0 = lax.rem(i * tm, seq)
    pos = (pos0 + lax.broadcasted_iota(jnp.int32, (tm, 1), 0)).astype(F32)
    ang = pos * invf_ref[...]
    cos = jnp.cos(ang)
    sin = jnp.sin(ang)
    scale = (MLA_NOPE + MLA_ROPE) ** -0.5 * LOG2E

    lat = lat_ref[...].astype(F32)
    qn = _rms(lat[:, :MLA_LORA], qn_ref[...]).astype(MXU_DTYPE)
    kvn = _rms(lat[:, MLA_LORA:], kvn_ref[...]).astype(MXU_DTYPE)
    qm = _dot(qn, wq_ref[...])
    rot0 = MLA_HEADS * MLA_QK_PAD
    for h in range(MLA_HEADS):
        c = h * MLA_QK_PAD
        nope = qm[:, c:c + LANES]
        pe = qm[:, c + LANES:c + 2 * LANES] * cos + qm[:, rot0 + h * LANES:rot0 + (h + 1) * LANES] * sin
        q_ref[:, c:c + LANES] = (nope * scale).astype(q_ref.dtype)
        q_ref[:, c + LANES:c + 2 * LANES] = (pe * scale).astype(q_ref.dtype)

    kr = kr_ref[...].astype(F32)
    kpe = (kr[:, :LANES] * cos + kr[:, LANES:] * sin).astype(k_ref.dtype)
    kn = _dot(kvn, wk_ref[...])
    for h in range(MLA_HEADS):
        c = h * MLA_QK_PAD
        k_ref[:, c:c + LANES] = kn[:, h * LANES:(h + 1) * LANES].astype(k_ref.dtype)
        k_ref[:, c + LANES:c + 2 * LANES] = kpe
    vt_ref[...] = _dot_t(wvt_ref[...], kvn).astype(vt_ref.dtype)


def _mla_proj(h2, qn, kvn, wq, wk, wvt, invf, *, seq, tm=512):
    n = h2.shape[0]
    qk_cols = MLA_HEADS * MLA_QK_PAD
    v_rows = MLA_HEADS * MLA_V
    full = lambda a: pl.BlockSpec(a.shape, lambda i: (0,) * a.ndim)
    return pl.pallas_call(
        functools.partial(_mla_proj_kernel, seq=seq, tm=tm),
        grid=(n // tm,),
        in_specs=[
            pl.BlockSpec((tm, 2 * MLA_LORA), lambda i: (i, 0)),
            pl.BlockSpec((tm, 2 * LANES), lambda i: (i, COL_KROPE // (2 * LANES))),
            full(qn), full(kvn), full(wq), full(wk), full(wvt), full(invf),
        ],
        out_specs=[
            pl.BlockSpec((tm, qk_cols), lambda i: (i, 0)),
            pl.BlockSpec((tm, qk_cols), lambda i: (i, 0)),
            pl.BlockSpec((v_rows, tm), lambda i: (0, i)),
        ],
        out_shape=[
            jax.ShapeDtypeStruct((n, qk_cols), MXU_DTYPE),
            jax.ShapeDtypeStruct((n, qk_cols), MXU_DTYPE),
            jax.ShapeDtypeStruct((v_rows, n), MXU_DTYPE),
        ],
        compiler_params=_cparams(("parallel",)),
        name="mla_proj",
    )(h2, h2, qn, kvn, wq, wk, wvt, invf)


def _prep_mla(w_uq, w_ukv):
    r = MLA_LORA
    wq3 = w_uq.reshape(r, MLA_HEADS, MLA_NOPE + MLA_ROPE)
    nope, pe = wq3[..., :MLA_NOPE], wq3[..., MLA_NOPE:]
    z = jnp.zeros((r, MLA_HEADS, 64), w_uq.dtype)
    main = jnp.concatenate([nope, pe, z], axis=-1).reshape(r, MLA_HEADS * MLA_QK_PAD)
    half = MLA_ROPE // 2
    pe_rot = jnp.concatenate([-pe[..., half:], pe[..., :half]], axis=-1)
    rot = jnp.concatenate([pe_rot, z], axis=-1).reshape(r, MLA_HEADS * LANES)
    wq = jnp.concatenate([main, rot], axis=1).astype(MXU_DTYPE)
    wkv3 = w_ukv.reshape(r, MLA_HEADS, MLA_NOPE + MLA_V)
    wk = wkv3[..., :MLA_NOPE].reshape(r, MLA_HEADS * MLA_NOPE).astype(MXU_DTYPE)
    wvt = wkv3[..., MLA_NOPE:].reshape(r, MLA_HEADS * MLA_V).T.astype(MXU_DTYPE)
    return wq, wk, wvt


def _rope_inv_freq_row():
    half = MLA_ROPE // 2
    f = ROPE_BASE ** (-jnp.arange(half, dtype=F32) / half)
    return jnp.concatenate([f, f, jnp.zeros((LANES - MLA_ROPE,), F32)]).reshape(1, LANES)


def _with_ones_rows(v_t):
    return jnp.concatenate([v_t, jnp.ones((DEN_ROWS, v_t.shape[1]), v_t.dtype)], axis=0)


def _softmax_piece_t(scores, v_aug, m_ref, acc_ref, *, first):
    if first:
        s_t = scores()
        m_new = jnp.max(s_t, axis=0, keepdims=True)
        acc_ref[...] = _dot(v_aug, jnp.exp2(s_t - m_new).astype(MXU_DTYPE))
        m_ref[...] = m_new
        return

    m_old = m_ref[...]
    s_t = scores()
    pv = _dot(v_aug, jnp.exp2(s_t - m_old).astype(MXU_DTYPE))
    fits = jnp.max(jnp.max(s_t, axis=0, keepdims=True) - m_old) <= LAZY_MAX_SLACK

    @pl.when(fits)
    def _():
        acc_ref[...] += pv

    @pl.when(jnp.logical_not(fits))
    def _():
        s_again = scores()
        m_new = jnp.maximum(m_old, jnp.max(s_again, axis=0, keepdims=True))
        weights = jnp.exp2(s_again - m_new).astype(MXU_DTYPE)
        acc_ref[...] = jnp.exp2(m_old - m_new) * acc_ref[...] + _dot(v_aug, weights)
        m_ref[...] = m_new


def _softmax_result(acc_scr, dv):
    return acc_scr[0:dv, :] * (1.0 / acc_scr[dv:dv + 1, :])


def _flash_kernel(qt_ref, kt_ref, q_ref, k_ref, vt_ref, o_ref, m_scr, acc_scr, *, t, ksub, qsub):
    assert ksub == qsub
    step = pl.program_id(2)
    qi = qt_ref[step]
    ki = kt_ref[step]

    def piece(kp, qp, diagonal):
        k0, q0 = kp * ksub, qp * qsub
        ks = slice(k0, k0 + ksub)
        qs = slice(q0, q0 + qsub)

        def scores():
            s_t = _dot_t(k_ref[0, ks, :], q_ref[0, qs, :])
            if diagonal and k0 + ksub - 1 > q0:
                kpos = k0 + lax.broadcasted_iota(jnp.int32, (ksub, 1), 0)
                qpos = q0 + lax.broadcasted_iota(jnp.int32, (1, qsub), 1)
                s_t = jnp.where(kpos <= qpos, s_t, NEG_INF)
            return s_t

        _softmax_piece_t(scores, _with_ones_rows(vt_ref[:, ks]), m_scr.at[:, qs], acc_scr.at[:, qs],
                         first=diagonal and kp == qp)

    n = t // ksub

    @pl.when(ki == qi)
    def _():
        for p in range(n):
            piece(p, p, True)
        for kp in range(n):
            for qp in range(kp + 1, n):
                piece(kp, qp, True)

    @pl.when(ki < qi)
    def _():
        for kp in range(n):
            for qp in range(n):
                piece(kp, qp, False)

    @pl.when(ki == 0)
    def _():
        o_ref[0] = _softmax_result(acc_scr, MLA_V).T.astype(o_ref.dtype)


def _mla_flash(q3, k3, vt, *, t=2048, ksub=1024, qsub=1024):
    b, s, _ = q3.shape
    t = min(t, s)
    ksub, qsub = min(ksub, t), min(qsub, t)
    nq = s // t
    pairs = [(qi, ki) for qi in range(nq) for ki in range(qi, -1, -1)]
    qt = jnp.asarray(np.array([p[0] for p in pairs], np.int32))
    kt = jnp.asarray(np.array([p[1] for p in pairs], np.int32))
    grid_spec = pltpu.PrefetchScalarGridSpec(
        num_scalar_prefetch=2,
        grid=(b, MLA_HEADS, len(pairs)),
        in_specs=[
            pl.BlockSpec((1, t, MLA_QK_PAD), lambda bi, h, st, qt, kt: (bi, qt[st], h)),
            pl.BlockSpec((1, t, MLA_QK_PAD), lambda bi, h, st, qt, kt: (bi, kt[st], h)),
            pl.BlockSpec((MLA_V, t), lambda bi, h, st, qt, kt: (h, bi * nq + kt[st])),
        ],
        out_specs=pl.BlockSpec((1, t, MLA_V), lambda bi, h, st, qt, kt: (bi, qt[st], h)),
        scratch_shapes=[pltpu.VMEM((1, t), F32), pltpu.VMEM((MLA_V + DEN_ROWS, t), F32)],
    )
    return pl.pallas_call(
        functools.partial(_flash_kernel, t=t, ksub=ksub, qsub=qsub),
        grid_spec=grid_spec,
        out_shape=jax.ShapeDtypeStruct((b, s, MLA_HEADS * MLA_V), MXU_DTYPE),
        compiler_params=_cparams(("parallel", "parallel", "arbitrary")),
        name="mla_flash",
    )(qt, kt, q3, k3, vt)


def _dil_kernel(q_ref, kc_ref, kp_ref, vc_ref, vp_ref, o_ref, lse_ref, *, dilation, span, slopes, tl):
    n = pl.program_id(2)
    scale = HEAD_DIM ** -0.5
    blk = DIL_BLOCK
    a = lax.broadcasted_iota(jnp.int32, (blk, 1), 0)
    c = lax.broadcasted_iota(jnp.int32, (1, 2 * blk), 1)
    j = blk + a - c
    in_band = (j >= 0) & (j <= span)
    first_valid = in_band & ((c >= blk) | (n > 0))
    dist = (j * dilation).astype(F32)
    for hg in range(2):
        cols = slice(hg * LANES, (hg + 1) * LANES)
        bias = slopes[hg] * dist
        for sb in range(tl // blk):
            rows = slice(sb * blk, (sb + 1) * blk)
            q = (q_ref[0, rows, cols].astype(F32) * scale).astype(MXU_DTYPE)
            if sb == 0:
                kprev, vprev, valid = kp_ref[0, :, cols], vp_ref[0, :, cols], first_valid
            else:
                prev = slice((sb - 1) * blk, sb * blk)
                kprev, vprev, valid = kc_ref[0, prev, cols], vc_ref[0, prev, cols], in_band
            keys = jnp.concatenate([kprev, kc_ref[0, rows, cols]], axis=0)
            vals = jnp.concatenate([vprev, vc_ref[0, rows, cols]], axis=0)
            s = jnp.where(valid, _dot_t(q, keys) - bias, NEG_INF)
            m = jnp.max(s, axis=1, keepdims=True)
            e = jnp.where(valid, jnp.exp(s - m), 0.0)
            den = jnp.sum(e, axis=1, keepdims=True)
            o_ref[0, rows, cols] = _dot((e / den).astype(MXU_DTYPE), vals)
            lse_ref[0, rows, cols] = jnp.broadcast_to(m + jnp.log(den), (blk, LANES))


def _dilated_group(hd3, g, *, tl=512):
    b, s, _ = hd3.shape
    window, dilation = DIL_PAIRS[g]
    span = window // dilation
    seq_l = s // dilation
    tl = min(tl, seq_l)
    per_blk = tl // DIL_BLOCK
    w = 2 * LANES
    group_cols = [t * DIL_HEADS * HEAD_DIM + g * w for t in range(3)]
    if dilation == 1:
        hv = hd3
        qc, kc, vc = (c // w for c in group_cols)
    else:
        hv = jnp.concatenate([hd3[:, :, c:c + w] for c in group_cols], axis=2)
        hv = hv.reshape(b, seq_l, dilation * 3 * w)
        qc, kc, vc = 0, 1, 2
    stride = hv.shape[2] // (dilation * w)
    cur = lambda col: pl.BlockSpec((1, tl, w), lambda bi, r, n: (bi, n, r * stride + col))
    prev = lambda col: pl.BlockSpec(
        (1, DIL_BLOCK, w), lambda bi, r, n: (bi, jnp.maximum(n * per_blk - 1, 0), r * stride + col))
    out_spec = pl.BlockSpec((1, tl, w), lambda bi, r, n: (bi, n, r))
    out_sds = jax.ShapeDtypeStruct((b, seq_l, dilation * w), F32)
    o, lse = pl.pallas_call(
        functools.partial(_dil_kernel, dilation=dilation, span=span,
                          slopes=ALIBI_SLOPES[2 * g:2 * g + 2], tl=tl),
        grid=(b, dilation, seq_l // tl),
        in_specs=[cur(qc), cur(kc), prev(kc), cur(vc), prev(vc)],
        out_specs=[out_spec, out_spec],
        out_shape=[out_sds, out_sds],
        compiler_params=_cparams(("parallel", "parallel", "parallel")),
        name=f"dilated_g{g}",
    )(hv, hv, hv, hv, hv)
    return o.reshape(b, s, w), lse.reshape(b, s, w)


def _nsa_cmp_kernel(x_ref, pos_ref, wa_ref, wb_ref, w2k_ref, w2vt_ref, kc_ref, vct_ref):
    x = x_ref[0]
    xa = (x + pos_ref[0:1, :]).astype(MXU_DTYPE)
    xb = (x + pos_ref[1:2, :]).astype(MXU_DTYPE)
    first = _dot(xa, wa_ref[...])
    second = _dot(xb, wb_ref[...])
    second_next = pltpu.roll(second, x.shape[0] - 1, 0)
    pre = first + second_next
    hid = (pre * jax.nn.sigmoid(pre)).astype(MXU_DTYPE)
    kc_ref[0] = _dot(hid[:, :LANES], w2k_ref[...]).astype(kc_ref.dtype)
    vct_ref[0] = _dot_t(w2vt_ref[...], hid[:, LANES:]).astype(vct_ref.dtype)


def _nsa_compress(aux3, pos2, wa, wb, w2k, w2vt):
    b, s, _ = aux3.shape
    nchunk = s // NSA_CMP_STRIDE
    xk = aux3[:, :, :2 * LANES].reshape(b, nchunk, NSA_CMP_STRIDE * 2 * LANES)
    full = lambda a: pl.BlockSpec(a.shape, lambda bi: (0,) * a.ndim)
    return pl.pallas_call(
        _nsa_cmp_kernel,
        grid=(b,),
        in_specs=[pl.BlockSpec((1, nchunk, xk.shape[2]), lambda bi: (bi, 0, 0)),
                  full(pos2), full(wa), full(wb), full(w2k), full(w2vt)],
        out_specs=[pl.BlockSpec((1, nchunk, LANES), lambda bi: (bi, 0, 0)),
                   pl.BlockSpec((1, LANES, nchunk), lambda bi: (bi, 0, 0))],
        out_shape=[jax.ShapeDtypeStruct((b, nchunk, LANES), MXU_DTYPE),
                   jax.ShapeDtypeStruct((b, LANES, nchunk), MXU_DTYPE)],
        compiler_params=_cparams(("parallel",)),
        name="nsa_cmp",
    )(xk, pos2, wa, wb, w2k, w2vt)


def _prep_nsa_cmp(cmp_pos, phi_k1, phi_k2, phi_v1, phi_v2):
    half = NSA_CMP_LEN // 2
    pk = cmp_pos.reshape(2, half, HEAD_DIM)
    pos2 = jnp.concatenate([pk, pk], axis=-1).reshape(2, half * 2 * LANES)

    def halves(w1, is_v):
        w = w1.reshape(2, half, HEAD_DIM, HEAD_DIM)
        z = jnp.zeros_like(w)
        w = jnp.concatenate([z, w] if is_v else [w, z], axis=2)
        return w.reshape(2, half * 2 * LANES, HEAD_DIM)

    wk, wv = halves(phi_k1, False), halves(phi_v1, True)
    wa = jnp.concatenate([wk[0], wv[0]], axis=1).astype(MXU_DTYPE)
    wb = jnp.concatenate([wk[1], wv[1]], axis=1).astype(MXU_DTYPE)
    return pos2, wa, wb, phi_k2.astype(MXU_DTYPE), phi_v2.T.astype(MXU_DTYPE)


def _split3(x):
    hi = x.astype(MXU_DTYPE)
    r = x - hi.astype(F32)
    mid = r.astype(MXU_DTYPE)
    lo = (r - mid.astype(F32)).astype(MXU_DTYPE)
    return hi, mid, lo


def _pos_features(pos):
    hi = jnp.floor(pos / POS_SPLIT) * POS_SPLIT
    lo = pos - hi
    cols = jnp.stack([hi, hi, hi, lo, lo, lo], axis=1)
    return jnp.pad(cols, ((0, 0), (0, LANES - 6))).astype(MXU_DTYPE)


def _slope_features(tq):
    sig = jnp.asarray([s * LOG2E for s in ALIBI_SLOPES[DIL_HEADS:]], F32)
    pieces = jnp.stack(_split3(sig), axis=1)
    rows = jnp.concatenate([pieces, pieces], axis=1)
    rows = jnp.pad(rows, ((0, 0), (0, LANES - 6)))
    return jnp.repeat(rows, tq, axis=0)


def _masked_softmax_t(s_t, mask):
    s_t = jnp.where(mask, s_t, NEG_INF)
    m = jnp.max(s_t, axis=0, keepdims=True)
    e = jnp.exp2(s_t - m)
    den = jnp.sum(e, axis=0, keepdims=True)
    return e * jnp.where(m > 0.5 * NEG_INF, 1.0 / den, 0.0)


def _nsa_kernel(q_ref, k_ref, vt_ref, kc_ref, vct_ref, c2st_ref, pf_ref, cpf_ref, sf_ref, g_ref,
                o_ref, m_scr, acc_scr, bias_scr, any_scr, ocmp_scr, score_scr,
                *, tq, tk, topk, cmp_chunk):
    i = pl.program_id(1)
    t0 = i * tq
    nh = NSA_HEADS
    cols = nh * tq
    ncp = kc_ref.shape[1]
    ns = c2st_ref.shape[0]
    blocks_per_tile = tk // NSA_SEL_LEN

    q = q_ref[0]
    q4 = jnp.concatenate([q[:, h * LANES:(h + 1) * LANES] for h in range(nh)], axis=0)
    q4 = (q4.astype(F32) * (HEAD_DIM ** -0.5 * LOG2E)).astype(MXU_DTYPE)
    q4 = jnp.concatenate([q4, sf_ref[...]], axis=1)
    col = lax.broadcasted_iota(jnp.int32, (1, cols), 1)
    tpos = (t0 + (col & (tq - 1))).astype(F32)

    def cmp_part(rows):
        cidx = lax.broadcasted_iota(jnp.int32, (rows, 1), 0).astype(F32)
        c_end = cidx * NSA_CMP_STRIDE + (NSA_CMP_LEN - 1)
        kc = jnp.concatenate([kc_ref[0, 0:rows, :], cpf_ref[0:rows, :]], axis=1)
        p_cmp = _masked_softmax_t(_dot_t(kc, q4), c_end <= tpos)
        ocmp_scr[...] = _dot(vct_ref[0, :, 0:rows], p_cmp.astype(MXU_DTYPE))
        p_sum = p_cmp[:, 0:tq]
        for h in range(1, nh):
            p_sum = p_sum + p_cmp[:, h * tq:(h + 1) * tq]
        c2st = c2st_ref[:, 0:rows]
        if MXU_DTYPE == jnp.float32:
            score_scr[...] = _dot(c2st, p_sum)
        else:
            score_scr[...] = sum(_dot(c2st, piece) for piece in _split3(p_sum))

    n_variants = -(-ncp // cmp_chunk)
    n_ending = (t0 + tq) // NSA_CMP_STRIDE - 1
    need = jnp.clip((n_ending + cmp_chunk - 1) // cmp_chunk, 1, n_variants)
    for v in range(1, n_variants + 1):
        pl.when(need == v)(functools.partial(cmp_part, min(v * cmp_chunk, ncp)))
    o_cmp = ocmp_scr[...]
    score = score_scr[...]

    t1 = t0 + lax.broadcasted_iota(jnp.int32, (1, tq), 1)
    cur = _shr(t1, NSA_SEL_LEN).astype(F32)
    jj = lax.broadcasted_iota(jnp.int32, (ns, 1), 0).astype(F32)
    forced = (jj == 0.0) | (jj == cur) | (jj == cur - 1.0)
    score = jnp.where(jj > cur, -1.0, jnp.where(forced, NSA_FORCED_SCORE, score))
    bias = jnp.full((ns, tq), NEG_INF, F32)
    for _ in range(topk):
        best = jnp.max(score, axis=0, keepdims=True)
        idx = jnp.min(jnp.where(score == best, jj, float(ns)), axis=0, keepdims=True)
        hit = jj == idx
        bias = jnp.where(hit, 0.0, bias)
        score = jnp.where(hit, -2.0, score)
    bias_scr[...] = bias
    any_scr[...] = jnp.broadcast_to(jnp.max(bias, axis=1, keepdims=True), (ns, LANES))

    def slc_update(kt, causal):
        k0 = pl.multiple_of(kt * tk, tk)
        b0 = pl.multiple_of(kt * blocks_per_tile, blocks_per_tile)

        def scores():
            keys = jnp.concatenate([k_ref[0, pl.ds(k0, tk), 0:LANES], pf_ref[pl.ds(k0, tk), :]], axis=1)
            s_t = _dot_t(keys, q4)
            rows = [jnp.broadcast_to(bias_scr[pl.ds(b0 + r, 1), :], (NSA_SEL_LEN, tq))
                    for r in range(blocks_per_tile)]
            sel_bias = jnp.concatenate(rows, axis=0)
            s_t = s_t + jnp.concatenate([sel_bias] * nh, axis=1)
            if causal:
                kpos = (k0 + lax.broadcasted_iota(jnp.int32, (tk, 1), 0)).astype(F32)
                s_t = jnp.where(kpos <= tpos, s_t, NEG_INF)
            return s_t

        _softmax_piece_t(scores, _with_ones_rows(vt_ref[0:LANES, pl.ds(k0, tk)]), m_scr, acc_scr,
                         first=causal)

    last = t0 // tk
    slc_update(last, True)

    def slc_tile(back, carry):
        kt = last - 1 - back
        b0 = pl.multiple_of(kt * blocks_per_tile, blocks_per_tile)
        touched = jnp.max(any_scr[pl.ds(b0, blocks_per_tile), :]) == 0.0
        pl.when(touched)(functools.partial(slc_update, kt, False))
        return carry

    lax.fori_loop(0, last, slc_tile, 0)
    o_slc = _softmax_result(acc_scr, HEAD_DIM)

    wlen = NSA_WINDOW + tq
    ws = pl.multiple_of(jnp.maximum(t0 - NSA_WINDOW, 0), tq)
    keys = jnp.concatenate([k_ref[0, pl.ds(ws, wlen), LANES:2 * LANES], pf_ref[pl.ds(ws, wlen), :]], axis=1)
    wpos = (ws + lax.broadcasted_iota(jnp.int32, (wlen, 1), 0)).astype(F32)
    dist = tpos - wpos
    p_win = _masked_softmax_t(_dot_t(keys, q4), (dist >= 0.0) & (dist < float(NSA_WINDOW)))
    o_win = _dot(vt_ref[LANES:2 * LANES, pl.ds(ws, wlen)], p_win.astype(MXU_DTYPE))

    gates = jax.nn.sigmoid(g_ref[0]).T
    for h in range(nh):
        c = slice(h * tq, (h + 1) * tq)
        o = (gates[3 * h:3 * h + 1, :] * o_cmp[:, c] + gates[3 * h + 1:3 * h + 2, :] * o_slc[:, c]
             + gates[3 * h + 2:3 * h + 3, :] * o_win[:, c])
        o_ref[0, :, h * LANES:(h + 1) * LANES] = o.T.astype(o_ref.dtype)


def _nsa(h3, aux3, vt, kc, vct, *, tq=NSA_Q_BLOCK, tk=512):
    b, s, _ = h3.shape
    ncp = kc.shape[1]
    ns = s // NSA_SEL_LEN
    tk = min(tk, s)
    w = NSA_HEADS * HEAD_DIM
    c2st = _cmp_to_sel_t(ncp, ns)
    pf = _pos_features(jnp.arange(s, dtype=F32))
    cpf = _pos_features(jnp.arange(ncp, dtype=F32) * NSA_CMP_STRIDE + 0.5 * (NSA_CMP_LEN - 1))
    sf = _slope_features(tq).astype(MXU_DTYPE)
    const = lambda a: _resident(a.shape, lambda bi, i: (0,) * a.ndim)
    return pl.pallas_call(
        functools.partial(_nsa_kernel, tq=tq, tk=tk, topk=min(NSA_TOPK, ns), cmp_chunk=min(256, ncp)),
        grid=(b, s // tq),
        in_specs=[
            pl.BlockSpec((1, tq, w), lambda bi, i: (bi, i, COL_NSA_Q // w)),
            _resident((1, s, 2 * LANES), lambda bi, i: (bi, 0, COL_NSA_K // (2 * LANES))),
            _resident((2 * LANES, s), lambda bi, i: (0, bi)),
            _resident((1, ncp, LANES), lambda bi, i: (bi, 0, 0)),
            _resident((1, LANES, ncp), lambda bi, i: (bi, 0, 0)),
            const(c2st), const(pf), const(cpf), const(sf),
            pl.BlockSpec((1, tq, LANES), lambda bi, i: (bi, i, 2)),
        ],
        out_specs=pl.BlockSpec((1, tq, w), lambda bi, i: (bi, i, 0)),
        out_shape=jax.ShapeDtypeStruct((b, s, w), MXU_DTYPE),
        scratch_shapes=[pltpu.VMEM((1, NSA_HEADS * tq), F32),
                        pltpu.VMEM((HEAD_DIM + DEN_ROWS, NSA_HEADS * tq), F32),
                        pltpu.VMEM((ns, tq), F32), pltpu.VMEM((ns, LANES), F32),
                        pltpu.VMEM((HEAD_DIM, NSA_HEADS * tq), F32), pltpu.VMEM((ns, tq), F32)],
        compiler_params=_cparams(("parallel", "arbitrary")),
        name="nsa",
    )(h3, h3, vt, kc, vct, c2st, pf, cpf, sf, aux3)


def _cmp_to_sel_t(ncp, ns):
    nc = ncp - 1
    c = np.arange(ncp)[None, :]
    j = np.arange(ns)[:, None]
    start = c * NSA_CMP_STRIDE
    m = (start < (j + 1) * NSA_SEL_LEN) & (start + NSA_CMP_LEN - 1 >= j * NSA_SEL_LEN) & (c < nc)
    return jnp.asarray(m.astype(np.float32)).astype(MXU_DTYPE)


def _mixout_kernel(x_ref, mla_ref, d0_ref, d1_ref, d2_ref, l0_ref, l1_ref, l2_ref, nsa_ref, w_ref, o_ref):
    o_dil = (d0_ref[...], d1_ref[...], d2_ref[...])
    lse = (l0_ref[...], l1_ref[...], l2_ref[...])
    top = jnp.maximum(jnp.maximum(lse[0], lse[1]), lse[2])
    e = [jnp.exp(l - top) for l in lse]
    inv = 1.0 / (e[0] + e[1] + e[2])
    acc = x_ref[...] + _dot(mla_ref[...], w_ref[0:768, :])
    for g in range(3):
        mixed = (o_dil[g] * (e[g] * inv)).astype(MXU_DTYPE)
        acc += _dot(mixed, w_ref[768 + g * 256:768 + (g + 1) * 256, :])
    acc += _dot(nsa_ref[...], w_ref[1536:2048, :])
    o_ref[...] = acc


def _mixout(x2, o_mla, o_dil, lse_dil, o_nsa, w, *, tm=512):
    n, d = x2.shape
    row = lambda width: pl.BlockSpec((tm, width), lambda i: (i, 0))
    return pl.pallas_call(
        _mixout_kernel,
        grid=(n // tm,),
        in_specs=[row(d), row(o_mla.shape[1])] + [row(256)] * 6 + [row(o_nsa.shape[1]),
                  pl.BlockSpec(w.shape, lambda i: (0, 0))],
        out_specs=row(d),
        out_shape=jax.ShapeDtypeStruct((n, d), F32),
        compiler_params=_cparams(("parallel",)),
        name="mix_out",
    )(x2, o_mla, *o_dil, *lse_dil, o_nsa, w)


def kernel(x, ffn1_norm, ffn1_w_in, ffn1_w_out, mix_norm, w_mix_in, mla_q_norm, mla_w_uq, mla_kv_norm,
           mla_w_ukv, nsa_cmp_pos, nsa_phi_k1, nsa_phi_k2, nsa_phi_v1, nsa_phi_v2, w_mix_out, ffn2_norm,
           ffn2_w_in, ffn2_w_out, final_norm):
    b, s, d = x.shape
    depth = ffn1_w_in.shape[0]
    n = b * s
    invf = _rope_inv_freq_row()
    x2 = x.reshape(n, d)
    for l in range(depth):
        x2 = _ffn(x2, ffn1_norm[l], *_prep_ffn(ffn1_w_in[l], ffn1_w_out[l]))

        h2, hd2, aux2, nsa_vt = _mixin(x2, mix_norm[l], *_prep_mixin(w_mix_in[l]))
        h3 = h2.reshape(b, s, H_COLS)
        hd3 = hd2.reshape(b, s, DIL_COLS)
        aux3 = aux2.reshape(b, s, AUX_COLS)

        q, k, vt = _mla_proj(h2, mla_q_norm[l].reshape(1, -1), mla_kv_norm[l].reshape(1, -1),
                             *_prep_mla(mla_w_uq[l], mla_w_ukv[l]), invf, seq=s)
        o_mla = _mla_flash(q.reshape(b, s, -1), k.reshape(b, s, -1), vt)

        dil = [_dilated_group(hd3, g) for g in range(len(DIL_PAIRS))]

        kc, vct = _nsa_compress(aux3, *_prep_nsa_cmp(nsa_cmp_pos[l], nsa_phi_k1[l], nsa_phi_k2[l],
                                                      nsa_phi_v1[l], nsa_phi_v2[l]))
        o_nsa = _nsa(h3, aux3, nsa_vt, kc, vct)

        x2 = _mixout(x2, o_mla.reshape(n, -1), [o.reshape(n, -1) for o, _ in dil],
                     [e.reshape(n, -1) for _, e in dil], o_nsa.reshape(n, -1),
                     w_mix_out[l].astype(MXU_DTYPE))

        gf = final_norm if l == depth - 1 else None
        x2 = _ffn(x2, ffn2_norm[l], *_prep_ffn(ffn2_w_in[l], ffn2_w_out[l]), gf)
    return x2.reshape(b, s, d)
```

```python
import functools
import math

import numpy as np
import jax
import jax.numpy as jnp
from jax import lax
from jax.experimental import pallas as pl
from jax.experimental.pallas import tpu as pltpu

F32 = jnp.float32
MXU_DTYPE = jnp.bfloat16
VMEM_LIMIT_BYTES = 56 * 1024 * 1024
LANES = 128
LOG2E = math.log2(math.e)
FFN_TILE = 512
DEN_ROWS = 16
LAZY_MAX_SLACK = 16.0

HEAD_DIM = 128
RMS_EPS = 1e-6
NEG_INF = -1e30

MLA_HEADS = 6
MLA_LORA = 512
MLA_NOPE = 128
MLA_ROPE = 64
MLA_V = 128
MLA_QK_PAD = 256
ROPE_BASE = 10000.0

DIL_PAIRS = ((128, 1), (512, 4), (2048, 16))
DIL_HEADS = 6
DIL_BLOCK = 128

NSA_HEADS = 4
NSA_CMP_LEN = 32
NSA_CMP_STRIDE = 16
NSA_SEL_LEN = 64
NSA_TOPK = 16
NSA_WINDOW = 512
NSA_FORCED_SCORE = 100.0
NSA_Q_BLOCK = 256
POS_SPLIT = 128

N_ALIBI = DIL_HEADS + NSA_HEADS
ALIBI_SLOPES = tuple(float(2.0 ** (-8.0 * i / N_ALIBI)) for i in range(1, N_ALIBI + 1))

COL_QLAT = 0
COL_KVLAT = 512
COL_NSA_Q = 1024
COL_KROPE = 1536
COL_NSA_K = 1792
H_COLS = 2048
DIL_GROUP_COLS = 768
AUX_COLS = 384


def _cparams(sem):
    return pltpu.CompilerParams(dimension_semantics=sem, vmem_limit_bytes=VMEM_LIMIT_BYTES)


def _rms(x, g):
    ms = jnp.mean(x * x, axis=-1, keepdims=True)
    return (x * lax.rsqrt(ms + RMS_EPS)) * g


def _dot(a, b):
    return jnp.dot(a, b, preferred_element_type=F32)


def _shr(x, pow2):
    return lax.shift_right_logical(x, int(pow2).bit_length() - 1)


def _dot_t(a, b):
    return lax.dot_general(a, b, (((1,), (1,)), ((), ())), preferred_element_type=F32)


def _resident(shape, index_map):
    return pl.BlockSpec(shape, index_map, pipeline_mode=pl.Buffered(1))


def _ffn_kernel(x_ref, g_ref, wg_ref, wu_ref, wo_ref, *rest, final):
    if final:
        gf_ref, o_ref, xn_ref = rest
    else:
        o_ref, xn_ref = rest
    j = pl.program_id(1)

    @pl.when(j == 0)
    def _():
        x = x_ref[...]
        xn_ref[...] = _rms(x, g_ref[...]).astype(xn_ref.dtype)
        o_ref[...] = x

    xn = xn_ref[...]
    gate = _dot(xn, wg_ref[...])
    up = _dot(xn, wu_ref[...])
    h = (0.5 * gate) * jax.nn.sigmoid(gate) * up
    o_ref[...] += _dot(h.astype(MXU_DTYPE), wo_ref[...])

    if final:
        @pl.when(j == pl.num_programs(1) - 1)
        def _():
            o_ref[...] = _rms(o_ref[...], gf_ref[...])


def _ffn(x2, g, w_in, wo, gf=None, *, tm=1024, tf=FFN_TILE):
    n, d = x2.shape
    dffp = wo.shape[0]
    nff = dffp // tf
    tm = min(tm, n)
    final = gf is not None
    in_specs = [
        _resident((tm, d), lambda i, j: (i, 0)),
        pl.BlockSpec((1, d), lambda i, j: (0, 0)),
        pl.BlockSpec((d, tf), lambda i, j: (0, j)),
        pl.BlockSpec((d, tf), lambda i, j: (0, nff + j)),
        pl.BlockSpec((tf, d), lambda i, j: (j, 0)),
    ]
    args = [x2, g.reshape(1, d), w_in, w_in, wo]
    if final:
        in_specs.append(pl.BlockSpec((1, d), lambda i, j: (0, 0)))
        args.append(gf.reshape(1, d))
    return pl.pallas_call(
        functools.partial(_ffn_kernel, final=final),
        grid=(n // tm, nff),
        in_specs=in_specs,
        out_specs=pl.BlockSpec((tm, d), lambda i, j: (i, 0)),
        out_shape=jax.ShapeDtypeStruct((n, d), F32),
        scratch_shapes=[pltpu.VMEM((tm, d), MXU_DTYPE)],
        compiler_params=_cparams(("parallel", "arbitrary")),
        name="ffn_final" if final else "ffn",
    )(*args)


def _prep_ffn(w_in, w_out, tf=FFN_TILE):
    d, two_dff = w_in.shape
    dff = two_dff // 2
    dffp = -(-dff // tf) * tf
    z = jnp.zeros((d, dffp - dff), MXU_DTYPE)
    w = jnp.concatenate([w_in[:, :dff].astype(MXU_DTYPE), z, w_in[:, dff:].astype(MXU_DTYPE), z], axis=1)
    wo = jnp.pad(w_out, ((0, dffp - dff), (0, 0))).astype(MXU_DTYPE)
    return w, wo


def _mixin_body(x_ref, g_ref, wm_ref, wd_ref, wa_ref, wvt_ref, hm_ref, a_ref, vt_ref, dil_refs, relayout_scr):
    xn = _rms(x_ref[...], g_ref[...]).astype(MXU_DTYPE)
    hm_ref[...] = _dot(xn, wm_ref[...]).astype(hm_ref.dtype)
    a_ref[...] = _dot(xn, wa_ref[...])
    vt_ref[...] = _dot_t(wvt_ref[...], xn).astype(vt_ref.dtype)
    hd = _dot(xn, wd_ref[...])
    tm = hd.shape[0]
    for g, out_ref in enumerate(dil_refs):
        dilation = DIL_PAIRS[g][1]
        cols = hd[:, g * DIL_GROUP_COLS:(g + 1) * DIL_GROUP_COLS]
        if dilation == 1:
            out_ref[...] = cols.astype(out_ref.dtype)
            continue
        for c in range(DIL_GROUP_COLS // LANES):
            relayout_scr[c] = cols[:, c * LANES:(c + 1) * LANES]
        for r in range(dilation):
            for c in range(DIL_GROUP_COLS // LANES):
                piece = relayout_scr[c, pl.ds(r, tm // dilation, stride=dilation), :]
                out_ref[:, r * DIL_GROUP_COLS + c * LANES:r * DIL_GROUP_COLS + (c + 1) * LANES] = (
                    piece.astype(out_ref.dtype))


def _mixin_kernel(x_ref, g_ref, wm_ref, wd_ref, wa_ref, wvt_ref, hm_ref, a_ref, vt_ref, d0_ref, d1_ref,
                  d2_ref, relayout_scr):
    _mixin_body(x_ref, g_ref, wm_ref, wd_ref, wa_ref, wvt_ref, hm_ref, a_ref, vt_ref,
                (d0_ref, d1_ref, d2_ref), relayout_scr)


def _mixin(x2, g, wm, wd, wa, wvt, *, tm=512):
    n, d = x2.shape
    weight = lambda a: _resident(a.shape, lambda i: (0, 0))
    dils = [dil for _, dil in DIL_PAIRS]
    return pl.pallas_call(
        _mixin_kernel,
        grid=(n // tm,),
        in_specs=[
            pl.BlockSpec((tm, d), lambda i: (i, 0)),
            pl.BlockSpec((1, d), lambda i: (0, 0)),
            weight(wm), weight(wd), weight(wa), weight(wvt),
        ],
        out_specs=[
            pl.BlockSpec((tm, H_COLS), lambda i: (i, 0)),
            pl.BlockSpec((tm, AUX_COLS), lambda i: (i, 0)),
            pl.BlockSpec((2 * LANES, tm), lambda i: (0, i)),
        ] + [pl.BlockSpec((tm // dil, dil * DIL_GROUP_COLS), lambda i: (i, 0)) for dil in dils],
        out_shape=[
            jax.ShapeDtypeStruct((n, H_COLS), MXU_DTYPE),
            jax.ShapeDtypeStruct((n, AUX_COLS), F32),
            jax.ShapeDtypeStruct((2 * LANES, n), MXU_DTYPE),
        ] + [jax.ShapeDtypeStruct((n // dil, dil * DIL_GROUP_COLS), MXU_DTYPE) for dil in dils],
        scratch_shapes=[pltpu.VMEM((DIL_GROUP_COLS // LANES, tm, LANES), F32)],
        compiler_params=_cparams(("parallel",)),
        name="mix_in",
    )(x2, g.reshape(1, d), wm, wd, wa, wvt)


def _rot_half_cols(w):
    half = w.shape[1] // 2
    return jnp.concatenate([-w[:, half:], w[:, :half]], axis=1)


def _prep_mixin(w):
    d = w.shape[0]
    o = 0
    parts = {}
    for name, width in (("q_lat", 512), ("kv_lat", 512), ("k_rope", 64), ("dq", 768), ("dk", 768),
                        ("dv", 768), ("nq", 512), ("nkc", 128), ("nvc", 128), ("nks", 128),
                        ("nvs", 128), ("nkw", 128), ("nvw", 128), ("ng", 12)):
        parts[name] = w[:, o:o + width]
        o += width
    z64 = jnp.zeros((d, 64), w.dtype)
    kr = parts["k_rope"]
    wm = jnp.concatenate([
        parts["q_lat"], parts["kv_lat"], parts["nq"],
        kr, z64, _rot_half_cols(kr), z64,
        parts["nks"], parts["nkw"]], axis=1).astype(MXU_DTYPE)
    gw = 2 * HEAD_DIM
    wd = jnp.concatenate([parts[name][:, g * gw:(g + 1) * gw] for g in range(len(DIL_PAIRS))
                          for name in ("dq", "dk", "dv")], axis=1).astype(MXU_DTYPE)
    wa = jnp.concatenate([parts["nkc"], parts["nvc"], parts["ng"],
                          jnp.zeros((d, LANES - 12), w.dtype)], axis=1).astype(MXU_DTYPE)
    wvt = jnp.concatenate([parts["nvs"], parts["nvw"]], axis=1).T.astype(MXU_DTYPE)
    return wm, wd, wa, wvt


def _mla_proj_kernel(lat_ref, kr_ref, qn_ref, kvn_ref, wq_ref, wk_ref, wvt_ref, invf_ref,
                     q_ref, k_ref, vt_ref, *, seq, tm):
    i = pl.program_id(0)
    pos0 = lax.rem(i * tm, seq)
    pos = (pos0 + lax.broadcasted_iota(jnp.int32, (tm, 1), 0)).astype(F32)
    ang = pos * invf_ref[...]
    cos = jnp.cos(ang)
    sin = jnp.sin(ang)
    scale = (MLA_NOPE + MLA_ROPE) ** -0.5 * LOG2E

    lat = lat_ref[...].astype(F32)
    qn = _rms(lat[:, :MLA_LORA], qn_ref[...]).astype(MXU_DTYPE)
    kvn = _rms(lat[:, MLA_LORA:], kvn_ref[...]).astype(MXU_DTYPE)
    qm = _dot(qn, wq_ref[...])
    rot0 = MLA_HEADS * MLA_QK_PAD
    for h in range(MLA_HEADS):
        c = h * MLA_QK_PAD
        nope = qm[:, c:c + LANES]
        pe = qm[:, c + LANES:c + 2 * LANES] * cos + qm[:, rot0 + h * LANES:rot0 + (h + 1) * LANES] * sin
        q_ref[:, c:c + LANES] = (nope * scale).astype(q_ref.dtype)
        q_ref[:, c + LANES:c + 2 * LANES] = (pe * scale).astype(q_ref.dtype)

    kr = kr_ref[...].astype(F32)
    kpe = (kr[:, :LANES] * cos + kr[:, LANES:] * sin).astype(k_ref.dtype)
    kn = _dot(kvn, wk_ref[...])
    for h in range(MLA_HEADS):
        c = h * MLA_QK_PAD
        k_ref[:, c:c + LANES] = kn[:, h * LANES:(h + 1) * LANES].astype(k_ref.dtype)
        k_ref[:, c + LANES:c + 2 * LANES] = kpe
    vt_ref[...] = _dot_t(wvt_ref[...], kvn).astype(vt_ref.dtype)


def _mla_proj(h2, qn, kvn, wq, wk, wvt, invf, *, seq, tm=512):
    n = h2.shape[0]
    qk_cols = MLA_HEADS * MLA_QK_PAD
    v_rows = MLA_HEADS * MLA_V
    full = lambda a: pl.BlockSpec(a.shape, lambda i: (0,) * a.ndim)
    return pl.pallas_call(
        functools.partial(_mla_proj_kernel, seq=seq, tm=tm),
        grid=(n // tm,),
        in_specs=[
            pl.BlockSpec((tm, 2 * MLA_LORA), lambda i: (i, 0)),
            pl.BlockSpec((tm, 2 * LANES), lambda i: (i, COL_KROPE // (2 * LANES))),
            full(qn), full(kvn), full(wq), full(wk), full(wvt), full(invf),
        ],
        out_specs=[
            pl.BlockSpec((tm, qk_cols), lambda i: (i, 0)),
            pl.BlockSpec((tm, qk_cols), lambda i: (i, 0)),
            pl.BlockSpec((v_rows, tm), lambda i: (0, i)),
        ],
        out_shape=[
            jax.ShapeDtypeStruct((n, qk_cols), MXU_DTYPE),
            jax.ShapeDtypeStruct((n, qk_cols), MXU_DTYPE),
            jax.ShapeDtypeStruct((v_rows, n), MXU_DTYPE),
        ],
        compiler_params=_cparams(("parallel",)),
        name="mla_proj",
    )(h2, h2, qn, kvn, wq, wk, wvt, invf)


def _prep_mla(w_uq, w_ukv):
    r = MLA_LORA
    wq3 = w_uq.reshape(r, MLA_HEADS, MLA_NOPE + MLA_ROPE)
    nope, pe = wq3[..., :MLA_NOPE], wq3[..., MLA_NOPE:]
    z = jnp.zeros((r, MLA_HEADS, 64), w_uq.dtype)
    main = jnp.concatenate([nope, pe, z], axis=-1).reshape(r, MLA_HEADS * MLA_QK_PAD)
    half = MLA_ROPE // 2
    pe_rot = jnp.concatenate([-pe[..., half:], pe[..., :half]], axis=-1)
    rot = jnp.concatenate([pe_rot, z], axis=-1).reshape(r, MLA_HEADS * LANES)
    wq = jnp.concatenate([main, rot], axis=1).astype(MXU_DTYPE)
    wkv3 = w_ukv.reshape(r, MLA_HEADS, MLA_NOPE + MLA_V)
    wk = wkv3[..., :MLA_NOPE].reshape(r, MLA_HEADS * MLA_NOPE).astype(MXU_DTYPE)
    wvt = wkv3[..., MLA_NOPE:].reshape(r, MLA_HEADS * MLA_V).T.astype(MXU_DTYPE)
    return wq, wk, wvt


def _rope_inv_freq_row():
    half = MLA_ROPE // 2
    f = ROPE_BASE ** (-jnp.arange(half, dtype=F32) / half)
    return jnp.concatenate([f, f, jnp.zeros((LANES - MLA_ROPE,), F32)]).reshape(1, LANES)


def _with_ones_rows(v_t):
    return jnp.concatenate([v_t, jnp.ones((DEN_ROWS, v_t.shape[1]), v_t.dtype)], axis=0)


def _softmax_piece_t(scores, v_aug, m_ref, acc_ref, *, first):
    if first:
        s_t = scores()
        m_new = jnp.max(s_t, axis=0, keepdims=True)
        acc_ref[...] = _dot(v_aug, jnp.exp2(s_t - m_new).astype(MXU_DTYPE))
        m_ref[...] = m_new
        return

    m_old = m_ref[...]
    s_t = scores()
    pv = _dot(v_aug, jnp.exp2(s_t - m_old).astype(MXU_DTYPE))
    fits = jnp.max(jnp.max(s_t, axis=0, keepdims=True) - m_old) <= LAZY_MAX_SLACK

    @pl.when(fits)
    def _():
        acc_ref[...] += pv

    @pl.when(jnp.logical_not(fits))
    def _():
        s_again = scores()
        m_new = jnp.maximum(m_old, jnp.max(s_again, axis=0, keepdims=True))
        weights = jnp.exp2(s_again - m_new).astype(MXU_DTYPE)
        acc_ref[...] = jnp.exp2(m_old - m_new) * acc_ref[...] + _dot(v_aug, weights)
        m_ref[...] = m_new


def _softmax_result(acc_scr, dv):
    return acc_scr[0:dv, :] * (1.0 / acc_scr[dv:dv + 1, :])


def _flash_kernel(qt_ref, kt_ref, q_ref, k_ref, vt_ref, o_ref, m_scr, acc_scr, *, t, dsub):
    step = pl.program_id(2)
    qi = qt_ref[step]
    ki = kt_ref[step]

    def piece(k0, q0, size, diagonal):
        ks = slice(k0, k0 + size)
        qs = slice(q0, q0 + size)

        def scores():
            s_t = _dot_t(k_ref[0, ks, :], q_ref[0, qs, :])
            if diagonal and k0 + size - 1 > q0:
                kpos = k0 + lax.broadcasted_iota(jnp.int32, (size, 1), 0)
                qpos = q0 + lax.broadcasted_iota(jnp.int32, (1, size), 1)
                s_t = jnp.where(kpos <= qpos, s_t, NEG_INF)
            return s_t

        _softmax_piece_t(scores, _with_ones_rows(vt_ref[:, ks]), m_scr.at[:, qs], acc_scr.at[:, qs],
                         first=diagonal and k0 == q0)

    @pl.when(ki == qi)
    def _():
        starts = range(0, t, dsub)
        for p in starts:
            piece(p, p, dsub, True)
        for k0 in starts:
            for q0 in starts:
                if q0 > k0:
                    piece(k0, q0, dsub, True)

    @pl.when(ki < qi)
    def _():
        piece(0, 0, t, False)

    @pl.when(ki == 0)
    def _():
        o_ref[0] = _softmax_result(acc_scr, MLA_V).T.astype(o_ref.dtype)


def _mla_flash(q3, k3, vt, *, t=2048, dsub=1024):
    b, s, _ = q3.shape
    t = min(t, s)
    dsub = min(dsub, t)
    nq = s // t
    pairs = [(qi, ki) for qi in range(nq) for ki in range(qi, -1, -1)]
    qt = jnp.asarray(np.array([p[0] for p in pairs], np.int32))
    kt = jnp.asarray(np.array([p[1] for p in pairs], np.int32))
    grid_spec = pltpu.PrefetchScalarGridSpec(
        num_scalar_prefetch=2,
        grid=(b, MLA_HEADS, len(pairs)),
        in_specs=[
            pl.BlockSpec((1, t, MLA_QK_PAD), lambda bi, h, st, qt, kt: (bi, qt[st], h)),
            pl.BlockSpec((1, t, MLA_QK_PAD), lambda bi, h, st, qt, kt: (bi, kt[st], h)),
            pl.BlockSpec((MLA_V, t), lambda bi, h, st, qt, kt: (h, bi * nq + kt[st])),
        ],
        out_specs=pl.BlockSpec((1, t, MLA_V), lambda bi, h, st, qt, kt: (bi, qt[st], h)),
        scratch_shapes=[pltpu.VMEM((1, t), F32), pltpu.VMEM((MLA_V + DEN_ROWS, t), F32)],
    )
    return pl.pallas_call(
        functools.partial(_flash_kernel, t=t, dsub=dsub),
        grid_spec=grid_spec,
        out_shape=jax.ShapeDtypeStruct((b, s, MLA_HEADS * MLA_V), MXU_DTYPE),
        compiler_params=_cparams(("parallel", "parallel", "arbitrary")),
        name="mla_flash",
    )(qt, kt, q3, k3, vt)


def _dil_kernel(q_ref, kc_ref, kp_ref, vc_ref, vp_ref, o_ref, lse_ref, *, dilation, span, slopes, tl):
    n = pl.program_id(2)
    scale = HEAD_DIM ** -0.5
    blk = DIL_BLOCK
    a = lax.broadcasted_iota(jnp.int32, (blk, 1), 0)
    c = lax.broadcasted_iota(jnp.int32, (1, 2 * blk), 1)
    j = blk + a - c
    in_band = (j >= 0) & (j <= span)
    first_valid = in_band & ((c >= blk) | (n > 0))
    dist = (j * dilation).astype(F32)
    for hg in range(2):
        cols = slice(hg * LANES, (hg + 1) * LANES)
        bias = slopes[hg] * dist
        for sb in range(tl // blk):
            rows = slice(sb * blk, (sb + 1) * blk)
            q = (q_ref[0, rows, cols].astype(F32) * scale).astype(MXU_DTYPE)
            if sb == 0:
                kprev, vprev, valid = kp_ref[0, :, cols], vp_ref[0, :, cols], first_valid
            else:
                prev = slice((sb - 1) * blk, sb * blk)
                kprev, vprev, valid = kc_ref[0, prev, cols], vc_ref[0, prev, cols], in_band
            keys = jnp.concatenate([kprev, kc_ref[0, rows, cols]], axis=0)
            vals = jnp.concatenate([vprev, vc_ref[0, rows, cols]], axis=0)
            s = jnp.where(valid, _dot_t(q, keys) - bias, NEG_INF)
            m = jnp.max(s, axis=1, keepdims=True)
            e = jnp.where(valid, jnp.exp(s - m), 0.0)
            den = jnp.sum(e, axis=1, keepdims=True)
            o_ref[0, rows, cols] = _dot((e / den).astype(MXU_DTYPE), vals)
            lse_ref[0, rows, cols] = jnp.broadcast_to(m + jnp.log(den), (blk, LANES))


def _dilated_group(hg, g, b, *, tl=512):
    window, dilation = DIL_PAIRS[g]
    span = window // dilation
    seq_l = hg.shape[0] // b
    tl = min(tl, seq_l)
    per_blk = tl // DIL_BLOCK
    w = 2 * LANES
    hv = hg.reshape(b, seq_l, dilation * DIL_GROUP_COLS)
    qc, kc, vc = 0, 1, 2
    stride = DIL_GROUP_COLS // w
    cur = lambda col: pl.BlockSpec((1, tl, w), lambda bi, r, n: (bi, n, r * stride + col))
    prev = lambda col: pl.BlockSpec(
        (1, DIL_BLOCK, w), lambda bi, r, n: (bi, jnp.maximum(n * per_blk - 1, 0), r * stride + col))
    out_spec = pl.BlockSpec((1, tl, w), lambda bi, r, n: (bi, n, r))
    out_sds = jax.ShapeDtypeStruct((b, seq_l, dilation * w), F32)
    o, lse = pl.pallas_call(
        functools.partial(_dil_kernel, dilation=dilation, span=span,
                          slopes=ALIBI_SLOPES[2 * g:2 * g + 2], tl=tl),
        grid=(b, dilation, seq_l // tl),
        in_specs=[cur(qc), cur(kc), prev(kc), cur(vc), prev(vc)],
        out_specs=[out_spec, out_spec],
        out_shape=[out_sds, out_sds],
        compiler_params=_cparams(("parallel", "parallel", "parallel")),
        name=f"dilated_g{g}",
    )(hv, hv, hv, hv, hv)
    return o.reshape(b * seq_l, dilation * w), lse.reshape(b * seq_l, dilation * w)


def _nsa_cmp_kernel(x_ref, pos_ref, wa_ref, wb_ref, w2k_ref, w2vt_ref, kc_ref, vct_ref):
    x = x_ref[0]
    xa = (x + pos_ref[0:1, :]).astype(MXU_DTYPE)
    xb = (x + pos_ref[1:2, :]).astype(MXU_DTYPE)
    first = _dot(xa, wa_ref[...])
    second = _dot(xb, wb_ref[...])
    second_next = pltpu.roll(second, x.shape[0] - 1, 0)
    pre = first + second_next
    hid = (pre * jax.nn.sigmoid(pre)).astype(MXU_DTYPE)
    kc_ref[0] = _dot(hid[:, :LANES], w2k_ref[...]).astype(kc_ref.dtype)
    vct_ref[0] = _dot_t(w2vt_ref[...], hid[:, LANES:]).astype(vct_ref.dtype)


def _nsa_compress(aux3, pos2, wa, wb, w2k, w2vt):
    b, s, _ = aux3.shape
    nchunk = s // NSA_CMP_STRIDE
    xk = aux3[:, :, :2 * LANES].reshape(b, nchunk, NSA_CMP_STRIDE * 2 * LANES)
    full = lambda a: pl.BlockSpec(a.shape, lambda bi: (0,) * a.ndim)
    return pl.pallas_call(
        _nsa_cmp_kernel,
        grid=(b,),
        in_specs=[pl.BlockSpec((1, nchunk, xk.shape[2]), lambda bi: (bi, 0, 0)),
                  full(pos2), full(wa), full(wb), full(w2k), full(w2vt)],
        out_specs=[pl.BlockSpec((1, nchunk, LANES), lambda bi: (bi, 0, 0)),
                   pl.BlockSpec((1, LANES, nchunk), lambda bi: (bi, 0, 0))],
        out_shape=[jax.ShapeDtypeStruct((b, nchunk, LANES), MXU_DTYPE),
                   jax.ShapeDtypeStruct((b, LANES, nchunk), MXU_DTYPE)],
        compiler_params=_cparams(("parallel",)),
        name="nsa_cmp",
    )(xk, pos2, wa, wb, w2k, w2vt)


def _prep_nsa_cmp(cmp_pos, phi_k1, phi_k2, phi_v1, phi_v2):
    half = NSA_CMP_LEN // 2
    pk = cmp_pos.reshape(2, half, HEAD_DIM)
    pos2 = jnp.concatenate([pk, pk], axis=-1).reshape(2, half * 2 * LANES)

    def halves(w1, is_v):
        w = w1.reshape(2, half, HEAD_DIM, HEAD_DIM)
        z = jnp.zeros_like(w)
        w = jnp.concatenate([z, w] if is_v else [w, z], axis=2)
        return w.reshape(2, half * 2 * LANES, HEAD_DIM)

    wk, wv = halves(phi_k1, False), halves(phi_v1, True)
    wa = jnp.concatenate([wk[0], wv[0]], axis=1).astype(MXU_DTYPE)
    wb = jnp.concatenate([wk[1], wv[1]], axis=1).astype(MXU_DTYPE)
    return pos2, wa, wb, phi_k2.astype(MXU_DTYPE), phi_v2.T.astype(MXU_DTYPE)


def _split3(x):
    hi = x.astype(MXU_DTYPE)
    r = x - hi.astype(F32)
    mid = r.astype(MXU_DTYPE)
    lo = (r - mid.astype(F32)).astype(MXU_DTYPE)
    return hi, mid, lo


def _pos_features(pos):
    hi = jnp.floor(pos / POS_SPLIT) * POS_SPLIT
    lo = pos - hi
    cols = jnp.stack([hi, hi, hi, lo, lo, lo], axis=1)
    return jnp.pad(cols, ((0, 0), (0, LANES - 6))).astype(MXU_DTYPE)


def _slope_features(tq):
    sig = jnp.asarray([s * LOG2E for s in ALIBI_SLOPES[DIL_HEADS:]], F32)
    pieces = jnp.stack(_split3(sig), axis=1)
    rows = jnp.concatenate([pieces, pieces], axis=1)
    rows = jnp.pad(rows, ((0, 0), (0, LANES - 6)))
    return jnp.repeat(rows, tq, axis=0)


def _masked_softmax_t(s_t, mask):
    s_t = jnp.where(mask, s_t, NEG_INF)
    m = jnp.max(s_t, axis=0, keepdims=True)
    e = jnp.exp2(s_t - m)
    den = jnp.sum(e, axis=0, keepdims=True)
    return e * jnp.where(m > 0.5 * NEG_INF, 1.0 / den, 0.0)


def _nsa_kernel(q_ref, k_ref, vt_ref, kc_ref, vct_ref, c2st_ref, pf_ref, cpf_ref, sf_ref, g_ref,
                o_ref, m_scr, acc_scr, bias_scr, any_scr, ocmp_scr, score_scr,
                *, tq, tk, topk, cmp_chunk):
    i = pl.program_id(1)
    t0 = i * tq
    nh = NSA_HEADS
    cols = nh * tq
    ncp = kc_ref.shape[1]
    ns = c2st_ref.shape[0]
    blocks_per_tile = tk // NSA_SEL_LEN

    q = q_ref[0]
    q4 = jnp.concatenate([q[:, h * LANES:(h + 1) * LANES] for h in range(nh)], axis=0)
    q4 = (q4.astype(F32) * (HEAD_DIM ** -0.5 * LOG2E)).astype(MXU_DTYPE)
    q4 = jnp.concatenate([q4, sf_ref[...]], axis=1)
    col = lax.broadcasted_iota(jnp.int32, (1, cols), 1)
    tpos = (t0 + (col & (tq - 1))).astype(F32)

    def cmp_part(rows):
        cidx = lax.broadcasted_iota(jnp.int32, (rows, 1), 0).astype(F32)
        c_end = cidx * NSA_CMP_STRIDE + (NSA_CMP_LEN - 1)
        kc = jnp.concatenate([kc_ref[0, 0:rows, :], cpf_ref[0:rows, :]], axis=1)
        p_cmp = _masked_softmax_t(_dot_t(kc, q4), c_end <= tpos)
        ocmp_scr[...] = _dot(vct_ref[0, :, 0:rows], p_cmp.astype(MXU_DTYPE))
        p_sum = p_cmp[:, 0:tq]
        for h in range(1, nh):
            p_sum = p_sum + p_cmp[:, h * tq:(h + 1) * tq]
        c2st = c2st_ref[:, 0:rows]
        if MXU_DTYPE == jnp.float32:
            score_scr[...] = _dot(c2st, p_sum)
        else:
            score_scr[...] = sum(_dot(c2st, piece) for piece in _split3(p_sum))

    n_variants = -(-ncp // cmp_chunk)
    n_ending = (t0 + tq) // NSA_CMP_STRIDE - 1
    need = jnp.clip((n_ending + cmp_chunk - 1) // cmp_chunk, 1, n_variants)
    for v in range(1, n_variants + 1):
        pl.when(need == v)(functools.partial(cmp_part, min(v * cmp_chunk, ncp)))
    o_cmp = ocmp_scr[...]
    score = score_scr[...]

    t1 = t0 + lax.broadcasted_iota(jnp.int32, (1, tq), 1)
    cur = _shr(t1, NSA_SEL_LEN).astype(F32)
    jj = lax.broadcasted_iota(jnp.int32, (ns, 1), 0).astype(F32)
    forced = (jj == 0.0) | (jj == cur) | (jj == cur - 1.0)
    score = jnp.where(jj > cur, -1.0, jnp.where(forced, NSA_FORCED_SCORE, score))
    bias = jnp.full((ns, tq), NEG_INF, F32)
    for _ in range(topk):
        best = jnp.max(score, axis=0, keepdims=True)
        idx = jnp.min(jnp.where(score == best, jj, float(ns)), axis=0, keepdims=True)
        hit = jj == idx
        bias = jnp.where(hit, 0.0, bias)
        score = jnp.where(hit, -2.0, score)
    bias_scr[...] = bias
    any_scr[...] = jnp.broadcast_to(jnp.max(bias, axis=1, keepdims=True), (ns, LANES))

    def slc_update(kt, causal):
        k0 = pl.multiple_of(kt * tk, tk)
        b0 = pl.multiple_of(kt * blocks_per_tile, blocks_per_tile)

        def scores():
            keys = jnp.concatenate([k_ref[0, pl.ds(k0, tk), 0:LANES], pf_ref[pl.ds(k0, tk), :]], axis=1)
            s_t = _dot_t(keys, q4)
            rows = [jnp.broadcast_to(bias_scr[pl.ds(b0 + r, 1), :], (NSA_SEL_LEN, tq))
                    for r in range(blocks_per_tile)]
            sel_bias = jnp.concatenate(rows, axis=0)
            s_t = s_t + jnp.concatenate([sel_bias] * nh, axis=1)
            if causal:
                kpos = (k0 + lax.broadcasted_iota(jnp.int32, (tk, 1), 0)).astype(F32)
                s_t = jnp.where(kpos <= tpos, s_t, NEG_INF)
            return s_t

        _softmax_piece_t(scores, _with_ones_rows(vt_ref[0:LANES, pl.ds(k0, tk)]), m_scr, acc_scr,
                         first=causal)

    last = t0 // tk
    slc_update(last, True)

    def slc_tile(back, carry):
        kt = last - 1 - back
        b0 = pl.multiple_of(kt * blocks_per_tile, blocks_per_tile)
        touched = jnp.max(any_scr[pl.ds(b0, blocks_per_tile), :]) == 0.0
        pl.when(touched)(functools.partial(slc_update, kt, False))
        return carry

    lax.fori_loop(0, last, slc_tile, 0)
    o_slc = _softmax_result(acc_scr, HEAD_DIM)

    wlen = NSA_WINDOW + tq
    ws = pl.multiple_of(jnp.maximum(t0 - NSA_WINDOW, 0), tq)
    keys = jnp.concatenate([k_ref[0, pl.ds(ws, wlen), LANES:2 * LANES], pf_ref[pl.ds(ws, wlen), :]], axis=1)
    wpos = (ws + lax.broadcasted_iota(jnp.int32, (wlen, 1), 0)).astype(F32)
    dist = tpos - wpos
    p_win = _masked_softmax_t(_dot_t(keys, q4), (dist >= 0.0) & (dist < float(NSA_WINDOW)))
    o_win = _dot(vt_ref[LANES:2 * LANES, pl.ds(ws, wlen)], p_win.astype(MXU_DTYPE))

    gates = jax.nn.sigmoid(g_ref[0]).T
    for h in range(nh):
        c = slice(h * tq, (h + 1) * tq)
        o = (gates[3 * h:3 * h + 1, :] * o_cmp[:, c] + gates[3 * h + 1:3 * h + 2, :] * o_slc[:, c]
             + gates[3 * h + 2:3 * h + 3, :] * o_win[:, c])
        o_ref[0, :, h * LANES:(h + 1) * LANES] = o.T.astype(o_ref.dtype)


def _nsa(h3, aux3, vt, kc, vct, *, tq=NSA_Q_BLOCK, tk=512):
    b, s, _ = h3.shape
    ncp = kc.shape[1]
    ns = s // NSA_SEL_LEN
    tk = min(tk, s)
    w = NSA_HEADS * HEAD_DIM
    c2st = _cmp_to_sel_t(ncp, ns)
    pf = _pos_features(jnp.arange(s, dtype=F32))
    cpf = _pos_features(jnp.arange(ncp, dtype=F32) * NSA_CMP_STRIDE + 0.5 * (NSA_CMP_LEN - 1))
    sf = _slope_features(tq).astype(MXU_DTYPE)
    const = lambda a: _resident(a.shape, lambda bi, i: (0,) * a.ndim)
    return pl.pallas_call(
        functools.partial(_nsa_kernel, tq=tq, tk=tk, topk=min(NSA_TOPK, ns), cmp_chunk=min(256, ncp)),
        grid=(b, s // tq),
        in_specs=[
            pl.BlockSpec((1, tq, w), lambda bi, i: (bi, i, COL_NSA_Q // w)),
            _resident((1, s, 2 * LANES), lambda bi, i: (bi, 0, COL_NSA_K // (2 * LANES))),
            _resident((2 * LANES, s), lambda bi, i: (0, bi)),
            _resident((1, ncp, LANES), lambda bi, i: (bi, 0, 0)),
            _resident((1, LANES, ncp), lambda bi, i: (bi, 0, 0)),
            const(c2st), const(pf), const(cpf), const(sf),
            pl.BlockSpec((1, tq, LANES), lambda bi, i: (bi, i, 2)),
        ],
        out_specs=pl.BlockSpec((1, tq, w), lambda bi, i: (bi, i, 0)),
        out_shape=jax.ShapeDtypeStruct((b, s, w), MXU_DTYPE),
        scratch_shapes=[pltpu.VMEM((1, NSA_HEADS * tq), F32),
                        pltpu.VMEM((HEAD_DIM + DEN_ROWS, NSA_HEADS * tq), F32),
                        pltpu.VMEM((ns, tq), F32), pltpu.VMEM((ns, LANES), F32),
                        pltpu.VMEM((HEAD_DIM, NSA_HEADS * tq), F32), pltpu.VMEM((ns, tq), F32)],
        compiler_params=_cparams(("parallel", "arbitrary")),
        name="nsa",
    )(h3, h3, vt, kc, vct, c2st, pf, cpf, sf, aux3)


def _cmp_to_sel_t(ncp, ns):
    nc = ncp - 1
    c = np.arange(ncp)[None, :]
    j = np.arange(ns)[:, None]
    start = c * NSA_CMP_STRIDE
    m = (start < (j + 1) * NSA_SEL_LEN) & (start + NSA_CMP_LEN - 1 >= j * NSA_SEL_LEN) & (c < nc)
    return jnp.asarray(m.astype(np.float32)).astype(MXU_DTYPE)


def _token_rows(ref, dilation, scr):
    if dilation == 1:
        return ref[...]
    tm = ref.shape[0] * dilation
    for r in range(dilation):
        for c in range(2):
            lo = r * 2 * LANES + c * LANES
            scr[c, pl.ds(r, tm // dilation, stride=dilation), :] = ref[:, lo:lo + LANES]
    return jnp.concatenate([scr[0], scr[1]], axis=1)


def _mixout_kernel(x_ref, mla_ref, d0_ref, d1_ref, d2_ref, l0_ref, l1_ref, l2_ref, nsa_ref, w_ref, o_ref,
                   relayout_scr):
    dils = [dil for _, dil in DIL_PAIRS]
    o_dil = [_token_rows(r, dil, relayout_scr.at[2 * g:2 * g + 2])
             for g, (r, dil) in enumerate(zip((d0_ref, d1_ref, d2_ref), dils))]
    lse = [_token_rows(r, dil, relayout_scr.at[6 + 2 * g:8 + 2 * g])
           for g, (r, dil) in enumerate(zip((l0_ref, l1_ref, l2_ref), dils))]
    top = jnp.maximum(jnp.maximum(lse[0], lse[1]), lse[2])
    e = [jnp.exp(l - top) for l in lse]
    inv = 1.0 / (e[0] + e[1] + e[2])
    acc = x_ref[...] + _dot(mla_ref[...], w_ref[0:768, :])
    for g in range(3):
        mixed = (o_dil[g] * (e[g] * inv)).astype(MXU_DTYPE)
        acc += _dot(mixed, w_ref[768 + g * 256:768 + (g + 1) * 256, :])
    acc += _dot(nsa_ref[...], w_ref[1536:2048, :])
    o_ref[...] = acc


def _mixout(x2, o_mla, o_dil, lse_dil, o_nsa, w, *, tm=512):
    n, d = x2.shape
    row = lambda width: pl.BlockSpec((tm, width), lambda i: (i, 0))
    dil_specs = [pl.BlockSpec((tm // dil, dil * 2 * LANES), lambda i: (i, 0)) for _, dil in DIL_PAIRS]
    return pl.pallas_call(
        _mixout_kernel,
        grid=(n // tm,),
        in_specs=[row(d), row(o_mla.shape[1])] + dil_specs + dil_specs + [
            row(o_nsa.shape[1]), pl.BlockSpec(w.shape, lambda i: (0, 0))],
        out_specs=row(d),
        out_shape=jax.ShapeDtypeStruct((n, d), F32),
        scratch_shapes=[pltpu.VMEM((12, tm, LANES), F32)],
        compiler_params=_cparams(("parallel",)),
        name="mix_out",
    )(x2, o_mla, *o_dil, *lse_dil, o_nsa, w)


def kernel(x, ffn1_norm, ffn1_w_in, ffn1_w_out, mix_norm, w_mix_in, mla_q_norm, mla_w_uq, mla_kv_norm,
           mla_w_ukv, nsa_cmp_pos, nsa_phi_k1, nsa_phi_k2, nsa_phi_v1, nsa_phi_v2, w_mix_out, ffn2_norm,
           ffn2_w_in, ffn2_w_out, final_norm):
    b, s, d = x.shape
    depth = ffn1_w_in.shape[0]
    n = b * s
    invf = _rope_inv_freq_row()
    x2 = x.reshape(n, d)
    for l in range(depth):
        x2 = _ffn(x2, ffn1_norm[l], *_prep_ffn(ffn1_w_in[l], ffn1_w_out[l]))

        h2, aux2, nsa_vt, *dil_in = _mixin(x2, mix_norm[l], *_prep_mixin(w_mix_in[l]))
        h3 = h2.reshape(b, s, H_COLS)
        aux3 = aux2.reshape(b, s, AUX_COLS)

        q, k, vt = _mla_proj(h2, mla_q_norm[l].reshape(1, -1), mla_kv_norm[l].reshape(1, -1),
                             *_prep_mla(mla_w_uq[l], mla_w_ukv[l]), invf, seq=s)
        o_mla = _mla_flash(q.reshape(b, s, -1), k.reshape(b, s, -1), vt)

        dil = [_dilated_group(hg, g, b) for g, hg in enumerate(dil_in)]

        kc, vct = _nsa_compress(aux3, *_prep_nsa_cmp(nsa_cmp_pos[l], nsa_phi_k1[l], nsa_phi_k2[l],
                                                      nsa_phi_v1[l], nsa_phi_v2[l]))
        o_nsa = _nsa(h3, aux3, nsa_vt, kc, vct)

        x2 = _mixout(x2, o_mla.reshape(n, -1), [o for o, _ in dil], [e for _, e in dil],
                     o_nsa.reshape(n, -1), w_mix_out[l].astype(MXU_DTYPE))

        gf = final_norm if l == depth - 1 else None
        x2 = _ffn(x2, ffn2_norm[l], *_prep_ffn(ffn2_w_in[l], ffn2_w_out[l]), gf)
    return x2.reshape(b, s, d)
```

```python
import functools
import math

import numpy as np
import jax
import jax.numpy as jnp
from jax import lax
from jax.experimental import pallas as pl
from jax.experimental.pallas import tpu as pltpu

F32 = jnp.float32
MXU_DTYPE = jnp.bfloat16
VMEM_LIMIT_BYTES = 56 * 1024 * 1024
LANES = 128
LOG2E = math.log2(math.e)
FFN_TILE = 512
DEN_ROWS = 16
LAZY_MAX_SLACK = 16.0

HEAD_DIM = 128
RMS_EPS = 1e-6
NEG_INF = -1e30

MLA_HEADS = 6
MLA_LORA = 512
MLA_NOPE = 128
MLA_ROPE = 64
MLA_V = 128
MLA_QK_PAD = 256
ROPE_BASE = 10000.0

DIL_PAIRS = ((128, 1), (512, 4), (2048, 16))
DIL_HEADS = 6
DIL_BLOCK = 128

NSA_HEADS = 4
NSA_CMP_LEN = 32
NSA_CMP_STRIDE = 16
NSA_SEL_LEN = 64
NSA_TOPK = 16
NSA_WINDOW = 512
NSA_FORCED_SCORE = 100.0
NSA_Q_BLOCK = 256
POS_SPLIT = 128

N_ALIBI = DIL_HEADS + NSA_HEADS
ALIBI_SLOPES = tuple(float(2.0 ** (-8.0 * i / N_ALIBI)) for i in range(1, N_ALIBI + 1))

COL_QLAT = 0
COL_KVLAT = 512
COL_NSA_Q = 1024
COL_KROPE = 1536
COL_NSA_K = 1792
H_COLS = 2048
DIL_GROUP_COLS = 768
AUX_COLS = 384


def _cparams(sem):
    return pltpu.CompilerParams(dimension_semantics=sem, vmem_limit_bytes=VMEM_LIMIT_BYTES)


def _rms(x, g):
    ms = jnp.mean(x * x, axis=-1, keepdims=True)
    return (x * lax.rsqrt(ms + RMS_EPS)) * g


def _dot(a, b):
    return jnp.dot(a, b, preferred_element_type=F32)


def _shr(x, pow2):
    return lax.shift_right_logical(x, int(pow2).bit_length() - 1)


def _dot_t(a, b):
    return lax.dot_general(a, b, (((1,), (1,)), ((), ())), preferred_element_type=F32)


def _resident(shape, index_map):
    return pl.BlockSpec(shape, index_map, pipeline_mode=pl.Buffered(1))


def _ffn_kernel(x_ref, g_ref, wg_ref, wu_ref, wo_ref, *rest, final):
    if final:
        gf_ref, o_ref, xn_ref = rest
    else:
        o_ref, xn_ref = rest
    j = pl.program_id(1)

    @pl.when(j == 0)
    def _():
        x = x_ref[...]
        xn_ref[...] = _rms(x, g_ref[...]).astype(xn_ref.dtype)
        o_ref[...] = x

    xn = xn_ref[...]
    gate = _dot(xn, wg_ref[...])
    up = _dot(xn, wu_ref[...])
    h = (0.5 * gate) * jax.nn.sigmoid(gate) * up
    o_ref[...] += _dot(h.astype(MXU_DTYPE), wo_ref[...])

    if final:
        @pl.when(j == pl.num_programs(1) - 1)
        def _():
            o_ref[...] = _rms(o_ref[...], gf_ref[...])


def _ffn(x2, g, w_in, wo, gf=None, *, tm=1024, tf=FFN_TILE):
    n, d = x2.shape
    dffp = wo.shape[0]
    nff = dffp // tf
    tm = min(tm, n)
    final = gf is not None
    in_specs = [
        _resident((tm, d), lambda i, j: (i, 0)),
        pl.BlockSpec((1, d), lambda i, j: (0, 0)),
        pl.BlockSpec((d, tf), lambda i, j: (0, j)),
        pl.BlockSpec((d, tf), lambda i, j: (0, nff + j)),
        pl.BlockSpec((tf, d), lambda i, j: (j, 0)),
    ]
    args = [x2, g.reshape(1, d), w_in, w_in, wo]
    if final:
        in_specs.append(pl.BlockSpec((1, d), lambda i, j: (0, 0)))
        args.append(gf.reshape(1, d))
    return pl.pallas_call(
        functools.partial(_ffn_kernel, final=final),
        grid=(n // tm, nff),
        in_specs=in_specs,
        out_specs=pl.BlockSpec((tm, d), lambda i, j: (i, 0)),
        out_shape=jax.ShapeDtypeStruct((n, d), F32),
        scratch_shapes=[pltpu.VMEM((tm, d), MXU_DTYPE)],
        compiler_params=_cparams(("parallel", "arbitrary")),
        name="ffn_final" if final else "ffn",
    )(*args)


def _prep_ffn(w_in, w_out, tf=FFN_TILE):
    d, two_dff = w_in.shape
    dff = two_dff // 2
    dffp = -(-dff // tf) * tf
    z = jnp.zeros((d, dffp - dff), MXU_DTYPE)
    w = jnp.concatenate([w_in[:, :dff].astype(MXU_DTYPE), z, w_in[:, dff:].astype(MXU_DTYPE), z], axis=1)
    wo = jnp.pad(w_out, ((0, dffp - dff), (0, 0))).astype(MXU_DTYPE)
    return w, wo


def _mixin_body(x_ref, g_ref, wm_ref, wd_ref, wa_ref, wvt_ref, hm_ref, a_ref, vt_ref, dil_refs, relayout_scr):
    xn = _rms(x_ref[...], g_ref[...]).astype(MXU_DTYPE)
    hm_ref[...] = _dot(xn, wm_ref[...]).astype(hm_ref.dtype)
    a_ref[...] = _dot(xn, wa_ref[...])
    vt_ref[...] = _dot_t(wvt_ref[...], xn).astype(vt_ref.dtype)
    hd = _dot(xn, wd_ref[...])
    tm = hd.shape[0]
    for g, out_ref in enumerate(dil_refs):
        dilation = DIL_PAIRS[g][1]
        cols = hd[:, g * DIL_GROUP_COLS:(g + 1) * DIL_GROUP_COLS]
        if dilation == 1:
            out_ref[...] = cols.astype(out_ref.dtype)
            continue
        for c in range(DIL_GROUP_COLS // LANES):
            relayout_scr[c] = cols[:, c * LANES:(c + 1) * LANES]
        for r in range(dilation):
            for c in range(DIL_GROUP_COLS // LANES):
                piece = relayout_scr[c, pl.ds(r, tm // dilation, stride=dilation), :]
                out_ref[:, r * DIL_GROUP_COLS + c * LANES:r * DIL_GROUP_COLS + (c + 1) * LANES] = (
                    piece.astype(out_ref.dtype))


def _mixin_kernel(x_ref, g_ref, wm_ref, wd_ref, wa_ref, wvt_ref, hm_ref, a_ref, vt_ref, d0_ref, d1_ref,
                  d2_ref, relayout_scr):
    _mixin_body(x_ref, g_ref, wm_ref, wd_ref, wa_ref, wvt_ref, hm_ref, a_ref, vt_ref,
                (d0_ref, d1_ref, d2_ref), relayout_scr)


def _mixin(x2, g, wm, wd, wa, wvt, *, tm=512):
    n, d = x2.shape
    weight = lambda a: _resident(a.shape, lambda i: (0, 0))
    dils = [dil for _, dil in DIL_PAIRS]
    return pl.pallas_call(
        _mixin_kernel,
        grid=(n // tm,),
        in_specs=[
            pl.BlockSpec((tm, d), lambda i: (i, 0)),
            pl.BlockSpec((1, d), lambda i: (0, 0)),
            weight(wm), weight(wd), weight(wa), weight(wvt),
        ],
        out_specs=[
            pl.BlockSpec((tm, H_COLS), lambda i: (i, 0)),
            pl.BlockSpec((tm, AUX_COLS), lambda i: (i, 0)),
            pl.BlockSpec((2 * LANES, tm), lambda i: (0, i)),
        ] + [pl.BlockSpec((tm // dil, dil * DIL_GROUP_COLS), lambda i: (i, 0)) for dil in dils],
        out_shape=[
            jax.ShapeDtypeStruct((n, H_COLS), MXU_DTYPE),
            jax.ShapeDtypeStruct((n, AUX_COLS), F32),
            jax.ShapeDtypeStruct((2 * LANES, n), MXU_DTYPE),
        ] + [jax.ShapeDtypeStruct((n // dil, dil * DIL_GROUP_COLS), MXU_DTYPE) for dil in dils],
        scratch_shapes=[pltpu.VMEM((DIL_GROUP_COLS // LANES, tm, LANES), F32)],
        compiler_params=_cparams(("parallel",)),
        name="mix_in",
    )(x2, g.reshape(1, d), wm, wd, wa, wvt)


def _rot_half_cols(w):
    half = w.shape[1] // 2
    return jnp.concatenate([-w[:, half:], w[:, :half]], axis=1)


def _prep_mixin(w):
    d = w.shape[0]
    o = 0
    parts = {}
    for name, width in (("q_lat", 512), ("kv_lat", 512), ("k_rope", 64), ("dq", 768), ("dk", 768),
                        ("dv", 768), ("nq", 512), ("nkc", 128), ("nvc", 128), ("nks", 128),
                        ("nvs", 128), ("nkw", 128), ("nvw", 128), ("ng", 12)):
        parts[name] = w[:, o:o + width]
        o += width
    z64 = jnp.zeros((d, 64), w.dtype)
    kr = parts["k_rope"]
    wm = jnp.concatenate([
        parts["q_lat"], parts["kv_lat"], parts["nq"],
        kr, z64, _rot_half_cols(kr), z64,
        parts["nks"], parts["nkw"]], axis=1).astype(MXU_DTYPE)
    gw = 2 * HEAD_DIM
    wd = jnp.concatenate([parts[name][:, g * gw:(g + 1) * gw] for g in range(len(DIL_PAIRS))
                          for name in ("dq", "dk", "dv")], axis=1).astype(MXU_DTYPE)
    wa = jnp.concatenate([parts["nkc"], parts["nvc"], parts["ng"],
                          jnp.zeros((d, LANES - 12), w.dtype)], axis=1).astype(MXU_DTYPE)
    wvt = jnp.concatenate([parts["nvs"], parts["nvw"]], axis=1).T.astype(MXU_DTYPE)
    return wm, wd, wa, wvt


def _rope_tables_kernel(invf_ref, cos_ref, sin_ref):
    tm = cos_ref.shape[0]
    pos = (pl.program_id(0) * tm + lax.broadcasted_iota(jnp.int32, (tm, 1), 0)).astype(F32)
    ang = pos * invf_ref[...]
    cos_ref[...] = jnp.cos(ang)
    sin_ref[...] = jnp.sin(ang)


def _rope_tables(seq, *, tm=512):
    tm = min(tm, seq)
    spec = pl.BlockSpec((tm, LANES), lambda i: (i, 0))
    sds = jax.ShapeDtypeStruct((seq, LANES), F32)
    return pl.pallas_call(
        _rope_tables_kernel,
        grid=(seq // tm,),
        in_specs=[pl.BlockSpec((1, LANES), lambda i: (0, 0))],
        out_specs=[spec, spec],
        out_shape=[sds, sds],
        compiler_params=_cparams(("parallel",)),
        name="rope_tables",
    )(_rope_inv_freq_row())


def _mla_proj_kernel(lat_ref, kr_ref, qn_ref, kvn_ref, wq_ref, wk_ref, wvt_ref, cos_ref, sin_ref,
                     q_ref, k_ref, vt_ref):
    cos = cos_ref[...]
    sin = sin_ref[...]
    scale = (MLA_NOPE + MLA_ROPE) ** -0.5 * LOG2E

    lat = lat_ref[...].astype(F32)
    qn = _rms(lat[:, :MLA_LORA], qn_ref[...]).astype(MXU_DTYPE)
    kvn = _rms(lat[:, MLA_LORA:], kvn_ref[...]).astype(MXU_DTYPE)
    qm = _dot(qn, wq_ref[...])
    rot0 = MLA_HEADS * MLA_QK_PAD
    for h in range(MLA_HEADS):
        c = h * MLA_QK_PAD
        nope = qm[:, c:c + LANES]
        pe = qm[:, c + LANES:c + 2 * LANES] * cos + qm[:, rot0 + h * LANES:rot0 + (h + 1) * LANES] * sin
        q_ref[:, c:c + LANES] = (nope * scale).astype(q_ref.dtype)
        q_ref[:, c + LANES:c + 2 * LANES] = (pe * scale).astype(q_ref.dtype)

    kr = kr_ref[...].astype(F32)
    kpe = (kr[:, :LANES] * cos + kr[:, LANES:] * sin).astype(k_ref.dtype)
    kn = _dot(kvn, wk_ref[...])
    for h in range(MLA_HEADS):
        c = h * MLA_QK_PAD
        k_ref[:, c:c + LANES] = kn[:, h * LANES:(h + 1) * LANES].astype(k_ref.dtype)
        k_ref[:, c + LANES:c + 2 * LANES] = kpe
    vt_ref[...] = _dot_t(wvt_ref[...], kvn).astype(vt_ref.dtype)


def _mla_proj(h2, qn, kvn, wq, wk, wvt, cos, sin, *, tm=512):
    n = h2.shape[0]
    tm = min(tm, cos.shape[0])
    tiles_per_seq = cos.shape[0] // tm
    qk_cols = MLA_HEADS * MLA_QK_PAD
    v_rows = MLA_HEADS * MLA_V
    full = lambda a: pl.BlockSpec(a.shape, lambda i: (0,) * a.ndim)
    table = pl.BlockSpec((tm, LANES), lambda i: (i % tiles_per_seq, 0))
    return pl.pallas_call(
        _mla_proj_kernel,
        grid=(n // tm,),
        in_specs=[
            pl.BlockSpec((tm, 2 * MLA_LORA), lambda i: (i, 0)),
            pl.BlockSpec((tm, 2 * LANES), lambda i: (i, COL_KROPE // (2 * LANES))),
            full(qn), full(kvn), full(wq), full(wk), full(wvt), table, table,
        ],
        out_specs=[
            pl.BlockSpec((tm, qk_cols), lambda i: (i, 0)),
            pl.BlockSpec((tm, qk_cols), lambda i: (i, 0)),
            pl.BlockSpec((v_rows, tm), lambda i: (0, i)),
        ],
        out_shape=[
            jax.ShapeDtypeStruct((n, qk_cols), MXU_DTYPE),
            jax.ShapeDtypeStruct((n, qk_cols), MXU_DTYPE),
            jax.ShapeDtypeStruct((v_rows, n), MXU_DTYPE),
        ],
        compiler_params=_cparams(("parallel",)),
        name="mla_proj",
    )(h2, h2, qn, kvn, wq, wk, wvt, cos, sin)


def _prep_mla(w_uq, w_ukv):
    r = MLA_LORA
    wq3 = w_uq.reshape(r, MLA_HEADS, MLA_NOPE + MLA_ROPE)
    nope, pe = wq3[..., :MLA_NOPE], wq3[..., MLA_NOPE:]
    z = jnp.zeros((r, MLA_HEADS, 64), w_uq.dtype)
    main = jnp.concatenate([nope, pe, z], axis=-1).reshape(r, MLA_HEADS * MLA_QK_PAD)
    half = MLA_ROPE // 2
    pe_rot = jnp.concatenate([-pe[..., half:], pe[..., :half]], axis=-1)
    rot = jnp.concatenate([pe_rot, z], axis=-1).reshape(r, MLA_HEADS * LANES)
    wq = jnp.concatenate([main, rot], axis=1).astype(MXU_DTYPE)
    wkv3 = w_ukv.reshape(r, MLA_HEADS, MLA_NOPE + MLA_V)
    wk = wkv3[..., :MLA_NOPE].reshape(r, MLA_HEADS * MLA_NOPE).astype(MXU_DTYPE)
    wvt = wkv3[..., MLA_NOPE:].reshape(r, MLA_HEADS * MLA_V).T.astype(MXU_DTYPE)
    return wq, wk, wvt


def _rope_inv_freq_row():
    half = MLA_ROPE // 2
    f = ROPE_BASE ** (-jnp.arange(half, dtype=F32) / half)
    return jnp.concatenate([f, f, jnp.zeros((LANES - MLA_ROPE,), F32)]).reshape(1, LANES)


def _with_ones_rows(v_t):
    return jnp.concatenate([v_t, jnp.ones((DEN_ROWS, v_t.shape[1]), v_t.dtype)], axis=0)


def _softmax_piece_t(scores, v_aug, m_ref, acc_ref, *, first):
    if first:
        s_t = scores()
        m_new = jnp.max(s_t, axis=0, keepdims=True)
        acc_ref[...] = _dot(v_aug, jnp.exp2(s_t - m_new).astype(MXU_DTYPE))
        m_ref[...] = m_new
        return

    m_old = m_ref[...]
    s_t = scores()
    pv = _dot(v_aug, jnp.exp2(s_t - m_old).astype(MXU_DTYPE))
    fits = jnp.max(jnp.max(s_t, axis=0, keepdims=True) - m_old) <= LAZY_MAX_SLACK

    @pl.when(fits)
    def _():
        acc_ref[...] += pv

    @pl.when(jnp.logical_not(fits))
    def _():
        s_again = scores()
        m_new = jnp.maximum(m_old, jnp.max(s_again, axis=0, keepdims=True))
        weights = jnp.exp2(s_again - m_new).astype(MXU_DTYPE)
        acc_ref[...] = jnp.exp2(m_old - m_new) * acc_ref[...] + _dot(v_aug, weights)
        m_ref[...] = m_new


def _softmax_result(acc_scr, dv):
    return acc_scr[0:dv, :] * (1.0 / acc_scr[dv:dv + 1, :])


def _flash_kernel(qt_ref, kt_ref, q_ref, k_ref, vt_ref, o_ref, m_scr, acc_scr, *, t, dsub):
    step = pl.program_id(2)
    qi = qt_ref[step]
    ki = kt_ref[step]

    def piece(k0, q0, size, diagonal):
        ks = slice(k0, k0 + size)
        qs = slice(q0, q0 + size)

        def scores():
            s_t = _dot_t(k_ref[0, ks, :], q_ref[0, qs, :])
            if diagonal and k0 + size - 1 > q0:
                kpos = k0 + lax.broadcasted_iota(jnp.int32, (size, 1), 0)
                qpos = q0 + lax.broadcasted_iota(jnp.int32, (1, size), 1)
                s_t = jnp.where(kpos <= qpos, s_t, NEG_INF)
            return s_t

        _softmax_piece_t(scores, _with_ones_rows(vt_ref[:, ks]), m_scr.at[:, qs], acc_scr.at[:, qs],
                         first=diagonal and k0 == q0)

    @pl.when(ki == qi)
    def _():
        starts = range(0, t, dsub)
        for p in starts:
            piece(p, p, dsub, True)
        for k0 in starts:
            for q0 in starts:
                if q0 > k0:
                    piece(k0, q0, dsub, True)

    @pl.when(ki < qi)
    def _():
        piece(0, 0, t, False)

    @pl.when(ki == 0)
    def _():
        o_ref[0] = _softmax_result(acc_scr, MLA_V).T.astype(o_ref.dtype)


def _mla_flash(q3, k3, vt, *, t=2048, dsub=1024):
    b, s, _ = q3.shape
    t = min(t, s)
    dsub = min(dsub, t)
    nq = s // t
    pairs = [(qi, ki) for qi in range(nq) for ki in range(qi, -1, -1)]
    qt = jnp.asarray(np.array([p[0] for p in pairs], np.int32))
    kt = jnp.asarray(np.array([p[1] for p in pairs], np.int32))
    grid_spec = pltpu.PrefetchScalarGridSpec(
        num_scalar_prefetch=2,
        grid=(b, MLA_HEADS, len(pairs)),
        in_specs=[
            pl.BlockSpec((1, t, MLA_QK_PAD), lambda bi, h, st, qt, kt: (bi, qt[st], h)),
            pl.BlockSpec((1, t, MLA_QK_PAD), lambda bi, h, st, qt, kt: (bi, kt[st], h)),
            pl.BlockSpec((MLA_V, t), lambda bi, h, st, qt, kt: (h, bi * nq + kt[st])),
        ],
        out_specs=pl.BlockSpec((1, t, MLA_V), lambda bi, h, st, qt, kt: (bi, qt[st], h)),
        scratch_shapes=[pltpu.VMEM((1, t), F32), pltpu.VMEM((MLA_V + DEN_ROWS, t), F32)],
    )
    return pl.pallas_call(
        functools.partial(_flash_kernel, t=t, dsub=dsub),
        grid_spec=grid_spec,
        out_shape=jax.ShapeDtypeStruct((b, s, MLA_HEADS * MLA_V), MXU_DTYPE),
        compiler_params=_cparams(("parallel", "parallel", "arbitrary")),
        name="mla_flash",
    )(qt, kt, q3, k3, vt)


def _dil_kernel(q_ref, kc_ref, kp_ref, vc_ref, vp_ref, o_ref, lse_ref, *, dilation, span, slopes, tl):
    n = pl.program_id(2)
    scale = HEAD_DIM ** -0.5
    blk = DIL_BLOCK
    a = lax.broadcasted_iota(jnp.int32, (blk, 1), 0)
    c = lax.broadcasted_iota(jnp.int32, (1, 2 * blk), 1)
    j = blk + a - c
    in_band = (j >= 0) & (j <= span)
    first_valid = in_band & ((c >= blk) | (n > 0))
    dist = (j * dilation).astype(F32)
    for hg in range(2):
        cols = slice(hg * LANES, (hg + 1) * LANES)
        bias = slopes[hg] * dist
        for sb in range(tl // blk):
            rows = slice(sb * blk, (sb + 1) * blk)
            q = (q_ref[0, rows, cols].astype(F32) * scale).astype(MXU_DTYPE)
            if sb == 0:
                kprev, vprev, valid = kp_ref[0, :, cols], vp_ref[0, :, cols], first_valid
            else:
                prev = slice((sb - 1) * blk, sb * blk)
                kprev, vprev, valid = kc_ref[0, prev, cols], vc_ref[0, prev, cols], in_band
            keys = jnp.concatenate([kprev, kc_ref[0, rows, cols]], axis=0)
            vals = jnp.concatenate([vprev, vc_ref[0, rows, cols]], axis=0)
            s = jnp.where(valid, _dot_t(q, keys) - bias, NEG_INF)
            m = jnp.max(s, axis=1, keepdims=True)
            e = jnp.where(valid, jnp.exp(s - m), 0.0)
            den = jnp.sum(e, axis=1, keepdims=True)
            o_ref[0, rows, cols] = _dot((e / den).astype(MXU_DTYPE), vals)
            lse_ref[0, rows, cols] = jnp.broadcast_to(m + jnp.log(den), (blk, LANES))


def _dilated_group(hg, g, b, *, tl=512):
    window, dilation = DIL_PAIRS[g]
    span = window // dilation
    seq_l = hg.shape[0] // b
    tl = min(tl, seq_l)
    per_blk = tl // DIL_BLOCK
    w = 2 * LANES
    hv = hg.reshape(b, seq_l, dilation * DIL_GROUP_COLS)
    qc, kc, vc = 0, 1, 2
    stride = DIL_GROUP_COLS // w
    cur = lambda col: pl.BlockSpec((1, tl, w), lambda bi, r, n: (bi, n, r * stride + col))
    prev = lambda col: pl.BlockSpec(
        (1, DIL_BLOCK, w), lambda bi, r, n: (bi, jnp.maximum(n * per_blk - 1, 0), r * stride + col))
    out_spec = pl.BlockSpec((1, tl, w), lambda bi, r, n: (bi, n, r))
    out_sds = jax.ShapeDtypeStruct((b, seq_l, dilation * w), F32)
    o, lse = pl.pallas_call(
        functools.partial(_dil_kernel, dilation=dilation, span=span,
                          slopes=ALIBI_SLOPES[2 * g:2 * g + 2], tl=tl),
        grid=(b, dilation, seq_l // tl),
        in_specs=[cur(qc), cur(kc), prev(kc), cur(vc), prev(vc)],
        out_specs=[out_spec, out_spec],
        out_shape=[out_sds, out_sds],
        compiler_params=_cparams(("parallel", "parallel", "parallel")),
        name=f"dilated_g{g}",
    )(hv, hv, hv, hv, hv)
    return o.reshape(b * seq_l, dilation * w), lse.reshape(b * seq_l, dilation * w)


def _nsa_cmp_kernel(x_ref, pos_ref, wa_ref, wb_ref, w2k_ref, w2vt_ref, kc_ref, vct_ref):
    x = x_ref[0]
    xa = (x + pos_ref[0:1, :]).astype(MXU_DTYPE)
    xb = (x + pos_ref[1:2, :]).astype(MXU_DTYPE)
    first = _dot(xa, wa_ref[...])
    second = _dot(xb, wb_ref[...])
    second_next = pltpu.roll(second, x.shape[0] - 1, 0)
    pre = first + second_next
    hid = (pre * jax.nn.sigmoid(pre)).astype(MXU_DTYPE)
    kc_ref[0] = _dot(hid[:, :LANES], w2k_ref[...]).astype(kc_ref.dtype)
    vct_ref[0] = _dot_t(w2vt_ref[...], hid[:, LANES:]).astype(vct_ref.dtype)


def _nsa_compress(aux3, pos2, wa, wb, w2k, w2vt):
    b, s, _ = aux3.shape
    nchunk = s // NSA_CMP_STRIDE
    xk = aux3[:, :, :2 * LANES].reshape(b, nchunk, NSA_CMP_STRIDE * 2 * LANES)
    full = lambda a: pl.BlockSpec(a.shape, lambda bi: (0,) * a.ndim)
    return pl.pallas_call(
        _nsa_cmp_kernel,
        grid=(b,),
        in_specs=[pl.BlockSpec((1, nchunk, xk.shape[2]), lambda bi: (bi, 0, 0)),
                  full(pos2), full(wa), full(wb), full(w2k), full(w2vt)],
        out_specs=[pl.BlockSpec((1, nchunk, LANES), lambda bi: (bi, 0, 0)),
                   pl.BlockSpec((1, LANES, nchunk), lambda bi: (bi, 0, 0))],
        out_shape=[jax.ShapeDtypeStruct((b, nchunk, LANES), MXU_DTYPE),
                   jax.ShapeDtypeStruct((b, LANES, nchunk), MXU_DTYPE)],
        compiler_params=_cparams(("parallel",)),
        name="nsa_cmp",
    )(xk, pos2, wa, wb, w2k, w2vt)


def _prep_nsa_cmp(cmp_pos, phi_k1, phi_k2, phi_v1, phi_v2):
    half = NSA_CMP_LEN // 2
    pk = cmp_pos.reshape(2, half, HEAD_DIM)
    pos2 = jnp.concatenate([pk, pk], axis=-1).reshape(2, half * 2 * LANES)

    def halves(w1, is_v):
        w = w1.reshape(2, half, HEAD_DIM, HEAD_DIM)
        z = jnp.zeros_like(w)
        w = jnp.concatenate([z, w] if is_v else [w, z], axis=2)
        return w.reshape(2, half * 2 * LANES, HEAD_DIM)

    wk, wv = halves(phi_k1, False), halves(phi_v1, True)
    wa = jnp.concatenate([wk[0], wv[0]], axis=1).astype(MXU_DTYPE)
    wb = jnp.concatenate([wk[1], wv[1]], axis=1).astype(MXU_DTYPE)
    return pos2, wa, wb, phi_k2.astype(MXU_DTYPE), phi_v2.T.astype(MXU_DTYPE)


def _split3(x):
    hi = x.astype(MXU_DTYPE)
    r = x - hi.astype(F32)
    mid = r.astype(MXU_DTYPE)
    lo = (r - mid.astype(F32)).astype(MXU_DTYPE)
    return hi, mid, lo


def _pos_features(pos):
    hi = jnp.floor(pos / POS_SPLIT) * POS_SPLIT
    lo = pos - hi
    cols = jnp.stack([hi, hi, hi, lo, lo, lo], axis=1)
    return jnp.pad(cols, ((0, 0), (0, LANES - 6))).astype(MXU_DTYPE)


def _slope_features(tq):
    sig = jnp.asarray([s * LOG2E for s in ALIBI_SLOPES[DIL_HEADS:]], F32)
    pieces = jnp.stack(_split3(sig), axis=1)
    rows = jnp.concatenate([pieces, pieces], axis=1)
    rows = jnp.pad(rows, ((0, 0), (0, LANES - 6)))
    return jnp.repeat(rows, tq, axis=0)


def _masked_softmax_t(s_t, mask):
    s_t = jnp.where(mask, s_t, NEG_INF)
    m = jnp.max(s_t, axis=0, keepdims=True)
    e = jnp.exp2(s_t - m)
    den = jnp.sum(e, axis=0, keepdims=True)
    return e * jnp.where(m > 0.5 * NEG_INF, 1.0 / den, 0.0)


def _nsa_kernel(q_ref, k_ref, vt_ref, kc_ref, vct_ref, c2st_ref, pf_ref, cpf_ref, sf_ref, wband_ref, g_ref,
                o_ref, m_scr, acc_scr, bias_scr, any_scr, ocmp_scr, score_scr,
                *, tq, tk, topk, cmp_chunk):
    i = pl.program_id(1)
    t0 = i * tq
    nh = NSA_HEADS
    cols = nh * tq
    ncp = kc_ref.shape[1]
    ns = c2st_ref.shape[0]
    blocks_per_tile = tk // NSA_SEL_LEN

    q = q_ref[0]
    q4 = jnp.concatenate([q[:, h * LANES:(h + 1) * LANES] for h in range(nh)], axis=0)
    q4 = (q4.astype(F32) * (HEAD_DIM ** -0.5 * LOG2E)).astype(MXU_DTYPE)
    q4 = jnp.concatenate([q4, sf_ref[...]], axis=1)
    col = lax.broadcasted_iota(jnp.int32, (1, cols), 1)
    tpos = (t0 + (col & (tq - 1))).astype(F32)

    def cmp_part(rows):
        cidx = lax.broadcasted_iota(jnp.int32, (rows, 1), 0).astype(F32)
        c_end = cidx * NSA_CMP_STRIDE + (NSA_CMP_LEN - 1)
        kc = jnp.concatenate([kc_ref[0, 0:rows, :], cpf_ref[0:rows, :]], axis=1)
        p_cmp = _masked_softmax_t(_dot_t(kc, q4), c_end <= tpos)
        ocmp_scr[...] = _dot(vct_ref[0, :, 0:rows], p_cmp.astype(MXU_DTYPE))
        p_sum = p_cmp[:, 0:tq]
        for h in range(1, nh):
            p_sum = p_sum + p_cmp[:, h * tq:(h + 1) * tq]
        c2st = c2st_ref[:, 0:rows]
        if MXU_DTYPE == jnp.float32:
            score_scr[...] = _dot(c2st, p_sum)
        else:
            score_scr[...] = sum(_dot(c2st, piece) for piece in _split3(p_sum))

    n_variants = -(-ncp // cmp_chunk)
    n_ending = (t0 + tq) // NSA_CMP_STRIDE - 1
    need = jnp.clip((n_ending + cmp_chunk - 1) // cmp_chunk, 1, n_variants)
    for v in range(1, n_variants + 1):
        pl.when(need == v)(functools.partial(cmp_part, min(v * cmp_chunk, ncp)))
    o_cmp = ocmp_scr[...]
    score = score_scr[...]

    t1 = t0 + lax.broadcasted_iota(jnp.int32, (1, tq), 1)
    cur = _shr(t1, NSA_SEL_LEN).astype(F32)
    jj = lax.broadcasted_iota(jnp.int32, (ns, 1), 0).astype(F32)
    forced = (jj == 0.0) | (jj == cur) | (jj == cur - 1.0)
    n_forced = 1.0 + jnp.where(cur >= 1.0, 1.0, 0.0) + jnp.where(cur >= 2.0, 1.0, 0.0)
    score = jnp.where(jj > cur, -1.0, jnp.where(forced, -2.0, score))
    bias = jnp.where(forced, 0.0, NEG_INF)

    def pick(score, bias, wanted):
        best = jnp.max(score, axis=0, keepdims=True)
        idx = jnp.min(jnp.where(score == best, jj, float(ns)), axis=0, keepdims=True)
        hit = jj == idx
        if wanted is not None:
            hit = hit & wanted
        return jnp.where(hit, -2.0, score), jnp.where(hit, 0.0, bias)

    max_forced = 3
    for _ in range(topk - max_forced):
        score, bias = pick(score, bias, None)
    bias_scr[...] = bias
    score_scr[...] = score

    @pl.when(t0 < (max_forced - 1) * NSA_SEL_LEN)
    def _():
        score, bias = score_scr[...], bias_scr[...]
        for done in range(topk - max_forced, topk - 1):
            score, bias = pick(score, bias, n_forced + done < float(topk))
        bias_scr[...] = bias

    bias = bias_scr[...]
    any_scr[...] = jnp.broadcast_to(jnp.max(bias, axis=1, keepdims=True), (ns, LANES))

    def slc_update(kt, causal):
        k0 = pl.multiple_of(kt * tk, tk)
        b0 = pl.multiple_of(kt * blocks_per_tile, blocks_per_tile)

        def scores():
            keys = jnp.concatenate([k_ref[0, pl.ds(k0, tk), 0:LANES], pf_ref[pl.ds(k0, tk), :]], axis=1)
            s_t = _dot_t(keys, q4)
            rows = [jnp.broadcast_to(bias_scr[pl.ds(b0 + r, 1), :], (NSA_SEL_LEN, tq))
                    for r in range(blocks_per_tile)]
            sel_bias = jnp.concatenate(rows, axis=0)
            s_t = s_t + jnp.concatenate([sel_bias] * nh, axis=1)
            if causal:
                kpos = (k0 + lax.broadcasted_iota(jnp.int32, (tk, 1), 0)).astype(F32)
                s_t = jnp.where(kpos <= tpos, s_t, NEG_INF)
            return s_t

        _softmax_piece_t(scores, _with_ones_rows(vt_ref[0:LANES, pl.ds(k0, tk)]), m_scr, acc_scr,
                         first=causal)

    last = t0 // tk
    slc_update(last, True)

    def slc_tile(back, carry):
        kt = last - 1 - back
        b0 = pl.multiple_of(kt * blocks_per_tile, blocks_per_tile)
        touched = jnp.max(any_scr[pl.ds(b0, blocks_per_tile), :]) == 0.0
        pl.when(touched)(functools.partial(slc_update, kt, False))
        return carry

    lax.fori_loop(0, last, slc_tile, 0)
    o_slc = _softmax_result(acc_scr, HEAD_DIM)

    wlen = NSA_WINDOW + tq
    ws = pl.multiple_of(jnp.maximum(t0 - NSA_WINDOW, 0), tq)
    keys = jnp.concatenate([k_ref[0, pl.ds(ws, wlen), LANES:2 * LANES], pf_ref[pl.ds(ws, wlen), :]], axis=1)
    band = wband_ref[jnp.minimum(i, wband_ref.shape[0] - 1)]
    s_t = _dot_t(keys, q4) + jnp.concatenate([band] * nh, axis=1)
    weights = jnp.exp2(s_t - jnp.max(s_t, axis=0, keepdims=True)).astype(MXU_DTYPE)
    win = _dot(_with_ones_rows(vt_ref[LANES:2 * LANES, pl.ds(ws, wlen)]), weights)
    o_win = win[0:HEAD_DIM, :] * (1.0 / win[HEAD_DIM:HEAD_DIM + 1, :])

    gates = jax.nn.sigmoid(g_ref[0]).T
    for h in range(nh):
        c = slice(h * tq, (h + 1) * tq)
        o = (gates[3 * h:3 * h + 1, :] * o_cmp[:, c] + gates[3 * h + 1:3 * h + 2, :] * o_slc[:, c]
             + gates[3 * h + 2:3 * h + 3, :] * o_win[:, c])
        o_ref[0, :, h * LANES:(h + 1) * LANES] = o.T.astype(o_ref.dtype)


def _nsa(h3, aux3, vt, kc, vct, *, tq=NSA_Q_BLOCK, tk=512):
    b, s, _ = h3.shape
    ncp = kc.shape[1]
    ns = s // NSA_SEL_LEN
    tk = min(tk, s)
    w = NSA_HEADS * HEAD_DIM
    c2st = _cmp_to_sel_t(ncp, ns)
    pf = _pos_features(jnp.arange(s, dtype=F32))
    cpf = _pos_features(jnp.arange(ncp, dtype=F32) * NSA_CMP_STRIDE + 0.5 * (NSA_CMP_LEN - 1))
    sf = _slope_features(tq).astype(MXU_DTYPE)
    wband = _window_band_bias(tq)
    const = lambda a: _resident(a.shape, lambda bi, i: (0,) * a.ndim)
    return pl.pallas_call(
        functools.partial(_nsa_kernel, tq=tq, tk=tk, topk=min(NSA_TOPK, ns), cmp_chunk=min(256, ncp)),
        grid=(b, s // tq),
        in_specs=[
            pl.BlockSpec((1, tq, w), lambda bi, i: (bi, i, COL_NSA_Q // w)),
            _resident((1, s, 2 * LANES), lambda bi, i: (bi, 0, COL_NSA_K // (2 * LANES))),
            _resident((2 * LANES, s), lambda bi, i: (0, bi)),
            _resident((1, ncp, LANES), lambda bi, i: (bi, 0, 0)),
            _resident((1, LANES, ncp), lambda bi, i: (bi, 0, 0)),
            const(c2st), const(pf), const(cpf), const(sf), const(wband),
            pl.BlockSpec((1, tq, LANES), lambda bi, i: (bi, i, 2)),
        ],
        out_specs=pl.BlockSpec((1, tq, w), lambda bi, i: (bi, i, 0)),
        out_shape=jax.ShapeDtypeStruct((b, s, w), MXU_DTYPE),
        scratch_shapes=[pltpu.VMEM((1, NSA_HEADS * tq), F32),
                        pltpu.VMEM((HEAD_DIM + DEN_ROWS, NSA_HEADS * tq), F32),
                        pltpu.VMEM((ns, tq), F32), pltpu.VMEM((ns, LANES), F32),
                        pltpu.VMEM((HEAD_DIM, NSA_HEADS * tq), F32), pltpu.VMEM((ns, tq), F32)],
        compiler_params=_cparams(("parallel", "arbitrary")),
        name="nsa",
    )(h3, h3, vt, kc, vct, c2st, pf, cpf, sf, wband, aux3)


def _window_band_bias(tq):
    wlen = NSA_WINDOW + tq
    u = np.arange(tq)[None, :]
    w = np.arange(wlen)[:, None]
    offsets = list(range(0, NSA_WINDOW, tq)) + [NSA_WINDOW]
    tiles = []
    for off in offsets:
        dist = off + u - w
        tiles.append(np.where((dist >= 0) & (dist < NSA_WINDOW), 0.0, NEG_INF))
    return jnp.asarray(np.stack(tiles).astype(np.float32))


def _cmp_to_sel_t(ncp, ns):
    nc = ncp - 1
    c = np.arange(ncp)[None, :]
    j = np.arange(ns)[:, None]
    start = c * NSA_CMP_STRIDE
    m = (start < (j + 1) * NSA_SEL_LEN) & (start + NSA_CMP_LEN - 1 >= j * NSA_SEL_LEN) & (c < nc)
    return jnp.asarray(m.astype(np.float32)).astype(MXU_DTYPE)


def _token_rows(ref, dilation, scr):
    if dilation == 1:
        return ref[...]
    tm = ref.shape[0] * dilation
    for r in range(dilation):
        for c in range(2):
            lo = r * 2 * LANES + c * LANES
            scr[c, pl.ds(r, tm // dilation, stride=dilation), :] = ref[:, lo:lo + LANES]
    return jnp.concatenate([scr[0], scr[1]], axis=1)


def _mixout_kernel(x_ref, mla_ref, d0_ref, d1_ref, d2_ref, l0_ref, l1_ref, l2_ref, nsa_ref, w_ref, o_ref,
                   relayout_scr):
    dils = [dil for _, dil in DIL_PAIRS]
    o_dil = [_token_rows(r, dil, relayout_scr.at[2 * g:2 * g + 2])
             for g, (r, dil) in enumerate(zip((d0_ref, d1_ref, d2_ref), dils))]
    lse = [_token_rows(r, dil, relayout_scr.at[6 + 2 * g:8 + 2 * g])
           for g, (r, dil) in enumerate(zip((l0_ref, l1_ref, l2_ref), dils))]
    top = jnp.maximum(jnp.maximum(lse[0], lse[1]), lse[2])
    e = [jnp.exp(l - top) for l in lse]
    inv = 1.0 / (e[0] + e[1] + e[2])
    acc = x_ref[...] + _dot(mla_ref[...], w_ref[0:768, :])
    for g in range(3):
        mixed = (o_dil[g] * (e[g] * inv)).astype(MXU_DTYPE)
        acc += _dot(mixed, w_ref[768 + g * 256:768 + (g + 1) * 256, :])
    acc += _dot(nsa_ref[...], w_ref[1536:2048, :])
    o_ref[...] = acc


def _mixout(x2, o_mla, o_dil, lse_dil, o_nsa, w, *, tm=512):
    n, d = x2.shape
    row = lambda width: pl.BlockSpec((tm, width), lambda i: (i, 0))
    dil_specs = [pl.BlockSpec((tm // dil, dil * 2 * LANES), lambda i: (i, 0)) for _, dil in DIL_PAIRS]
    return pl.pallas_call(
        _mixout_kernel,
        grid=(n // tm,),
        in_specs=[row(d), row(o_mla.shape[1])] + dil_specs + dil_specs + [
            row(o_nsa.shape[1]), pl.BlockSpec(w.shape, lambda i: (0, 0))],
        out_specs=row(d),
        out_shape=jax.ShapeDtypeStruct((n, d), F32),
        scratch_shapes=[pltpu.VMEM((12, tm, LANES), F32)],
        compiler_params=_cparams(("parallel",)),
        name="mix_out",
    )(x2, o_mla, *o_dil, *lse_dil, o_nsa, w)


def kernel(x, ffn1_norm, ffn1_w_in, ffn1_w_out, mix_norm, w_mix_in, mla_q_norm, mla_w_uq, mla_kv_norm,
           mla_w_ukv, nsa_cmp_pos, nsa_phi_k1, nsa_phi_k2, nsa_phi_v1, nsa_phi_v2, w_mix_out, ffn2_norm,
           ffn2_w_in, ffn2_w_out, final_norm):
    b, s, d = x.shape
    depth = ffn1_w_in.shape[0]
    n = b * s
    cos, sin = _rope_tables(s)
    x2 = x.reshape(n, d)
    for l in range(depth):
        x2 = _ffn(x2, ffn1_norm[l], *_prep_ffn(ffn1_w_in[l], ffn1_w_out[l]))

        h2, aux2, nsa_vt, *dil_in = _mixin(x2, mix_norm[l], *_prep_mixin(w_mix_in[l]))
        h3 = h2.reshape(b, s, H_COLS)
        aux3 = aux2.reshape(b, s, AUX_COLS)

        q, k, vt = _mla_proj(h2, mla_q_norm[l].reshape(1, -1), mla_kv_norm[l].reshape(1, -1),
                             *_prep_mla(mla_w_uq[l], mla_w_ukv[l]), cos, sin)
        o_mla = _mla_flash(q.reshape(b, s, -1), k.reshape(b, s, -1), vt)

        dil = [_dilated_group(hg, g, b) for g, hg in enumerate(dil_in)]

        kc, vct = _nsa_compress(aux3, *_prep_nsa_cmp(nsa_cmp_pos[l], nsa_phi_k1[l], nsa_phi_k2[l],
                                                      nsa_phi_v1[l], nsa_phi_v2[l]))
        o_nsa = _nsa(h3, aux3, nsa_vt, kc, vct)

        x2 = _mixout(x2, o_mla.reshape(n, -1), [o for o, _ in dil], [e for _, e in dil],
                     o_nsa.reshape(n, -1), w_mix_out[l].astype(MXU_DTYPE))

        gf = final_norm if l == depth - 1 else None
        x2 = _ffn(x2, ffn2_norm[l], *_prep_ffn(ffn2_w_in[l], ffn2_w_out[l]), gf)
    return x2.reshape(b, s, d)
```

```python
import functools
import math

import numpy as np
import jax
import jax.numpy as jnp
from jax import lax
from jax.experimental import pallas as pl
from jax.experimental.pallas import tpu as pltpu

F32 = jnp.float32
MXU_DTYPE = jnp.bfloat16
VMEM_LIMIT_BYTES = 56 * 1024 * 1024
LANES = 128
LOG2E = math.log2(math.e)
FFN_TILE = 512
DEN_ROWS = 16
LAZY_MAX_SLACK = 16.0

HEAD_DIM = 128
RMS_EPS = 1e-6
NEG_INF = -1e30

MLA_HEADS = 6
MLA_LORA = 512
MLA_NOPE = 128
MLA_ROPE = 64
MLA_V = 128
MLA_QK_PAD = 256
ROPE_BASE = 10000.0

DIL_PAIRS = ((128, 1), (512, 4), (2048, 16))
DIL_HEADS = 6
DIL_BLOCK = 128

NSA_HEADS = 4
NSA_CMP_LEN = 32
NSA_CMP_STRIDE = 16
NSA_SEL_LEN = 64
NSA_TOPK = 16
NSA_WINDOW = 512
NSA_FORCED_SCORE = 100.0
NSA_Q_BLOCK = 256
POS_SPLIT = 128

N_ALIBI = DIL_HEADS + NSA_HEADS
ALIBI_SLOPES = tuple(float(2.0 ** (-8.0 * i / N_ALIBI)) for i in range(1, N_ALIBI + 1))

COL_QLAT = 0
COL_KVLAT = 512
COL_NSA_Q = 1024
COL_KROPE = 1536
COL_NSA_K = 1792
H_COLS = 2048
DIL_GROUP_COLS = 768
AUX_COLS = 384


def _cparams(sem):
    return pltpu.CompilerParams(dimension_semantics=sem, vmem_limit_bytes=VMEM_LIMIT_BYTES)


def _rms(x, g):
    ms = jnp.mean(x * x, axis=-1, keepdims=True)
    return (x * lax.rsqrt(ms + RMS_EPS)) * g


def _dot(a, b):
    return jnp.dot(a, b, preferred_element_type=F32)


def _shr(x, pow2):
    return lax.shift_right_logical(x, int(pow2).bit_length() - 1)


def _dot_t(a, b):
    return lax.dot_general(a, b, (((1,), (1,)), ((), ())), preferred_element_type=F32)


def _resident(shape, index_map):
    return pl.BlockSpec(shape, index_map, pipeline_mode=pl.Buffered(1))


def _ffn_kernel(x_ref, g_ref, wg_ref, wu_ref, wo_ref, *rest, final):
    if final:
        gf_ref, o_ref, xn_ref = rest
    else:
        o_ref, xn_ref = rest
    j = pl.program_id(1)

    @pl.when(j == 0)
    def _():
        x = x_ref[...]
        xn_ref[...] = _rms(x, g_ref[...]).astype(xn_ref.dtype)
        o_ref[...] = x

    xn = xn_ref[...]
    gate = _dot(xn, wg_ref[...])
    up = _dot(xn, wu_ref[...])
    h = (0.5 * gate) * jax.nn.sigmoid(gate) * up
    o_ref[...] += _dot(h.astype(MXU_DTYPE), wo_ref[...])

    if final:
        @pl.when(j == pl.num_programs(1) - 1)
        def _():
            o_ref[...] = _rms(o_ref[...], gf_ref[...])


def _ffn(x2, g, w_in, wo, gf=None, *, tm=512, tf=FFN_TILE):
    n, d = x2.shape
    dffp = wo.shape[0]
    nff = dffp // tf
    tm = min(tm, n)
    final = gf is not None
    in_specs = [
        pl.BlockSpec((tm, d), lambda i, j: (i, 0)),
        pl.BlockSpec((1, d), lambda i, j: (0, 0)),
        pl.BlockSpec((d, tf), lambda i, j: (0, j)),
        pl.BlockSpec((d, tf), lambda i, j: (0, nff + j)),
        pl.BlockSpec((tf, d), lambda i, j: (j, 0)),
    ]
    args = [x2, g.reshape(1, d), w_in, w_in, wo]
    if final:
        in_specs.append(pl.BlockSpec((1, d), lambda i, j: (0, 0)))
        args.append(gf.reshape(1, d))
    return pl.pallas_call(
        functools.partial(_ffn_kernel, final=final),
        grid=(n // tm, nff),
        in_specs=in_specs,
        out_specs=pl.BlockSpec((tm, d), lambda i, j: (i, 0)),
        out_shape=jax.ShapeDtypeStruct((n, d), F32),
        scratch_shapes=[pltpu.VMEM((tm, d), MXU_DTYPE)],
        compiler_params=_cparams(("parallel", "arbitrary")),
        name="ffn_final" if final else "ffn",
    )(*args)


def _prep_ffn(w_in, w_out, tf=FFN_TILE):
    d, two_dff = w_in.shape
    dff = two_dff // 2
    dffp = -(-dff // tf) * tf
    z = jnp.zeros((d, dffp - dff), MXU_DTYPE)
    w = jnp.concatenate([w_in[:, :dff].astype(MXU_DTYPE), z, w_in[:, dff:].astype(MXU_DTYPE), z], axis=1)
    wo = jnp.pad(w_out, ((0, dffp - dff), (0, 0))).astype(MXU_DTYPE)
    return w, wo


def _mixin_body(x_ref, g_ref, wm_ref, wd_ref, wa_ref, wvt_ref, hm_ref, a_ref, vt_ref, dil_refs, relayout_scr):
    xn = _rms(x_ref[...], g_ref[...]).astype(MXU_DTYPE)
    hm_ref[...] = _dot(xn, wm_ref[...]).astype(hm_ref.dtype)
    a_ref[...] = _dot(xn, wa_ref[...])
    vt_ref[...] = _dot_t(wvt_ref[...], xn).astype(vt_ref.dtype)
    hd = _dot(xn, wd_ref[...])
    tm = hd.shape[0]
    for g, out_ref in enumerate(dil_refs):
        dilation = DIL_PAIRS[g][1]
        cols = hd[:, g * DIL_GROUP_COLS:(g + 1) * DIL_GROUP_COLS]
        if dilation == 1:
            out_ref[...] = cols.astype(out_ref.dtype)
            continue
        for c in range(DIL_GROUP_COLS // LANES):
            relayout_scr[c] = cols[:, c * LANES:(c + 1) * LANES]
        for r in range(dilation):
            for c in range(DIL_GROUP_COLS // LANES):
                piece = relayout_scr[c, pl.ds(r, tm // dilation, stride=dilation), :]
                out_ref[:, r * DIL_GROUP_COLS + c * LANES:r * DIL_GROUP_COLS + (c + 1) * LANES] = (
                    piece.astype(out_ref.dtype))


def _mixin_kernel(x_ref, g_ref, wm_ref, wd_ref, wa_ref, wvt_ref, hm_ref, a_ref, vt_ref, d0_ref, d1_ref,
                  d2_ref, relayout_scr):
    _mixin_body(x_ref, g_ref, wm_ref, wd_ref, wa_ref, wvt_ref, hm_ref, a_ref, vt_ref,
                (d0_ref, d1_ref, d2_ref), relayout_scr)


def _mixin(x2, g, wm, wd, wa, wvt, *, tm=512):
    n, d = x2.shape
    weight = lambda a: _resident(a.shape, lambda i: (0, 0))
    dils = [dil for _, dil in DIL_PAIRS]
    return pl.pallas_call(
        _mixin_kernel,
        grid=(n // tm,),
        in_specs=[
            pl.BlockSpec((tm, d), lambda i: (i, 0)),
            pl.BlockSpec((1, d), lambda i: (0, 0)),
            weight(wm), weight(wd), weight(wa), weight(wvt),
        ],
        out_specs=[
            pl.BlockSpec((tm, H_COLS), lambda i: (i, 0)),
            pl.BlockSpec((tm, AUX_COLS), lambda i: (i, 0)),
            pl.BlockSpec((2 * LANES, tm), lambda i: (0, i)),
        ] + [pl.BlockSpec((tm // dil, dil * DIL_GROUP_COLS), lambda i: (i, 0)) for dil in dils],
        out_shape=[
            jax.ShapeDtypeStruct((n, H_COLS), MXU_DTYPE),
            jax.ShapeDtypeStruct((n, AUX_COLS), F32),
            jax.ShapeDtypeStruct((2 * LANES, n), MXU_DTYPE),
        ] + [jax.ShapeDtypeStruct((n // dil, dil * DIL_GROUP_COLS), MXU_DTYPE) for dil in dils],
        scratch_shapes=[pltpu.VMEM((DIL_GROUP_COLS // LANES, tm, LANES), F32)],
        compiler_params=_cparams(("parallel",)),
        name="mix_in",
    )(x2, g.reshape(1, d), wm, wd, wa, wvt)


def _rot_half_cols(w):
    half = w.shape[1] // 2
    return jnp.concatenate([-w[:, half:], w[:, :half]], axis=1)


def _prep_mixin(w):
    d = w.shape[0]
    o = 0
    parts = {}
    for name, width in (("q_lat", 512), ("kv_lat", 512), ("k_rope", 64), ("dq", 768), ("dk", 768),
                        ("dv", 768), ("nq", 512), ("nkc", 128), ("nvc", 128), ("nks", 128),
                        ("nvs", 128), ("nkw", 128), ("nvw", 128), ("ng", 12)):
        parts[name] = w[:, o:o + width]
        o += width
    z64 = jnp.zeros((d, 64), w.dtype)
    kr = parts["k_rope"]
    wm = jnp.concatenate([
        parts["q_lat"], parts["kv_lat"], parts["nq"],
        kr, z64, _rot_half_cols(kr), z64,
        parts["nks"], parts["nkw"]], axis=1).astype(MXU_DTYPE)
    gw = 2 * HEAD_DIM
    wd = jnp.concatenate([parts[name][:, g * gw:(g + 1) * gw] for g in range(len(DIL_PAIRS))
                          for name in ("dq", "dk", "dv")], axis=1).astype(MXU_DTYPE)
    wa = jnp.concatenate([parts["nkc"], parts["nvc"], parts["ng"],
                          jnp.zeros((d, LANES - 12), w.dtype)], axis=1).astype(MXU_DTYPE)
    wvt = jnp.concatenate([parts["nvs"], parts["nvw"]], axis=1).T.astype(MXU_DTYPE)
    return wm, wd, wa, wvt


def _rope_tables_kernel(invf_ref, cos_ref, sin_ref):
    tm = cos_ref.shape[0]
    pos = (pl.program_id(0) * tm + lax.broadcasted_iota(jnp.int32, (tm, 1), 0)).astype(F32)
    ang = pos * invf_ref[...]
    cos_ref[...] = jnp.cos(ang)
    sin_ref[...] = jnp.sin(ang)


def _rope_tables(seq, *, tm=512):
    tm = min(tm, seq)
    spec = pl.BlockSpec((tm, LANES), lambda i: (i, 0))
    sds = jax.ShapeDtypeStruct((seq, LANES), F32)
    return pl.pallas_call(
        _rope_tables_kernel,
        grid=(seq // tm,),
        in_specs=[pl.BlockSpec((1, LANES), lambda i: (0, 0))],
        out_specs=[spec, spec],
        out_shape=[sds, sds],
        compiler_params=_cparams(("parallel",)),
        name="rope_tables",
    )(_rope_inv_freq_row())


def _mla_proj_kernel(lat_ref, kr_ref, qn_ref, kvn_ref, wq_ref, wk_ref, wvt_ref, cos_ref, sin_ref,
                     q_ref, k_ref, vt_ref):
    cos = cos_ref[...]
    sin = sin_ref[...]
    scale = (MLA_NOPE + MLA_ROPE) ** -0.5 * LOG2E

    lat = lat_ref[...].astype(F32)
    qn = _rms(lat[:, :MLA_LORA], qn_ref[...]).astype(MXU_DTYPE)
    kvn = _rms(lat[:, MLA_LORA:], kvn_ref[...]).astype(MXU_DTYPE)
    qm = _dot(qn, wq_ref[...])
    rot0 = MLA_HEADS * MLA_QK_PAD
    for h in range(MLA_HEADS):
        c = h * MLA_QK_PAD
        nope = qm[:, c:c + LANES]
        pe = qm[:, c + LANES:c + 2 * LANES] * cos + qm[:, rot0 + h * LANES:rot0 + (h + 1) * LANES] * sin
        q_ref[:, c:c + LANES] = (nope * scale).astype(q_ref.dtype)
        q_ref[:, c + LANES:c + 2 * LANES] = (pe * scale).astype(q_ref.dtype)

    kr = kr_ref[...].astype(F32)
    kpe = (kr[:, :LANES] * cos + kr[:, LANES:] * sin).astype(k_ref.dtype)
    kn = _dot(kvn, wk_ref[...])
    for h in range(MLA_HEADS):
        c = h * MLA_QK_PAD
        k_ref[:, c:c + LANES] = kn[:, h * LANES:(h + 1) * LANES].astype(k_ref.dtype)
        k_ref[:, c + LANES:c + 2 * LANES] = kpe
    vt_ref[...] = _dot_t(wvt_ref[...], kvn).astype(vt_ref.dtype)


def _mla_proj(h2, qn, kvn, wq, wk, wvt, cos, sin, *, tm=512):
    n = h2.shape[0]
    tm = min(tm, cos.shape[0])
    tiles_per_seq = cos.shape[0] // tm
    qk_cols = MLA_HEADS * MLA_QK_PAD
    v_rows = MLA_HEADS * MLA_V
    full = lambda a: pl.BlockSpec(a.shape, lambda i: (0,) * a.ndim)
    table = pl.BlockSpec((tm, LANES), lambda i: (i % tiles_per_seq, 0))
    return pl.pallas_call(
        _mla_proj_kernel,
        grid=(n // tm,),
        in_specs=[
            pl.BlockSpec((tm, 2 * MLA_LORA), lambda i: (i, 0)),
            pl.BlockSpec((tm, 2 * LANES), lambda i: (i, COL_KROPE // (2 * LANES))),
            full(qn), full(kvn), full(wq), full(wk), full(wvt), table, table,
        ],
        out_specs=[
            pl.BlockSpec((tm, qk_cols), lambda i: (i, 0)),
            pl.BlockSpec((tm, qk_cols), lambda i: (i, 0)),
            pl.BlockSpec((v_rows, tm), lambda i: (0, i)),
        ],
        out_shape=[
            jax.ShapeDtypeStruct((n, qk_cols), MXU_DTYPE),
            jax.ShapeDtypeStruct((n, qk_cols), MXU_DTYPE),
            jax.ShapeDtypeStruct((v_rows, n), MXU_DTYPE),
        ],
        compiler_params=_cparams(("parallel",)),
        name="mla_proj",
    )(h2, h2, qn, kvn, wq, wk, wvt, cos, sin)


def _prep_mla(w_uq, w_ukv):
    r = MLA_LORA
    wq3 = w_uq.reshape(r, MLA_HEADS, MLA_NOPE + MLA_ROPE)
    nope, pe = wq3[..., :MLA_NOPE], wq3[..., MLA_NOPE:]
    z = jnp.zeros((r, MLA_HEADS, 64), w_uq.dtype)
    main = jnp.concatenate([nope, pe, z], axis=-1).reshape(r, MLA_HEADS * MLA_QK_PAD)
    half = MLA_ROPE // 2
    pe_rot = jnp.concatenate([-pe[..., half:], pe[..., :half]], axis=-1)
    rot = jnp.concatenate([pe_rot, z], axis=-1).reshape(r, MLA_HEADS * LANES)
    wq = jnp.concatenate([main, rot], axis=1).astype(MXU_DTYPE)
    wkv3 = w_ukv.reshape(r, MLA_HEADS, MLA_NOPE + MLA_V)
    wk = wkv3[..., :MLA_NOPE].reshape(r, MLA_HEADS * MLA_NOPE).astype(MXU_DTYPE)
    wvt = wkv3[..., MLA_NOPE:].reshape(r, MLA_HEADS * MLA_V).T.astype(MXU_DTYPE)
    return wq, wk, wvt


def _rope_inv_freq_row():
    half = MLA_ROPE // 2
    f = ROPE_BASE ** (-jnp.arange(half, dtype=F32) / half)
    return jnp.concatenate([f, f, jnp.zeros((LANES - MLA_ROPE,), F32)]).reshape(1, LANES)


def _with_ones_rows(v_t):
    return jnp.concatenate([v_t, jnp.ones((DEN_ROWS, v_t.shape[1]), v_t.dtype)], axis=0)


def _softmax_piece_t(scores, v_aug, m_ref, acc_ref, *, first):
    if first:
        s_t = scores()
        m_new = jnp.max(s_t, axis=0, keepdims=True)
        acc_ref[...] = _dot(v_aug, jnp.exp2(s_t - m_new).astype(MXU_DTYPE))
        m_ref[...] = m_new
        return

    m_old = m_ref[...]
    s_t = scores()
    pv = _dot(v_aug, jnp.exp2(s_t - m_old).astype(MXU_DTYPE))
    fits = jnp.max(jnp.max(s_t, axis=0, keepdims=True) - m_old) <= LAZY_MAX_SLACK

    @pl.when(fits)
    def _():
        acc_ref[...] += pv

    @pl.when(jnp.logical_not(fits))
    def _():
        s_again = scores()
        m_new = jnp.maximum(m_old, jnp.max(s_again, axis=0, keepdims=True))
        weights = jnp.exp2(s_again - m_new).astype(MXU_DTYPE)
        acc_ref[...] = jnp.exp2(m_old - m_new) * acc_ref[...] + _dot(v_aug, weights)
        m_ref[...] = m_new


def _softmax_result(acc_scr, dv):
    return acc_scr[0:dv, :] * (1.0 / acc_scr[dv:dv + 1, :])


def _flash_kernel(qt_ref, kt_ref, q_ref, k_ref, vt_ref, o_ref, m_scr, acc_scr, *, t, dsub):
    step = pl.program_id(2)
    qi = qt_ref[step]
    ki = kt_ref[step]

    def piece(k0, q0, size, diagonal):
        ks = slice(k0, k0 + size)
        qs = slice(q0, q0 + size)

        def scores():
            s_t = _dot_t(k_ref[0, ks, :], q_ref[0, qs, :])
            if diagonal and k0 + size - 1 > q0:
                kpos = k0 + lax.broadcasted_iota(jnp.int32, (size, 1), 0)
                qpos = q0 + lax.broadcasted_iota(jnp.int32, (1, size), 1)
                s_t = jnp.where(kpos <= qpos, s_t, NEG_INF)
            return s_t

        _softmax_piece_t(scores, _with_ones_rows(vt_ref[:, ks]), m_scr.at[:, qs], acc_scr.at[:, qs],
                         first=diagonal and k0 == q0)

    @pl.when(ki == qi)
    def _():
        starts = range(0, t, dsub)
        for p in starts:
            piece(p, p, dsub, True)
        for k0 in starts:
            for q0 in starts:
                if q0 > k0:
                    piece(k0, q0, dsub, True)

    @pl.when(ki < qi)
    def _():
        piece(0, 0, t, False)

    @pl.when(ki == 0)
    def _():
        o_ref[0] = _softmax_result(acc_scr, MLA_V).T.astype(o_ref.dtype)


def _mla_flash(q3, k3, vt, *, t=2048, dsub=1024):
    b, s, _ = q3.shape
    t = min(t, s)
    dsub = min(dsub, t)
    nq = s // t
    pairs = [(qi, ki) for qi in range(nq) for ki in range(qi, -1, -1)]
    qt = jnp.asarray(np.array([p[0] for p in pairs], np.int32))
    kt = jnp.asarray(np.array([p[1] for p in pairs], np.int32))
    grid_spec = pltpu.PrefetchScalarGridSpec(
        num_scalar_prefetch=2,
        grid=(b, MLA_HEADS, len(pairs)),
        in_specs=[
            pl.BlockSpec((1, t, MLA_QK_PAD), lambda bi, h, st, qt, kt: (bi, qt[st], h)),
            pl.BlockSpec((1, t, MLA_QK_PAD), lambda bi, h, st, qt, kt: (bi, kt[st], h)),
            pl.BlockSpec((MLA_V, t), lambda bi, h, st, qt, kt: (h, bi * nq + kt[st])),
        ],
        out_specs=pl.BlockSpec((1, t, MLA_V), lambda bi, h, st, qt, kt: (bi, qt[st], h)),
        scratch_shapes=[pltpu.VMEM((1, t), F32), pltpu.VMEM((MLA_V + DEN_ROWS, t), F32)],
    )
    return pl.pallas_call(
        functools.partial(_flash_kernel, t=t, dsub=dsub),
        grid_spec=grid_spec,
        out_shape=jax.ShapeDtypeStruct((b, s, MLA_HEADS * MLA_V), MXU_DTYPE),
        compiler_params=_cparams(("parallel", "parallel", "arbitrary")),
        name="mla_flash",
    )(qt, kt, q3, k3, vt)


def _dil_kernel(q_ref, kc_ref, kp_ref, vc_ref, vp_ref, o_ref, lse_ref, *, dilation, span, slopes, tl):
    n = pl.program_id(2)
    scale = HEAD_DIM ** -0.5
    blk = DIL_BLOCK
    a = lax.broadcasted_iota(jnp.int32, (blk, 1), 0)
    c = lax.broadcasted_iota(jnp.int32, (1, 2 * blk), 1)
    j = blk + a - c
    in_band = (j >= 0) & (j <= span)
    first_valid = in_band & ((c >= blk) | (n > 0))
    dist = (j * dilation).astype(F32)
    for hg in range(2):
        cols = slice(hg * LANES, (hg + 1) * LANES)
        bias = slopes[hg] * dist
        for sb in range(tl // blk):
            rows = slice(sb * blk, (sb + 1) * blk)
            q = (q_ref[0, rows, cols].astype(F32) * scale).astype(MXU_DTYPE)
            if sb == 0:
                kprev, vprev, valid = kp_ref[0, :, cols], vp_ref[0, :, cols], first_valid
            else:
                prev = slice((sb - 1) * blk, sb * blk)
                kprev, vprev, valid = kc_ref[0, prev, cols], vc_ref[0, prev, cols], in_band
            keys = jnp.concatenate([kprev, kc_ref[0, rows, cols]], axis=0)
            vals = jnp.concatenate([vprev, vc_ref[0, rows, cols]], axis=0)
            s = jnp.where(valid, _dot_t(q, keys) - bias, NEG_INF)
            m = jnp.max(s, axis=1, keepdims=True)
            e = jnp.where(valid, jnp.exp(s - m), 0.0)
            den = jnp.sum(e, axis=1, keepdims=True)
            o_ref[0, rows, cols] = _dot((e / den).astype(MXU_DTYPE), vals)
            lse_ref[0, rows, cols] = jnp.broadcast_to(m + jnp.log(den), (blk, LANES))


def _dilated_group(hg, g, b, *, tl=1024):
    window, dilation = DIL_PAIRS[g]
    span = window // dilation
    seq_l = hg.shape[0] // b
    tl = min(tl, seq_l)
    per_blk = tl // DIL_BLOCK
    w = 2 * LANES
    hv = hg.reshape(b, seq_l, dilation * DIL_GROUP_COLS)
    qc, kc, vc = 0, 1, 2
    stride = DIL_GROUP_COLS // w
    cur = lambda col: pl.BlockSpec((1, tl, w), lambda bi, r, n: (bi, n, r * stride + col))
    prev = lambda col: pl.BlockSpec(
        (1, DIL_BLOCK, w), lambda bi, r, n: (bi, jnp.maximum(n * per_blk - 1, 0), r * stride + col))
    out_spec = pl.BlockSpec((1, tl, w), lambda bi, r, n: (bi, n, r))
    out_sds = jax.ShapeDtypeStruct((b, seq_l, dilation * w), F32)
    o, lse = pl.pallas_call(
        functools.partial(_dil_kernel, dilation=dilation, span=span,
                          slopes=ALIBI_SLOPES[2 * g:2 * g + 2], tl=tl),
        grid=(b, dilation, seq_l // tl),
        in_specs=[cur(qc), cur(kc), prev(kc), cur(vc), prev(vc)],
        out_specs=[out_spec, out_spec],
        out_shape=[out_sds, out_sds],
        compiler_params=_cparams(("parallel", "parallel", "parallel")),
        name=f"dilated_g{g}",
    )(hv, hv, hv, hv, hv)
    return o.reshape(b * seq_l, dilation * w), lse.reshape(b * seq_l, dilation * w)


def _nsa_cmp_kernel(x_ref, pos_ref, wa_ref, wb_ref, w2k_ref, w2vt_ref, kc_ref, vct_ref):
    x = x_ref[0]
    xa = (x + pos_ref[0:1, :]).astype(MXU_DTYPE)
    xb = (x + pos_ref[1:2, :]).astype(MXU_DTYPE)
    first = _dot(xa, wa_ref[...])
    second = _dot(xb, wb_ref[...])
    second_next = pltpu.roll(second, x.shape[0] - 1, 0)
    pre = first + second_next
    hid = (pre * jax.nn.sigmoid(pre)).astype(MXU_DTYPE)
    kc_ref[0] = _dot(hid[:, :LANES], w2k_ref[...]).astype(kc_ref.dtype)
    vct_ref[0] = _dot_t(w2vt_ref[...], hid[:, LANES:]).astype(vct_ref.dtype)


def _nsa_compress(aux3, pos2, wa, wb, w2k, w2vt):
    b, s, _ = aux3.shape
    nchunk = s // NSA_CMP_STRIDE
    xk = aux3[:, :, :2 * LANES].reshape(b, nchunk, NSA_CMP_STRIDE * 2 * LANES)
    full = lambda a: pl.BlockSpec(a.shape, lambda bi: (0,) * a.ndim)
    return pl.pallas_call(
        _nsa_cmp_kernel,
        grid=(b,),
        in_specs=[pl.BlockSpec((1, nchunk, xk.shape[2]), lambda bi: (bi, 0, 0)),
                  full(pos2), full(wa), full(wb), full(w2k), full(w2vt)],
        out_specs=[pl.BlockSpec((1, nchunk, LANES), lambda bi: (bi, 0, 0)),
                   pl.BlockSpec((1, LANES, nchunk), lambda bi: (bi, 0, 0))],
        out_shape=[jax.ShapeDtypeStruct((b, nchunk, LANES), MXU_DTYPE),
                   jax.ShapeDtypeStruct((b, LANES, nchunk), MXU_DTYPE)],
        compiler_params=_cparams(("parallel",)),
        name="nsa_cmp",
    )(xk, pos2, wa, wb, w2k, w2vt)


def _prep_nsa_cmp(cmp_pos, phi_k1, phi_k2, phi_v1, phi_v2):
    half = NSA_CMP_LEN // 2
    pk = cmp_pos.reshape(2, half, HEAD_DIM)
    pos2 = jnp.concatenate([pk, pk], axis=-1).reshape(2, half * 2 * LANES)

    def halves(w1, is_v):
        w = w1.reshape(2, half, HEAD_DIM, HEAD_DIM)
        z = jnp.zeros_like(w)
        w = jnp.concatenate([z, w] if is_v else [w, z], axis=2)
        return w.reshape(2, half * 2 * LANES, HEAD_DIM)

    wk, wv = halves(phi_k1, False), halves(phi_v1, True)
    wa = jnp.concatenate([wk[0], wv[0]], axis=1).astype(MXU_DTYPE)
    wb = jnp.concatenate([wk[1], wv[1]], axis=1).astype(MXU_DTYPE)
    return pos2, wa, wb, phi_k2.astype(MXU_DTYPE), phi_v2.T.astype(MXU_DTYPE)


def _split3(x):
    hi = x.astype(MXU_DTYPE)
    r = x - hi.astype(F32)
    mid = r.astype(MXU_DTYPE)
    lo = (r - mid.astype(F32)).astype(MXU_DTYPE)
    return hi, mid, lo


def _pos_features(pos):
    hi = jnp.floor(pos / POS_SPLIT) * POS_SPLIT
    lo = pos - hi
    cols = jnp.stack([hi, hi, hi, lo, lo, lo], axis=1)
    return jnp.pad(cols, ((0, 0), (0, LANES - 6))).astype(MXU_DTYPE)


def _slope_features(tq):
    sig = jnp.asarray([s * LOG2E for s in ALIBI_SLOPES[DIL_HEADS:]], F32)
    pieces = jnp.stack(_split3(sig), axis=1)
    rows = jnp.concatenate([pieces, pieces], axis=1)
    rows = jnp.pad(rows, ((0, 0), (0, LANES - 6)))
    return jnp.repeat(rows, tq, axis=0)


def _masked_softmax_t(s_t, mask):
    s_t = jnp.where(mask, s_t, NEG_INF)
    m = jnp.max(s_t, axis=0, keepdims=True)
    e = jnp.exp2(s_t - m)
    den = jnp.sum(e, axis=0, keepdims=True)
    return e * jnp.where(m > 0.5 * NEG_INF, 1.0 / den, 0.0)


def _nsa_kernel(q_ref, k_ref, vt_ref, kc_ref, vct_ref, c2st_ref, pf_ref, cpf_ref, sf_ref, wband_ref, g_ref,
                o_ref, m_scr, acc_scr, bias_scr, any_scr, ocmp_scr, score_scr,
                *, tq, tk, topk, cmp_chunk):
    i = pl.program_id(1)
    t0 = i * tq
    nh = NSA_HEADS
    cols = nh * tq
    ncp = kc_ref.shape[1]
    ns = c2st_ref.shape[0]
    blocks_per_tile = tk // NSA_SEL_LEN

    q = q_ref[0]
    q4 = jnp.concatenate([q[:, h * LANES:(h + 1) * LANES] for h in range(nh)], axis=0)
    q4 = (q4.astype(F32) * (HEAD_DIM ** -0.5 * LOG2E)).astype(MXU_DTYPE)
    q4 = jnp.concatenate([q4, sf_ref[...]], axis=1)
    col = lax.broadcasted_iota(jnp.int32, (1, cols), 1)
    tpos = (t0 + (col & (tq - 1))).astype(F32)

    def cmp_part(rows):
        cidx = lax.broadcasted_iota(jnp.int32, (rows, 1), 0).astype(F32)
        c_end = cidx * NSA_CMP_STRIDE + (NSA_CMP_LEN - 1)
        kc = jnp.concatenate([kc_ref[0, 0:rows, :], cpf_ref[0:rows, :]], axis=1)
        p_cmp = _masked_softmax_t(_dot_t(kc, q4), c_end <= tpos)
        ocmp_scr[...] = _dot(vct_ref[0, :, 0:rows], p_cmp.astype(MXU_DTYPE))
        p_sum = p_cmp[:, 0:tq]
        for h in range(1, nh):
            p_sum = p_sum + p_cmp[:, h * tq:(h + 1) * tq]
        c2st = c2st_ref[:, 0:rows]
        if MXU_DTYPE == jnp.float32:
            score_scr[...] = _dot(c2st, p_sum)
        else:
            score_scr[...] = sum(_dot(c2st, piece) for piece in _split3(p_sum))

    n_variants = -(-ncp // cmp_chunk)
    n_ending = (t0 + tq) // NSA_CMP_STRIDE - 1
    need = jnp.clip((n_ending + cmp_chunk - 1) // cmp_chunk, 1, n_variants)
    for v in range(1, n_variants + 1):
        pl.when(need == v)(functools.partial(cmp_part, min(v * cmp_chunk, ncp)))
    o_cmp = ocmp_scr[...]
    score = score_scr[...]

    t1 = t0 + lax.broadcasted_iota(jnp.int32, (1, tq), 1)
    cur = _shr(t1, NSA_SEL_LEN).astype(F32)
    jj = lax.broadcasted_iota(jnp.int32, (ns, 1), 0).astype(F32)
    forced = (jj == 0.0) | (jj == cur) | (jj == cur - 1.0)
    n_forced = 1.0 + jnp.where(cur >= 1.0, 1.0, 0.0) + jnp.where(cur >= 2.0, 1.0, 0.0)
    score = jnp.where(jj > cur, -1.0, jnp.where(forced, -2.0, score))
    bias = jnp.where(forced, 0.0, NEG_INF)

    def pick(score, bias, wanted):
        best = jnp.max(score, axis=0, keepdims=True)
        idx = jnp.min(jnp.where(score == best, jj, float(ns)), axis=0, keepdims=True)
        hit = jj == idx
        if wanted is not None:
            hit = hit & wanted
        return jnp.where(hit, -2.0, score), jnp.where(hit, 0.0, bias)

    max_forced = 3
    for _ in range(topk - max_forced):
        score, bias = pick(score, bias, None)
    bias_scr[...] = bias
    score_scr[...] = score

    @pl.when(t0 < (max_forced - 1) * NSA_SEL_LEN)
    def _():
        score, bias = score_scr[...], bias_scr[...]
        for done in range(topk - max_forced, topk - 1):
            score, bias = pick(score, bias, n_forced + done < float(topk))
        bias_scr[...] = bias

    bias = bias_scr[...]
    any_scr[...] = jnp.broadcast_to(jnp.max(bias, axis=1, keepdims=True), (ns, LANES))

    def slc_update(kt, causal):
        k0 = pl.multiple_of(kt * tk, tk)
        b0 = pl.multiple_of(kt * blocks_per_tile, blocks_per_tile)

        def scores():
            keys = jnp.concatenate([k_ref[0, pl.ds(k0, tk), 0:LANES], pf_ref[pl.ds(k0, tk), :]], axis=1)
            s_t = _dot_t(keys, q4)
            rows = [jnp.broadcast_to(bias_scr[pl.ds(b0 + r, 1), :], (NSA_SEL_LEN, tq))
                    for r in range(blocks_per_tile)]
            sel_bias = jnp.concatenate(rows, axis=0)
            s_t = s_t + jnp.concatenate([sel_bias] * nh, axis=1)
            if causal:
                kpos = (k0 + lax.broadcasted_iota(jnp.int32, (tk, 1), 0)).astype(F32)
                s_t = jnp.where(kpos <= tpos, s_t, NEG_INF)
            return s_t

        _softmax_piece_t(scores, _with_ones_rows(vt_ref[0:LANES, pl.ds(k0, tk)]), m_scr, acc_scr,
                         first=causal)

    last = t0 // tk
    slc_update(last, True)

    def slc_tile(back, carry):
        kt = last - 1 - back
        b0 = pl.multiple_of(kt * blocks_per_tile, blocks_per_tile)
        touched = jnp.max(any_scr[pl.ds(b0, blocks_per_tile), :]) == 0.0
        pl.when(touched)(functools.partial(slc_update, kt, False))
        return carry

    lax.fori_loop(0, last, slc_tile, 0)
    o_slc = _softmax_result(acc_scr, HEAD_DIM)

    wlen = NSA_WINDOW + tq
    ws = pl.multiple_of(jnp.maximum(t0 - NSA_WINDOW, 0), tq)
    keys = jnp.concatenate([k_ref[0, pl.ds(ws, wlen), LANES:2 * LANES], pf_ref[pl.ds(ws, wlen), :]], axis=1)
    band = wband_ref[jnp.minimum(i, wband_ref.shape[0] - 1)]
    s_t = _dot_t(keys, q4) + jnp.concatenate([band] * nh, axis=1)
    weights = jnp.exp2(s_t - jnp.max(s_t, axis=0, keepdims=True)).astype(MXU_DTYPE)
    win = _dot(_with_ones_rows(vt_ref[LANES:2 * LANES, pl.ds(ws, wlen)]), weights)
    o_win = win[0:HEAD_DIM, :] * (1.0 / win[HEAD_DIM:HEAD_DIM + 1, :])

    gates = jax.nn.sigmoid(g_ref[0]).T
    for h in range(nh):
        c = slice(h * tq, (h + 1) * tq)
        o = (gates[3 * h:3 * h + 1, :] * o_cmp[:, c] + gates[3 * h + 1:3 * h + 2, :] * o_slc[:, c]
             + gates[3 * h + 2:3 * h + 3, :] * o_win[:, c])
        o_ref[0, :, h * LANES:(h + 1) * LANES] = o.T.astype(o_ref.dtype)


def _nsa(h3, aux3, vt, kc, vct, *, tq=NSA_Q_BLOCK, tk=512):
    b, s, _ = h3.shape
    ncp = kc.shape[1]
    ns = s // NSA_SEL_LEN
    tk = min(tk, s)
    w = NSA_HEADS * HEAD_DIM
    c2st = _cmp_to_sel_t(ncp, ns)
    pf = _pos_features(jnp.arange(s, dtype=F32))
    cpf = _pos_features(jnp.arange(ncp, dtype=F32) * NSA_CMP_STRIDE + 0.5 * (NSA_CMP_LEN - 1))
    sf = _slope_features(tq).astype(MXU_DTYPE)
    wband = _window_band_bias(tq)
    const = lambda a: _resident(a.shape, lambda bi, i: (0,) * a.ndim)
    return pl.pallas_call(
        functools.partial(_nsa_kernel, tq=tq, tk=tk, topk=min(NSA_TOPK, ns), cmp_chunk=min(128, ncp)),
        grid=(b, s // tq),
        in_specs=[
            pl.BlockSpec((1, tq, w), lambda bi, i: (bi, i, COL_NSA_Q // w)),
            _resident((1, s, 2 * LANES), lambda bi, i: (bi, 0, COL_NSA_K // (2 * LANES))),
            _resident((2 * LANES, s), lambda bi, i: (0, bi)),
            _resident((1, ncp, LANES), lambda bi, i: (bi, 0, 0)),
            _resident((1, LANES, ncp), lambda bi, i: (bi, 0, 0)),
            const(c2st), const(pf), const(cpf), const(sf), const(wband),
            pl.BlockSpec((1, tq, LANES), lambda bi, i: (bi, i, 2)),
        ],
        out_specs=pl.BlockSpec((1, tq, w), lambda bi, i: (bi, i, 0)),
        out_shape=jax.ShapeDtypeStruct((b, s, w), MXU_DTYPE),
        scratch_shapes=[pltpu.VMEM((1, NSA_HEADS * tq), F32),
                        pltpu.VMEM((HEAD_DIM + DEN_ROWS, NSA_HEADS * tq), F32),
                        pltpu.VMEM((ns, tq), F32), pltpu.VMEM((ns, LANES), F32),
                        pltpu.VMEM((HEAD_DIM, NSA_HEADS * tq), F32), pltpu.VMEM((ns, tq), F32)],
        compiler_params=_cparams(("parallel", "arbitrary")),
        name="nsa",
    )(h3, h3, vt, kc, vct, c2st, pf, cpf, sf, wband, aux3)


def _window_band_bias(tq):
    wlen = NSA_WINDOW + tq
    u = np.arange(tq)[None, :]
    w = np.arange(wlen)[:, None]
    offsets = list(range(0, NSA_WINDOW, tq)) + [NSA_WINDOW]
    tiles = []
    for off in offsets:
        dist = off + u - w
        tiles.append(np.where((dist >= 0) & (dist < NSA_WINDOW), 0.0, NEG_INF))
    return jnp.asarray(np.stack(tiles).astype(np.float32))


def _cmp_to_sel_t(ncp, ns):
    nc = ncp - 1
    c = np.arange(ncp)[None, :]
    j = np.arange(ns)[:, None]
    start = c * NSA_CMP_STRIDE
    m = (start < (j + 1) * NSA_SEL_LEN) & (start + NSA_CMP_LEN - 1 >= j * NSA_SEL_LEN) & (c < nc)
    return jnp.asarray(m.astype(np.float32)).astype(MXU_DTYPE)


def _token_rows(ref, dilation, scr):
    if dilation == 1:
        return ref[...]
    tm = ref.shape[0] * dilation
    for r in range(dilation):
        for c in range(2):
            lo = r * 2 * LANES + c * LANES
            scr[c, pl.ds(r, tm // dilation, stride=dilation), :] = ref[:, lo:lo + LANES]
    return jnp.concatenate([scr[0], scr[1]], axis=1)


def _mixout_kernel(x_ref, mla_ref, d0_ref, d1_ref, d2_ref, l0_ref, l1_ref, l2_ref, nsa_ref, w_ref, o_ref,
                   relayout_scr):
    dils = [dil for _, dil in DIL_PAIRS]
    o_dil = [_token_rows(r, dil, relayout_scr.at[2 * g:2 * g + 2])
             for g, (r, dil) in enumerate(zip((d0_ref, d1_ref, d2_ref), dils))]
    lse = [_token_rows(r, dil, relayout_scr.at[6 + 2 * g:8 + 2 * g])
           for g, (r, dil) in enumerate(zip((l0_ref, l1_ref, l2_ref), dils))]
    top = jnp.maximum(jnp.maximum(lse[0], lse[1]), lse[2])
    e = [jnp.exp(l - top) for l in lse]
    inv = 1.0 / (e[0] + e[1] + e[2])
    acc = x_ref[...] + _dot(mla_ref[...], w_ref[0:768, :])
    for g in range(3):
        mixed = (o_dil[g] * (e[g] * inv)).astype(MXU_DTYPE)
        acc += _dot(mixed, w_ref[768 + g * 256:768 + (g + 1) * 256, :])
    acc += _dot(nsa_ref[...], w_ref[1536:2048, :])
    o_ref[...] = acc


def _mixout(x2, o_mla, o_dil, lse_dil, o_nsa, w, *, tm=512):
    n, d = x2.shape
    row = lambda width: pl.BlockSpec((tm, width), lambda i: (i, 0))
    dil_specs = [pl.BlockSpec((tm // dil, dil * 2 * LANES), lambda i: (i, 0)) for _, dil in DIL_PAIRS]
    return pl.pallas_call(
        _mixout_kernel,
        grid=(n // tm,),
        in_specs=[row(d), row(o_mla.shape[1])] + dil_specs + dil_specs + [
            row(o_nsa.shape[1]), pl.BlockSpec(w.shape, lambda i: (0, 0))],
        out_specs=row(d),
        out_shape=jax.ShapeDtypeStruct((n, d), F32),
        scratch_shapes=[pltpu.VMEM((12, tm, LANES), F32)],
        compiler_params=_cparams(("parallel",)),
        name="mix_out",
    )(x2, o_mla, *o_dil, *lse_dil, o_nsa, w)


def kernel(x, ffn1_norm, ffn1_w_in, ffn1_w_out, mix_norm, w_mix_in, mla_q_norm, mla_w_uq, mla_kv_norm,
           mla_w_ukv, nsa_cmp_pos, nsa_phi_k1, nsa_phi_k2, nsa_phi_v1, nsa_phi_v2, w_mix_out, ffn2_norm,
           ffn2_w_in, ffn2_w_out, final_norm):
    b, s, d = x.shape
    depth = ffn1_w_in.shape[0]
    n = b * s
    cos, sin = _rope_tables(s)
    x2 = x.reshape(n, d)
    for l in range(depth):
        x2 = _ffn(x2, ffn1_norm[l], *_prep_ffn(ffn1_w_in[l], ffn1_w_out[l]))

        h2, aux2, nsa_vt, *dil_in = _mixin(x2, mix_norm[l], *_prep_mixin(w_mix_in[l]))
        h3 = h2.reshape(b, s, H_COLS)
        aux3 = aux2.reshape(b, s, AUX_COLS)

        q, k, vt = _mla_proj(h2, mla_q_norm[l].reshape(1, -1), mla_kv_norm[l].reshape(1, -1),
                             *_prep_mla(mla_w_uq[l], mla_w_ukv[l]), cos, sin)
        o_mla = _mla_flash(q.reshape(b, s, -1), k.reshape(b, s, -1), vt)

        dil = [_dilated_group(hg, g, b) for g, hg in enumerate(dil_in)]

        kc, vct = _nsa_compress(aux3, *_prep_nsa_cmp(nsa_cmp_pos[l], nsa_phi_k1[l], nsa_phi_k2[l],
                                                      nsa_phi_v1[l], nsa_phi_v2[l]))
        o_nsa = _nsa(h3, aux3, nsa_vt, kc, vct)

        x2 = _mixout(x2, o_mla.reshape(n, -1), [o for o, _ in dil], [e for _, e in dil],
                     o_nsa.reshape(n, -1), w_mix_out[l].astype(MXU_DTYPE))

        gf = final_norm if l == depth - 1 else None
        x2 = _ffn(x2, ffn2_norm[l], *_prep_ffn(ffn2_w_in[l], ffn2_w_out[l]), gf)
    return x2.reshape(b, s, d)
```

```python
import functools
import math

import numpy as np
import jax
import jax.numpy as jnp
from jax import lax
from jax.experimental import pallas as pl
from jax.experimental.pallas import tpu as pltpu

F32 = jnp.float32
MXU_DTYPE = jnp.bfloat16
VMEM_LIMIT_BYTES = 56 * 1024 * 1024
LANES = 128
LOG2E = math.log2(math.e)
FFN_TILE = 512
DEN_ROWS = 16
LAZY_MAX_SLACK = 16.0

HEAD_DIM = 128
RMS_EPS = 1e-6
NEG_INF = -1e30

MLA_HEADS = 6
MLA_LORA = 512
MLA_NOPE = 128
MLA_ROPE = 64
MLA_V = 128
MLA_QK_PAD = 256
ROPE_BASE = 10000.0

DIL_PAIRS = ((128, 1), (512, 4), (2048, 16))
DIL_HEADS = 6
DIL_BLOCK = 128

NSA_HEADS = 4
NSA_CMP_LEN = 32
NSA_CMP_STRIDE = 16
NSA_SEL_LEN = 64
NSA_TOPK = 16
NSA_WINDOW = 512
NSA_FORCED_SCORE = 100.0
NSA_Q_BLOCK = 256
POS_SPLIT = 128

N_ALIBI = DIL_HEADS + NSA_HEADS
ALIBI_SLOPES = tuple(float(2.0 ** (-8.0 * i / N_ALIBI)) for i in range(1, N_ALIBI + 1))

COL_QLAT = 0
COL_KVLAT = 512
COL_NSA_Q = 1024
COL_KROPE = 1536
COL_NSA_K = 1792
H_COLS = 2048
DIL_GROUP_COLS = 768
AUX_COLS = 384


def _cparams(sem):
    return pltpu.CompilerParams(dimension_semantics=sem, vmem_limit_bytes=VMEM_LIMIT_BYTES)


def _rms(x, g):
    ms = jnp.mean(x * x, axis=-1, keepdims=True)
    return (x * lax.rsqrt(ms + RMS_EPS)) * g


def _dot(a, b):
    return jnp.dot(a, b, preferred_element_type=F32)


def _shr(x, pow2):
    return lax.shift_right_logical(x, int(pow2).bit_length() - 1)


def _dot_t(a, b):
    return lax.dot_general(a, b, (((1,), (1,)), ((), ())), preferred_element_type=F32)


def _resident(shape, index_map):
    return pl.BlockSpec(shape, index_map, pipeline_mode=pl.Buffered(1))


def _ffn_kernel(x_ref, g_ref, wg_ref, wu_ref, wo_ref, *rest, final):
    if final:
        gf_ref, o_ref, xn_ref = rest
    else:
        o_ref, xn_ref = rest
    j = pl.program_id(1)

    @pl.when(j == 0)
    def _():
        x = x_ref[...]
        xn_ref[...] = _rms(x, g_ref[...]).astype(xn_ref.dtype)
        o_ref[...] = x

    xn = xn_ref[...]
    gate = _dot(xn, wg_ref[...])
    up = _dot(xn, wu_ref[...])
    h = (0.5 * gate) * jax.nn.sigmoid(gate) * up
    o_ref[...] += _dot(h.astype(MXU_DTYPE), wo_ref[...])

    if final:
        @pl.when(j == pl.num_programs(1) - 1)
        def _():
            o_ref[...] = _rms(o_ref[...], gf_ref[...])


def _ffn(x2, g, w_in, wo, gf=None, *, tm=512, tf=FFN_TILE):
    n, d = x2.shape
    dffp = wo.shape[0]
    nff = dffp // tf
    tm = min(tm, n)
    final = gf is not None
    in_specs = [
        pl.BlockSpec((tm, d), lambda i, j: (i, 0)),
        pl.BlockSpec((1, d), lambda i, j: (0, 0)),
        pl.BlockSpec((d, tf), lambda i, j: (0, j)),
        pl.BlockSpec((d, tf), lambda i, j: (0, nff + j)),
        pl.BlockSpec((tf, d), lambda i, j: (j, 0)),
    ]
    args = [x2, g.reshape(1, d), w_in, w_in, wo]
    if final:
        in_specs.append(pl.BlockSpec((1, d), lambda i, j: (0, 0)))
        args.append(gf.reshape(1, d))
    return pl.pallas_call(
        functools.partial(_ffn_kernel, final=final),
        grid=(n // tm, nff),
        in_specs=in_specs,
        out_specs=pl.BlockSpec((tm, d), lambda i, j: (i, 0)),
        out_shape=jax.ShapeDtypeStruct((n, d), F32),
        scratch_shapes=[pltpu.VMEM((tm, d), MXU_DTYPE)],
        compiler_params=_cparams(("parallel", "arbitrary")),
        name="ffn_final" if final else "ffn",
    )(*args)


def _prep_ffn(w_in, w_out, tf=FFN_TILE):
    d, two_dff = w_in.shape
    dff = two_dff // 2
    dffp = -(-dff // tf) * tf
    z = jnp.zeros((d, dffp - dff), MXU_DTYPE)
    w = jnp.concatenate([w_in[:, :dff].astype(MXU_DTYPE), z, w_in[:, dff:].astype(MXU_DTYPE), z], axis=1)
    wo = jnp.pad(w_out, ((0, dffp - dff), (0, 0))).astype(MXU_DTYPE)
    return w, wo


def _mixin_body(x_ref, g_ref, wm_ref, wd_ref, wa_ref, wvt_ref, hm_ref, gate_ref, cmp_ref, vt_ref, dil_refs,
                relayout_scr):
    xn = _rms(x_ref[...], g_ref[...]).astype(MXU_DTYPE)
    hm_ref[...] = _dot(xn, wm_ref[...]).astype(hm_ref.dtype)
    aux = _dot(xn, wa_ref[...])
    gate_ref[...] = aux[:, 2 * LANES:]
    for c in range(2):
        relayout_scr[c] = aux[:, c * LANES:(c + 1) * LANES]
    for r in range(NSA_CMP_STRIDE):
        for c in range(2):
            lo = r * 2 * LANES + c * LANES
            cmp_ref[:, lo:lo + LANES] = relayout_scr[c, pl.ds(r, aux.shape[0] // NSA_CMP_STRIDE,
                                                               stride=NSA_CMP_STRIDE), :]
    vt_ref[...] = _dot_t(wvt_ref[...], xn).astype(vt_ref.dtype)
    hd = _dot(xn, wd_ref[...])
    tm = hd.shape[0]
    for g, out_ref in enumerate(dil_refs):
        dilation = DIL_PAIRS[g][1]
        cols = hd[:, g * DIL_GROUP_COLS:(g + 1) * DIL_GROUP_COLS]
        if dilation == 1:
            out_ref[...] = cols.astype(out_ref.dtype)
            continue
        for c in range(DIL_GROUP_COLS // LANES):
            relayout_scr[c] = cols[:, c * LANES:(c + 1) * LANES]
        for r in range(dilation):
            for c in range(DIL_GROUP_COLS // LANES):
                piece = relayout_scr[c, pl.ds(r, tm // dilation, stride=dilation), :]
                out_ref[:, r * DIL_GROUP_COLS + c * LANES:r * DIL_GROUP_COLS + (c + 1) * LANES] = (
                    piece.astype(out_ref.dtype))


def _mixin_kernel(x_ref, g_ref, wm_ref, wd_ref, wa_ref, wvt_ref, hm_ref, gate_ref, cmp_ref, vt_ref, d0_ref,
                  d1_ref, d2_ref, relayout_scr):
    _mixin_body(x_ref, g_ref, wm_ref, wd_ref, wa_ref, wvt_ref, hm_ref, gate_ref, cmp_ref, vt_ref,
                (d0_ref, d1_ref, d2_ref), relayout_scr)


def _mixin(x2, g, wm, wd, wa, wvt, *, tm=512):
    n, d = x2.shape
    weight = lambda a: _resident(a.shape, lambda i: (0, 0))
    dils = [dil for _, dil in DIL_PAIRS]
    chunk = NSA_CMP_STRIDE
    return pl.pallas_call(
        _mixin_kernel,
        grid=(n // tm,),
        in_specs=[
            pl.BlockSpec((tm, d), lambda i: (i, 0)),
            pl.BlockSpec((1, d), lambda i: (0, 0)),
            weight(wm), weight(wd), weight(wa), weight(wvt),
        ],
        out_specs=[
            pl.BlockSpec((tm, H_COLS), lambda i: (i, 0)),
            pl.BlockSpec((tm, LANES), lambda i: (i, 0)),
            pl.BlockSpec((tm // chunk, chunk * 2 * LANES), lambda i: (i, 0)),
            pl.BlockSpec((2 * LANES, tm), lambda i: (0, i)),
        ] + [pl.BlockSpec((tm // dil, dil * DIL_GROUP_COLS), lambda i: (i, 0)) for dil in dils],
        out_shape=[
            jax.ShapeDtypeStruct((n, H_COLS), MXU_DTYPE),
            jax.ShapeDtypeStruct((n, LANES), F32),
            jax.ShapeDtypeStruct((n // chunk, chunk * 2 * LANES), F32),
            jax.ShapeDtypeStruct((2 * LANES, n), MXU_DTYPE),
        ] + [jax.ShapeDtypeStruct((n // dil, dil * DIL_GROUP_COLS), MXU_DTYPE) for dil in dils],
        scratch_shapes=[pltpu.VMEM((DIL_GROUP_COLS // LANES, tm, LANES), F32)],
        compiler_params=_cparams(("parallel",)),
        name="mix_in",
    )(x2, g.reshape(1, d), wm, wd, wa, wvt)


def _rot_half_cols(w):
    half = w.shape[1] // 2
    return jnp.concatenate([-w[:, half:], w[:, :half]], axis=1)


def _prep_mixin(w):
    d = w.shape[0]
    o = 0
    parts = {}
    for name, width in (("q_lat", 512), ("kv_lat", 512), ("k_rope", 64), ("dq", 768), ("dk", 768),
                        ("dv", 768), ("nq", 512), ("nkc", 128), ("nvc", 128), ("nks", 128),
                        ("nvs", 128), ("nkw", 128), ("nvw", 128), ("ng", 12)):
        parts[name] = w[:, o:o + width]
        o += width
    z64 = jnp.zeros((d, 64), w.dtype)
    kr = parts["k_rope"]
    wm = jnp.concatenate([
        parts["q_lat"], parts["kv_lat"], parts["nq"],
        kr, z64, _rot_half_cols(kr), z64,
        parts["nks"], parts["nkw"]], axis=1).astype(MXU_DTYPE)
    gw = 2 * HEAD_DIM
    wd = jnp.concatenate([parts[name][:, g * gw:(g + 1) * gw] for g in range(len(DIL_PAIRS))
                          for name in ("dq", "dk", "dv")], axis=1).astype(MXU_DTYPE)
    wa = jnp.concatenate([parts["nkc"], parts["nvc"], parts["ng"],
                          jnp.zeros((d, LANES - 12), w.dtype)], axis=1).astype(MXU_DTYPE)
    wvt = jnp.concatenate([parts["nvs"], parts["nvw"]], axis=1).T.astype(MXU_DTYPE)
    return wm, wd, wa, wvt


def _rope_tables_kernel(invf_ref, cos_ref, sin_ref):
    tm = cos_ref.shape[0]
    pos = (pl.program_id(0) * tm + lax.broadcasted_iota(jnp.int32, (tm, 1), 0)).astype(F32)
    ang = pos * invf_ref[...]
    cos_ref[...] = jnp.cos(ang)
    sin_ref[...] = jnp.sin(ang)


def _rope_tables(seq, *, tm=512):
    tm = min(tm, seq)
    spec = pl.BlockSpec((tm, LANES), lambda i: (i, 0))
    sds = jax.ShapeDtypeStruct((seq, LANES), F32)
    return pl.pallas_call(
        _rope_tables_kernel,
        grid=(seq // tm,),
        in_specs=[pl.BlockSpec((1, LANES), lambda i: (0, 0))],
        out_specs=[spec, spec],
        out_shape=[sds, sds],
        compiler_params=_cparams(("parallel",)),
        name="rope_tables",
    )(_rope_inv_freq_row())


def _mla_proj_kernel(lat_ref, kr_ref, qn_ref, kvn_ref, wq_ref, wk_ref, wvt_ref, cos_ref, sin_ref,
                     q_ref, k_ref, vt_ref):
    cos = cos_ref[...]
    sin = sin_ref[...]
    scale = (MLA_NOPE + MLA_ROPE) ** -0.5 * LOG2E

    lat = lat_ref[...].astype(F32)
    qn = _rms(lat[:, :MLA_LORA], qn_ref[...]).astype(MXU_DTYPE)
    kvn = _rms(lat[:, MLA_LORA:], kvn_ref[...]).astype(MXU_DTYPE)
    qm = _dot(qn, wq_ref[...])
    rot0 = MLA_HEADS * MLA_QK_PAD
    for h in range(MLA_HEADS):
        c = h * MLA_QK_PAD
        nope = qm[:, c:c + LANES]
        pe = qm[:, c + LANES:c + 2 * LANES] * cos + qm[:, rot0 + h * LANES:rot0 + (h + 1) * LANES] * sin
        q_ref[:, c:c + LANES] = (nope * scale).astype(q_ref.dtype)
        q_ref[:, c + LANES:c + 2 * LANES] = (pe * scale).astype(q_ref.dtype)

    kr = kr_ref[...].astype(F32)
    kpe = (kr[:, :LANES] * cos + kr[:, LANES:] * sin).astype(k_ref.dtype)
    kn = _dot(kvn, wk_ref[...])
    for h in range(MLA_HEADS):
        c = h * MLA_QK_PAD
        k_ref[:, c:c + LANES] = kn[:, h * LANES:(h + 1) * LANES].astype(k_ref.dtype)
        k_ref[:, c + LANES:c + 2 * LANES] = kpe
    vt_ref[...] = _dot_t(wvt_ref[...], kvn).astype(vt_ref.dtype)


def _mla_proj(h2, qn, kvn, wq, wk, wvt, cos, sin, *, tm=512):
    n = h2.shape[0]
    tm = min(tm, cos.shape[0])
    tiles_per_seq = cos.shape[0] // tm
    qk_cols = MLA_HEADS * MLA_QK_PAD
    v_rows = MLA_HEADS * MLA_V
    full = lambda a: pl.BlockSpec(a.shape, lambda i: (0,) * a.ndim)
    table = pl.BlockSpec((tm, LANES), lambda i: (i % tiles_per_seq, 0))
    return pl.pallas_call(
        _mla_proj_kernel,
        grid=(n // tm,),
        in_specs=[
            pl.BlockSpec((tm, 2 * MLA_LORA), lambda i: (i, 0)),
            pl.BlockSpec((tm, 2 * LANES), lambda i: (i, COL_KROPE // (2 * LANES))),
            full(qn), full(kvn), full(wq), full(wk), full(wvt), table, table,
        ],
        out_specs=[
            pl.BlockSpec((tm, qk_cols), lambda i: (i, 0)),
            pl.BlockSpec((tm, qk_cols), lambda i: (i, 0)),
            pl.BlockSpec((v_rows, tm), lambda i: (0, i)),
        ],
        out_shape=[
            jax.ShapeDtypeStruct((n, qk_cols), MXU_DTYPE),
            jax.ShapeDtypeStruct((n, qk_cols), MXU_DTYPE),
            jax.ShapeDtypeStruct((v_rows, n), MXU_DTYPE),
        ],
        compiler_params=_cparams(("parallel",)),
        name="mla_proj",
    )(h2, h2, qn, kvn, wq, wk, wvt, cos, sin)


def _prep_mla(w_uq, w_ukv):
    r = MLA_LORA
    wq3 = w_uq.reshape(r, MLA_HEADS, MLA_NOPE + MLA_ROPE)
    nope, pe = wq3[..., :MLA_NOPE], wq3[..., MLA_NOPE:]
    z = jnp.zeros((r, MLA_HEADS, 64), w_uq.dtype)
    main = jnp.concatenate([nope, pe, z], axis=-1).reshape(r, MLA_HEADS * MLA_QK_PAD)
    half = MLA_ROPE // 2
    pe_rot = jnp.concatenate([-pe[..., half:], pe[..., :half]], axis=-1)
    rot = jnp.concatenate([pe_rot, z], axis=-1).reshape(r, MLA_HEADS * LANES)
    wq = jnp.concatenate([main, rot], axis=1).astype(MXU_DTYPE)
    wkv3 = w_ukv.reshape(r, MLA_HEADS, MLA_NOPE + MLA_V)
    wk = wkv3[..., :MLA_NOPE].reshape(r, MLA_HEADS * MLA_NOPE).astype(MXU_DTYPE)
    wvt = wkv3[..., MLA_NOPE:].reshape(r, MLA_HEADS * MLA_V).T.astype(MXU_DTYPE)
    return wq, wk, wvt


def _rope_inv_freq_row():
    half = MLA_ROPE // 2
    f = ROPE_BASE ** (-jnp.arange(half, dtype=F32) / half)
    return jnp.concatenate([f, f, jnp.zeros((LANES - MLA_ROPE,), F32)]).reshape(1, LANES)


def _with_ones_rows(v_t):
    return jnp.concatenate([v_t, jnp.ones((DEN_ROWS, v_t.shape[1]), v_t.dtype)], axis=0)


def _softmax_piece_t(scores, v_aug, m_ref, acc_ref, *, first):
    if first:
        s_t = scores()
        m_new = jnp.max(s_t, axis=0, keepdims=True)
        acc_ref[...] = _dot(v_aug, jnp.exp2(s_t - m_new).astype(MXU_DTYPE))
        m_ref[...] = m_new
        return

    m_old = m_ref[...]
    s_t = scores()
    pv = _dot(v_aug, jnp.exp2(s_t - m_old).astype(MXU_DTYPE))
    fits = jnp.max(jnp.max(s_t, axis=0, keepdims=True) - m_old) <= LAZY_MAX_SLACK

    @pl.when(fits)
    def _():
        acc_ref[...] += pv

    @pl.when(jnp.logical_not(fits))
    def _():
        s_again = scores()
        m_new = jnp.maximum(m_old, jnp.max(s_again, axis=0, keepdims=True))
        weights = jnp.exp2(s_again - m_new).astype(MXU_DTYPE)
        acc_ref[...] = jnp.exp2(m_old - m_new) * acc_ref[...] + _dot(v_aug, weights)
        m_ref[...] = m_new


def _softmax_result(acc_scr, dv):
    return acc_scr[0:dv, :] * (1.0 / acc_scr[dv:dv + 1, :])


def _flash_kernel(qt_ref, kt_ref, q_ref, k_ref, vt_ref, o_ref, m_scr, acc_scr, *, t, dsub):
    step = pl.program_id(2)
    qi = qt_ref[step]
    ki = kt_ref[step]

    def piece(k0, q0, size, diagonal):
        ks = slice(k0, k0 + size)
        qs = slice(q0, q0 + size)

        def scores():
            s_t = _dot_t(k_ref[0, ks, :], q_ref[0, qs, :])
            if diagonal and k0 + size - 1 > q0:
                kpos = k0 + lax.broadcasted_iota(jnp.int32, (size, 1), 0)
                qpos = q0 + lax.broadcasted_iota(jnp.int32, (1, size), 1)
                s_t = jnp.where(kpos <= qpos, s_t, NEG_INF)
            return s_t

        _softmax_piece_t(scores, _with_ones_rows(vt_ref[:, ks]), m_scr.at[:, qs], acc_scr.at[:, qs],
                         first=diagonal and k0 == q0)

    @pl.when(ki == qi)
    def _():
        starts = range(0, t, dsub)
        for p in starts:
            piece(p, p, dsub, True)
        for k0 in starts:
            for q0 in starts:
                if q0 > k0:
                    piece(k0, q0, dsub, True)

    @pl.when(ki < qi)
    def _():
        piece(0, 0, t, False)

    @pl.when(ki == 0)
    def _():
        o_ref[0] = _softmax_result(acc_scr, MLA_V).T.astype(o_ref.dtype)


def _mla_flash(q3, k3, vt, *, t=2048, dsub=1024):
    b, s, _ = q3.shape
    t = min(t, s)
    dsub = min(dsub, t)
    nq = s // t
    pairs = [(qi, ki) for qi in range(nq) for ki in range(qi, -1, -1)]
    qt = jnp.asarray(np.array([p[0] for p in pairs], np.int32))
    kt = jnp.asarray(np.array([p[1] for p in pairs], np.int32))
    grid_spec = pltpu.PrefetchScalarGridSpec(
        num_scalar_prefetch=2,
        grid=(b, MLA_HEADS, len(pairs)),
        in_specs=[
            pl.BlockSpec((1, t, MLA_QK_PAD), lambda bi, h, st, qt, kt: (bi, qt[st], h)),
            pl.BlockSpec((1, t, MLA_QK_PAD), lambda bi, h, st, qt, kt: (bi, kt[st], h)),
            pl.BlockSpec((MLA_V, t), lambda bi, h, st, qt, kt: (h, bi * nq + kt[st])),
        ],
        out_specs=pl.BlockSpec((1, t, MLA_V), lambda bi, h, st, qt, kt: (bi, qt[st], h)),
        scratch_shapes=[pltpu.VMEM((1, t), F32), pltpu.VMEM((MLA_V + DEN_ROWS, t), F32)],
    )
    return pl.pallas_call(
        functools.partial(_flash_kernel, t=t, dsub=dsub),
        grid_spec=grid_spec,
        out_shape=jax.ShapeDtypeStruct((b, s, MLA_HEADS * MLA_V), MXU_DTYPE),
        compiler_params=_cparams(("parallel", "parallel", "arbitrary")),
        name="mla_flash",
    )(qt, kt, q3, k3, vt)


def _dil_kernel(q_ref, kc_ref, kp_ref, vc_ref, vp_ref, o_ref, lse_ref, *, dilation, span, slopes, tl):
    n = pl.program_id(2)
    scale = HEAD_DIM ** -0.5
    blk = DIL_BLOCK
    a = lax.broadcasted_iota(jnp.int32, (blk, 1), 0)
    c = lax.broadcasted_iota(jnp.int32, (1, 2 * blk), 1)
    j = blk + a - c
    in_band = (j >= 0) & (j <= span)
    first_valid = in_band & ((c >= blk) | (n > 0))
    dist = (j * dilation).astype(F32)
    for hg in range(2):
        cols = slice(hg * LANES, (hg + 1) * LANES)
        bias = slopes[hg] * dist
        for sb in range(tl // blk):
            rows = slice(sb * blk, (sb + 1) * blk)
            q = (q_ref[0, rows, cols].astype(F32) * scale).astype(MXU_DTYPE)
            if sb == 0:
                kprev, vprev, valid = kp_ref[0, :, cols], vp_ref[0, :, cols], first_valid
            else:
                prev = slice((sb - 1) * blk, sb * blk)
                kprev, vprev, valid = kc_ref[0, prev, cols], vc_ref[0, prev, cols], in_band
            keys = jnp.concatenate([kprev, kc_ref[0, rows, cols]], axis=0)
            vals = jnp.concatenate([vprev, vc_ref[0, rows, cols]], axis=0)
            s = jnp.where(valid, _dot_t(q, keys) - bias, NEG_INF)
            m = jnp.max(s, axis=1, keepdims=True)
            e = jnp.where(valid, jnp.exp(s - m), 0.0)
            den = jnp.sum(e, axis=1, keepdims=True)
            o_ref[0, rows, cols] = _dot((e / den).astype(MXU_DTYPE), vals)
            lse_ref[0, rows, cols] = jnp.broadcast_to(m + jnp.log(den), (blk, LANES))


def _dilated_group(hg, g, b, *, tl=1024):
    window, dilation = DIL_PAIRS[g]
    span = window // dilation
    seq_l = hg.shape[0] // b
    tl = min(tl, seq_l)
    per_blk = tl // DIL_BLOCK
    w = 2 * LANES
    hv = hg.reshape(b, seq_l, dilation * DIL_GROUP_COLS)
    qc, kc, vc = 0, 1, 2
    stride = DIL_GROUP_COLS // w
    cur = lambda col: pl.BlockSpec((1, tl, w), lambda bi, r, n: (bi, n, r * stride + col))
    prev = lambda col: pl.BlockSpec(
        (1, DIL_BLOCK, w), lambda bi, r, n: (bi, jnp.maximum(n * per_blk - 1, 0), r * stride + col))
    out_spec = pl.BlockSpec((1, tl, w), lambda bi, r, n: (bi, n, r))
    out_sds = jax.ShapeDtypeStruct((b, seq_l, dilation * w), F32)
    o, lse = pl.pallas_call(
        functools.partial(_dil_kernel, dilation=dilation, span=span,
                          slopes=ALIBI_SLOPES[2 * g:2 * g + 2], tl=tl),
        grid=(b, dilation, seq_l // tl),
        in_specs=[cur(qc), cur(kc), prev(kc), cur(vc), prev(vc)],
        out_specs=[out_spec, out_spec],
        out_shape=[out_sds, out_sds],
        compiler_params=_cparams(("parallel", "parallel", "parallel")),
        name=f"dilated_g{g}",
    )(hv, hv, hv, hv, hv)
    return o.reshape(b * seq_l, dilation * w), lse.reshape(b * seq_l, dilation * w)


def _nsa_cmp_kernel(x_ref, pos_ref, wa_ref, wb_ref, w2k_ref, w2vt_ref, kc_ref, vct_ref):
    x = x_ref[0]
    xa = (x + pos_ref[0:1, :]).astype(MXU_DTYPE)
    xb = (x + pos_ref[1:2, :]).astype(MXU_DTYPE)
    first = _dot(xa, wa_ref[...])
    second = _dot(xb, wb_ref[...])
    second_next = pltpu.roll(second, x.shape[0] - 1, 0)
    pre = first + second_next
    hid = (pre * jax.nn.sigmoid(pre)).astype(MXU_DTYPE)
    kc_ref[0] = _dot(hid[:, :LANES], w2k_ref[...]).astype(kc_ref.dtype)
    vct_ref[0] = _dot_t(w2vt_ref[...], hid[:, LANES:]).astype(vct_ref.dtype)


def _nsa_compress(cmp_in, b, pos2, wa, wb, w2k, w2vt):
    nchunk = cmp_in.shape[0] // b
    xk = cmp_in.reshape(b, nchunk, cmp_in.shape[1])
    full = lambda a: pl.BlockSpec(a.shape, lambda bi: (0,) * a.ndim)
    return pl.pallas_call(
        _nsa_cmp_kernel,
        grid=(b,),
        in_specs=[pl.BlockSpec((1, nchunk, xk.shape[2]), lambda bi: (bi, 0, 0)),
                  full(pos2), full(wa), full(wb), full(w2k), full(w2vt)],
        out_specs=[pl.BlockSpec((1, nchunk, LANES), lambda bi: (bi, 0, 0)),
                   pl.BlockSpec((1, LANES, nchunk), lambda bi: (bi, 0, 0))],
        out_shape=[jax.ShapeDtypeStruct((b, nchunk, LANES), MXU_DTYPE),
                   jax.ShapeDtypeStruct((b, LANES, nchunk), MXU_DTYPE)],
        compiler_params=_cparams(("parallel",)),
        name="nsa_cmp",
    )(xk, pos2, wa, wb, w2k, w2vt)


def _prep_nsa_cmp(cmp_pos, phi_k1, phi_k2, phi_v1, phi_v2):
    half = NSA_CMP_LEN // 2
    pk = cmp_pos.reshape(2, half, HEAD_DIM)
    pos2 = jnp.concatenate([pk, pk], axis=-1).reshape(2, half * 2 * LANES)

    def halves(w1, is_v):
        w = w1.reshape(2, half, HEAD_DIM, HEAD_DIM)
        z = jnp.zeros_like(w)
        w = jnp.concatenate([z, w] if is_v else [w, z], axis=2)
        return w.reshape(2, half * 2 * LANES, HEAD_DIM)

    wk, wv = halves(phi_k1, False), halves(phi_v1, True)
    wa = jnp.concatenate([wk[0], wv[0]], axis=1).astype(MXU_DTYPE)
    wb = jnp.concatenate([wk[1], wv[1]], axis=1).astype(MXU_DTYPE)
    return pos2, wa, wb, phi_k2.astype(MXU_DTYPE), phi_v2.T.astype(MXU_DTYPE)


def _split3(x):
    hi = x.astype(MXU_DTYPE)
    r = x - hi.astype(F32)
    mid = r.astype(MXU_DTYPE)
    lo = (r - mid.astype(F32)).astype(MXU_DTYPE)
    return hi, mid, lo


def _pos_features(pos):
    hi = jnp.floor(pos / POS_SPLIT) * POS_SPLIT
    lo = pos - hi
    cols = jnp.stack([hi, hi, hi, lo, lo, lo], axis=1)
    return jnp.pad(cols, ((0, 0), (0, LANES - 6))).astype(MXU_DTYPE)


def _slope_features(tq):
    sig = jnp.asarray([s * LOG2E for s in ALIBI_SLOPES[DIL_HEADS:]], F32)
    pieces = jnp.stack(_split3(sig), axis=1)
    rows = jnp.concatenate([pieces, pieces], axis=1)
    rows = jnp.pad(rows, ((0, 0), (0, LANES - 6)))
    return jnp.repeat(rows, tq, axis=0)


def _masked_softmax_t(s_t, mask):
    s_t = jnp.where(mask, s_t, NEG_INF)
    m = jnp.max(s_t, axis=0, keepdims=True)
    e = jnp.exp2(s_t - m)
    den = jnp.sum(e, axis=0, keepdims=True)
    return e * jnp.where(m > 0.5 * NEG_INF, 1.0 / den, 0.0)


def _nsa_kernel(q_ref, k_ref, vt_ref, kc_ref, vct_ref, c2st_ref, pf_ref, cpf_ref, sf_ref, wband_ref, g_ref,
                o_ref, m_scr, acc_scr, bias_scr, touched_smem, ocmp_scr, score_scr,
                *, tq, tk, topk, cmp_chunk):
    i = pl.program_id(1)
    t0 = i * tq
    nh = NSA_HEADS
    cols = nh * tq
    ncp = kc_ref.shape[1]
    ns = c2st_ref.shape[0]
    blocks_per_tile = tk // NSA_SEL_LEN

    q = q_ref[0]
    q4 = jnp.concatenate([q[:, h * LANES:(h + 1) * LANES] for h in range(nh)], axis=0)
    q4 = (q4.astype(F32) * (HEAD_DIM ** -0.5 * LOG2E)).astype(MXU_DTYPE)
    q4 = jnp.concatenate([q4, sf_ref[...]], axis=1)
    col = lax.broadcasted_iota(jnp.int32, (1, cols), 1)
    tpos = (t0 + (col & (tq - 1))).astype(F32)

    def cmp_part(rows):
        cidx = lax.broadcasted_iota(jnp.int32, (rows, 1), 0).astype(F32)
        c_end = cidx * NSA_CMP_STRIDE + (NSA_CMP_LEN - 1)
        kc = jnp.concatenate([kc_ref[0, 0:rows, :], cpf_ref[0:rows, :]], axis=1)
        p_cmp = _masked_softmax_t(_dot_t(kc, q4), c_end <= tpos)
        ocmp_scr[...] = _dot(vct_ref[0, :, 0:rows], p_cmp.astype(MXU_DTYPE))
        p_sum = p_cmp[:, 0:tq]
        for h in range(1, nh):
            p_sum = p_sum + p_cmp[:, h * tq:(h + 1) * tq]
        c2st = c2st_ref[:, 0:rows]
        if MXU_DTYPE == jnp.float32:
            score_scr[...] = _dot(c2st, p_sum)
        else:
            score_scr[...] = sum(_dot(c2st, piece) for piece in _split3(p_sum))

    n_variants = -(-ncp // cmp_chunk)
    n_ending = (t0 + tq) // NSA_CMP_STRIDE - 1
    need = jnp.clip((n_ending + cmp_chunk - 1) // cmp_chunk, 1, n_variants)
    for v in range(1, n_variants + 1):
        pl.when(need == v)(functools.partial(cmp_part, min(v * cmp_chunk, ncp)))
    o_cmp = ocmp_scr[...]
    score = score_scr[...]

    t1 = t0 + lax.broadcasted_iota(jnp.int32, (1, tq), 1)
    cur = _shr(t1, NSA_SEL_LEN).astype(F32)
    jj = lax.broadcasted_iota(jnp.int32, (ns, 1), 0).astype(F32)
    forced = (jj == 0.0) | (jj == cur) | (jj == cur - 1.0)
    n_forced = 1.0 + jnp.where(cur >= 1.0, 1.0, 0.0) + jnp.where(cur >= 2.0, 1.0, 0.0)
    score = jnp.where(jj > cur, -1.0, jnp.where(forced, -2.0, score))
    bias = jnp.where(forced, 0.0, NEG_INF)

    def pick(score, bias, wanted):
        best = jnp.max(score, axis=0, keepdims=True)
        idx = jnp.min(jnp.where(score == best, jj, float(ns)), axis=0, keepdims=True)
        hit = jj == idx
        if wanted is not None:
            hit = hit & wanted
        return jnp.where(hit, -2.0, score), jnp.where(hit, 0.0, bias)

    max_forced = 3
    for _ in range(topk - max_forced):
        score, bias = pick(score, bias, None)
    bias_scr[...] = bias
    score_scr[...] = score

    @pl.when(t0 < (max_forced - 1) * NSA_SEL_LEN)
    def _():
        score, bias = score_scr[...], bias_scr[...]
        for done in range(topk - max_forced, topk - 1):
            score, bias = pick(score, bias, n_forced + done < float(topk))
        bias_scr[...] = bias

    bias = bias_scr[...]
    block_any = jnp.broadcast_to(jnp.max(bias, axis=1, keepdims=True), (ns, LANES))
    tile_any = jnp.max(block_any.reshape(ns // blocks_per_tile, blocks_per_tile, LANES), axis=1)
    for kt in range(ns // blocks_per_tile):
        touched_smem[kt] = tile_any[kt, 0]

    def slc_update(kt, causal):
        k0 = pl.multiple_of(kt * tk, tk)
        b0 = pl.multiple_of(kt * blocks_per_tile, blocks_per_tile)

        def scores():
            keys = jnp.concatenate([k_ref[0, pl.ds(k0, tk), 0:LANES], pf_ref[pl.ds(k0, tk), :]], axis=1)
            s_t = _dot_t(keys, q4)
            rows = [jnp.broadcast_to(bias_scr[pl.ds(b0 + r, 1), :], (NSA_SEL_LEN, tq))
                    for r in range(blocks_per_tile)]
            sel_bias = jnp.concatenate(rows, axis=0)
            s_t = s_t + jnp.concatenate([sel_bias] * nh, axis=1)
            if causal:
                kpos = (k0 + lax.broadcasted_iota(jnp.int32, (tk, 1), 0)).astype(F32)
                s_t = jnp.where(kpos <= tpos, s_t, NEG_INF)
            return s_t

        _softmax_piece_t(scores, _with_ones_rows(vt_ref[0:LANES, pl.ds(k0, tk)]), m_scr, acc_scr,
                         first=causal)

    last = t0 // tk
    slc_update(last, True)

    def slc_tile(back, carry):
        kt = last - 1 - back
        pl.when(touched_smem[kt] == 0.0)(functools.partial(slc_update, kt, False))
        return carry

    lax.fori_loop(0, last, slc_tile, 0)
    o_slc = _softmax_result(acc_scr, HEAD_DIM)

    wlen = NSA_WINDOW + tq
    ws = pl.multiple_of(jnp.maximum(t0 - NSA_WINDOW, 0), tq)
    keys = jnp.concatenate([k_ref[0, pl.ds(ws, wlen), LANES:2 * LANES], pf_ref[pl.ds(ws, wlen), :]], axis=1)
    band = wband_ref[jnp.minimum(i, wband_ref.shape[0] - 1)]
    s_t = _dot_t(keys, q4) + jnp.concatenate([band] * nh, axis=1)
    weights = jnp.exp2(s_t - jnp.max(s_t, axis=0, keepdims=True)).astype(MXU_DTYPE)
    win = _dot(_with_ones_rows(vt_ref[LANES:2 * LANES, pl.ds(ws, wlen)]), weights)
    o_win = win[0:HEAD_DIM, :] * (1.0 / win[HEAD_DIM:HEAD_DIM + 1, :])

    gates = jax.nn.sigmoid(g_ref[0]).T
    for h in range(nh):
        c = slice(h * tq, (h + 1) * tq)
        o = (gates[3 * h:3 * h + 1, :] * o_cmp[:, c] + gates[3 * h + 1:3 * h + 2, :] * o_slc[:, c]
             + gates[3 * h + 2:3 * h + 3, :] * o_win[:, c])
        o_ref[0, :, h * LANES:(h + 1) * LANES] = o.T.astype(o_ref.dtype)


def _nsa(h3, gates3, vt, kc, vct, *, tq=NSA_Q_BLOCK, tk=512):
    b, s, _ = h3.shape
    ncp = kc.shape[1]
    ns = s // NSA_SEL_LEN
    tk = min(tk, s)
    w = NSA_HEADS * HEAD_DIM
    c2st = _cmp_to_sel_t(ncp, ns)
    pf = _pos_features(jnp.arange(s, dtype=F32))
    cpf = _pos_features(jnp.arange(ncp, dtype=F32) * NSA_CMP_STRIDE + 0.5 * (NSA_CMP_LEN - 1))
    sf = _slope_features(tq).astype(MXU_DTYPE)
    wband = _window_band_bias(tq)
    const = lambda a: _resident(a.shape, lambda bi, i: (0,) * a.ndim)
    return pl.pallas_call(
        functools.partial(_nsa_kernel, tq=tq, tk=tk, topk=min(NSA_TOPK, ns), cmp_chunk=min(128, ncp)),
        grid=(b, s // tq),
        in_specs=[
            pl.BlockSpec((1, tq, w), lambda bi, i: (bi, i, COL_NSA_Q // w)),
            _resident((1, s, 2 * LANES), lambda bi, i: (bi, 0, COL_NSA_K // (2 * LANES))),
            _resident((2 * LANES, s), lambda bi, i: (0, bi)),
            _resident((1, ncp, LANES), lambda bi, i: (bi, 0, 0)),
            _resident((1, LANES, ncp), lambda bi, i: (bi, 0, 0)),
            const(c2st), const(pf), const(cpf), const(sf), const(wband),
            pl.BlockSpec((1, tq, LANES), lambda bi, i: (bi, i, 0)),
        ],
        out_specs=pl.BlockSpec((1, tq, w), lambda bi, i: (bi, i, 0)),
        out_shape=jax.ShapeDtypeStruct((b, s, w), MXU_DTYPE),
        scratch_shapes=[pltpu.VMEM((1, NSA_HEADS * tq), F32),
                        pltpu.VMEM((HEAD_DIM + DEN_ROWS, NSA_HEADS * tq), F32),
                        pltpu.VMEM((ns, tq), F32), pltpu.SMEM((s // tk,), F32),
                        pltpu.VMEM((HEAD_DIM, NSA_HEADS * tq), F32), pltpu.VMEM((ns, tq), F32)],
        compiler_params=_cparams(("parallel", "arbitrary")),
        name="nsa",
    )(h3, h3, vt, kc, vct, c2st, pf, cpf, sf, wband, gates3)


def _window_band_bias(tq):
    wlen = NSA_WINDOW + tq
    u = np.arange(tq)[None, :]
    w = np.arange(wlen)[:, None]
    offsets = list(range(0, NSA_WINDOW, tq)) + [NSA_WINDOW]
    tiles = []
    for off in offsets:
        dist = off + u - w
        tiles.append(np.where((dist >= 0) & (dist < NSA_WINDOW), 0.0, NEG_INF))
    return jnp.asarray(np.stack(tiles).astype(np.float32))


def _cmp_to_sel_t(ncp, ns):
    nc = ncp - 1
    c = np.arange(ncp)[None, :]
    j = np.arange(ns)[:, None]
    start = c * NSA_CMP_STRIDE
    m = (start < (j + 1) * NSA_SEL_LEN) & (start + NSA_CMP_LEN - 1 >= j * NSA_SEL_LEN) & (c < nc)
    return jnp.asarray(m.astype(np.float32)).astype(MXU_DTYPE)


def _token_rows(ref, dilation, scr):
    if dilation == 1:
        return ref[...]
    tm = ref.shape[0] * dilation
    for r in range(dilation):
        for c in range(2):
            lo = r * 2 * LANES + c * LANES
            scr[c, pl.ds(r, tm // dilation, stride=dilation), :] = ref[:, lo:lo + LANES]
    return jnp.concatenate([scr[0], scr[1]], axis=1)


def _mixout_kernel(x_ref, mla_ref, d0_ref, d1_ref, d2_ref, l0_ref, l1_ref, l2_ref, nsa_ref, w_ref, o_ref,
                   relayout_scr):
    dils = [dil for _, dil in DIL_PAIRS]
    o_dil = [_token_rows(r, dil, relayout_scr.at[2 * g:2 * g + 2])
             for g, (r, dil) in enumerate(zip((d0_ref, d1_ref, d2_ref), dils))]
    lse = [_token_rows(r, dil, relayout_scr.at[6 + 2 * g:8 + 2 * g])
           for g, (r, dil) in enumerate(zip((l0_ref, l1_ref, l2_ref), dils))]
    top = jnp.maximum(jnp.maximum(lse[0], lse[1]), lse[2])
    e = [jnp.exp(l - top) for l in lse]
    inv = 1.0 / (e[0] + e[1] + e[2])
    acc = x_ref[...] + _dot(mla_ref[...], w_ref[0:768, :])
    for g in range(3):
        mixed = (o_dil[g] * (e[g] * inv)).astype(MXU_DTYPE)
        acc += _dot(mixed, w_ref[768 + g * 256:768 + (g + 1) * 256, :])
    acc += _dot(nsa_ref[...], w_ref[1536:2048, :])
    o_ref[...] = acc


def _mixout(x2, o_mla, o_dil, lse_dil, o_nsa, w, *, tm=512):
    n, d = x2.shape
    row = lambda width: pl.BlockSpec((tm, width), lambda i: (i, 0))
    dil_specs = [pl.BlockSpec((tm // dil, dil * 2 * LANES), lambda i: (i, 0)) for _, dil in DIL_PAIRS]
    return pl.pallas_call(
        _mixout_kernel,
        grid=(n // tm,),
        in_specs=[row(d), row(o_mla.shape[1])] + dil_specs + dil_specs + [
            row(o_nsa.shape[1]), pl.BlockSpec(w.shape, lambda i: (0, 0))],
        out_specs=row(d),
        out_shape=jax.ShapeDtypeStruct((n, d), F32),
        scratch_shapes=[pltpu.VMEM((12, tm, LANES), F32)],
        compiler_params=_cparams(("parallel",)),
        name="mix_out",
    )(x2, o_mla, *o_dil, *lse_dil, o_nsa, w)


def kernel(x, ffn1_norm, ffn1_w_in, ffn1_w_out, mix_norm, w_mix_in, mla_q_norm, mla_w_uq, mla_kv_norm,
           mla_w_ukv, nsa_cmp_pos, nsa_phi_k1, nsa_phi_k2, nsa_phi_v1, nsa_phi_v2, w_mix_out, ffn2_norm,
           ffn2_w_in, ffn2_w_out, final_norm):
    b, s, d = x.shape
    depth = ffn1_w_in.shape[0]
    n = b * s
    cos, sin = _rope_tables(s)
    x2 = x.reshape(n, d)
    for l in range(depth):
        x2 = _ffn(x2, ffn1_norm[l], *_prep_ffn(ffn1_w_in[l], ffn1_w_out[l]))

        h2, gates2, cmp_in, nsa_vt, *dil_in = _mixin(x2, mix_norm[l], *_prep_mixin(w_mix_in[l]))
        h3 = h2.reshape(b, s, H_COLS)

        q, k, vt = _mla_proj(h2, mla_q_norm[l].reshape(1, -1), mla_kv_norm[l].reshape(1, -1),
                             *_prep_mla(mla_w_uq[l], mla_w_ukv[l]), cos, sin)
        o_mla = _mla_flash(q.reshape(b, s, -1), k.reshape(b, s, -1), vt)

        dil = [_dilated_group(hg, g, b) for g, hg in enumerate(dil_in)]

        kc, vct = _nsa_compress(cmp_in, b, *_prep_nsa_cmp(nsa_cmp_pos[l], nsa_phi_k1[l], nsa_phi_k2[l],
                                                          nsa_phi_v1[l], nsa_phi_v2[l]))
        o_nsa = _nsa(h3, gates2.reshape(b, s, LANES), nsa_vt, kc, vct)

        x2 = _mixout(x2, o_mla.reshape(n, -1), [o for o, _ in dil], [e for _, e in dil],
                     o_nsa.reshape(n, -1), w_mix_out[l].astype(MXU_DTYPE))

        gf = final_norm if l == depth - 1 else None
        x2 = _ffn(x2, ffn2_norm[l], *_prep_ffn(ffn2_w_in[l], ffn2_w_out[l]), gf)
    return x2.reshape(b, s, d)
```

```python
import functools
import math

import numpy as np
import jax
import jax.numpy as jnp
from jax import lax
from jax.experimental import pallas as pl
from jax.experimental.pallas import tpu as pltpu

F32 = jnp.float32
MXU_DTYPE = jnp.bfloat16
VMEM_LIMIT_BYTES = 56 * 1024 * 1024
LANES = 128
LOG2E = math.log2(math.e)
FFN_TILE = 512
DEN_ROWS = 16
LAZY_MAX_SLACK = 16.0

HEAD_DIM = 128
RMS_EPS = 1e-6
NEG_INF = -1e30

MLA_HEADS = 6
MLA_LORA = 512
MLA_NOPE = 128
MLA_ROPE = 64
MLA_V = 128
MLA_QK_PAD = 256
ROPE_BASE = 10000.0

DIL_PAIRS = ((128, 1), (512, 4), (2048, 16))
DIL_HEADS = 6
DIL_BLOCK = 128

NSA_HEADS = 4
NSA_CMP_LEN = 32
NSA_CMP_STRIDE = 16
NSA_SEL_LEN = 64
NSA_TOPK = 16
NSA_WINDOW = 512
NSA_FORCED_SCORE = 100.0
NSA_Q_BLOCK = 256
POS_SPLIT = 128

N_ALIBI = DIL_HEADS + NSA_HEADS
ALIBI_SLOPES = tuple(float(2.0 ** (-8.0 * i / N_ALIBI)) for i in range(1, N_ALIBI + 1))

COL_QLAT = 0
COL_KVLAT = 512
COL_NSA_Q = 1024
COL_KROPE = 1536
COL_NSA_K = 1792
H_COLS = 2048
DIL_GROUP_COLS = 768
AUX_COLS = 384


def _cparams(sem):
    return pltpu.CompilerParams(dimension_semantics=sem, vmem_limit_bytes=VMEM_LIMIT_BYTES)


def _rms(x, g):
    ms = jnp.mean(x * x, axis=-1, keepdims=True)
    return (x * lax.rsqrt(ms + RMS_EPS)) * g


def _dot(a, b):
    return jnp.dot(a, b, preferred_element_type=F32)


def _shr(x, pow2):
    return lax.shift_right_logical(x, int(pow2).bit_length() - 1)


def _dot_t(a, b):
    return lax.dot_general(a, b, (((1,), (1,)), ((), ())), preferred_element_type=F32)


def _resident(shape, index_map):
    return pl.BlockSpec(shape, index_map, pipeline_mode=pl.Buffered(1))


def _ffn_kernel(x_ref, g_ref, wg_ref, wu_ref, wo_ref, *rest, final):
    if final:
        gf_ref, o_ref, xn_ref = rest
    else:
        o_ref, xn_ref = rest
    j = pl.program_id(1)

    @pl.when(j == 0)
    def _():
        x = x_ref[...]
        xn_ref[...] = _rms(x, g_ref[...]).astype(xn_ref.dtype)
        o_ref[...] = x

    xn = xn_ref[...]
    gate = _dot(xn, wg_ref[...])
    up = _dot(xn, wu_ref[...])
    h = (0.5 * gate) * jax.nn.sigmoid(gate) * up
    o_ref[...] += _dot(h.astype(MXU_DTYPE), wo_ref[...])

    if final:
        @pl.when(j == pl.num_programs(1) - 1)
        def _():
            o_ref[...] = _rms(o_ref[...], gf_ref[...])


def _ffn(x2, g, wg, wu, wo, gf=None, *, tm=512, tf=FFN_TILE):
    n, d = x2.shape
    dffp = wo.shape[0]
    nff = dffp // tf
    tm = min(tm, n)
    final = gf is not None
    in_specs = [
        pl.BlockSpec((tm, d), lambda i, j: (i, 0)),
        pl.BlockSpec((1, d), lambda i, j: (0, 0)),
        pl.BlockSpec((d, tf), lambda i, j: (0, j)),
        pl.BlockSpec((d, tf), lambda i, j: (0, j)),
        pl.BlockSpec((tf, d), lambda i, j: (j, 0)),
    ]
    args = [x2, g.reshape(1, d), wg, wu, wo]
    if final:
        in_specs.append(pl.BlockSpec((1, d), lambda i, j: (0, 0)))
        args.append(gf.reshape(1, d))
    return pl.pallas_call(
        functools.partial(_ffn_kernel, final=final),
        grid=(n // tm, nff),
        in_specs=in_specs,
        out_specs=pl.BlockSpec((tm, d), lambda i, j: (i, 0)),
        out_shape=jax.ShapeDtypeStruct((n, d), F32),
        scratch_shapes=[pltpu.VMEM((tm, d), MXU_DTYPE)],
        compiler_params=_cparams(("parallel", "arbitrary")),
        name="ffn_final" if final else "ffn",
    )(*args)


def _prep_ffn(w_in, w_out, tf=FFN_TILE):
    dff = w_in.shape[1] // 2
    pad = -(-dff // tf) * tf - dff
    wg = jnp.pad(w_in[:, :dff].astype(MXU_DTYPE), ((0, 0), (0, pad)))
    wu = jnp.pad(w_in[:, dff:].astype(MXU_DTYPE), ((0, 0), (0, pad)))
    wo = jnp.pad(w_out.astype(MXU_DTYPE), ((0, pad), (0, 0)))
    return wg, wu, wo


def _mixin_body(x_ref, g_ref, wm_ref, wd_ref, wa_ref, wvt_ref, hm_ref, gate_ref, cmp_ref, vt_ref, dil_refs,
                relayout_scr):
    xn = _rms(x_ref[...], g_ref[...]).astype(MXU_DTYPE)
    hm_ref[...] = _dot(xn, wm_ref[...]).astype(hm_ref.dtype)
    aux = _dot(xn, wa_ref[...])
    gate_ref[...] = aux[:, 2 * LANES:]
    for c in range(2):
        relayout_scr[c] = aux[:, c * LANES:(c + 1) * LANES]
    for r in range(NSA_CMP_STRIDE):
        for c in range(2):
            lo = r * 2 * LANES + c * LANES
            cmp_ref[:, lo:lo + LANES] = relayout_scr[c, pl.ds(r, aux.shape[0] // NSA_CMP_STRIDE,
                                                               stride=NSA_CMP_STRIDE), :]
    vt_ref[...] = _dot_t(wvt_ref[...], xn).astype(vt_ref.dtype)
    hd = _dot(xn, wd_ref[...])
    tm = hd.shape[0]
    for g, out_ref in enumerate(dil_refs):
        dilation = DIL_PAIRS[g][1]
        cols = hd[:, g * DIL_GROUP_COLS:(g + 1) * DIL_GROUP_COLS]
        if dilation == 1:
            out_ref[...] = cols.astype(out_ref.dtype)
            continue
        for c in range(DIL_GROUP_COLS // LANES):
            relayout_scr[c] = cols[:, c * LANES:(c + 1) * LANES]
        for r in range(dilation):
            for c in range(DIL_GROUP_COLS // LANES):
                piece = relayout_scr[c, pl.ds(r, tm // dilation, stride=dilation), :]
                out_ref[:, r * DIL_GROUP_COLS + c * LANES:r * DIL_GROUP_COLS + (c + 1) * LANES] = (
                    piece.astype(out_ref.dtype))


def _mixin_kernel(x_ref, g_ref, wm_ref, wd_ref, wa_ref, wvt_ref, hm_ref, gate_ref, cmp_ref, vt_ref, d0_ref,
                  d1_ref, d2_ref, relayout_scr):
    _mixin_body(x_ref, g_ref, wm_ref, wd_ref, wa_ref, wvt_ref, hm_ref, gate_ref, cmp_ref, vt_ref,
                (d0_ref, d1_ref, d2_ref), relayout_scr)


def _mixin(x2, g, wm, wd, wa, wvt, *, tm=512):
    n, d = x2.shape
    weight = lambda a: _resident(a.shape, lambda i: (0, 0))
    dils = [dil for _, dil in DIL_PAIRS]
    chunk = NSA_CMP_STRIDE
    return pl.pallas_call(
        _mixin_kernel,
        grid=(n // tm,),
        in_specs=[
            pl.BlockSpec((tm, d), lambda i: (i, 0)),
            pl.BlockSpec((1, d), lambda i: (0, 0)),
            weight(wm), weight(wd), weight(wa), weight(wvt),
        ],
        out_specs=[
            pl.BlockSpec((tm, H_COLS), lambda i: (i, 0)),
            pl.BlockSpec((tm, LANES), lambda i: (i, 0)),
            pl.BlockSpec((tm // chunk, chunk * 2 * LANES), lambda i: (i, 0)),
            pl.BlockSpec((2 * LANES, tm), lambda i: (0, i)),
        ] + [pl.BlockSpec((tm // dil, dil * DIL_GROUP_COLS), lambda i: (i, 0)) for dil in dils],
        out_shape=[
            jax.ShapeDtypeStruct((n, H_COLS), MXU_DTYPE),
            jax.ShapeDtypeStruct((n, LANES), F32),
            jax.ShapeDtypeStruct((n // chunk, chunk * 2 * LANES), F32),
            jax.ShapeDtypeStruct((2 * LANES, n), MXU_DTYPE),
        ] + [jax.ShapeDtypeStruct((n // dil, dil * DIL_GROUP_COLS), MXU_DTYPE) for dil in dils],
        scratch_shapes=[pltpu.VMEM((DIL_GROUP_COLS // LANES, tm, LANES), F32)],
        compiler_params=_cparams(("parallel",)),
        name="mix_in",
    )(x2, g.reshape(1, d), wm, wd, wa, wvt)


def _rot_half_cols(w):
    half = w.shape[1] // 2
    return jnp.concatenate([-w[:, half:], w[:, :half]], axis=1)


def _prep_mixin(w):
    d = w.shape[0]
    o = 0
    parts = {}
    for name, width in (("q_lat", 512), ("kv_lat", 512), ("k_rope", 64), ("dq", 768), ("dk", 768),
                        ("dv", 768), ("nq", 512), ("nkc", 128), ("nvc", 128), ("nks", 128),
                        ("nvs", 128), ("nkw", 128), ("nvw", 128), ("ng", 12)):
        parts[name] = w[:, o:o + width]
        o += width
    z64 = jnp.zeros((d, 64), w.dtype)
    kr = parts["k_rope"]
    wm = jnp.concatenate([
        parts["q_lat"], parts["kv_lat"], parts["nq"],
        kr, z64, _rot_half_cols(kr), z64,
        parts["nks"], parts["nkw"]], axis=1).astype(MXU_DTYPE)
    gw = 2 * HEAD_DIM
    wd = jnp.concatenate([parts[name][:, g * gw:(g + 1) * gw] for g in range(len(DIL_PAIRS))
                          for name in ("dq", "dk", "dv")], axis=1).astype(MXU_DTYPE)
    wa = jnp.concatenate([parts["nkc"], parts["nvc"], parts["ng"],
                          jnp.zeros((d, LANES - 12), w.dtype)], axis=1).astype(MXU_DTYPE)
    wvt = jnp.concatenate([parts["nvs"], parts["nvw"]], axis=1).T.astype(MXU_DTYPE)
    return wm, wd, wa, wvt


def _rope_tables_kernel(invf_ref, cos_ref, sin_ref):
    tm = cos_ref.shape[0]
    pos = (pl.program_id(0) * tm + lax.broadcasted_iota(jnp.int32, (tm, 1), 0)).astype(F32)
    ang = pos * invf_ref[...]
    cos_ref[...] = jnp.cos(ang)
    sin_ref[...] = jnp.sin(ang)


def _rope_tables(seq, *, tm=512):
    tm = min(tm, seq)
    spec = pl.BlockSpec((tm, LANES), lambda i: (i, 0))
    sds = jax.ShapeDtypeStruct((seq, LANES), F32)
    return pl.pallas_call(
        _rope_tables_kernel,
        grid=(seq // tm,),
        in_specs=[pl.BlockSpec((1, LANES), lambda i: (0, 0))],
        out_specs=[spec, spec],
        out_shape=[sds, sds],
        compiler_params=_cparams(("parallel",)),
        name="rope_tables",
    )(_rope_inv_freq_row())


def _mla_proj_kernel(lat_ref, kr_ref, qn_ref, kvn_ref, wq_ref, wk_ref, wvt_ref, cos_ref, sin_ref,
                     q_ref, k_ref, vt_ref):
    cos = cos_ref[...]
    sin = sin_ref[...]
    scale = (MLA_NOPE + MLA_ROPE) ** -0.5 * LOG2E

    lat = lat_ref[...].astype(F32)
    qn = _rms(lat[:, :MLA_LORA], qn_ref[...]).astype(MXU_DTYPE)
    kvn = _rms(lat[:, MLA_LORA:], kvn_ref[...]).astype(MXU_DTYPE)
    qm = _dot(qn, wq_ref[...])
    rot0 = MLA_HEADS * MLA_QK_PAD
    for h in range(MLA_HEADS):
        c = h * MLA_QK_PAD
        nope = qm[:, c:c + LANES]
        pe = qm[:, c + LANES:c + 2 * LANES] * cos + qm[:, rot0 + h * LANES:rot0 + (h + 1) * LANES] * sin
        q_ref[:, c:c + LANES] = (nope * scale).astype(q_ref.dtype)
        q_ref[:, c + LANES:c + 2 * LANES] = (pe * scale).astype(q_ref.dtype)

    kr = kr_ref[...].astype(F32)
    kpe = (kr[:, :LANES] * cos + kr[:, LANES:] * sin).astype(k_ref.dtype)
    kn = _dot(kvn, wk_ref[...])
    for h in range(MLA_HEADS):
        c = h * MLA_QK_PAD
        k_ref[:, c:c + LANES] = kn[:, h * LANES:(h + 1) * LANES].astype(k_ref.dtype)
        k_ref[:, c + LANES:c + 2 * LANES] = kpe
    vt_ref[...] = _dot_t(wvt_ref[...], kvn).astype(vt_ref.dtype)


def _mla_proj(h2, qn, kvn, wq, wk, wvt, cos, sin, *, tm=512):
    n = h2.shape[0]
    tm = min(tm, cos.shape[0])
    tiles_per_seq = cos.shape[0] // tm
    qk_cols = MLA_HEADS * MLA_QK_PAD
    v_rows = MLA_HEADS * MLA_V
    full = lambda a: pl.BlockSpec(a.shape, lambda i: (0,) * a.ndim)
    table = pl.BlockSpec((tm, LANES), lambda i: (i % tiles_per_seq, 0))
    return pl.pallas_call(
        _mla_proj_kernel,
        grid=(n // tm,),
        in_specs=[
            pl.BlockSpec((tm, 2 * MLA_LORA), lambda i: (i, 0)),
            pl.BlockSpec((tm, 2 * LANES), lambda i: (i, COL_KROPE // (2 * LANES))),
            full(qn), full(kvn), full(wq), full(wk), full(wvt), table, table,
        ],
        out_specs=[
            pl.BlockSpec((tm, qk_cols), lambda i: (i, 0)),
            pl.BlockSpec((tm, qk_cols), lambda i: (i, 0)),
            pl.BlockSpec((v_rows, tm), lambda i: (0, i)),
        ],
        out_shape=[
            jax.ShapeDtypeStruct((n, qk_cols), MXU_DTYPE),
            jax.ShapeDtypeStruct((n, qk_cols), MXU_DTYPE),
            jax.ShapeDtypeStruct((v_rows, n), MXU_DTYPE),
        ],
        compiler_params=_cparams(("parallel",)),
        name="mla_proj",
    )(h2, h2, qn, kvn, wq, wk, wvt, cos, sin)


def _prep_mla(w_uq, w_ukv):
    r = MLA_LORA
    wq3 = w_uq.reshape(r, MLA_HEADS, MLA_NOPE + MLA_ROPE)
    nope, pe = wq3[..., :MLA_NOPE], wq3[..., MLA_NOPE:]
    z = jnp.zeros((r, MLA_HEADS, 64), w_uq.dtype)
    main = jnp.concatenate([nope, pe, z], axis=-1).reshape(r, MLA_HEADS * MLA_QK_PAD)
    half = MLA_ROPE // 2
    pe_rot = jnp.concatenate([-pe[..., half:], pe[..., :half]], axis=-1)
    rot = jnp.concatenate([pe_rot, z], axis=-1).reshape(r, MLA_HEADS * LANES)
    wq = jnp.concatenate([main, rot], axis=1).astype(MXU_DTYPE)
    wkv3 = w_ukv.reshape(r, MLA_HEADS, MLA_NOPE + MLA_V)
    wk = wkv3[..., :MLA_NOPE].reshape(r, MLA_HEADS * MLA_NOPE).astype(MXU_DTYPE)
    wvt = wkv3[..., MLA_NOPE:].reshape(r, MLA_HEADS * MLA_V).T.astype(MXU_DTYPE)
    return wq, wk, wvt


def _rope_inv_freq_row():
    half = MLA_ROPE // 2
    f = ROPE_BASE ** (-jnp.arange(half, dtype=F32) / half)
    return jnp.concatenate([f, f, jnp.zeros((LANES - MLA_ROPE,), F32)]).reshape(1, LANES)


def _with_ones_rows(v_t):
    return jnp.concatenate([v_t, jnp.ones((DEN_ROWS, v_t.shape[1]), v_t.dtype)], axis=0)


def _softmax_piece_t(scores, v_aug, m_ref, acc_ref, *, first):
    if first:
        s_t = scores()
        m_new = jnp.max(s_t, axis=0, keepdims=True)
        acc_ref[...] = _dot(v_aug, jnp.exp2(s_t - m_new).astype(MXU_DTYPE))
        m_ref[...] = m_new
        return

    m_old = m_ref[...]
    s_t = scores()
    pv = _dot(v_aug, jnp.exp2(s_t - m_old).astype(MXU_DTYPE))
    fits = jnp.max(jnp.max(s_t, axis=0, keepdims=True) - m_old) <= LAZY_MAX_SLACK

    @pl.when(fits)
    def _():
        acc_ref[...] += pv

    @pl.when(jnp.logical_not(fits))
    def _():
        s_again = scores()
        m_new = jnp.maximum(m_old, jnp.max(s_again, axis=0, keepdims=True))
        weights = jnp.exp2(s_again - m_new).astype(MXU_DTYPE)
        acc_ref[...] = jnp.exp2(m_old - m_new) * acc_ref[...] + _dot(v_aug, weights)
        m_ref[...] = m_new


def _softmax_result(acc_scr, dv):
    return acc_scr[0:dv, :] * (1.0 / acc_scr[dv:dv + 1, :])


def _flash_kernel(qt_ref, kt_ref, q_ref, k_ref, vt_ref, o_ref, m_scr, acc_scr, *, t, dsub):
    step = pl.program_id(2)
    qi = qt_ref[step]
    ki = kt_ref[step]

    def piece(k0, q0, size, diagonal):
        ks = slice(k0, k0 + size)
        qs = slice(q0, q0 + size)

        def scores():
            s_t = _dot_t(k_ref[0, ks, :], q_ref[0, qs, :])
            if diagonal and k0 + size - 1 > q0:
                kpos = k0 + lax.broadcasted_iota(jnp.int32, (size, 1), 0)
                qpos = q0 + lax.broadcasted_iota(jnp.int32, (1, size), 1)
                s_t = jnp.where(kpos <= qpos, s_t, NEG_INF)
            return s_t

        _softmax_piece_t(scores, _with_ones_rows(vt_ref[:, ks]), m_scr.at[:, qs], acc_scr.at[:, qs],
                         first=diagonal and k0 == q0)

    @pl.when(ki == qi)
    def _():
        starts = range(0, t, dsub)
        for p in starts:
            piece(p, p, dsub, True)
        for k0 in starts:
            for q0 in starts:
                if q0 > k0:
                    piece(k0, q0, dsub, True)

    @pl.when(ki < qi)
    def _():
        piece(0, 0, t, False)

    @pl.when(ki == 0)
    def _():
        o_ref[0] = _softmax_result(acc_scr, MLA_V).T.astype(o_ref.dtype)


def _mla_flash(q3, k3, vt, *, t=2048, dsub=1024):
    b, s, _ = q3.shape
    t = min(t, s)
    dsub = min(dsub, t)
    nq = s // t
    pairs = [(qi, ki) for qi in range(nq) for ki in range(qi, -1, -1)]
    qt = jnp.asarray(np.array([p[0] for p in pairs], np.int32))
    kt = jnp.asarray(np.array([p[1] for p in pairs], np.int32))
    grid_spec = pltpu.PrefetchScalarGridSpec(
        num_scalar_prefetch=2,
        grid=(b, MLA_HEADS, len(pairs)),
        in_specs=[
            pl.BlockSpec((1, t, MLA_QK_PAD), lambda bi, h, st, qt, kt: (bi, qt[st], h)),
            pl.BlockSpec((1, t, MLA_QK_PAD), lambda bi, h, st, qt, kt: (bi, kt[st], h)),
            pl.BlockSpec((MLA_V, t), lambda bi, h, st, qt, kt: (h, bi * nq + kt[st])),
        ],
        out_specs=pl.BlockSpec((1, t, MLA_V), lambda bi, h, st, qt, kt: (bi, qt[st], h)),
        scratch_shapes=[pltpu.VMEM((1, t), F32), pltpu.VMEM((MLA_V + DEN_ROWS, t), F32)],
    )
    return pl.pallas_call(
        functools.partial(_flash_kernel, t=t, dsub=dsub),
        grid_spec=grid_spec,
        out_shape=jax.ShapeDtypeStruct((b, s, MLA_HEADS * MLA_V), MXU_DTYPE),
        compiler_params=_cparams(("parallel", "parallel", "arbitrary")),
        name="mla_flash",
    )(qt, kt, q3, k3, vt)


def _dil_kernel(q_ref, kc_ref, kp_ref, vc_ref, vp_ref, o_ref, lse_ref, *, dilation, span, slopes, tl):
    n = pl.program_id(2)
    scale = HEAD_DIM ** -0.5
    blk = DIL_BLOCK
    a = lax.broadcasted_iota(jnp.int32, (blk, 1), 0)
    c = lax.broadcasted_iota(jnp.int32, (1, 2 * blk), 1)
    j = blk + a - c
    in_band = (j >= 0) & (j <= span)
    first_valid = in_band & ((c >= blk) | (n > 0))
    dist = (j * dilation).astype(F32)
    for hg in range(2):
        cols = slice(hg * LANES, (hg + 1) * LANES)
        bias = slopes[hg] * dist
        for sb in range(tl // blk):
            rows = slice(sb * blk, (sb + 1) * blk)
            q = (q_ref[0, rows, cols].astype(F32) * scale).astype(MXU_DTYPE)
            if sb == 0:
                kprev, vprev, valid = kp_ref[0, :, cols], vp_ref[0, :, cols], first_valid
            else:
                prev = slice((sb - 1) * blk, sb * blk)
                kprev, vprev, valid = kc_ref[0, prev, cols], vc_ref[0, prev, cols], in_band
            keys = jnp.concatenate([kprev, kc_ref[0, rows, cols]], axis=0)
            vals = jnp.concatenate([vprev, vc_ref[0, rows, cols]], axis=0)
            s = jnp.where(valid, _dot_t(q, keys) - bias, NEG_INF)
            m = jnp.max(s, axis=1, keepdims=True)
            e = jnp.where(valid, jnp.exp(s - m), 0.0)
            den = jnp.sum(e, axis=1, keepdims=True)
            o_ref[0, rows, cols] = _dot((e / den).astype(MXU_DTYPE), vals)
            lse_ref[0, rows, cols] = jnp.broadcast_to(m + jnp.log(den), (blk, LANES))


def _dilated_group(hg, g, b, *, tl=1024):
    window, dilation = DIL_PAIRS[g]
    span = window // dilation
    seq_l = hg.shape[0] // b
    tl = min(tl, seq_l)
    per_blk = tl // DIL_BLOCK
    w = 2 * LANES
    hv = hg.reshape(b, seq_l, dilation * DIL_GROUP_COLS)
    qc, kc, vc = 0, 1, 2
    stride = DIL_GROUP_COLS // w
    cur = lambda col: pl.BlockSpec((1, tl, w), lambda bi, r, n: (bi, n, r * stride + col))
    prev = lambda col: pl.BlockSpec(
        (1, DIL_BLOCK, w), lambda bi, r, n: (bi, jnp.maximum(n * per_blk - 1, 0), r * stride + col))
    out_spec = pl.BlockSpec((1, tl, w), lambda bi, r, n: (bi, n, r))
    out_sds = jax.ShapeDtypeStruct((b, seq_l, dilation * w), F32)
    o, lse = pl.pallas_call(
        functools.partial(_dil_kernel, dilation=dilation, span=span,
                          slopes=ALIBI_SLOPES[2 * g:2 * g + 2], tl=tl),
        grid=(b, dilation, seq_l // tl),
        in_specs=[cur(qc), cur(kc), prev(kc), cur(vc), prev(vc)],
        out_specs=[out_spec, out_spec],
        out_shape=[out_sds, out_sds],
        compiler_params=_cparams(("parallel", "parallel", "parallel")),
        name=f"dilated_g{g}",
    )(hv, hv, hv, hv, hv)
    return o.reshape(b * seq_l, dilation * w), lse.reshape(b * seq_l, dilation * w)


def _nsa_cmp_kernel(x_ref, pos_ref, wa_ref, wb_ref, w2k_ref, w2vt_ref, kc_ref, vct_ref):
    x = x_ref[0]
    xa = (x + pos_ref[0:1, :]).astype(MXU_DTYPE)
    xb = (x + pos_ref[1:2, :]).astype(MXU_DTYPE)
    first = _dot(xa, wa_ref[...])
    second = _dot(xb, wb_ref[...])
    second_next = pltpu.roll(second, x.shape[0] - 1, 0)
    pre = first + second_next
    hid = (pre * jax.nn.sigmoid(pre)).astype(MXU_DTYPE)
    kc_ref[0] = _dot(hid[:, :LANES], w2k_ref[...]).astype(kc_ref.dtype)
    vct_ref[0] = _dot_t(w2vt_ref[...], hid[:, LANES:]).astype(vct_ref.dtype)


def _nsa_compress(cmp_in, b, pos2, wa, wb, w2k, w2vt):
    nchunk = cmp_in.shape[0] // b
    xk = cmp_in.reshape(b, nchunk, cmp_in.shape[1])
    full = lambda a: pl.BlockSpec(a.shape, lambda bi: (0,) * a.ndim)
    return pl.pallas_call(
        _nsa_cmp_kernel,
        grid=(b,),
        in_specs=[pl.BlockSpec((1, nchunk, xk.shape[2]), lambda bi: (bi, 0, 0)),
                  full(pos2), full(wa), full(wb), full(w2k), full(w2vt)],
        out_specs=[pl.BlockSpec((1, nchunk, LANES), lambda bi: (bi, 0, 0)),
                   pl.BlockSpec((1, LANES, nchunk), lambda bi: (bi, 0, 0))],
        out_shape=[jax.ShapeDtypeStruct((b, nchunk, LANES), MXU_DTYPE),
                   jax.ShapeDtypeStruct((b, LANES, nchunk), MXU_DTYPE)],
        compiler_params=_cparams(("parallel",)),
        name="nsa_cmp",
    )(xk, pos2, wa, wb, w2k, w2vt)


def _prep_nsa_cmp(cmp_pos, phi_k1, phi_k2, phi_v1, phi_v2):
    half = NSA_CMP_LEN // 2
    pk = cmp_pos.reshape(2, half, HEAD_DIM)
    pos2 = jnp.concatenate([pk, pk], axis=-1).reshape(2, half * 2 * LANES)

    def halves(w1, is_v):
        w = w1.reshape(2, half, HEAD_DIM, HEAD_DIM)
        z = jnp.zeros_like(w)
        w = jnp.concatenate([z, w] if is_v else [w, z], axis=2)
        return w.reshape(2, half * 2 * LANES, HEAD_DIM)

    wk, wv = halves(phi_k1, False), halves(phi_v1, True)
    wa = jnp.concatenate([wk[0], wv[0]], axis=1).astype(MXU_DTYPE)
    wb = jnp.concatenate([wk[1], wv[1]], axis=1).astype(MXU_DTYPE)
    return pos2, wa, wb, phi_k2.astype(MXU_DTYPE), phi_v2.T.astype(MXU_DTYPE)


def _split3(x):
    hi = x.astype(MXU_DTYPE)
    r = x - hi.astype(F32)
    mid = r.astype(MXU_DTYPE)
    lo = (r - mid.astype(F32)).astype(MXU_DTYPE)
    return hi, mid, lo


def _pos_features(pos):
    hi = jnp.floor(pos / POS_SPLIT) * POS_SPLIT
    lo = pos - hi
    cols = jnp.stack([hi, hi, hi, lo, lo, lo], axis=1)
    return jnp.pad(cols, ((0, 0), (0, LANES - 6))).astype(MXU_DTYPE)


def _slope_features(tq):
    sig = jnp.asarray([s * LOG2E for s in ALIBI_SLOPES[DIL_HEADS:]], F32)
    pieces = jnp.stack(_split3(sig), axis=1)
    rows = jnp.concatenate([pieces, pieces], axis=1)
    rows = jnp.pad(rows, ((0, 0), (0, LANES - 6)))
    return jnp.repeat(rows, tq, axis=0)


def _masked_softmax_t(s_t, mask):
    s_t = jnp.where(mask, s_t, NEG_INF)
    m = jnp.max(s_t, axis=0, keepdims=True)
    e = jnp.exp2(s_t - m)
    den = jnp.sum(e, axis=0, keepdims=True)
    return e * jnp.where(m > 0.5 * NEG_INF, 1.0 / den, 0.0)


def _nsa_kernel(q_ref, k_ref, vt_ref, kc_ref, vct_ref, c2st_ref, pf_ref, cpf_ref, sf_ref, wband_ref, g_ref,
                o_ref, m_scr, acc_scr, bias_scr, touched_smem, ocmp_scr, score_scr,
                *, tq, tk, topk, cmp_chunk, sel_chunk):
    i = pl.program_id(1)
    t0 = i * tq
    nh = NSA_HEADS
    cols = nh * tq
    ncp = kc_ref.shape[1]
    ns = c2st_ref.shape[0]
    blocks_per_tile = tk // NSA_SEL_LEN

    q = q_ref[0]
    q4 = jnp.concatenate([q[:, h * LANES:(h + 1) * LANES] for h in range(nh)], axis=0)
    q4 = (q4.astype(F32) * (HEAD_DIM ** -0.5 * LOG2E)).astype(MXU_DTYPE)
    q4 = jnp.concatenate([q4, sf_ref[...]], axis=1)
    col = lax.broadcasted_iota(jnp.int32, (1, cols), 1)
    tpos = (t0 + (col & (tq - 1))).astype(F32)

    def cmp_part(rows):
        cidx = lax.broadcasted_iota(jnp.int32, (rows, 1), 0).astype(F32)
        c_end = cidx * NSA_CMP_STRIDE + (NSA_CMP_LEN - 1)
        kc = jnp.concatenate([kc_ref[0, 0:rows, :], cpf_ref[0:rows, :]], axis=1)
        p_cmp = _masked_softmax_t(_dot_t(kc, q4), c_end <= tpos)
        ocmp_scr[...] = _dot(vct_ref[0, :, 0:rows], p_cmp.astype(MXU_DTYPE))
        p_sum = p_cmp[:, 0:tq]
        for h in range(1, nh):
            p_sum = p_sum + p_cmp[:, h * tq:(h + 1) * tq]
        c2st = c2st_ref[:, 0:rows]
        if MXU_DTYPE == jnp.float32:
            score_scr[...] = _dot(c2st, p_sum)
        else:
            score_scr[...] = sum(_dot(c2st, piece) for piece in _split3(p_sum))

    n_variants = -(-ncp // cmp_chunk)
    n_ending = (t0 + tq) // NSA_CMP_STRIDE - 1
    need = jnp.clip((n_ending + cmp_chunk - 1) // cmp_chunk, 1, n_variants)
    for v in range(1, n_variants + 1):
        pl.when(need == v)(functools.partial(cmp_part, min(v * cmp_chunk, ncp)))
    o_cmp = ocmp_scr[...]

    t1 = t0 + lax.broadcasted_iota(jnp.int32, (1, tq), 1)
    cur = _shr(t1, NSA_SEL_LEN).astype(F32)
    max_forced = 3
    n_forced = 1.0 + jnp.where(cur >= 1.0, 1.0, 0.0) + jnp.where(cur >= 2.0, 1.0, 0.0)

    def select_blocks(rows, may_lack_forced):
        jj = lax.broadcasted_iota(jnp.int32, (rows, 1), 0).astype(F32)
        forced = (jj == 0.0) | (jj == cur) | (jj == cur - 1.0)
        score = jnp.where(jj > cur, -1.0, jnp.where(forced, -2.0, score_scr[0:rows, :]))
        bias = jnp.where(forced, 0.0, NEG_INF)

        def pick(score, bias, wanted):
            best = jnp.max(score, axis=0, keepdims=True)
            idx = jnp.min(jnp.where(score == best, jj, float(rows)), axis=0, keepdims=True)
            hit = jj == idx
            if wanted is not None:
                hit = hit & wanted
            return jnp.where(hit, -2.0, score), jnp.where(hit, 0.0, bias)

        for _ in range(topk - max_forced):
            score, bias = pick(score, bias, None)
        bias_scr[0:rows, :] = bias
        if rows < ns:
            bias_scr[rows:ns, :] = jnp.full((ns - rows, tq), NEG_INF, F32)
        if may_lack_forced:
            score_scr[0:rows, :] = score

            @pl.when(t0 < (max_forced - 1) * NSA_SEL_LEN)
            def _():
                score, bias = score_scr[0:rows, :], bias_scr[0:rows, :]
                for done in range(topk - max_forced, topk - 1):
                    score, bias = pick(score, bias, n_forced + done < float(topk))
                bias_scr[0:rows, :] = bias

    sel_variants = -(-ns // sel_chunk)
    n_started = (t0 + tq) // NSA_SEL_LEN
    need_sel = jnp.clip((n_started + sel_chunk - 1) // sel_chunk, 1, sel_variants)
    assert (max_forced - 1) * NSA_SEL_LEN + tq <= sel_chunk * NSA_SEL_LEN
    for v in range(1, sel_variants + 1):
        pl.when(need_sel == v)(functools.partial(select_blocks, min(v * sel_chunk, ns), v == 1))

    bias = bias_scr[...]
    block_any = jnp.broadcast_to(jnp.max(bias, axis=1, keepdims=True), (ns, LANES))
    tile_any = jnp.max(block_any.reshape(ns // blocks_per_tile, blocks_per_tile, LANES), axis=1)
    for kt in range(ns // blocks_per_tile):
        touched_smem[kt] = tile_any[kt, 0]

    def slc_update(kt, causal):
        k0 = pl.multiple_of(kt * tk, tk)
        b0 = pl.multiple_of(kt * blocks_per_tile, blocks_per_tile)

        def scores():
            keys = jnp.concatenate([k_ref[0, pl.ds(k0, tk), 0:LANES], pf_ref[pl.ds(k0, tk), :]], axis=1)
            s_t = _dot_t(keys, q4)
            rows = [jnp.broadcast_to(bias_scr[pl.ds(b0 + r, 1), :], (NSA_SEL_LEN, tq))
                    for r in range(blocks_per_tile)]
            sel_bias = jnp.concatenate(rows, axis=0)
            s_t = s_t + jnp.concatenate([sel_bias] * nh, axis=1)
            if causal:
                kpos = (k0 + lax.broadcasted_iota(jnp.int32, (tk, 1), 0)).astype(F32)
                s_t = jnp.where(kpos <= tpos, s_t, NEG_INF)
            return s_t

        _softmax_piece_t(scores, _with_ones_rows(vt_ref[0:LANES, pl.ds(k0, tk)]), m_scr, acc_scr,
                         first=causal)

    last = t0 // tk
    slc_update(last, True)

    def slc_tile(back, carry):
        kt = last - 1 - back
        pl.when(touched_smem[kt] == 0.0)(functools.partial(slc_update, kt, False))
        return carry

    lax.fori_loop(0, last, slc_tile, 0)
    o_slc = _softmax_result(acc_scr, HEAD_DIM)

    wlen = NSA_WINDOW + tq
    ws = pl.multiple_of(jnp.maximum(t0 - NSA_WINDOW, 0), tq)
    keys = jnp.concatenate([k_ref[0, pl.ds(ws, wlen), LANES:2 * LANES], pf_ref[pl.ds(ws, wlen), :]], axis=1)
    band = wband_ref[jnp.minimum(i, wband_ref.shape[0] - 1)]
    s_t = _dot_t(keys, q4) + jnp.concatenate([band] * nh, axis=1)
    weights = jnp.exp2(s_t - jnp.max(s_t, axis=0, keepdims=True)).astype(MXU_DTYPE)
    win = _dot(_with_ones_rows(vt_ref[LANES:2 * LANES, pl.ds(ws, wlen)]), weights)
    o_win = win[0:HEAD_DIM, :] * (1.0 / win[HEAD_DIM:HEAD_DIM + 1, :])

    gates = jax.nn.sigmoid(g_ref[0]).T
    for h in range(nh):
        c = slice(h * tq, (h + 1) * tq)
        o = (gates[3 * h:3 * h + 1, :] * o_cmp[:, c] + gates[3 * h + 1:3 * h + 2, :] * o_slc[:, c]
             + gates[3 * h + 2:3 * h + 3, :] * o_win[:, c])
        o_ref[0, :, h * LANES:(h + 1) * LANES] = o.T.astype(o_ref.dtype)


def _nsa(h3, gates3, vt, kc, vct, *, tq=NSA_Q_BLOCK, tk=512):
    b, s, _ = h3.shape
    ncp = kc.shape[1]
    ns = s // NSA_SEL_LEN
    tk = min(tk, s)
    w = NSA_HEADS * HEAD_DIM
    c2st = _cmp_to_sel_t(ncp, ns)
    pf = _pos_features(jnp.arange(s, dtype=F32))
    cpf = _pos_features(jnp.arange(ncp, dtype=F32) * NSA_CMP_STRIDE + 0.5 * (NSA_CMP_LEN - 1))
    sf = _slope_features(tq).astype(MXU_DTYPE)
    wband = _window_band_bias(tq)
    const = lambda a: _resident(a.shape, lambda bi, i: (0,) * a.ndim)
    return pl.pallas_call(
        functools.partial(_nsa_kernel, tq=tq, tk=tk, topk=min(NSA_TOPK, ns), cmp_chunk=min(128, ncp),
                          sel_chunk=min(64, ns)),
        grid=(b, s // tq),
        in_specs=[
            pl.BlockSpec((1, tq, w), lambda bi, i: (bi, i, COL_NSA_Q // w)),
            _resident((1, s, 2 * LANES), lambda bi, i: (bi, 0, COL_NSA_K // (2 * LANES))),
            _resident((2 * LANES, s), lambda bi, i: (0, bi)),
            _resident((1, ncp, LANES), lambda bi, i: (bi, 0, 0)),
            _resident((1, LANES, ncp), lambda bi, i: (bi, 0, 0)),
            const(c2st), const(pf), const(cpf), const(sf), const(wband),
            pl.BlockSpec((1, tq, LANES), lambda bi, i: (bi, i, 0)),
        ],
        out_specs=pl.BlockSpec((1, tq, w), lambda bi, i: (bi, i, 0)),
        out_shape=jax.ShapeDtypeStruct((b, s, w), MXU_DTYPE),
        scratch_shapes=[pltpu.VMEM((1, NSA_HEADS * tq), F32),
                        pltpu.VMEM((HEAD_DIM + DEN_ROWS, NSA_HEADS * tq), F32),
                        pltpu.VMEM((ns, tq), F32), pltpu.SMEM((s // tk,), F32),
                        pltpu.VMEM((HEAD_DIM, NSA_HEADS * tq), F32), pltpu.VMEM((ns, tq), F32)],
        compiler_params=_cparams(("parallel", "arbitrary")),
        name="nsa",
    )(h3, h3, vt, kc, vct, c2st, pf, cpf, sf, wband, gates3)


def _window_band_bias(tq):
    wlen = NSA_WINDOW + tq
    u = np.arange(tq)[None, :]
    w = np.arange(wlen)[:, None]
    offsets = list(range(0, NSA_WINDOW, tq)) + [NSA_WINDOW]
    tiles = []
    for off in offsets:
        dist = off + u - w
        tiles.append(np.where((dist >= 0) & (dist < NSA_WINDOW), 0.0, NEG_INF))
    return jnp.asarray(np.stack(tiles).astype(np.float32))


def _cmp_to_sel_t(ncp, ns):
    nc = ncp - 1
    c = np.arange(ncp)[None, :]
    j = np.arange(ns)[:, None]
    start = c * NSA_CMP_STRIDE
    m = (start < (j + 1) * NSA_SEL_LEN) & (start + NSA_CMP_LEN - 1 >= j * NSA_SEL_LEN) & (c < nc)
    return jnp.asarray(m.astype(np.float32)).astype(MXU_DTYPE)


def _token_rows(ref, dilation, scr):
    if dilation == 1:
        return ref[...]
    tm = ref.shape[0] * dilation
    for r in range(dilation):
        for c in range(2):
            lo = r * 2 * LANES + c * LANES
            scr[c, pl.ds(r, tm // dilation, stride=dilation), :] = ref[:, lo:lo + LANES]
    return jnp.concatenate([scr[0], scr[1]], axis=1)


def _mixout_kernel(x_ref, mla_ref, d0_ref, d1_ref, d2_ref, l0_ref, l1_ref, l2_ref, nsa_ref, w_ref, o_ref,
                   relayout_scr):
    dils = [dil for _, dil in DIL_PAIRS]
    o_dil = [_token_rows(r, dil, relayout_scr.at[2 * g:2 * g + 2])
             for g, (r, dil) in enumerate(zip((d0_ref, d1_ref, d2_ref), dils))]
    lse = [_token_rows(r, dil, relayout_scr.at[6 + 2 * g:8 + 2 * g])
           for g, (r, dil) in enumerate(zip((l0_ref, l1_ref, l2_ref), dils))]
    top = jnp.maximum(jnp.maximum(lse[0], lse[1]), lse[2])
    e = [jnp.exp(l - top) for l in lse]
    inv = 1.0 / (e[0] + e[1] + e[2])
    acc = x_ref[...] + _dot(mla_ref[...], w_ref[0:768, :])
    for g in range(3):
        mixed = (o_dil[g] * (e[g] * inv)).astype(MXU_DTYPE)
        acc += _dot(mixed, w_ref[768 + g * 256:768 + (g + 1) * 256, :])
    acc += _dot(nsa_ref[...], w_ref[1536:2048, :])
    o_ref[...] = acc


def _mixout(x2, o_mla, o_dil, lse_dil, o_nsa, w, *, tm=512):
    n, d = x2.shape
    row = lambda width: pl.BlockSpec((tm, width), lambda i: (i, 0))
    dil_specs = [pl.BlockSpec((tm // dil, dil * 2 * LANES), lambda i: (i, 0)) for _, dil in DIL_PAIRS]
    return pl.pallas_call(
        _mixout_kernel,
        grid=(n // tm,),
        in_specs=[row(d), row(o_mla.shape[1])] + dil_specs + dil_specs + [
            row(o_nsa.shape[1]), pl.BlockSpec(w.shape, lambda i: (0, 0))],
        out_specs=row(d),
        out_shape=jax.ShapeDtypeStruct((n, d), F32),
        scratch_shapes=[pltpu.VMEM((12, tm, LANES), F32)],
        compiler_params=_cparams(("parallel",)),
        name="mix_out",
    )(x2, o_mla, *o_dil, *lse_dil, o_nsa, w)


def kernel(x, ffn1_norm, ffn1_w_in, ffn1_w_out, mix_norm, w_mix_in, mla_q_norm, mla_w_uq, mla_kv_norm,
           mla_w_ukv, nsa_cmp_pos, nsa_phi_k1, nsa_phi_k2, nsa_phi_v1, nsa_phi_v2, w_mix_out, ffn2_norm,
           ffn2_w_in, ffn2_w_out, final_norm):
    b, s, d = x.shape
    depth = ffn1_w_in.shape[0]
    n = b * s
    cos, sin = _rope_tables(s)
    x2 = x.reshape(n, d)
    for l in range(depth):
        x2 = _ffn(x2, ffn1_norm[l], *_prep_ffn(ffn1_w_in[l], ffn1_w_out[l]))

        h2, gates2, cmp_in, nsa_vt, *dil_in = _mixin(x2, mix_norm[l], *_prep_mixin(w_mix_in[l]))
        h3 = h2.reshape(b, s, H_COLS)

        q, k, vt = _mla_proj(h2, mla_q_norm[l].reshape(1, -1), mla_kv_norm[l].reshape(1, -1),
                             *_prep_mla(mla_w_uq[l], mla_w_ukv[l]), cos, sin)
        o_mla = _mla_flash(q.reshape(b, s, -1), k.reshape(b, s, -1), vt)

        dil = [_dilated_group(hg, g, b) for g, hg in enumerate(dil_in)]

        kc, vct = _nsa_compress(cmp_in, b, *_prep_nsa_cmp(nsa_cmp_pos[l], nsa_phi_k1[l], nsa_phi_k2[l],
                                                          nsa_phi_v1[l], nsa_phi_v2[l]))
        o_nsa = _nsa(h3, gates2.reshape(b, s, LANES), nsa_vt, kc, vct)

        x2 = _mixout(x2, o_mla.reshape(n, -1), [o for o, _ in dil], [e for _, e in dil],
                     o_nsa.reshape(n, -1), w_mix_out[l].astype(MXU_DTYPE))

        gf = final_norm if l == depth - 1 else None
        x2 = _ffn(x2, ffn2_norm[l], *_prep_ffn(ffn2_w_in[l], ffn2_w_out[l]), gf)
    return x2.reshape(b, s, d)
```

```python
import functools
import math

import numpy as np
import jax
import jax.numpy as jnp
from jax import lax
from jax.experimental import pallas as pl
from jax.experimental.pallas import tpu as pltpu

F32 = jnp.float32
MXU_DTYPE = jnp.bfloat16
VMEM_LIMIT_BYTES = 56 * 1024 * 1024
LANES = 128
LOG2E = math.log2(math.e)
FFN_TILE = 512
DEN_ROWS = 16
LAZY_MAX_SLACK = 16.0

HEAD_DIM = 128
RMS_EPS = 1e-6
NEG_INF = -1e30

MLA_HEADS = 6
MLA_LORA = 512
MLA_NOPE = 128
MLA_ROPE = 64
MLA_V = 128
MLA_QK_PAD = 256
ROPE_BASE = 10000.0

DIL_PAIRS = ((128, 1), (512, 4), (2048, 16))
DIL_HEADS = 6
DIL_BLOCK = 128

NSA_HEADS = 4
NSA_CMP_LEN = 32
NSA_CMP_STRIDE = 16
NSA_SEL_LEN = 64
NSA_TOPK = 16
NSA_WINDOW = 512
NSA_FORCED_SCORE = 100.0
NSA_Q_BLOCK = 256
POS_SPLIT = 128

N_ALIBI = DIL_HEADS + NSA_HEADS
ALIBI_SLOPES = tuple(float(2.0 ** (-8.0 * i / N_ALIBI)) for i in range(1, N_ALIBI + 1))

COL_NSA_Q = 1024
COL_KROPE = 1536
COL_NSA_K = 1792
H_COLS = 2048
DIL_GROUP_COLS = 768

assert NSA_FORCED_SCORE > 2 * NSA_HEADS


def _cparams(sem):
    return pltpu.CompilerParams(dimension_semantics=sem, vmem_limit_bytes=VMEM_LIMIT_BYTES)


def _rms(x, g):
    ms = jnp.mean(x * x, axis=-1, keepdims=True)
    return (x * lax.rsqrt(ms + RMS_EPS)) * g


def _dot(a, b):
    return jnp.dot(a, b, preferred_element_type=F32)


def _shr(x, pow2):
    return lax.shift_right_logical(x, int(pow2).bit_length() - 1)


def _dot_t(a, b):
    return lax.dot_general(a, b, (((1,), (1,)), ((), ())), preferred_element_type=F32)


def _resident(shape, index_map):
    return pl.BlockSpec(shape, index_map, pipeline_mode=pl.Buffered(1))


def _ffn_kernel(x_ref, g_ref, wg_ref, wu_ref, wo_ref, *rest, final):
    if final:
        gf_ref, o_ref, xn_ref = rest
    else:
        o_ref, xn_ref = rest
    j = pl.program_id(1)

    @pl.when(j == 0)
    def _():
        x = x_ref[...]
        xn_ref[...] = _rms(x, g_ref[...]).astype(xn_ref.dtype)
        o_ref[...] = x

    xn = xn_ref[...]
    half = wg_ref.shape[1] // 2
    acc = None
    for c in range(2):
        cs = slice(c * half, (c + 1) * half)
        gate = _dot(xn, wg_ref[:, cs])
        up = _dot(xn, wu_ref[:, cs])
        h = (0.5 * gate) * jax.nn.sigmoid(gate) * up
        part = _dot(h.astype(MXU_DTYPE), wo_ref[cs, :])
        acc = part if acc is None else acc + part
    o_ref[...] += acc

    if final:
        @pl.when(j == pl.num_programs(1) - 1)
        def _():
            o_ref[...] = _rms(o_ref[...], gf_ref[...])


def _ffn(x2, g, wg, wu, wo, gf=None, *, tm=1024, tf=FFN_TILE):
    n, d = x2.shape
    dffp = wo.shape[0]
    nff = dffp // tf
    tm = min(tm, n)
    final = gf is not None
    in_specs = [
        pl.BlockSpec((tm, d), lambda i, j: (i, 0)),
        pl.BlockSpec((1, d), lambda i, j: (0, 0)),
        pl.BlockSpec((d, tf), lambda i, j: (0, j)),
        pl.BlockSpec((d, tf), lambda i, j: (0, j)),
        pl.BlockSpec((tf, d), lambda i, j: (j, 0)),
    ]
    args = [x2, g.reshape(1, d), wg, wu, wo]
    if final:
        in_specs.append(pl.BlockSpec((1, d), lambda i, j: (0, 0)))
        args.append(gf.reshape(1, d))
    return pl.pallas_call(
        functools.partial(_ffn_kernel, final=final),
        grid=(n // tm, nff),
        in_specs=in_specs,
        out_specs=pl.BlockSpec((tm, d), lambda i, j: (i, 0)),
        out_shape=jax.ShapeDtypeStruct((n, d), F32),
        scratch_shapes=[pltpu.VMEM((tm, d), MXU_DTYPE)],
        compiler_params=_cparams(("parallel", "arbitrary")),
        name="ffn_final" if final else "ffn",
    )(*args)


def _prep_ffn(w_in, w_out, tf=FFN_TILE):
    dff = w_in.shape[1] // 2
    pad = -(-dff // tf) * tf - dff
    wg = jnp.pad(w_in[:, :dff].astype(MXU_DTYPE), ((0, 0), (0, pad)))
    wu = jnp.pad(w_in[:, dff:].astype(MXU_DTYPE), ((0, 0), (0, pad)))
    wo = jnp.pad(w_out.astype(MXU_DTYPE), ((0, pad), (0, 0)))
    return wg, wu, wo


def _mixin_body(x_ref, g_ref, wm_ref, wd_ref, wa_ref, wvt_ref, hm_ref, gate_ref, cmp_ref, vt_ref, dil_refs,
                relayout_scr):
    xn = _rms(x_ref[...], g_ref[...]).astype(MXU_DTYPE)
    hm_ref[...] = _dot(xn, wm_ref[...]).astype(hm_ref.dtype)
    aux = _dot(xn, wa_ref[...])
    gate_ref[...] = aux[:, 2 * LANES:]
    for c in range(2):
        relayout_scr[c] = aux[:, c * LANES:(c + 1) * LANES]
    for r in range(NSA_CMP_STRIDE):
        for c in range(2):
            lo = r * 2 * LANES + c * LANES
            cmp_ref[:, lo:lo + LANES] = relayout_scr[c, pl.ds(r, aux.shape[0] // NSA_CMP_STRIDE,
                                                               stride=NSA_CMP_STRIDE), :]
    vt_ref[...] = _dot_t(wvt_ref[...], xn).astype(vt_ref.dtype)
    hd = _dot(xn, wd_ref[...])
    tm = hd.shape[0]
    for g, out_ref in enumerate(dil_refs):
        dilation = DIL_PAIRS[g][1]
        cols = hd[:, g * DIL_GROUP_COLS:(g + 1) * DIL_GROUP_COLS]
        if dilation == 1:
            out_ref[...] = cols.astype(out_ref.dtype)
            continue
        for c in range(DIL_GROUP_COLS // LANES):
            relayout_scr[c] = cols[:, c * LANES:(c + 1) * LANES]
        for r in range(dilation):
            for c in range(DIL_GROUP_COLS // LANES):
                piece = relayout_scr[c, pl.ds(r, tm // dilation, stride=dilation), :]
                out_ref[:, r * DIL_GROUP_COLS + c * LANES:r * DIL_GROUP_COLS + (c + 1) * LANES] = (
                    piece.astype(out_ref.dtype))


def _mixin_kernel(x_ref, g_ref, wm_ref, wd_ref, wa_ref, wvt_ref, hm_ref, gate_ref, cmp_ref, vt_ref, d0_ref,
                  d1_ref, d2_ref, relayout_scr):
    _mixin_body(x_ref, g_ref, wm_ref, wd_ref, wa_ref, wvt_ref, hm_ref, gate_ref, cmp_ref, vt_ref,
                (d0_ref, d1_ref, d2_ref), relayout_scr)


def _mixin(x2, g, wm, wd, wa, wvt, *, tm=512):
    n, d = x2.shape
    weight = lambda a: _resident(a.shape, lambda i: (0, 0))
    dils = [dil for _, dil in DIL_PAIRS]
    chunk = NSA_CMP_STRIDE
    return pl.pallas_call(
        _mixin_kernel,
        grid=(n // tm,),
        in_specs=[
            pl.BlockSpec((tm, d), lambda i: (i, 0)),
            pl.BlockSpec((1, d), lambda i: (0, 0)),
            weight(wm), weight(wd), weight(wa), weight(wvt),
        ],
        out_specs=[
            pl.BlockSpec((tm, H_COLS), lambda i: (i, 0)),
            pl.BlockSpec((tm, LANES), lambda i: (i, 0)),
            pl.BlockSpec((tm // chunk, chunk * 2 * LANES), lambda i: (i, 0)),
            pl.BlockSpec((2 * LANES, tm), lambda i: (0, i)),
        ] + [pl.BlockSpec((tm // dil, dil * DIL_GROUP_COLS), lambda i: (i, 0)) for dil in dils],
        out_shape=[
            jax.ShapeDtypeStruct((n, H_COLS), MXU_DTYPE),
            jax.ShapeDtypeStruct((n, LANES), F32),
            jax.ShapeDtypeStruct((n // chunk, chunk * 2 * LANES), F32),
            jax.ShapeDtypeStruct((2 * LANES, n), MXU_DTYPE),
        ] + [jax.ShapeDtypeStruct((n // dil, dil * DIL_GROUP_COLS), MXU_DTYPE) for dil in dils],
        scratch_shapes=[pltpu.VMEM((DIL_GROUP_COLS // LANES, tm, LANES), F32)],
        compiler_params=_cparams(("parallel",)),
        name="mix_in",
    )(x2, g.reshape(1, d), wm, wd, wa, wvt)


def _rot_half_cols(w):
    half = w.shape[1] // 2
    return jnp.concatenate([-w[:, half:], w[:, :half]], axis=1)


def _prep_mixin(w):
    d = w.shape[0]
    w = w.astype(MXU_DTYPE)
    parts = {"q_lat": w[:, 0:512], "kv_lat": w[:, 512:1024], "k_rope": w[:, 1024:1088]}
    aligned = (("dq", 768), ("dk", 768), ("dv", 768), ("nq", 512), ("nkc", 128), ("nvc", 128),
               ("nks", 128), ("nvs", 128), ("nkw", 128), ("nvw", 128))
    start = 1088
    stop = start + sum(width for _, width in aligned)
    rest = w[:, start:stop]
    parts["ng"] = w[:, stop:]
    o = 0
    for name, width in aligned:
        parts[name] = rest[:, o:o + width]
        o += width
    z64 = jnp.zeros((d, 64), w.dtype)
    kr = parts["k_rope"]
    wm = jnp.concatenate([
        parts["q_lat"], parts["kv_lat"], parts["nq"],
        kr, z64, _rot_half_cols(kr), z64,
        parts["nks"], parts["nkw"]], axis=1).astype(MXU_DTYPE)
    gw = 2 * HEAD_DIM
    wd = jnp.concatenate([parts[name][:, g * gw:(g + 1) * gw] for g in range(len(DIL_PAIRS))
                          for name in ("dq", "dk", "dv")], axis=1).astype(MXU_DTYPE)
    wa = jnp.concatenate([parts["nkc"], parts["nvc"], parts["ng"],
                          jnp.zeros((d, LANES - 12), w.dtype)], axis=1).astype(MXU_DTYPE)
    wvt = jnp.concatenate([parts["nvs"], parts["nvw"]], axis=1).T.astype(MXU_DTYPE)
    return wm, wd, wa, wvt


def _rope_tables_kernel(invf_ref, cos_ref, sin_ref):
    tm = cos_ref.shape[0]
    pos = (pl.program_id(0) * tm + lax.broadcasted_iota(jnp.int32, (tm, 1), 0)).astype(F32)
    ang = pos * invf_ref[...]
    cos_ref[...] = jnp.cos(ang)
    sin_ref[...] = jnp.sin(ang)


def _rope_tables(seq, *, tm=512):
    tm = min(tm, seq)
    spec = pl.BlockSpec((tm, LANES), lambda i: (i, 0))
    sds = jax.ShapeDtypeStruct((seq, LANES), F32)
    return pl.pallas_call(
        _rope_tables_kernel,
        grid=(seq // tm,),
        in_specs=[pl.BlockSpec((1, LANES), lambda i: (0, 0))],
        out_specs=[spec, spec],
        out_shape=[sds, sds],
        compiler_params=_cparams(("parallel",)),
        name="rope_tables",
    )(_rope_inv_freq_row())


def _mla_proj_kernel(lat_ref, kr_ref, qn_ref, kvn_ref, wq_ref, wk_ref, wvt_ref, cos_ref, sin_ref,
                     q_ref, k_ref, vt_ref):
    cos = cos_ref[...]
    sin = sin_ref[...]
    scale = (MLA_NOPE + MLA_ROPE) ** -0.5 * LOG2E

    lat = lat_ref[...].astype(F32)
    qn = _rms(lat[:, :MLA_LORA], qn_ref[...]).astype(MXU_DTYPE)
    kvn = _rms(lat[:, MLA_LORA:], kvn_ref[...]).astype(MXU_DTYPE)
    qm = _dot(qn, wq_ref[...])
    rot0 = MLA_HEADS * MLA_QK_PAD
    for h in range(MLA_HEADS):
        c = h * MLA_QK_PAD
        nope = qm[:, c:c + LANES]
        pe = qm[:, c + LANES:c + 2 * LANES] * cos + qm[:, rot0 + h * LANES:rot0 + (h + 1) * LANES] * sin
        q_ref[:, c:c + LANES] = (nope * scale).astype(q_ref.dtype)
        q_ref[:, c + LANES:c + 2 * LANES] = (pe * scale).astype(q_ref.dtype)

    kr = kr_ref[...].astype(F32)
    kpe = (kr[:, :LANES] * cos + kr[:, LANES:] * sin).astype(k_ref.dtype)
    kn = _dot(kvn, wk_ref[...])
    for h in range(MLA_HEADS):
        c = h * MLA_QK_PAD
        k_ref[:, c:c + LANES] = kn[:, h * LANES:(h + 1) * LANES].astype(k_ref.dtype)
        k_ref[:, c + LANES:c + 2 * LANES] = kpe
    vt_ref[...] = _dot_t(wvt_ref[...], kvn).astype(vt_ref.dtype)


def _mla_proj(h2, qn, kvn, wq, wk, wvt, cos, sin, *, tm=512):
    n = h2.shape[0]
    tm = min(tm, cos.shape[0])
    tiles_per_seq = cos.shape[0] // tm
    qk_cols = MLA_HEADS * MLA_QK_PAD
    v_rows = MLA_HEADS * MLA_V
    full = lambda a: pl.BlockSpec(a.shape, lambda i: (0,) * a.ndim)
    table = pl.BlockSpec((tm, LANES), lambda i: (i % tiles_per_seq, 0))
    return pl.pallas_call(
        _mla_proj_kernel,
        grid=(n // tm,),
        in_specs=[
            pl.BlockSpec((tm, 2 * MLA_LORA), lambda i: (i, 0)),
            pl.BlockSpec((tm, 2 * LANES), lambda i: (i, COL_KROPE // (2 * LANES))),
            full(qn), full(kvn), full(wq), full(wk), full(wvt), table, table,
        ],
        out_specs=[
            pl.BlockSpec((tm, qk_cols), lambda i: (i, 0)),
            pl.BlockSpec((tm, qk_cols), lambda i: (i, 0)),
            pl.BlockSpec((v_rows, tm), lambda i: (0, i)),
        ],
        out_shape=[
            jax.ShapeDtypeStruct((n, qk_cols), MXU_DTYPE),
            jax.ShapeDtypeStruct((n, qk_cols), MXU_DTYPE),
            jax.ShapeDtypeStruct((v_rows, n), MXU_DTYPE),
        ],
        compiler_params=_cparams(("parallel",)),
        name="mla_proj",
    )(h2, h2, qn, kvn, wq, wk, wvt, cos, sin)


def _prep_mla(w_uq, w_ukv):
    r = MLA_LORA
    wq3 = w_uq.reshape(r, MLA_HEADS, MLA_NOPE + MLA_ROPE)
    nope, pe = wq3[..., :MLA_NOPE], wq3[..., MLA_NOPE:]
    z = jnp.zeros((r, MLA_HEADS, 64), w_uq.dtype)
    main = jnp.concatenate([nope, pe, z], axis=-1).reshape(r, MLA_HEADS * MLA_QK_PAD)
    half = MLA_ROPE // 2
    pe_rot = jnp.concatenate([-pe[..., half:], pe[..., :half]], axis=-1)
    rot = jnp.concatenate([pe_rot, z], axis=-1).reshape(r, MLA_HEADS * LANES)
    wq = jnp.concatenate([main, rot], axis=1).astype(MXU_DTYPE)
    wkv3 = w_ukv.reshape(r, MLA_HEADS, MLA_NOPE + MLA_V)
    wk = wkv3[..., :MLA_NOPE].reshape(r, MLA_HEADS * MLA_NOPE).astype(MXU_DTYPE)
    wvt = wkv3[..., MLA_NOPE:].reshape(r, MLA_HEADS * MLA_V).T.astype(MXU_DTYPE)
    return wq, wk, wvt


def _rope_inv_freq_row():
    half = MLA_ROPE // 2
    f = ROPE_BASE ** (-jnp.arange(half, dtype=F32) / half)
    return jnp.concatenate([f, f, jnp.zeros((LANES - MLA_ROPE,), F32)]).reshape(1, LANES)


def _with_ones_rows(v_t):
    return jnp.concatenate([v_t, jnp.ones((DEN_ROWS, v_t.shape[1]), v_t.dtype)], axis=0)


def _softmax_piece_t(scores, v_aug, m_ref, acc_ref, *, first):
    if first:
        s_t = scores()
        m_new = jnp.max(s_t, axis=0, keepdims=True)
        acc_ref[...] = _dot(v_aug, jnp.exp2(s_t - m_new).astype(MXU_DTYPE))
        m_ref[...] = m_new
        return

    m_old = m_ref[...]
    s_t = scores()
    pv = _dot(v_aug, jnp.exp2(s_t - m_old).astype(MXU_DTYPE))
    fits = jnp.max(jnp.max(s_t, axis=0, keepdims=True) - m_old) <= LAZY_MAX_SLACK

    @pl.when(fits)
    def _():
        acc_ref[...] += pv

    @pl.when(jnp.logical_not(fits))
    def _():
        s_again = scores()
        m_new = jnp.maximum(m_old, jnp.max(s_again, axis=0, keepdims=True))
        weights = jnp.exp2(s_again - m_new).astype(MXU_DTYPE)
        acc_ref[...] = jnp.exp2(m_old - m_new) * acc_ref[...] + _dot(v_aug, weights)
        m_ref[...] = m_new


def _softmax_result(acc_scr, dv):
    return acc_scr[0:dv, :] * (1.0 / acc_scr[dv:dv + 1, :])


def _flash_kernel(qt_ref, kt_ref, q_ref, k_ref, vt_ref, o_ref, m_scr, acc_scr, *, t, dsub):
    step = pl.program_id(2)
    qi = qt_ref[step]
    ki = kt_ref[step]

    def piece(k0, q0, size, diagonal):
        ks = slice(k0, k0 + size)
        qs = slice(q0, q0 + size)

        def scores():
            s_t = _dot_t(k_ref[0, ks, :], q_ref[0, qs, :])
            if diagonal and k0 + size - 1 > q0:
                kpos = k0 + lax.broadcasted_iota(jnp.int32, (size, 1), 0)
                qpos = q0 + lax.broadcasted_iota(jnp.int32, (1, size), 1)
                s_t = jnp.where(kpos <= qpos, s_t, NEG_INF)
            return s_t

        _softmax_piece_t(scores, _with_ones_rows(vt_ref[:, ks]), m_scr.at[:, qs], acc_scr.at[:, qs],
                         first=diagonal and k0 == q0)

    @pl.when(ki == qi)
    def _():
        starts = range(0, t, dsub)
        for p in starts:
            piece(p, p, dsub, True)
        for k0 in starts:
            for q0 in starts:
                if q0 > k0:
                    piece(k0, q0, dsub, True)

    @pl.when(ki < qi)
    def _():
        piece(0, 0, t, False)

    @pl.when(ki == 0)
    def _():
        o_ref[0] = _softmax_result(acc_scr, MLA_V).T.astype(o_ref.dtype)


def _mla_flash(q3, k3, vt, *, t=2048, dsub=1024):
    b, s, _ = q3.shape
    t = min(t, s)
    dsub = min(dsub, t)
    nq = s // t
    pairs = [(qi, ki) for qi in range(nq) for ki in range(qi, -1, -1)]
    qt = jnp.asarray(np.array([p[0] for p in pairs], np.int32))
    kt = jnp.asarray(np.array([p[1] for p in pairs], np.int32))
    grid_spec = pltpu.PrefetchScalarGridSpec(
        num_scalar_prefetch=2,
        grid=(b, MLA_HEADS, len(pairs)),
        in_specs=[
            pl.BlockSpec((1, t, MLA_QK_PAD), lambda bi, h, st, qt, kt: (bi, qt[st], h)),
            pl.BlockSpec((1, t, MLA_QK_PAD), lambda bi, h, st, qt, kt: (bi, kt[st], h)),
            pl.BlockSpec((MLA_V, t), lambda bi, h, st, qt, kt: (h, bi * nq + kt[st])),
        ],
        out_specs=pl.BlockSpec((1, t, MLA_V), lambda bi, h, st, qt, kt: (bi, qt[st], h)),
        scratch_shapes=[pltpu.VMEM((1, t), F32), pltpu.VMEM((MLA_V + DEN_ROWS, t), F32)],
    )
    return pl.pallas_call(
        functools.partial(_flash_kernel, t=t, dsub=dsub),
        grid_spec=grid_spec,
        out_shape=jax.ShapeDtypeStruct((b, s, MLA_HEADS * MLA_V), MXU_DTYPE),
        compiler_params=_cparams(("parallel", "parallel", "arbitrary")),
        name="mla_flash",
    )(qt, kt, q3, k3, vt)


def _dil_kernel(q_ref, kc_ref, kp_ref, vc_ref, vp_ref, o_ref, lse_ref, *, dilation, span, slopes, tl):
    n = pl.program_id(2)
    scale = HEAD_DIM ** -0.5
    blk = DIL_BLOCK
    a = lax.broadcasted_iota(jnp.int32, (blk, 1), 0)
    c = lax.broadcasted_iota(jnp.int32, (1, 2 * blk), 1)
    j = blk + a - c
    in_band = (j >= 0) & (j <= span)
    first_valid = in_band & ((c >= blk) | (n > 0))
    dist = (j * dilation).astype(F32)
    for hg in range(2):
        cols = slice(hg * LANES, (hg + 1) * LANES)
        bias = slopes[hg] * dist
        for sb in range(tl // blk):
            rows = slice(sb * blk, (sb + 1) * blk)
            q = (q_ref[0, rows, cols].astype(F32) * scale).astype(MXU_DTYPE)
            if sb == 0:
                kprev, vprev, valid = kp_ref[0, :, cols], vp_ref[0, :, cols], first_valid
            else:
                prev = slice((sb - 1) * blk, sb * blk)
                kprev, vprev, valid = kc_ref[0, prev, cols], vc_ref[0, prev, cols], in_band
            keys = jnp.concatenate([kprev, kc_ref[0, rows, cols]], axis=0)
            vals = jnp.concatenate([vprev, vc_ref[0, rows, cols]], axis=0)
            s = jnp.where(valid, _dot_t(q, keys) - bias, NEG_INF)
            m = jnp.max(s, axis=1, keepdims=True)
            e = jnp.where(valid, jnp.exp(s - m), 0.0)
            den = jnp.sum(e, axis=1, keepdims=True)
            o_ref[0, rows, cols] = _dot((e / den).astype(MXU_DTYPE), vals)
            lse_ref[0, rows, cols] = jnp.broadcast_to(m + jnp.log(den), (blk, LANES))


def _dilated_group(hg, g, b, *, tl=1024):
    window, dilation = DIL_PAIRS[g]
    span = window // dilation
    seq_l = hg.shape[0] // b
    tl = min(tl, seq_l)
    per_blk = tl // DIL_BLOCK
    w = 2 * LANES
    hv = hg.reshape(b, seq_l, dilation * DIL_GROUP_COLS)
    qc, kc, vc = 0, 1, 2
    stride = DIL_GROUP_COLS // w
    cur = lambda col: pl.BlockSpec((1, tl, w), lambda bi, r, n: (bi, n, r * stride + col))
    prev = lambda col: pl.BlockSpec(
        (1, DIL_BLOCK, w), lambda bi, r, n: (bi, jnp.maximum(n * per_blk - 1, 0), r * stride + col))
    out_spec = pl.BlockSpec((1, tl, w), lambda bi, r, n: (bi, n, r))
    out_sds = jax.ShapeDtypeStruct((b, seq_l, dilation * w), F32)
    o, lse = pl.pallas_call(
        functools.partial(_dil_kernel, dilation=dilation, span=span,
                          slopes=ALIBI_SLOPES[2 * g:2 * g + 2], tl=tl),
        grid=(b, dilation, seq_l // tl),
        in_specs=[cur(qc), cur(kc), prev(kc), cur(vc), prev(vc)],
        out_specs=[out_spec, out_spec],
        out_shape=[out_sds, out_sds],
        compiler_params=_cparams(("parallel", "parallel", "parallel")),
        name=f"dilated_g{g}",
    )(hv, hv, hv, hv, hv)
    return o.reshape(b * seq_l, dilation * w), lse.reshape(b * seq_l, dilation * w)


def _nsa_cmp_kernel(x_ref, pos_ref, wa_ref, wb_ref, w2k_ref, w2vt_ref, kc_ref, vct_ref):
    x = x_ref[0]
    xa = (x + pos_ref[0:1, :]).astype(MXU_DTYPE)
    xb = (x + pos_ref[1:2, :]).astype(MXU_DTYPE)
    first = _dot(xa, wa_ref[...])
    second = _dot(xb, wb_ref[...])
    second_next = pltpu.roll(second, x.shape[0] - 1, 0)
    pre = first + second_next
    hid = (pre * jax.nn.sigmoid(pre)).astype(MXU_DTYPE)
    kc_ref[0] = _dot(hid[:, :LANES], w2k_ref[...]).astype(kc_ref.dtype)
    vct_ref[0] = _dot_t(w2vt_ref[...], hid[:, LANES:]).astype(vct_ref.dtype)


def _nsa_compress(cmp_in, b, pos2, wa, wb, w2k, w2vt):
    nchunk = cmp_in.shape[0] // b
    xk = cmp_in.reshape(b, nchunk, cmp_in.shape[1])
    full = lambda a: pl.BlockSpec(a.shape, lambda bi: (0,) * a.ndim)
    return pl.pallas_call(
        _nsa_cmp_kernel,
        grid=(b,),
        in_specs=[pl.BlockSpec((1, nchunk, xk.shape[2]), lambda bi: (bi, 0, 0)),
                  full(pos2), full(wa), full(wb), full(w2k), full(w2vt)],
        out_specs=[pl.BlockSpec((1, nchunk, LANES), lambda bi: (bi, 0, 0)),
                   pl.BlockSpec((1, LANES, nchunk), lambda bi: (bi, 0, 0))],
        out_shape=[jax.ShapeDtypeStruct((b, nchunk, LANES), MXU_DTYPE),
                   jax.ShapeDtypeStruct((b, LANES, nchunk), MXU_DTYPE)],
        compiler_params=_cparams(("parallel",)),
        name="nsa_cmp",
    )(xk, pos2, wa, wb, w2k, w2vt)


def _prep_nsa_cmp(cmp_pos, phi_k1, phi_k2, phi_v1, phi_v2):
    half = NSA_CMP_LEN // 2
    pk = cmp_pos.reshape(2, half, HEAD_DIM)
    pos2 = jnp.concatenate([pk, pk], axis=-1).reshape(2, half * 2 * LANES)

    def halves(w1, is_v):
        w = w1.reshape(2, half, HEAD_DIM, HEAD_DIM)
        z = jnp.zeros_like(w)
        w = jnp.concatenate([z, w] if is_v else [w, z], axis=2)
        return w.reshape(2, half * 2 * LANES, HEAD_DIM)

    wk, wv = halves(phi_k1, False), halves(phi_v1, True)
    wa = jnp.concatenate([wk[0], wv[0]], axis=1).astype(MXU_DTYPE)
    wb = jnp.concatenate([wk[1], wv[1]], axis=1).astype(MXU_DTYPE)
    return pos2, wa, wb, phi_k2.astype(MXU_DTYPE), phi_v2.T.astype(MXU_DTYPE)


def _split3(x):
    hi = x.astype(MXU_DTYPE)
    r = x - hi.astype(F32)
    mid = r.astype(MXU_DTYPE)
    lo = (r - mid.astype(F32)).astype(MXU_DTYPE)
    return hi, mid, lo


def _pos_features(pos):
    hi = jnp.floor(pos / POS_SPLIT) * POS_SPLIT
    lo = pos - hi
    cols = jnp.stack([hi, hi, hi, lo, lo, lo], axis=1)
    return jnp.pad(cols, ((0, 0), (0, LANES - 6))).astype(MXU_DTYPE)


def _slope_features(tq):
    sig = jnp.asarray([s * LOG2E for s in ALIBI_SLOPES[DIL_HEADS:]], F32)
    pieces = jnp.stack(_split3(sig), axis=1)
    rows = jnp.concatenate([pieces, pieces], axis=1)
    rows = jnp.pad(rows, ((0, 0), (0, LANES - 6)))
    return jnp.repeat(rows, tq, axis=0)


def _masked_softmax_t(s_t, mask):
    s_t = jnp.where(mask, s_t, NEG_INF)
    m = jnp.max(s_t, axis=0, keepdims=True)
    e = jnp.exp2(s_t - m)
    den = jnp.sum(e, axis=0, keepdims=True)
    return e * jnp.where(m > 0.5 * NEG_INF, 1.0 / den, 0.0)


def _nsa_kernel(q_ref, k_ref, vt_ref, kc_ref, vct_ref, c2st_ref, pf_ref, cpf_ref, sf_ref, wband_ref, g_ref,
                o_ref, m_scr, acc_scr, bias_scr, touched_smem, ocmp_scr, score_scr,
                *, tq, tk, topk, cmp_chunk, sel_chunk):
    i = pl.program_id(1)
    t0 = i * tq
    nh = NSA_HEADS
    cols = nh * tq
    ncp = kc_ref.shape[1]
    ns = c2st_ref.shape[0]
    blocks_per_tile = tk // NSA_SEL_LEN

    q = q_ref[0]
    q4 = jnp.concatenate([q[:, h * LANES:(h + 1) * LANES] for h in range(nh)], axis=0)
    q4 = (q4.astype(F32) * (HEAD_DIM ** -0.5 * LOG2E)).astype(MXU_DTYPE)
    q4 = jnp.concatenate([q4, sf_ref[...]], axis=1)
    col = lax.broadcasted_iota(jnp.int32, (1, cols), 1)
    tpos = (t0 + (col & (tq - 1))).astype(F32)

    def cmp_part(rows):
        cidx = lax.broadcasted_iota(jnp.int32, (rows, 1), 0).astype(F32)
        c_end = cidx * NSA_CMP_STRIDE + (NSA_CMP_LEN - 1)
        kc = jnp.concatenate([kc_ref[0, 0:rows, :], cpf_ref[0:rows, :]], axis=1)
        p_cmp = _masked_softmax_t(_dot_t(kc, q4), c_end <= tpos)
        ocmp_scr[...] = _dot(vct_ref[0, :, 0:rows], p_cmp.astype(MXU_DTYPE))
        p_sum = p_cmp[:, 0:tq]
        for h in range(1, nh):
            p_sum = p_sum + p_cmp[:, h * tq:(h + 1) * tq]
        c2st = c2st_ref[:, 0:rows]
        if MXU_DTYPE == jnp.float32:
            score_scr[...] = _dot(c2st, p_sum)
        else:
            score_scr[...] = sum(_dot(c2st, piece) for piece in _split3(p_sum))

    n_variants = -(-ncp // cmp_chunk)
    n_ending = (t0 + tq) // NSA_CMP_STRIDE - 1
    need = jnp.clip((n_ending + cmp_chunk - 1) // cmp_chunk, 1, n_variants)
    for v in range(1, n_variants + 1):
        pl.when(need == v)(functools.partial(cmp_part, min(v * cmp_chunk, ncp)))
    o_cmp = ocmp_scr[...]

    t1 = t0 + lax.broadcasted_iota(jnp.int32, (1, tq), 1)
    cur = _shr(t1, NSA_SEL_LEN).astype(F32)
    max_forced = 3
    n_forced = 1.0 + jnp.where(cur >= 1.0, 1.0, 0.0) + jnp.where(cur >= 2.0, 1.0, 0.0)

    def select_blocks(rows, may_lack_forced):
        jj = lax.broadcasted_iota(jnp.int32, (rows, 1), 0).astype(F32)
        forced = (jj == 0.0) | (jj == cur) | (jj == cur - 1.0)
        score = jnp.where(jj > cur, -1.0, jnp.where(forced, -2.0, score_scr[0:rows, :]))
        bias = jnp.where(forced, 0.0, NEG_INF)

        def pick(score, bias, wanted):
            best = jnp.max(score, axis=0, keepdims=True)
            idx = jnp.min(jnp.where(score == best, jj, float(rows)), axis=0, keepdims=True)
            hit = jj == idx
            if wanted is not None:
                hit = hit & wanted
            return jnp.where(hit, -2.0, score), jnp.where(hit, 0.0, bias)

        for _ in range(topk - max_forced):
            score, bias = pick(score, bias, None)
        bias_scr[0:rows, :] = bias
        if rows < ns:
            bias_scr[rows:ns, :] = jnp.full((ns - rows, tq), NEG_INF, F32)
        if may_lack_forced:
            score_scr[0:rows, :] = score

            @pl.when(t0 < (max_forced - 1) * NSA_SEL_LEN)
            def _():
                score, bias = score_scr[0:rows, :], bias_scr[0:rows, :]
                for done in range(topk - max_forced, topk - 1):
                    score, bias = pick(score, bias, n_forced + done < float(topk))
                bias_scr[0:rows, :] = bias

    sel_variants = -(-ns // sel_chunk)
    n_started = (t0 + tq) // NSA_SEL_LEN
    need_sel = jnp.clip((n_started + sel_chunk - 1) // sel_chunk, 1, sel_variants)
    assert (max_forced - 1) * NSA_SEL_LEN + tq <= sel_chunk * NSA_SEL_LEN
    for v in range(1, sel_variants + 1):
        pl.when(need_sel == v)(functools.partial(select_blocks, min(v * sel_chunk, ns), v == 1))

    bias = bias_scr[...]
    block_any = jnp.broadcast_to(jnp.max(bias, axis=1, keepdims=True), (ns, LANES))
    tile_any = jnp.max(block_any.reshape(ns // blocks_per_tile, blocks_per_tile, LANES), axis=1)
    for kt in range(ns // blocks_per_tile):
        touched_smem[kt] = tile_any[kt, 0]

    def slc_update(kt, causal):
        k0 = pl.multiple_of(kt * tk, tk)
        b0 = pl.multiple_of(kt * blocks_per_tile, blocks_per_tile)

        def scores():
            keys = jnp.concatenate([k_ref[0, pl.ds(k0, tk), 0:LANES], pf_ref[pl.ds(k0, tk), :]], axis=1)
            s_t = _dot_t(keys, q4)
            rows = [jnp.broadcast_to(bias_scr[pl.ds(b0 + r, 1), :], (NSA_SEL_LEN, tq))
                    for r in range(blocks_per_tile)]
            sel_bias = jnp.concatenate(rows, axis=0)
            s_t = s_t + jnp.concatenate([sel_bias] * nh, axis=1)
            if causal:
                kpos = (k0 + lax.broadcasted_iota(jnp.int32, (tk, 1), 0)).astype(F32)
                s_t = jnp.where(kpos <= tpos, s_t, NEG_INF)
            return s_t

        _softmax_piece_t(scores, _with_ones_rows(vt_ref[0:LANES, pl.ds(k0, tk)]), m_scr, acc_scr,
                         first=causal)

    last = t0 // tk
    slc_update(last, True)

    def slc_tile(back, carry):
        kt = last - 1 - back
        pl.when(touched_smem[kt] == 0.0)(functools.partial(slc_update, kt, False))
        return carry

    lax.fori_loop(0, last, slc_tile, 0)
    o_slc = _softmax_result(acc_scr, HEAD_DIM)

    wlen = NSA_WINDOW + tq
    ws = pl.multiple_of(jnp.maximum(t0 - NSA_WINDOW, 0), tq)
    keys = jnp.concatenate([k_ref[0, pl.ds(ws, wlen), LANES:2 * LANES], pf_ref[pl.ds(ws, wlen), :]], axis=1)
    band = wband_ref[jnp.minimum(i, wband_ref.shape[0] - 1)]
    s_t = _dot_t(keys, q4) + jnp.concatenate([band] * nh, axis=1)
    weights = jnp.exp2(s_t - jnp.max(s_t, axis=0, keepdims=True)).astype(MXU_DTYPE)
    win = _dot(_with_ones_rows(vt_ref[LANES:2 * LANES, pl.ds(ws, wlen)]), weights)
    o_win = win[0:HEAD_DIM, :] * (1.0 / win[HEAD_DIM:HEAD_DIM + 1, :])

    gates = jax.nn.sigmoid(g_ref[0]).T
    for h in range(nh):
        c = slice(h * tq, (h + 1) * tq)
        o = (gates[3 * h:3 * h + 1, :] * o_cmp[:, c] + gates[3 * h + 1:3 * h + 2, :] * o_slc[:, c]
             + gates[3 * h + 2:3 * h + 3, :] * o_win[:, c])
        o_ref[0, :, h * LANES:(h + 1) * LANES] = o.T.astype(o_ref.dtype)


def _nsa(h3, gates3, vt, kc, vct, *, tq=NSA_Q_BLOCK, tk=512):
    b, s, _ = h3.shape
    ncp = kc.shape[1]
    ns = s // NSA_SEL_LEN
    tk = min(tk, s)
    w = NSA_HEADS * HEAD_DIM
    c2st = _cmp_to_sel_t(ncp, ns)
    pf = _pos_features(jnp.arange(s, dtype=F32))
    cpf = _pos_features(jnp.arange(ncp, dtype=F32) * NSA_CMP_STRIDE + 0.5 * (NSA_CMP_LEN - 1))
    sf = _slope_features(tq).astype(MXU_DTYPE)
    wband = _window_band_bias(tq)
    const = lambda a: _resident(a.shape, lambda bi, i: (0,) * a.ndim)
    return pl.pallas_call(
        functools.partial(_nsa_kernel, tq=tq, tk=tk, topk=min(NSA_TOPK, ns), cmp_chunk=min(128, ncp),
                          sel_chunk=min(64, ns)),
        grid=(b, s // tq),
        in_specs=[
            pl.BlockSpec((1, tq, w), lambda bi, i: (bi, i, COL_NSA_Q // w)),
            _resident((1, s, 2 * LANES), lambda bi, i: (bi, 0, COL_NSA_K // (2 * LANES))),
            _resident((2 * LANES, s), lambda bi, i: (0, bi)),
            _resident((1, ncp, LANES), lambda bi, i: (bi, 0, 0)),
            _resident((1, LANES, ncp), lambda bi, i: (bi, 0, 0)),
            const(c2st), const(pf), const(cpf), const(sf), const(wband),
            pl.BlockSpec((1, tq, LANES), lambda bi, i: (bi, i, 0)),
        ],
        out_specs=pl.BlockSpec((1, tq, w), lambda bi, i: (bi, i, 0)),
        out_shape=jax.ShapeDtypeStruct((b, s, w), MXU_DTYPE),
        scratch_shapes=[pltpu.VMEM((1, NSA_HEADS * tq), F32),
                        pltpu.VMEM((HEAD_DIM + DEN_ROWS, NSA_HEADS * tq), F32),
                        pltpu.VMEM((ns, tq), F32), pltpu.SMEM((s // tk,), F32),
                        pltpu.VMEM((HEAD_DIM, NSA_HEADS * tq), F32), pltpu.VMEM((ns, tq), F32)],
        compiler_params=_cparams(("parallel", "arbitrary")),
        name="nsa",
    )(h3, h3, vt, kc, vct, c2st, pf, cpf, sf, wband, gates3)


def _window_band_bias(tq):
    wlen = NSA_WINDOW + tq
    u = np.arange(tq)[None, :]
    w = np.arange(wlen)[:, None]
    offsets = list(range(0, NSA_WINDOW, tq)) + [NSA_WINDOW]
    tiles = []
    for off in offsets:
        dist = off + u - w
        tiles.append(np.where((dist >= 0) & (dist < NSA_WINDOW), 0.0, NEG_INF))
    return jnp.asarray(np.stack(tiles).astype(np.float32))


def _cmp_to_sel_t(ncp, ns):
    nc = ncp - 1
    c = np.arange(ncp)[None, :]
    j = np.arange(ns)[:, None]
    start = c * NSA_CMP_STRIDE
    m = (start < (j + 1) * NSA_SEL_LEN) & (start + NSA_CMP_LEN - 1 >= j * NSA_SEL_LEN) & (c < nc)
    return jnp.asarray(m.astype(np.float32)).astype(MXU_DTYPE)


def _token_rows(ref, dilation, scr):
    if dilation == 1:
        return ref[...]
    tm = ref.shape[0] * dilation
    for r in range(dilation):
        for c in range(2):
            lo = r * 2 * LANES + c * LANES
            scr[c, pl.ds(r, tm // dilation, stride=dilation), :] = ref[:, lo:lo + LANES]
    return jnp.concatenate([scr[0], scr[1]], axis=1)


def _mixout_kernel(x_ref, mla_ref, d0_ref, d1_ref, d2_ref, l0_ref, l1_ref, l2_ref, nsa_ref, w_ref, o_ref,
                   relayout_scr):
    dils = [dil for _, dil in DIL_PAIRS]
    o_dil = [_token_rows(r, dil, relayout_scr.at[2 * g:2 * g + 2])
             for g, (r, dil) in enumerate(zip((d0_ref, d1_ref, d2_ref), dils))]
    lse = [_token_rows(r, dil, relayout_scr.at[6 + 2 * g:8 + 2 * g])
           for g, (r, dil) in enumerate(zip((l0_ref, l1_ref, l2_ref), dils))]
    top = jnp.maximum(jnp.maximum(lse[0], lse[1]), lse[2])
    e = [jnp.exp(l - top) for l in lse]
    inv = 1.0 / (e[0] + e[1] + e[2])
    acc = x_ref[...] + _dot(mla_ref[...], w_ref[0:768, :])
    for g in range(3):
        mixed = (o_dil[g] * (e[g] * inv)).astype(MXU_DTYPE)
        acc += _dot(mixed, w_ref[768 + g * 256:768 + (g + 1) * 256, :])
    acc += _dot(nsa_ref[...], w_ref[1536:2048, :])
    o_ref[...] = acc


def _mixout(x2, o_mla, o_dil, lse_dil, o_nsa, w, *, tm=512):
    n, d = x2.shape
    row = lambda width: pl.BlockSpec((tm, width), lambda i: (i, 0))
    dil_specs = [pl.BlockSpec((tm // dil, dil * 2 * LANES), lambda i: (i, 0)) for _, dil in DIL_PAIRS]
    return pl.pallas_call(
        _mixout_kernel,
        grid=(n // tm,),
        in_specs=[row(d), row(o_mla.shape[1])] + dil_specs + dil_specs + [
            row(o_nsa.shape[1]), pl.BlockSpec(w.shape, lambda i: (0, 0))],
        out_specs=row(d),
        out_shape=jax.ShapeDtypeStruct((n, d), F32),
        scratch_shapes=[pltpu.VMEM((12, tm, LANES), F32)],
        compiler_params=_cparams(("parallel",)),
        name="mix_out",
    )(x2, o_mla, *o_dil, *lse_dil, o_nsa, w)


def kernel(x, ffn1_norm, ffn1_w_in, ffn1_w_out, mix_norm, w_mix_in, mla_q_norm, mla_w_uq, mla_kv_norm,
           mla_w_ukv, nsa_cmp_pos, nsa_phi_k1, nsa_phi_k2, nsa_phi_v1, nsa_phi_v2, w_mix_out, ffn2_norm,
           ffn2_w_in, ffn2_w_out, final_norm):
    b, s, d = x.shape
    depth = ffn1_w_in.shape[0]
    n = b * s
    cos, sin = _rope_tables(s)
    x2 = x.reshape(n, d)
    for l in range(depth):
        x2 = _ffn(x2, ffn1_norm[l], *_prep_ffn(ffn1_w_in[l], ffn1_w_out[l]))

        h2, gates2, cmp_in, nsa_vt, *dil_in = _mixin(x2, mix_norm[l], *_prep_mixin(w_mix_in[l]))
        h3 = h2.reshape(b, s, H_COLS)

        q, k, vt = _mla_proj(h2, mla_q_norm[l].reshape(1, -1), mla_kv_norm[l].reshape(1, -1),
                             *_prep_mla(mla_w_uq[l], mla_w_ukv[l]), cos, sin)
        o_mla = _mla_flash(q.reshape(b, s, -1), k.reshape(b, s, -1), vt)

        dil = [_dilated_group(hg, g, b) for g, hg in enumerate(dil_in)]

        kc, vct = _nsa_compress(cmp_in, b, *_prep_nsa_cmp(nsa_cmp_pos[l], nsa_phi_k1[l], nsa_phi_k2[l],
                                                          nsa_phi_v1[l], nsa_phi_v2[l]))
        o_nsa = _nsa(h3, gates2.reshape(b, s, LANES), nsa_vt, kc, vct)

        x2 = _mixout(x2, o_mla.reshape(n, -1), [o for o, _ in dil], [e for _, e in dil],
                     o_nsa.reshape(n, -1), w_mix_out[l].astype(MXU_DTYPE))

        gf = final_norm if l == depth - 1 else None
        x2 = _ffn(x2, ffn2_norm[l], *_prep_ffn(ffn2_w_in[l], ffn2_w_out[l]), gf)
    return x2.reshape(b, s, d)
```

```python
import functools
import math

import numpy as np
import jax
import jax.numpy as jnp
from jax import lax
from jax.experimental import pallas as pl
from jax.experimental.pallas import tpu as pltpu

F32 = jnp.float32
MXU_DTYPE = jnp.bfloat16
VMEM_LIMIT_BYTES = 56 * 1024 * 1024
LANES = 128
LOG2E = math.log2(math.e)
FFN_TILE = 512
DEN_ROWS = 16
LAZY_MAX_SLACK = 16.0

HEAD_DIM = 128
RMS_EPS = 1e-6
NEG_INF = -1e30

MLA_HEADS = 6
MLA_LORA = 512
MLA_NOPE = 128
MLA_ROPE = 64
MLA_V = 128
MLA_QK_PAD = 256
ROPE_BASE = 10000.0

DIL_PAIRS = ((128, 1), (512, 4), (2048, 16))
DIL_HEADS = 6
DIL_BLOCK = 128

NSA_HEADS = 4
NSA_CMP_LEN = 32
NSA_CMP_STRIDE = 16
NSA_SEL_LEN = 64
NSA_TOPK = 16
NSA_WINDOW = 512
NSA_FORCED_SCORE = 100.0
NSA_Q_BLOCK = 512
POS_SPLIT = 128

N_ALIBI = DIL_HEADS + NSA_HEADS
ALIBI_SLOPES = tuple(float(2.0 ** (-8.0 * i / N_ALIBI)) for i in range(1, N_ALIBI + 1))

COL_NSA_Q = 1024
COL_KROPE = 1536
COL_NSA_K = 1792
H_COLS = 2048
DIL_GROUP_COLS = 768

assert NSA_FORCED_SCORE > 2 * NSA_HEADS


def _cparams(sem):
    return pltpu.CompilerParams(dimension_semantics=sem, vmem_limit_bytes=VMEM_LIMIT_BYTES)


def _rms(x, g):
    ms = jnp.mean(x * x, axis=-1, keepdims=True)
    return (x * lax.rsqrt(ms + RMS_EPS)) * g


def _dot(a, b):
    return jnp.dot(a, b, preferred_element_type=F32)


def _shr(x, pow2):
    return lax.shift_right_logical(x, int(pow2).bit_length() - 1)


def _dot_t(a, b):
    return lax.dot_general(a, b, (((1,), (1,)), ((), ())), preferred_element_type=F32)


def _resident(shape, index_map):
    return pl.BlockSpec(shape, index_map, pipeline_mode=pl.Buffered(1))


def _ffn_kernel(x_ref, g_ref, wg_ref, wu_ref, wo_ref, *rest, final):
    if final:
        gf_ref, o_ref, xn_ref = rest
    else:
        o_ref, xn_ref = rest
    j = pl.program_id(1)

    @pl.when(j == 0)
    def _():
        x = x_ref[...]
        xn_ref[...] = _rms(x, g_ref[...]).astype(xn_ref.dtype)
        o_ref[...] = x

    xn = xn_ref[...]
    half = wg_ref.shape[1] // 2
    acc = None
    for c in range(2):
        cs = slice(c * half, (c + 1) * half)
        gate = _dot(xn, wg_ref[:, cs])
        up = _dot(xn, wu_ref[:, cs])
        h = (0.5 * gate) * jax.nn.sigmoid(gate) * up
        part = _dot(h.astype(MXU_DTYPE), wo_ref[cs, :])
        acc = part if acc is None else acc + part
    o_ref[...] += acc

    if final:
        @pl.when(j == pl.num_programs(1) - 1)
        def _():
            o_ref[...] = _rms(o_ref[...], gf_ref[...])


def _ffn(x2, g, wg, wu, wo, gf=None, *, tm=1024, tf=FFN_TILE):
    n, d = x2.shape
    dffp = wo.shape[0]
    nff = dffp // tf
    tm = min(tm, n)
    final = gf is not None
    in_specs = [
        pl.BlockSpec((tm, d), lambda i, j: (i, 0)),
        pl.BlockSpec((1, d), lambda i, j: (0, 0)),
        pl.BlockSpec((d, tf), lambda i, j: (0, j)),
        pl.BlockSpec((d, tf), lambda i, j: (0, j)),
        pl.BlockSpec((tf, d), lambda i, j: (j, 0)),
    ]
    args = [x2, g.reshape(1, d), wg, wu, wo]
    if final:
        in_specs.append(pl.BlockSpec((1, d), lambda i, j: (0, 0)))
        args.append(gf.reshape(1, d))
    return pl.pallas_call(
        functools.partial(_ffn_kernel, final=final),
        grid=(n // tm, nff),
        in_specs=in_specs,
        out_specs=pl.BlockSpec((tm, d), lambda i, j: (i, 0)),
        out_shape=jax.ShapeDtypeStruct((n, d), F32),
        scratch_shapes=[pltpu.VMEM((tm, d), MXU_DTYPE)],
        compiler_params=_cparams(("parallel", "arbitrary")),
        name="ffn_final" if final else "ffn",
    )(*args)


def _prep_ffn(w_in, w_out, tf=FFN_TILE):
    dff = w_in.shape[1] // 2
    pad = -(-dff // tf) * tf - dff
    wg = jnp.pad(w_in[:, :dff].astype(MXU_DTYPE), ((0, 0), (0, pad)))
    wu = jnp.pad(w_in[:, dff:].astype(MXU_DTYPE), ((0, 0), (0, pad)))
    wo = jnp.pad(w_out.astype(MXU_DTYPE), ((0, pad), (0, 0)))
    return wg, wu, wo


def _mixin_body(x_ref, g_ref, wm_ref, wd_ref, wa_ref, wvt_ref, hm_ref, gate_ref, cmp_ref, vt_ref, dil_refs,
                relayout_scr):
    xn = _rms(x_ref[...], g_ref[...]).astype(MXU_DTYPE)
    hm_ref[...] = _dot(xn, wm_ref[...]).astype(hm_ref.dtype)
    aux = _dot(xn, wa_ref[...])
    gate_ref[...] = aux[:, 2 * LANES:]
    for c in range(2):
        relayout_scr[c] = aux[:, c * LANES:(c + 1) * LANES]
    for r in range(NSA_CMP_STRIDE):
        for c in range(2):
            lo = r * 2 * LANES + c * LANES
            cmp_ref[:, lo:lo + LANES] = relayout_scr[c, pl.ds(r, aux.shape[0] // NSA_CMP_STRIDE,
                                                               stride=NSA_CMP_STRIDE), :]
    vt_ref[...] = _dot_t(wvt_ref[...], xn).astype(vt_ref.dtype)
    hd = _dot(xn, wd_ref[...])
    tm = hd.shape[0]
    for g, out_ref in enumerate(dil_refs):
        dilation = DIL_PAIRS[g][1]
        cols = hd[:, g * DIL_GROUP_COLS:(g + 1) * DIL_GROUP_COLS]
        if dilation == 1:
            out_ref[...] = cols.astype(out_ref.dtype)
            continue
        for c in range(DIL_GROUP_COLS // LANES):
            relayout_scr[c] = cols[:, c * LANES:(c + 1) * LANES]
        for r in range(dilation):
            for c in range(DIL_GROUP_COLS // LANES):
                piece = relayout_scr[c, pl.ds(r, tm // dilation, stride=dilation), :]
                out_ref[:, r * DIL_GROUP_COLS + c * LANES:r * DIL_GROUP_COLS + (c + 1) * LANES] = (
                    piece.astype(out_ref.dtype))


def _mixin_kernel(x_ref, g_ref, wm_ref, wd_ref, wa_ref, wvt_ref, hm_ref, gate_ref, cmp_ref, vt_ref, d0_ref,
                  d1_ref, d2_ref, relayout_scr):
    _mixin_body(x_ref, g_ref, wm_ref, wd_ref, wa_ref, wvt_ref, hm_ref, gate_ref, cmp_ref, vt_ref,
                (d0_ref, d1_ref, d2_ref), relayout_scr)


def _mixin(x2, g, wm, wd, wa, wvt, *, tm=512):
    n, d = x2.shape
    weight = lambda a: _resident(a.shape, lambda i: (0, 0))
    dils = [dil for _, dil in DIL_PAIRS]
    chunk = NSA_CMP_STRIDE
    return pl.pallas_call(
        _mixin_kernel,
        grid=(n // tm,),
        in_specs=[
            pl.BlockSpec((tm, d), lambda i: (i, 0)),
            pl.BlockSpec((1, d), lambda i: (0, 0)),
            weight(wm), weight(wd), weight(wa), weight(wvt),
        ],
        out_specs=[
            pl.BlockSpec((tm, H_COLS), lambda i: (i, 0)),
            pl.BlockSpec((tm, LANES), lambda i: (i, 0)),
            pl.BlockSpec((tm // chunk, chunk * 2 * LANES), lambda i: (i, 0)),
            pl.BlockSpec((2 * LANES, tm), lambda i: (0, i)),
        ] + [pl.BlockSpec((tm // dil, dil * DIL_GROUP_COLS), lambda i: (i, 0)) for dil in dils],
        out_shape=[
            jax.ShapeDtypeStruct((n, H_COLS), MXU_DTYPE),
            jax.ShapeDtypeStruct((n, LANES), F32),
            jax.ShapeDtypeStruct((n // chunk, chunk * 2 * LANES), F32),
            jax.ShapeDtypeStruct((2 * LANES, n), MXU_DTYPE),
        ] + [jax.ShapeDtypeStruct((n // dil, dil * DIL_GROUP_COLS), MXU_DTYPE) for dil in dils],
        scratch_shapes=[pltpu.VMEM((DIL_GROUP_COLS // LANES, tm, LANES), F32)],
        compiler_params=_cparams(("parallel",)),
        name="mix_in",
    )(x2, g.reshape(1, d), wm, wd, wa, wvt)


def _rot_half_cols(w):
    half = w.shape[1] // 2
    return jnp.concatenate([-w[:, half:], w[:, :half]], axis=1)


def _prep_mixin(w):
    d = w.shape[0]
    w = w.astype(MXU_DTYPE)
    parts = {"q_lat": w[:, 0:512], "kv_lat": w[:, 512:1024], "k_rope": w[:, 1024:1088]}
    aligned = (("dq", 768), ("dk", 768), ("dv", 768), ("nq", 512), ("nkc", 128), ("nvc", 128),
               ("nks", 128), ("nvs", 128), ("nkw", 128), ("nvw", 128))
    start = 1088
    stop = start + sum(width for _, width in aligned)
    rest = w[:, start:stop]
    parts["ng"] = w[:, stop:]
    o = 0
    for name, width in aligned:
        parts[name] = rest[:, o:o + width]
        o += width
    z64 = jnp.zeros((d, 64), w.dtype)
    kr = parts["k_rope"]
    wm = jnp.concatenate([
        parts["q_lat"], parts["kv_lat"], parts["nq"],
        kr, z64, _rot_half_cols(kr), z64,
        parts["nks"], parts["nkw"]], axis=1).astype(MXU_DTYPE)
    gw = 2 * HEAD_DIM
    wd = jnp.concatenate([parts[name][:, g * gw:(g + 1) * gw] for g in range(len(DIL_PAIRS))
                          for name in ("dq", "dk", "dv")], axis=1).astype(MXU_DTYPE)
    wa = jnp.concatenate([parts["nkc"], parts["nvc"], parts["ng"],
                          jnp.zeros((d, LANES - 12), w.dtype)], axis=1).astype(MXU_DTYPE)
    wvt = jnp.concatenate([parts["nvs"], parts["nvw"]], axis=1).T.astype(MXU_DTYPE)
    return wm, wd, wa, wvt


def _rope_tables_kernel(invf_ref, cos_ref, sin_ref):
    tm = cos_ref.shape[0]
    pos = (pl.program_id(0) * tm + lax.broadcasted_iota(jnp.int32, (tm, 1), 0)).astype(F32)
    ang = pos * invf_ref[...]
    cos_ref[...] = jnp.cos(ang)
    sin_ref[...] = jnp.sin(ang)


def _rope_tables(seq, *, tm=512):
    tm = min(tm, seq)
    spec = pl.BlockSpec((tm, LANES), lambda i: (i, 0))
    sds = jax.ShapeDtypeStruct((seq, LANES), F32)
    return pl.pallas_call(
        _rope_tables_kernel,
        grid=(seq // tm,),
        in_specs=[pl.BlockSpec((1, LANES), lambda i: (0, 0))],
        out_specs=[spec, spec],
        out_shape=[sds, sds],
        compiler_params=_cparams(("parallel",)),
        name="rope_tables",
    )(_rope_inv_freq_row())


def _mla_proj_kernel(lat_ref, kr_ref, qn_ref, kvn_ref, wq_ref, wk_ref, wvt_ref, cos_ref, sin_ref,
                     q_ref, k_ref, vt_ref):
    cos = cos_ref[...]
    sin = sin_ref[...]
    scale = (MLA_NOPE + MLA_ROPE) ** -0.5 * LOG2E

    lat = lat_ref[...].astype(F32)
    qn = _rms(lat[:, :MLA_LORA], qn_ref[...]).astype(MXU_DTYPE)
    kvn = _rms(lat[:, MLA_LORA:], kvn_ref[...]).astype(MXU_DTYPE)
    qm = _dot(qn, wq_ref[...])
    rot0 = MLA_HEADS * MLA_QK_PAD
    for h in range(MLA_HEADS):
        c = h * MLA_QK_PAD
        nope = qm[:, c:c + LANES]
        pe = qm[:, c + LANES:c + 2 * LANES] * cos + qm[:, rot0 + h * LANES:rot0 + (h + 1) * LANES] * sin
        q_ref[:, c:c + LANES] = (nope * scale).astype(q_ref.dtype)
        q_ref[:, c + LANES:c + 2 * LANES] = (pe * scale).astype(q_ref.dtype)

    kr = kr_ref[...].astype(F32)
    kpe = (kr[:, :LANES] * cos + kr[:, LANES:] * sin).astype(k_ref.dtype)
    kn = _dot(kvn, wk_ref[...])
    for h in range(MLA_HEADS):
        c = h * MLA_QK_PAD
        k_ref[:, c:c + LANES] = kn[:, h * LANES:(h + 1) * LANES].astype(k_ref.dtype)
        k_ref[:, c + LANES:c + 2 * LANES] = kpe
    vt_ref[...] = _dot_t(wvt_ref[...], kvn).astype(vt_ref.dtype)


def _mla_proj(h2, qn, kvn, wq, wk, wvt, cos, sin, *, tm=512):
    n = h2.shape[0]
    tm = min(tm, cos.shape[0])
    tiles_per_seq = cos.shape[0] // tm
    qk_cols = MLA_HEADS * MLA_QK_PAD
    v_rows = MLA_HEADS * MLA_V
    full = lambda a: pl.BlockSpec(a.shape, lambda i: (0,) * a.ndim)
    table = pl.BlockSpec((tm, LANES), lambda i: (i % tiles_per_seq, 0))
    return pl.pallas_call(
        _mla_proj_kernel,
        grid=(n // tm,),
        in_specs=[
            pl.BlockSpec((tm, 2 * MLA_LORA), lambda i: (i, 0)),
            pl.BlockSpec((tm, 2 * LANES), lambda i: (i, COL_KROPE // (2 * LANES))),
            full(qn), full(kvn), full(wq), full(wk), full(wvt), table, table,
        ],
        out_specs=[
            pl.BlockSpec((tm, qk_cols), lambda i: (i, 0)),
            pl.BlockSpec((tm, qk_cols), lambda i: (i, 0)),
            pl.BlockSpec((v_rows, tm), lambda i: (0, i)),
        ],
        out_shape=[
            jax.ShapeDtypeStruct((n, qk_cols), MXU_DTYPE),
            jax.ShapeDtypeStruct((n, qk_cols), MXU_DTYPE),
            jax.ShapeDtypeStruct((v_rows, n), MXU_DTYPE),
        ],
        compiler_params=_cparams(("parallel",)),
        name="mla_proj",
    )(h2, h2, qn, kvn, wq, wk, wvt, cos, sin)


def _prep_mla(w_uq, w_ukv):
    r = MLA_LORA
    wq3 = w_uq.reshape(r, MLA_HEADS, MLA_NOPE + MLA_ROPE)
    nope, pe = wq3[..., :MLA_NOPE], wq3[..., MLA_NOPE:]
    z = jnp.zeros((r, MLA_HEADS, 64), w_uq.dtype)
    main = jnp.concatenate([nope, pe, z], axis=-1).reshape(r, MLA_HEADS * MLA_QK_PAD)
    half = MLA_ROPE // 2
    pe_rot = jnp.concatenate([-pe[..., half:], pe[..., :half]], axis=-1)
    rot = jnp.concatenate([pe_rot, z], axis=-1).reshape(r, MLA_HEADS * LANES)
    wq = jnp.concatenate([main, rot], axis=1).astype(MXU_DTYPE)
    wkv3 = w_ukv.reshape(r, MLA_HEADS, MLA_NOPE + MLA_V)
    wk = wkv3[..., :MLA_NOPE].reshape(r, MLA_HEADS * MLA_NOPE).astype(MXU_DTYPE)
    wvt = wkv3[..., MLA_NOPE:].reshape(r, MLA_HEADS * MLA_V).T.astype(MXU_DTYPE)
    return wq, wk, wvt


def _rope_inv_freq_row():
    half = MLA_ROPE // 2
    f = ROPE_BASE ** (-jnp.arange(half, dtype=F32) / half)
    return jnp.concatenate([f, f, jnp.zeros((LANES - MLA_ROPE,), F32)]).reshape(1, LANES)


def _with_ones_rows(v_t):
    return jnp.concatenate([v_t, jnp.ones((DEN_ROWS, v_t.shape[1]), v_t.dtype)], axis=0)


def _softmax_piece_t(scores, v_aug, m_ref, acc_ref, *, first):
    if first:
        s_t = scores()
        m_new = jnp.max(s_t, axis=0, keepdims=True)
        acc_ref[...] = _dot(v_aug, jnp.exp2(s_t - m_new).astype(MXU_DTYPE))
        m_ref[...] = m_new
        return

    m_old = m_ref[...]
    s_t = scores()
    pv = _dot(v_aug, jnp.exp2(s_t - m_old).astype(MXU_DTYPE))
    fits = jnp.max(jnp.max(s_t, axis=0, keepdims=True) - m_old) <= LAZY_MAX_SLACK

    @pl.when(fits)
    def _():
        acc_ref[...] += pv

    @pl.when(jnp.logical_not(fits))
    def _():
        s_again = scores()
        m_new = jnp.maximum(m_old, jnp.max(s_again, axis=0, keepdims=True))
        weights = jnp.exp2(s_again - m_new).astype(MXU_DTYPE)
        acc_ref[...] = jnp.exp2(m_old - m_new) * acc_ref[...] + _dot(v_aug, weights)
        m_ref[...] = m_new


def _softmax_result(acc_scr, dv):
    return acc_scr[0:dv, :] * (1.0 / acc_scr[dv:dv + 1, :])


def _flash_kernel(qt_ref, kt_ref, q_ref, k_ref, vt_ref, o_ref, m_scr, acc_scr, *, t, dsub):
    step = pl.program_id(2)
    qi = qt_ref[step]
    ki = kt_ref[step]

    def piece(k0, q0, size, diagonal):
        ks = slice(k0, k0 + size)
        qs = slice(q0, q0 + size)

        def scores():
            s_t = _dot_t(k_ref[0, ks, :], q_ref[0, qs, :])
            if diagonal and k0 + size - 1 > q0:
                kpos = k0 + lax.broadcasted_iota(jnp.int32, (size, 1), 0)
                qpos = q0 + lax.broadcasted_iota(jnp.int32, (1, size), 1)
                s_t = jnp.where(kpos <= qpos, s_t, NEG_INF)
            return s_t

        _softmax_piece_t(scores, _with_ones_rows(vt_ref[:, ks]), m_scr.at[:, qs], acc_scr.at[:, qs],
                         first=diagonal and k0 == q0)

    @pl.when(ki == qi)
    def _():
        starts = range(0, t, dsub)
        for p in starts:
            piece(p, p, dsub, True)
        for k0 in starts:
            for q0 in starts:
                if q0 > k0:
                    piece(k0, q0, dsub, True)

    @pl.when(ki < qi)
    def _():
        piece(0, 0, t, False)

    @pl.when(ki == 0)
    def _():
        o_ref[0] = _softmax_result(acc_scr, MLA_V).T.astype(o_ref.dtype)


def _mla_flash(q3, k3, vt, *, t=2048, dsub=1024):
    b, s, _ = q3.shape
    t = min(t, s)
    dsub = min(dsub, t)
    nq = s // t
    pairs = [(qi, ki) for qi in range(nq) for ki in range(qi, -1, -1)]
    qt = jnp.asarray(np.array([p[0] for p in pairs], np.int32))
    kt = jnp.asarray(np.array([p[1] for p in pairs], np.int32))
    grid_spec = pltpu.PrefetchScalarGridSpec(
        num_scalar_prefetch=2,
        grid=(b, MLA_HEADS, len(pairs)),
        in_specs=[
            pl.BlockSpec((1, t, MLA_QK_PAD), lambda bi, h, st, qt, kt: (bi, qt[st], h)),
            pl.BlockSpec((1, t, MLA_QK_PAD), lambda bi, h, st, qt, kt: (bi, kt[st], h)),
            pl.BlockSpec((MLA_V, t), lambda bi, h, st, qt, kt: (h, bi * nq + kt[st])),
        ],
        out_specs=pl.BlockSpec((1, t, MLA_V), lambda bi, h, st, qt, kt: (bi, qt[st], h)),
        scratch_shapes=[pltpu.VMEM((1, t), F32), pltpu.VMEM((MLA_V + DEN_ROWS, t), F32)],
    )
    return pl.pallas_call(
        functools.partial(_flash_kernel, t=t, dsub=dsub),
        grid_spec=grid_spec,
        out_shape=jax.ShapeDtypeStruct((b, s, MLA_HEADS * MLA_V), MXU_DTYPE),
        compiler_params=_cparams(("parallel", "parallel", "arbitrary")),
        name="mla_flash",
    )(qt, kt, q3, k3, vt)


def _dil_kernel(q_ref, kc_ref, kp_ref, vc_ref, vp_ref, o_ref, lse_ref, *, dilation, span, slopes, tl):
    n = pl.program_id(2)
    scale = HEAD_DIM ** -0.5
    blk = DIL_BLOCK
    a = lax.broadcasted_iota(jnp.int32, (blk, 1), 0)
    c = lax.broadcasted_iota(jnp.int32, (1, 2 * blk), 1)
    j = blk + a - c
    in_band = (j >= 0) & (j <= span)
    first_valid = in_band & ((c >= blk) | (n > 0))
    dist = (j * dilation).astype(F32)
    for hg in range(2):
        cols = slice(hg * LANES, (hg + 1) * LANES)
        bias = slopes[hg] * dist
        for sb in range(tl // blk):
            rows = slice(sb * blk, (sb + 1) * blk)
            q = (q_ref[0, rows, cols].astype(F32) * scale).astype(MXU_DTYPE)
            if sb == 0:
                kprev, vprev, valid = kp_ref[0, :, cols], vp_ref[0, :, cols], first_valid
            else:
                prev = slice((sb - 1) * blk, sb * blk)
                kprev, vprev, valid = kc_ref[0, prev, cols], vc_ref[0, prev, cols], in_band
            keys = jnp.concatenate([kprev, kc_ref[0, rows, cols]], axis=0)
            vals = jnp.concatenate([vprev, vc_ref[0, rows, cols]], axis=0)
            s = jnp.where(valid, _dot_t(q, keys) - bias, NEG_INF)
            m = jnp.max(s, axis=1, keepdims=True)
            e = jnp.where(valid, jnp.exp(s - m), 0.0)
            den = jnp.sum(e, axis=1, keepdims=True)
            o_ref[0, rows, cols] = _dot((e / den).astype(MXU_DTYPE), vals)
            lse_ref[0, rows, cols] = jnp.broadcast_to(m + jnp.log(den), (blk, LANES))


def _dilated_group(hg, g, b, *, tl=1024):
    window, dilation = DIL_PAIRS[g]
    span = window // dilation
    seq_l = hg.shape[0] // b
    tl = min(tl, seq_l)
    per_blk = tl // DIL_BLOCK
    w = 2 * LANES
    hv = hg.reshape(b, seq_l, dilation * DIL_GROUP_COLS)
    qc, kc, vc = 0, 1, 2
    stride = DIL_GROUP_COLS // w
    cur = lambda col: pl.BlockSpec((1, tl, w), lambda bi, r, n: (bi, n, r * stride + col))
    prev = lambda col: pl.BlockSpec(
        (1, DIL_BLOCK, w), lambda bi, r, n: (bi, jnp.maximum(n * per_blk - 1, 0), r * stride + col))
    out_spec = pl.BlockSpec((1, tl, w), lambda bi, r, n: (bi, n, r))
    out_sds = jax.ShapeDtypeStruct((b, seq_l, dilation * w), F32)
    o, lse = pl.pallas_call(
        functools.partial(_dil_kernel, dilation=dilation, span=span,
                          slopes=ALIBI_SLOPES[2 * g:2 * g + 2], tl=tl),
        grid=(b, dilation, seq_l // tl),
        in_specs=[cur(qc), cur(kc), prev(kc), cur(vc), prev(vc)],
        out_specs=[out_spec, out_spec],
        out_shape=[out_sds, out_sds],
        compiler_params=_cparams(("parallel", "parallel", "parallel")),
        name=f"dilated_g{g}",
    )(hv, hv, hv, hv, hv)
    return o.reshape(b * seq_l, dilation * w), lse.reshape(b * seq_l, dilation * w)


def _nsa_cmp_kernel(x_ref, pos_ref, wa_ref, wb_ref, w2k_ref, w2vt_ref, kc_ref, vct_ref):
    x = x_ref[0]
    xa = (x + pos_ref[0:1, :]).astype(MXU_DTYPE)
    xb = (x + pos_ref[1:2, :]).astype(MXU_DTYPE)
    first = _dot(xa, wa_ref[...])
    second = _dot(xb, wb_ref[...])
    second_next = pltpu.roll(second, x.shape[0] - 1, 0)
    pre = first + second_next
    hid = (pre * jax.nn.sigmoid(pre)).astype(MXU_DTYPE)
    kc_ref[0] = _dot(hid[:, :LANES], w2k_ref[...]).astype(kc_ref.dtype)
    vct_ref[0] = _dot_t(w2vt_ref[...], hid[:, LANES:]).astype(vct_ref.dtype)


def _nsa_compress(cmp_in, b, pos2, wa, wb, w2k, w2vt):
    nchunk = cmp_in.shape[0] // b
    xk = cmp_in.reshape(b, nchunk, cmp_in.shape[1])
    full = lambda a: pl.BlockSpec(a.shape, lambda bi: (0,) * a.ndim)
    return pl.pallas_call(
        _nsa_cmp_kernel,
        grid=(b,),
        in_specs=[pl.BlockSpec((1, nchunk, xk.shape[2]), lambda bi: (bi, 0, 0)),
                  full(pos2), full(wa), full(wb), full(w2k), full(w2vt)],
        out_specs=[pl.BlockSpec((1, nchunk, LANES), lambda bi: (bi, 0, 0)),
                   pl.BlockSpec((1, LANES, nchunk), lambda bi: (bi, 0, 0))],
        out_shape=[jax.ShapeDtypeStruct((b, nchunk, LANES), MXU_DTYPE),
                   jax.ShapeDtypeStruct((b, LANES, nchunk), MXU_DTYPE)],
        compiler_params=_cparams(("parallel",)),
        name="nsa_cmp",
    )(xk, pos2, wa, wb, w2k, w2vt)


def _prep_nsa_cmp(cmp_pos, phi_k1, phi_k2, phi_v1, phi_v2):
    half = NSA_CMP_LEN // 2
    pk = cmp_pos.reshape(2, half, HEAD_DIM)
    pos2 = jnp.concatenate([pk, pk], axis=-1).reshape(2, half * 2 * LANES)

    def halves(w1, is_v):
        w = w1.reshape(2, half, HEAD_DIM, HEAD_DIM)
        z = jnp.zeros_like(w)
        w = jnp.concatenate([z, w] if is_v else [w, z], axis=2)
        return w.reshape(2, half * 2 * LANES, HEAD_DIM)

    wk, wv = halves(phi_k1, False), halves(phi_v1, True)
    wa = jnp.concatenate([wk[0], wv[0]], axis=1).astype(MXU_DTYPE)
    wb = jnp.concatenate([wk[1], wv[1]], axis=1).astype(MXU_DTYPE)
    return pos2, wa, wb, phi_k2.astype(MXU_DTYPE), phi_v2.T.astype(MXU_DTYPE)


def _split3(x):
    hi = x.astype(MXU_DTYPE)
    r = x - hi.astype(F32)
    mid = r.astype(MXU_DTYPE)
    lo = (r - mid.astype(F32)).astype(MXU_DTYPE)
    return hi, mid, lo


def _pos_features(pos):
    hi = jnp.floor(pos / POS_SPLIT) * POS_SPLIT
    lo = pos - hi
    cols = jnp.stack([hi, hi, hi, lo, lo, lo], axis=1)
    return jnp.pad(cols, ((0, 0), (0, LANES - 6))).astype(MXU_DTYPE)


def _slope_features(tq):
    sig = jnp.asarray([s * LOG2E for s in ALIBI_SLOPES[DIL_HEADS:]], F32)
    pieces = jnp.stack(_split3(sig), axis=1)
    rows = jnp.concatenate([pieces, pieces], axis=1)
    rows = jnp.pad(rows, ((0, 0), (0, LANES - 6)))
    return jnp.repeat(rows, tq, axis=0)


def _masked_softmax_t(s_t, mask):
    s_t = jnp.where(mask, s_t, NEG_INF)
    m = jnp.max(s_t, axis=0, keepdims=True)
    e = jnp.exp2(s_t - m)
    den = jnp.sum(e, axis=0, keepdims=True)
    return e * jnp.where(m > 0.5 * NEG_INF, 1.0 / den, 0.0)


def _nsa_kernel(q_ref, k_ref, vt_ref, kc_ref, vct_ref, c2st_ref, pf_ref, cpf_ref, sf_ref, wband_ref, g_ref,
                o_ref, m_scr, acc_scr, bias_scr, touched_smem, ocmp_scr, score_scr,
                *, tq, tk, topk, cmp_chunk, sel_chunk):
    i = pl.program_id(1)
    t0 = i * tq
    nh = NSA_HEADS
    cols = nh * tq
    ncp = kc_ref.shape[1]
    ns = c2st_ref.shape[0]
    blocks_per_tile = tk // NSA_SEL_LEN

    q = q_ref[0]
    q4 = jnp.concatenate([q[:, h * LANES:(h + 1) * LANES] for h in range(nh)], axis=0)
    q4 = (q4.astype(F32) * (HEAD_DIM ** -0.5 * LOG2E)).astype(MXU_DTYPE)
    q4 = jnp.concatenate([q4, sf_ref[...]], axis=1)
    col = lax.broadcasted_iota(jnp.int32, (1, cols), 1)
    tpos = (t0 + (col & (tq - 1))).astype(F32)

    def cmp_part(rows):
        cidx = lax.broadcasted_iota(jnp.int32, (rows, 1), 0).astype(F32)
        c_end = cidx * NSA_CMP_STRIDE + (NSA_CMP_LEN - 1)
        kc = jnp.concatenate([kc_ref[0, 0:rows, :], cpf_ref[0:rows, :]], axis=1)
        p_cmp = _masked_softmax_t(_dot_t(kc, q4), c_end <= tpos)
        ocmp_scr[...] = _dot(vct_ref[0, :, 0:rows], p_cmp.astype(MXU_DTYPE))
        p_sum = p_cmp[:, 0:tq]
        for h in range(1, nh):
            p_sum = p_sum + p_cmp[:, h * tq:(h + 1) * tq]
        c2st = c2st_ref[:, 0:rows]
        if MXU_DTYPE == jnp.float32:
            score_scr[...] = _dot(c2st, p_sum)
        else:
            score_scr[...] = sum(_dot(c2st, piece) for piece in _split3(p_sum))

    n_variants = -(-ncp // cmp_chunk)
    n_ending = (t0 + tq) // NSA_CMP_STRIDE - 1
    need = jnp.clip((n_ending + cmp_chunk - 1) // cmp_chunk, 1, n_variants)
    for v in range(1, n_variants + 1):
        pl.when(need == v)(functools.partial(cmp_part, min(v * cmp_chunk, ncp)))
    o_cmp = ocmp_scr[...]

    t1 = t0 + lax.broadcasted_iota(jnp.int32, (1, tq), 1)
    cur = _shr(t1, NSA_SEL_LEN).astype(F32)
    max_forced = 3
    n_forced = 1.0 + jnp.where(cur >= 1.0, 1.0, 0.0) + jnp.where(cur >= 2.0, 1.0, 0.0)

    def select_blocks(rows, may_lack_forced):
        jj = lax.broadcasted_iota(jnp.int32, (rows, 1), 0).astype(F32)
        forced = (jj == 0.0) | (jj == cur) | (jj == cur - 1.0)
        score = jnp.where(jj > cur, -1.0, jnp.where(forced, -2.0, score_scr[0:rows, :]))
        bias = jnp.where(forced, 0.0, NEG_INF)

        def pick(score, bias, wanted):
            best = jnp.max(score, axis=0, keepdims=True)
            idx = jnp.min(jnp.where(score == best, jj, float(rows)), axis=0, keepdims=True)
            hit = jj == idx
            if wanted is not None:
                hit = hit & wanted
            return jnp.where(hit, -2.0, score), jnp.where(hit, 0.0, bias)

        for _ in range(topk - max_forced):
            score, bias = pick(score, bias, None)
        bias_scr[0:rows, :] = bias
        if rows < ns:
            bias_scr[rows:ns, :] = jnp.full((ns - rows, tq), NEG_INF, F32)
        if may_lack_forced:
            score_scr[0:rows, :] = score

            @pl.when(t0 < (max_forced - 1) * NSA_SEL_LEN)
            def _():
                score, bias = score_scr[0:rows, :], bias_scr[0:rows, :]
                for done in range(topk - max_forced, topk - 1):
                    score, bias = pick(score, bias, n_forced + done < float(topk))
                bias_scr[0:rows, :] = bias

    sel_variants = -(-ns // sel_chunk)
    n_started = (t0 + tq) // NSA_SEL_LEN
    need_sel = jnp.clip((n_started + sel_chunk - 1) // sel_chunk, 1, sel_variants)
    assert (max_forced - 1) * NSA_SEL_LEN + tq <= sel_chunk * NSA_SEL_LEN
    for v in range(1, sel_variants + 1):
        pl.when(need_sel == v)(functools.partial(select_blocks, min(v * sel_chunk, ns), v == 1))

    bias = bias_scr[...]
    block_any = jnp.broadcast_to(jnp.max(bias, axis=1, keepdims=True), (ns, LANES))
    tile_any = jnp.max(block_any.reshape(ns // blocks_per_tile, blocks_per_tile, LANES), axis=1)
    for kt in range(ns // blocks_per_tile):
        touched_smem[kt] = tile_any[kt, 0]

    def slc_update(kt, causal):
        k0 = pl.multiple_of(kt * tk, tk)
        b0 = pl.multiple_of(kt * blocks_per_tile, blocks_per_tile)

        def scores():
            keys = jnp.concatenate([k_ref[0, pl.ds(k0, tk), 0:LANES], pf_ref[pl.ds(k0, tk), :]], axis=1)
            s_t = _dot_t(keys, q4)
            rows = [jnp.broadcast_to(bias_scr[pl.ds(b0 + r, 1), :], (NSA_SEL_LEN, tq))
                    for r in range(blocks_per_tile)]
            sel_bias = jnp.concatenate(rows, axis=0)
            s_t = s_t + jnp.concatenate([sel_bias] * nh, axis=1)
            if causal:
                kpos = (k0 + lax.broadcasted_iota(jnp.int32, (tk, 1), 0)).astype(F32)
                s_t = jnp.where(kpos <= tpos, s_t, NEG_INF)
            return s_t

        _softmax_piece_t(scores, _with_ones_rows(vt_ref[0:LANES, pl.ds(k0, tk)]), m_scr, acc_scr,
                         first=causal)

    last = t0 // tk
    slc_update(last, True)

    def slc_tile(back, carry):
        kt = last - 1 - back
        pl.when(touched_smem[kt] == 0.0)(functools.partial(slc_update, kt, False))
        return carry

    lax.fori_loop(0, last, slc_tile, 0)
    o_slc = _softmax_result(acc_scr, HEAD_DIM)

    wlen = NSA_WINDOW + tq
    ws = pl.multiple_of(jnp.maximum(t0 - NSA_WINDOW, 0), tq)
    keys = jnp.concatenate([k_ref[0, pl.ds(ws, wlen), LANES:2 * LANES], pf_ref[pl.ds(ws, wlen), :]], axis=1)
    band = wband_ref[jnp.minimum(i, wband_ref.shape[0] - 1)]
    s_t = _dot_t(keys, q4) + jnp.concatenate([band] * nh, axis=1)
    weights = jnp.exp2(s_t - jnp.max(s_t, axis=0, keepdims=True)).astype(MXU_DTYPE)
    win = _dot(_with_ones_rows(vt_ref[LANES:2 * LANES, pl.ds(ws, wlen)]), weights)
    o_win = win[0:HEAD_DIM, :] * (1.0 / win[HEAD_DIM:HEAD_DIM + 1, :])

    gates = jax.nn.sigmoid(g_ref[0]).T
    for h in range(nh):
        c = slice(h * tq, (h + 1) * tq)
        o = (gates[3 * h:3 * h + 1, :] * o_cmp[:, c] + gates[3 * h + 1:3 * h + 2, :] * o_slc[:, c]
             + gates[3 * h + 2:3 * h + 3, :] * o_win[:, c])
        o_ref[0, :, h * LANES:(h + 1) * LANES] = o.T.astype(o_ref.dtype)


def _nsa(h3, gates3, vt, kc, vct, *, tq=NSA_Q_BLOCK, tk=512):
    b, s, _ = h3.shape
    ncp = kc.shape[1]
    ns = s // NSA_SEL_LEN
    tk = min(tk, s)
    w = NSA_HEADS * HEAD_DIM
    c2st = _cmp_to_sel_t(ncp, ns)
    pf = _pos_features(jnp.arange(s, dtype=F32))
    cpf = _pos_features(jnp.arange(ncp, dtype=F32) * NSA_CMP_STRIDE + 0.5 * (NSA_CMP_LEN - 1))
    sf = _slope_features(tq).astype(MXU_DTYPE)
    wband = _window_band_bias(tq)
    const = lambda a: _resident(a.shape, lambda bi, i: (0,) * a.ndim)
    return pl.pallas_call(
        functools.partial(_nsa_kernel, tq=tq, tk=tk, topk=min(NSA_TOPK, ns), cmp_chunk=min(128, ncp),
                          sel_chunk=min(64, ns)),
        grid=(b, s // tq),
        in_specs=[
            pl.BlockSpec((1, tq, w), lambda bi, i: (bi, i, COL_NSA_Q // w)),
            _resident((1, s, 2 * LANES), lambda bi, i: (bi, 0, COL_NSA_K // (2 * LANES))),
            _resident((2 * LANES, s), lambda bi, i: (0, bi)),
            _resident((1, ncp, LANES), lambda bi, i: (bi, 0, 0)),
            _resident((1, LANES, ncp), lambda bi, i: (bi, 0, 0)),
            const(c2st), const(pf), const(cpf), const(sf), const(wband),
            pl.BlockSpec((1, tq, LANES), lambda bi, i: (bi, i, 0)),
        ],
        out_specs=pl.BlockSpec((1, tq, w), lambda bi, i: (bi, i, 0)),
        out_shape=jax.ShapeDtypeStruct((b, s, w), MXU_DTYPE),
        scratch_shapes=[pltpu.VMEM((1, NSA_HEADS * tq), F32),
                        pltpu.VMEM((HEAD_DIM + DEN_ROWS, NSA_HEADS * tq), F32),
                        pltpu.VMEM((ns, tq), F32), pltpu.SMEM((s // tk,), F32),
                        pltpu.VMEM((HEAD_DIM, NSA_HEADS * tq), F32), pltpu.VMEM((ns, tq), F32)],
        compiler_params=_cparams(("parallel", "arbitrary")),
        name="nsa",
    )(h3, h3, vt, kc, vct, c2st, pf, cpf, sf, wband, gates3)


def _window_band_bias(tq):
    wlen = NSA_WINDOW + tq
    u = np.arange(tq)[None, :]
    w = np.arange(wlen)[:, None]
    offsets = list(range(0, NSA_WINDOW, tq)) + [NSA_WINDOW]
    tiles = []
    for off in offsets:
        dist = off + u - w
        tiles.append(np.where((dist >= 0) & (dist < NSA_WINDOW), 0.0, NEG_INF))
    return jnp.asarray(np.stack(tiles).astype(np.float32))


def _cmp_to_sel_t(ncp, ns):
    nc = ncp - 1
    c = np.arange(ncp)[None, :]
    j = np.arange(ns)[:, None]
    start = c * NSA_CMP_STRIDE
    m = (start < (j + 1) * NSA_SEL_LEN) & (start + NSA_CMP_LEN - 1 >= j * NSA_SEL_LEN) & (c < nc)
    return jnp.asarray(m.astype(np.float32)).astype(MXU_DTYPE)


def _token_rows(ref, dilation, scr):
    if dilation == 1:
        return ref[...]
    tm = ref.shape[0] * dilation
    for r in range(dilation):
        for c in range(2):
            lo = r * 2 * LANES + c * LANES
            scr[c, pl.ds(r, tm // dilation, stride=dilation), :] = ref[:, lo:lo + LANES]
    return jnp.concatenate([scr[0], scr[1]], axis=1)


def _mixout_kernel(x_ref, mla_ref, d0_ref, d1_ref, d2_ref, l0_ref, l1_ref, l2_ref, nsa_ref, w_ref, o_ref,
                   relayout_scr):
    dils = [dil for _, dil in DIL_PAIRS]
    o_dil = [_token_rows(r, dil, relayout_scr.at[2 * g:2 * g + 2])
             for g, (r, dil) in enumerate(zip((d0_ref, d1_ref, d2_ref), dils))]
    lse = [_token_rows(r, dil, relayout_scr.at[6 + 2 * g:8 + 2 * g])
           for g, (r, dil) in enumerate(zip((l0_ref, l1_ref, l2_ref), dils))]
    top = jnp.maximum(jnp.maximum(lse[0], lse[1]), lse[2])
    e = [jnp.exp(l - top) for l in lse]
    inv = 1.0 / (e[0] + e[1] + e[2])
    acc = x_ref[...] + _dot(mla_ref[...], w_ref[0:768, :])
    for g in range(3):
        mixed = (o_dil[g] * (e[g] * inv)).astype(MXU_DTYPE)
        acc += _dot(mixed, w_ref[768 + g * 256:768 + (g + 1) * 256, :])
    acc += _dot(nsa_ref[...], w_ref[1536:2048, :])
    o_ref[...] = acc


def _mixout(x2, o_mla, o_dil, lse_dil, o_nsa, w, *, tm=512):
    n, d = x2.shape
    row = lambda width: pl.BlockSpec((tm, width), lambda i: (i, 0))
    dil_specs = [pl.BlockSpec((tm // dil, dil * 2 * LANES), lambda i: (i, 0)) for _, dil in DIL_PAIRS]
    return pl.pallas_call(
        _mixout_kernel,
        grid=(n // tm,),
        in_specs=[row(d), row(o_mla.shape[1])] + dil_specs + dil_specs + [
            row(o_nsa.shape[1]), pl.BlockSpec(w.shape, lambda i: (0, 0))],
        out_specs=row(d),
        out_shape=jax.ShapeDtypeStruct((n, d), F32),
        scratch_shapes=[pltpu.VMEM((12, tm, LANES), F32)],
        compiler_params=_cparams(("parallel",)),
        name="mix_out",
    )(x2, o_mla, *o_dil, *lse_dil, o_nsa, w)


def kernel(x, ffn1_norm, ffn1_w_in, ffn1_w_out, mix_norm, w_mix_in, mla_q_norm, mla_w_uq, mla_kv_norm,
           mla_w_ukv, nsa_cmp_pos, nsa_phi_k1, nsa_phi_k2, nsa_phi_v1, nsa_phi_v2, w_mix_out, ffn2_norm,
           ffn2_w_in, ffn2_w_out, final_norm):
    b, s, d = x.shape
    depth = ffn1_w_in.shape[0]
    n = b * s
    cos, sin = _rope_tables(s)
    x2 = x.reshape(n, d)
    for l in range(depth):
        x2 = _ffn(x2, ffn1_norm[l], *_prep_ffn(ffn1_w_in[l], ffn1_w_out[l]))

        h2, gates2, cmp_in, nsa_vt, *dil_in = _mixin(x2, mix_norm[l], *_prep_mixin(w_mix_in[l]))
        h3 = h2.reshape(b, s, H_COLS)

        q, k, vt = _mla_proj(h2, mla_q_norm[l].reshape(1, -1), mla_kv_norm[l].reshape(1, -1),
                             *_prep_mla(mla_w_uq[l], mla_w_ukv[l]), cos, sin)
        o_mla = _mla_flash(q.reshape(b, s, -1), k.reshape(b, s, -1), vt)

        dil = [_dilated_group(hg, g, b) for g, hg in enumerate(dil_in)]

        kc, vct = _nsa_compress(cmp_in, b, *_prep_nsa_cmp(nsa_cmp_pos[l], nsa_phi_k1[l], nsa_phi_k2[l],
                                                          nsa_phi_v1[l], nsa_phi_v2[l]))
        o_nsa = _nsa(h3, gates2.reshape(b, s, LANES), nsa_vt, kc, vct)

        x2 = _mixout(x2, o_mla.reshape(n, -1), [o for o, _ in dil], [e for _, e in dil],
                     o_nsa.reshape(n, -1), w_mix_out[l].astype(MXU_DTYPE))

        gf = final_norm if l == depth - 1 else None
        x2 = _ffn(x2, ffn2_norm[l], *_prep_ffn(ffn2_w_in[l], ffn2_w_out[l]), gf)
    return x2.reshape(b, s, d)
```

```python
import functools
import math

import numpy as np
import jax
import jax.numpy as jnp
from jax import lax
from jax.experimental import pallas as pl
from jax.experimental.pallas import tpu as pltpu

F32 = jnp.float32
MXU_DTYPE = jnp.bfloat16
VMEM_LIMIT_BYTES = 56 * 1024 * 1024
LANES = 128
LOG2E = math.log2(math.e)
FFN_TILE = 512
DEN_ROWS = 16
LAZY_MAX_SLACK = 16.0

HEAD_DIM = 128
RMS_EPS = 1e-6
NEG_INF = -1e30

MLA_HEADS = 6
MLA_LORA = 512
MLA_NOPE = 128
MLA_ROPE = 64
MLA_V = 128
MLA_QK_PAD = 256
ROPE_BASE = 10000.0

DIL_PAIRS = ((128, 1), (512, 4), (2048, 16))
DIL_HEADS = 6
DIL_BLOCK = 128

NSA_HEADS = 4
NSA_CMP_LEN = 32
NSA_CMP_STRIDE = 16
NSA_SEL_LEN = 64
NSA_TOPK = 16
NSA_WINDOW = 512
NSA_FORCED_SCORE = 100.0
NSA_Q_BLOCK = 512
POS_SPLIT = 128

N_ALIBI = DIL_HEADS + NSA_HEADS
ALIBI_SLOPES = tuple(float(2.0 ** (-8.0 * i / N_ALIBI)) for i in range(1, N_ALIBI + 1))

COL_NSA_Q = 1024
COL_KROPE = 1536
COL_NSA_K = 1792
H_COLS = 2048
DIL_GROUP_COLS = 768

assert NSA_FORCED_SCORE > 2 * NSA_HEADS


def _cparams(sem):
    return pltpu.CompilerParams(dimension_semantics=sem, vmem_limit_bytes=VMEM_LIMIT_BYTES)


def _rms(x, g):
    ms = jnp.mean(x * x, axis=-1, keepdims=True)
    return (x * lax.rsqrt(ms + RMS_EPS)) * g


def _dot(a, b):
    return jnp.dot(a, b, preferred_element_type=F32)


def _shr(x, pow2):
    return lax.shift_right_logical(x, int(pow2).bit_length() - 1)


def _dot_t(a, b):
    return lax.dot_general(a, b, (((1,), (1,)), ((), ())), preferred_element_type=F32)


def _resident(shape, index_map):
    return pl.BlockSpec(shape, index_map, pipeline_mode=pl.Buffered(1))


def _ffn_kernel(x_ref, g_ref, wg_ref, wu_ref, wo_ref, *rest, final, stages, nff):
    if final:
        gf_ref, o_ref, xn_ref = rest
    else:
        o_ref, xn_ref = rest
    j = pl.program_id(1)

    @pl.when(j == 0)
    def _():
        x = x_ref[...]
        xn_ref[...] = _rms(x, g_ref[0]).astype(xn_ref.dtype)
        o_ref[...] = x

    for s in range(1, stages):
        @pl.when(j == s * nff)
        def _(s=s):
            xn_ref[...] = _rms(o_ref[...], g_ref[s]).astype(xn_ref.dtype)

    xn = xn_ref[...]
    half = wg_ref.shape[2] // 2
    acc = None
    for c in range(2):
        cs = slice(c * half, (c + 1) * half)
        gate = _dot(xn, wg_ref[0, :, cs])
        up = _dot(xn, wu_ref[0, :, cs])
        h = (0.5 * gate) * jax.nn.sigmoid(gate) * up
        part = _dot(h.astype(MXU_DTYPE), wo_ref[0, cs, :])
        acc = part if acc is None else acc + part
    o_ref[...] += acc

    if final:
        @pl.when(j == pl.num_programs(1) - 1)
        def _():
            o_ref[...] = _rms(o_ref[...], gf_ref[...])


def _ffn(x2, g, wg, wu, wo, gf=None, *, tm=1024, tf=FFN_TILE):
    n, d = x2.shape
    stages, dffp, _ = wo.shape
    nff = dffp // tf
    tm = min(tm, n)
    final = gf is not None
    in_specs = [
        pl.BlockSpec((tm, d), lambda i, j: (i, 0)),
        pl.BlockSpec((stages, 1, d), lambda i, j: (0, 0, 0)),
        pl.BlockSpec((1, d, tf), lambda i, j: (j // nff, 0, j % nff)),
        pl.BlockSpec((1, d, tf), lambda i, j: (j // nff, 0, j % nff)),
        pl.BlockSpec((1, tf, d), lambda i, j: (j // nff, j % nff, 0)),
    ]
    args = [x2, g.reshape(stages, 1, d), wg, wu, wo]
    if final:
        in_specs.append(pl.BlockSpec((1, d), lambda i, j: (0, 0)))
        args.append(gf.reshape(1, d))
    return pl.pallas_call(
        functools.partial(_ffn_kernel, final=final, stages=stages, nff=nff),
        grid=(n // tm, stages * nff),
        in_specs=in_specs,
        out_specs=pl.BlockSpec((tm, d), lambda i, j: (i, 0)),
        out_shape=jax.ShapeDtypeStruct((n, d), F32),
        scratch_shapes=[pltpu.VMEM((tm, d), MXU_DTYPE)],
        compiler_params=_cparams(("parallel", "arbitrary")),
        name="ffn_final" if final else f"ffn_x{stages}",
    )(*args)


def _prep_ffn(members, tf=FFN_TILE):
    dff = members[0][1].shape[1] // 2
    pad = -(-dff // tf) * tf - dff
    g = jnp.stack([m[0] for m in members])
    wg = jnp.stack([jnp.pad(m[1][:, :dff].astype(MXU_DTYPE), ((0, 0), (0, pad))) for m in members])
    wu = jnp.stack([jnp.pad(m[1][:, dff:].astype(MXU_DTYPE), ((0, 0), (0, pad))) for m in members])
    wo = jnp.stack([jnp.pad(m[2].astype(MXU_DTYPE), ((0, pad), (0, 0))) for m in members])
    return g, wg, wu, wo


def _mixin_body(x_ref, g_ref, wm_ref, wd_ref, wa_ref, wvt_ref, hm_ref, gate_ref, cmp_ref, vt_ref, dil_refs,
                relayout_scr):
    xn = _rms(x_ref[...], g_ref[...]).astype(MXU_DTYPE)
    hm_ref[...] = _dot(xn, wm_ref[...]).astype(hm_ref.dtype)
    aux = _dot(xn, wa_ref[...])
    gate_ref[...] = aux[:, 2 * LANES:]
    for c in range(2):
        relayout_scr[c] = aux[:, c * LANES:(c + 1) * LANES]
    for r in range(NSA_CMP_STRIDE):
        for c in range(2):
            lo = r * 2 * LANES + c * LANES
            cmp_ref[:, lo:lo + LANES] = relayout_scr[c, pl.ds(r, aux.shape[0] // NSA_CMP_STRIDE,
                                                               stride=NSA_CMP_STRIDE), :]
    vt_ref[...] = _dot_t(wvt_ref[...], xn).astype(vt_ref.dtype)
    hd = _dot(xn, wd_ref[...])
    tm = hd.shape[0]
    for g, out_ref in enumerate(dil_refs):
        dilation = DIL_PAIRS[g][1]
        cols = hd[:, g * DIL_GROUP_COLS:(g + 1) * DIL_GROUP_COLS]
        if dilation == 1:
            out_ref[...] = cols.astype(out_ref.dtype)
            continue
        for c in range(DIL_GROUP_COLS // LANES):
            relayout_scr[c] = cols[:, c * LANES:(c + 1) * LANES]
        for r in range(dilation):
            for c in range(DIL_GROUP_COLS // LANES):
                piece = relayout_scr[c, pl.ds(r, tm // dilation, stride=dilation), :]
                out_ref[:, r * DIL_GROUP_COLS + c * LANES:r * DIL_GROUP_COLS + (c + 1) * LANES] = (
                    piece.astype(out_ref.dtype))


def _mixin_kernel(x_ref, g_ref, wm_ref, wd_ref, wa_ref, wvt_ref, hm_ref, gate_ref, cmp_ref, vt_ref, d0_ref,
                  d1_ref, d2_ref, relayout_scr):
    _mixin_body(x_ref, g_ref, wm_ref, wd_ref, wa_ref, wvt_ref, hm_ref, gate_ref, cmp_ref, vt_ref,
                (d0_ref, d1_ref, d2_ref), relayout_scr)


def _mixin(x2, g, wm, wd, wa, wvt, *, tm=512):
    n, d = x2.shape
    weight = lambda a: _resident(a.shape, lambda i: (0, 0))
    dils = [dil for _, dil in DIL_PAIRS]
    chunk = NSA_CMP_STRIDE
    return pl.pallas_call(
        _mixin_kernel,
        grid=(n // tm,),
        in_specs=[
            pl.BlockSpec((tm, d), lambda i: (i, 0)),
            pl.BlockSpec((1, d), lambda i: (0, 0)),
            weight(wm), weight(wd), weight(wa), weight(wvt),
        ],
        out_specs=[
            pl.BlockSpec((tm, H_COLS), lambda i: (i, 0)),
            pl.BlockSpec((tm, LANES), lambda i: (i, 0)),
            pl.BlockSpec((tm // chunk, chunk * 2 * LANES), lambda i: (i, 0)),
            pl.BlockSpec((2 * LANES, tm), lambda i: (0, i)),
        ] + [pl.BlockSpec((tm // dil, dil * DIL_GROUP_COLS), lambda i: (i, 0)) for dil in dils],
        out_shape=[
            jax.ShapeDtypeStruct((n, H_COLS), MXU_DTYPE),
            jax.ShapeDtypeStruct((n, LANES), F32),
            jax.ShapeDtypeStruct((n // chunk, chunk * 2 * LANES), F32),
            jax.ShapeDtypeStruct((2 * LANES, n), MXU_DTYPE),
        ] + [jax.ShapeDtypeStruct((n // dil, dil * DIL_GROUP_COLS), MXU_DTYPE) for dil in dils],
        scratch_shapes=[pltpu.VMEM((DIL_GROUP_COLS // LANES, tm, LANES), F32)],
        compiler_params=_cparams(("parallel",)),
        name="mix_in",
    )(x2, g.reshape(1, d), wm, wd, wa, wvt)


def _rot_half_cols(w):
    half = w.shape[1] // 2
    return jnp.concatenate([-w[:, half:], w[:, :half]], axis=1)


def _prep_mixin(w):
    d = w.shape[0]
    w = w.astype(MXU_DTYPE)
    parts = {"q_lat": w[:, 0:512], "kv_lat": w[:, 512:1024], "k_rope": w[:, 1024:1088]}
    aligned = (("dq", 768), ("dk", 768), ("dv", 768), ("nq", 512), ("nkc", 128), ("nvc", 128),
               ("nks", 128), ("nvs", 128), ("nkw", 128), ("nvw", 128))
    start = 1088
    stop = start + sum(width for _, width in aligned)
    rest = w[:, start:stop]
    parts["ng"] = w[:, stop:]
    o = 0
    for name, width in aligned:
        parts[name] = rest[:, o:o + width]
        o += width
    z64 = jnp.zeros((d, 64), w.dtype)
    kr = parts["k_rope"]
    wm = jnp.concatenate([
        parts["q_lat"], parts["kv_lat"], parts["nq"],
        kr, z64, _rot_half_cols(kr), z64,
        parts["nks"], parts["nkw"]], axis=1).astype(MXU_DTYPE)
    gw = 2 * HEAD_DIM
    wd = jnp.concatenate([parts[name][:, g * gw:(g + 1) * gw] for g in range(len(DIL_PAIRS))
                          for name in ("dq", "dk", "dv")], axis=1).astype(MXU_DTYPE)
    wa = jnp.concatenate([parts["nkc"], parts["nvc"], parts["ng"],
                          jnp.zeros((d, LANES - 12), w.dtype)], axis=1).astype(MXU_DTYPE)
    wvt = jnp.concatenate([parts["nvs"], parts["nvw"]], axis=1).T.astype(MXU_DTYPE)
    return wm, wd, wa, wvt


def _rope_tables_kernel(invf_ref, cos_ref, sin_ref):
    tm = cos_ref.shape[0]
    pos = (pl.program_id(0) * tm + lax.broadcasted_iota(jnp.int32, (tm, 1), 0)).astype(F32)
    ang = pos * invf_ref[...]
    cos_ref[...] = jnp.cos(ang)
    sin_ref[...] = jnp.sin(ang)


def _rope_tables(seq, *, tm=512):
    tm = min(tm, seq)
    spec = pl.BlockSpec((tm, LANES), lambda i: (i, 0))
    sds = jax.ShapeDtypeStruct((seq, LANES), F32)
    return pl.pallas_call(
        _rope_tables_kernel,
        grid=(seq // tm,),
        in_specs=[pl.BlockSpec((1, LANES), lambda i: (0, 0))],
        out_specs=[spec, spec],
        out_shape=[sds, sds],
        compiler_params=_cparams(("parallel",)),
        name="rope_tables",
    )(_rope_inv_freq_row())


def _mla_proj_kernel(lat_ref, kr_ref, qn_ref, kvn_ref, wq_ref, wk_ref, wvt_ref, cos_ref, sin_ref,
                     q_ref, k_ref, vt_ref):
    cos = cos_ref[...]
    sin = sin_ref[...]
    scale = (MLA_NOPE + MLA_ROPE) ** -0.5 * LOG2E

    lat = lat_ref[...].astype(F32)
    qn = _rms(lat[:, :MLA_LORA], qn_ref[...]).astype(MXU_DTYPE)
    kvn = _rms(lat[:, MLA_LORA:], kvn_ref[...]).astype(MXU_DTYPE)
    qm = _dot(qn, wq_ref[...])
    rot0 = MLA_HEADS * MLA_QK_PAD
    for h in range(MLA_HEADS):
        c = h * MLA_QK_PAD
        nope = qm[:, c:c + LANES]
        pe = qm[:, c + LANES:c + 2 * LANES] * cos + qm[:, rot0 + h * LANES:rot0 + (h + 1) * LANES] * sin
        q_ref[:, c:c + LANES] = (nope * scale).astype(q_ref.dtype)
        q_ref[:, c + LANES:c + 2 * LANES] = (pe * scale).astype(q_ref.dtype)

    kr = kr_ref[...].astype(F32)
    kpe = (kr[:, :LANES] * cos + kr[:, LANES:] * sin).astype(k_ref.dtype)
    kn = _dot(kvn, wk_ref[...])
    for h in range(MLA_HEADS):
        c = h * MLA_QK_PAD
        k_ref[:, c:c + LANES] = kn[:, h * LANES:(h + 1) * LANES].astype(k_ref.dtype)
        k_ref[:, c + LANES:c + 2 * LANES] = kpe
    vt_ref[...] = _dot_t(wvt_ref[...], kvn).astype(vt_ref.dtype)


def _mla_proj(h2, qn, kvn, wq, wk, wvt, cos, sin, *, tm=512):
    n = h2.shape[0]
    tm = min(tm, cos.shape[0])
    tiles_per_seq = cos.shape[0] // tm
    qk_cols = MLA_HEADS * MLA_QK_PAD
    v_rows = MLA_HEADS * MLA_V
    full = lambda a: pl.BlockSpec(a.shape, lambda i: (0,) * a.ndim)
    table = pl.BlockSpec((tm, LANES), lambda i: (i % tiles_per_seq, 0))
    return pl.pallas_call(
        _mla_proj_kernel,
        grid=(n // tm,),
        in_specs=[
            pl.BlockSpec((tm, 2 * MLA_LORA), lambda i: (i, 0)),
            pl.BlockSpec((tm, 2 * LANES), lambda i: (i, COL_KROPE // (2 * LANES))),
            full(qn), full(kvn), full(wq), full(wk), full(wvt), table, table,
        ],
        out_specs=[
            pl.BlockSpec((tm, qk_cols), lambda i: (i, 0)),
            pl.BlockSpec((tm, qk_cols), lambda i: (i, 0)),
            pl.BlockSpec((v_rows, tm), lambda i: (0, i)),
        ],
        out_shape=[
            jax.ShapeDtypeStruct((n, qk_cols), MXU_DTYPE),
            jax.ShapeDtypeStruct((n, qk_cols), MXU_DTYPE),
            jax.ShapeDtypeStruct((v_rows, n), MXU_DTYPE),
        ],
        compiler_params=_cparams(("parallel",)),
        name="mla_proj",
    )(h2, h2, qn, kvn, wq, wk, wvt, cos, sin)


def _prep_mla(w_uq, w_ukv):
    r = MLA_LORA
    wq3 = w_uq.reshape(r, MLA_HEADS, MLA_NOPE + MLA_ROPE)
    nope, pe = wq3[..., :MLA_NOPE], wq3[..., MLA_NOPE:]
    z = jnp.zeros((r, MLA_HEADS, 64), w_uq.dtype)
    main = jnp.concatenate([nope, pe, z], axis=-1).reshape(r, MLA_HEADS * MLA_QK_PAD)
    half = MLA_ROPE // 2
    pe_rot = jnp.concatenate([-pe[..., half:], pe[..., :half]], axis=-1)
    rot = jnp.concatenate([pe_rot, z], axis=-1).reshape(r, MLA_HEADS * LANES)
    wq = jnp.concatenate([main, rot], axis=1).astype(MXU_DTYPE)
    wkv3 = w_ukv.reshape(r, MLA_HEADS, MLA_NOPE + MLA_V)
    wk = wkv3[..., :MLA_NOPE].reshape(r, MLA_HEADS * MLA_NOPE).astype(MXU_DTYPE)
    wvt = wkv3[..., MLA_NOPE:].reshape(r, MLA_HEADS * MLA_V).T.astype(MXU_DTYPE)
    return wq, wk, wvt


def _rope_inv_freq_row():
    half = MLA_ROPE // 2
    f = ROPE_BASE ** (-jnp.arange(half, dtype=F32) / half)
    return jnp.concatenate([f, f, jnp.zeros((LANES - MLA_ROPE,), F32)]).reshape(1, LANES)


def _with_ones_rows(v_t):
    return jnp.concatenate([v_t, jnp.ones((DEN_ROWS, v_t.shape[1]), v_t.dtype)], axis=0)


def _softmax_piece_t(scores, v_aug, m_ref, acc_ref, *, first):
    if first:
        s_t = scores()
        m_new = jnp.max(s_t, axis=0, keepdims=True)
        acc_ref[...] = _dot(v_aug, jnp.exp2(s_t - m_new).astype(MXU_DTYPE))
        m_ref[...] = m_new
        return

    m_old = m_ref[...]
    s_t = scores()
    pv = _dot(v_aug, jnp.exp2(s_t - m_old).astype(MXU_DTYPE))
    fits = jnp.max(jnp.max(s_t, axis=0, keepdims=True) - m_old) <= LAZY_MAX_SLACK

    @pl.when(fits)
    def _():
        acc_ref[...] += pv

    @pl.when(jnp.logical_not(fits))
    def _():
        s_again = scores()
        m_new = jnp.maximum(m_old, jnp.max(s_again, axis=0, keepdims=True))
        weights = jnp.exp2(s_again - m_new).astype(MXU_DTYPE)
        acc_ref[...] = jnp.exp2(m_old - m_new) * acc_ref[...] + _dot(v_aug, weights)
        m_ref[...] = m_new


def _softmax_result(acc_scr, dv):
    return acc_scr[0:dv, :] * (1.0 / acc_scr[dv:dv + 1, :])


def _flash_kernel(qt_ref, kt_ref, q_ref, k_ref, vt_ref, o_ref, m_scr, acc_scr, *, t, dsub):
    step = pl.program_id(2)
    qi = qt_ref[step]
    ki = kt_ref[step]

    def piece(k0, q0, size, diagonal):
        ks = slice(k0, k0 + size)
        qs = slice(q0, q0 + size)

        def scores():
            s_t = _dot_t(k_ref[0, ks, :], q_ref[0, qs, :])
            if diagonal and k0 + size - 1 > q0:
                kpos = k0 + lax.broadcasted_iota(jnp.int32, (size, 1), 0)
                qpos = q0 + lax.broadcasted_iota(jnp.int32, (1, size), 1)
                s_t = jnp.where(kpos <= qpos, s_t, NEG_INF)
            return s_t

        _softmax_piece_t(scores, _with_ones_rows(vt_ref[:, ks]), m_scr.at[:, qs], acc_scr.at[:, qs],
                         first=diagonal and k0 == q0)

    @pl.when(ki == qi)
    def _():
        starts = range(0, t, dsub)
        for p in starts:
            piece(p, p, dsub, True)
        for k0 in starts:
            for q0 in starts:
                if q0 > k0:
                    piece(k0, q0, dsub, True)

    @pl.when(ki < qi)
    def _():
        piece(0, 0, t, False)

    @pl.when(ki == 0)
    def _():
        o_ref[0] = _softmax_result(acc_scr, MLA_V).T.astype(o_ref.dtype)


def _mla_flash(q3, k3, vt, *, t=2048, dsub=1024):
    b, s, _ = q3.shape
    t = min(t, s)
    dsub = min(dsub, t)
    nq = s // t
    pairs = [(qi, ki) for qi in range(nq) for ki in range(qi, -1, -1)]
    qt = jnp.asarray(np.array([p[0] for p in pairs], np.int32))
    kt = jnp.asarray(np.array([p[1] for p in pairs], np.int32))
    grid_spec = pltpu.PrefetchScalarGridSpec(
        num_scalar_prefetch=2,
        grid=(b, MLA_HEADS, len(pairs)),
        in_specs=[
            pl.BlockSpec((1, t, MLA_QK_PAD), lambda bi, h, st, qt, kt: (bi, qt[st], h)),
            pl.BlockSpec((1, t, MLA_QK_PAD), lambda bi, h, st, qt, kt: (bi, kt[st], h)),
            pl.BlockSpec((MLA_V, t), lambda bi, h, st, qt, kt: (h, bi * nq + kt[st])),
        ],
        out_specs=pl.BlockSpec((1, t, MLA_V), lambda bi, h, st, qt, kt: (bi, qt[st], h)),
        scratch_shapes=[pltpu.VMEM((1, t), F32), pltpu.VMEM((MLA_V + DEN_ROWS, t), F32)],
    )
    return pl.pallas_call(
        functools.partial(_flash_kernel, t=t, dsub=dsub),
        grid_spec=grid_spec,
        out_shape=jax.ShapeDtypeStruct((b, s, MLA_HEADS * MLA_V), MXU_DTYPE),
        compiler_params=_cparams(("parallel", "parallel", "arbitrary")),
        name="mla_flash",
    )(qt, kt, q3, k3, vt)


def _dil_kernel(q_ref, kc_ref, kp_ref, vc_ref, vp_ref, o_ref, lse_ref, *, dilation, span, slopes, tl):
    n = pl.program_id(2)
    scale = HEAD_DIM ** -0.5
    blk = DIL_BLOCK
    a = lax.broadcasted_iota(jnp.int32, (blk, 1), 0)
    c = lax.broadcasted_iota(jnp.int32, (1, 2 * blk), 1)
    j = blk + a - c
    in_band = (j >= 0) & (j <= span)
    first_valid = in_band & ((c >= blk) | (n > 0))
    dist = (j * dilation).astype(F32)
    for hg in range(2):
        cols = slice(hg * LANES, (hg + 1) * LANES)
        bias = slopes[hg] * dist
        for sb in range(tl // blk):
            rows = slice(sb * blk, (sb + 1) * blk)
            q = (q_ref[0, rows, cols].astype(F32) * scale).astype(MXU_DTYPE)
            if sb == 0:
                kprev, vprev, valid = kp_ref[0, :, cols], vp_ref[0, :, cols], first_valid
            else:
                prev = slice((sb - 1) * blk, sb * blk)
                kprev, vprev, valid = kc_ref[0, prev, cols], vc_ref[0, prev, cols], in_band
            keys = jnp.concatenate([kprev, kc_ref[0, rows, cols]], axis=0)
            vals = jnp.concatenate([vprev, vc_ref[0, rows, cols]], axis=0)
            s = jnp.where(valid, _dot_t(q, keys) - bias, NEG_INF)
            m = jnp.max(s, axis=1, keepdims=True)
            e = jnp.where(valid, jnp.exp(s - m), 0.0)
            den = jnp.sum(e, axis=1, keepdims=True)
            o_ref[0, rows, cols] = _dot((e / den).astype(MXU_DTYPE), vals)
            lse_ref[0, rows, cols] = jnp.broadcast_to(m + jnp.log(den), (blk, LANES))


def _dilated_group(hg, g, b, *, tl=1024):
    window, dilation = DIL_PAIRS[g]
    span = window // dilation
    seq_l = hg.shape[0] // b
    tl = min(tl, seq_l)
    per_blk = tl // DIL_BLOCK
    w = 2 * LANES
    hv = hg.reshape(b, seq_l, dilation * DIL_GROUP_COLS)
    qc, kc, vc = 0, 1, 2
    stride = DIL_GROUP_COLS // w
    cur = lambda col: pl.BlockSpec((1, tl, w), lambda bi, r, n: (bi, n, r * stride + col))
    prev = lambda col: pl.BlockSpec(
        (1, DIL_BLOCK, w), lambda bi, r, n: (bi, jnp.maximum(n * per_blk - 1, 0), r * stride + col))
    out_spec = pl.BlockSpec((1, tl, w), lambda bi, r, n: (bi, n, r))
    out_sds = jax.ShapeDtypeStruct((b, seq_l, dilation * w), F32)
    o, lse = pl.pallas_call(
        functools.partial(_dil_kernel, dilation=dilation, span=span,
                          slopes=ALIBI_SLOPES[2 * g:2 * g + 2], tl=tl),
        grid=(b, dilation, seq_l // tl),
        in_specs=[cur(qc), cur(kc), prev(kc), cur(vc), prev(vc)],
        out_specs=[out_spec, out_spec],
        out_shape=[out_sds, out_sds],
        compiler_params=_cparams(("parallel", "parallel", "parallel")),
        name=f"dilated_g{g}",
    )(hv, hv, hv, hv, hv)
    return o.reshape(b * seq_l, dilation * w), lse.reshape(b * seq_l, dilation * w)


def _nsa_cmp_kernel(x_ref, pos_ref, wa_ref, wb_ref, w2k_ref, w2vt_ref, kc_ref, vct_ref):
    x = x_ref[0]
    xa = (x + pos_ref[0:1, :]).astype(MXU_DTYPE)
    xb = (x + pos_ref[1:2, :]).astype(MXU_DTYPE)
    first = _dot(xa, wa_ref[...])
    second = _dot(xb, wb_ref[...])
    second_next = pltpu.roll(second, x.shape[0] - 1, 0)
    pre = first + second_next
    hid = (pre * jax.nn.sigmoid(pre)).astype(MXU_DTYPE)
    kc_ref[0] = _dot(hid[:, :LANES], w2k_ref[...]).astype(kc_ref.dtype)
    vct_ref[0] = _dot_t(w2vt_ref[...], hid[:, LANES:]).astype(vct_ref.dtype)


def _nsa_compress(cmp_in, b, pos2, wa, wb, w2k, w2vt):
    nchunk = cmp_in.shape[0] // b
    xk = cmp_in.reshape(b, nchunk, cmp_in.shape[1])
    full = lambda a: pl.BlockSpec(a.shape, lambda bi: (0,) * a.ndim)
    return pl.pallas_call(
        _nsa_cmp_kernel,
        grid=(b,),
        in_specs=[pl.BlockSpec((1, nchunk, xk.shape[2]), lambda bi: (bi, 0, 0)),
                  full(pos2), full(wa), full(wb), full(w2k), full(w2vt)],
        out_specs=[pl.BlockSpec((1, nchunk, LANES), lambda bi: (bi, 0, 0)),
                   pl.BlockSpec((1, LANES, nchunk), lambda bi: (bi, 0, 0))],
        out_shape=[jax.ShapeDtypeStruct((b, nchunk, LANES), MXU_DTYPE),
                   jax.ShapeDtypeStruct((b, LANES, nchunk), MXU_DTYPE)],
        compiler_params=_cparams(("parallel",)),
        name="nsa_cmp",
    )(xk, pos2, wa, wb, w2k, w2vt)


def _prep_nsa_cmp(cmp_pos, phi_k1, phi_k2, phi_v1, phi_v2):
    half = NSA_CMP_LEN // 2
    pk = cmp_pos.reshape(2, half, HEAD_DIM)
    pos2 = jnp.concatenate([pk, pk], axis=-1).reshape(2, half * 2 * LANES)

    def halves(w1, is_v):
        w = w1.reshape(2, half, HEAD_DIM, HEAD_DIM)
        z = jnp.zeros_like(w)
        w = jnp.concatenate([z, w] if is_v else [w, z], axis=2)
        return w.reshape(2, half * 2 * LANES, HEAD_DIM)

    wk, wv = halves(phi_k1, False), halves(phi_v1, True)
    wa = jnp.concatenate([wk[0], wv[0]], axis=1).astype(MXU_DTYPE)
    wb = jnp.concatenate([wk[1], wv[1]], axis=1).astype(MXU_DTYPE)
    return pos2, wa, wb, phi_k2.astype(MXU_DTYPE), phi_v2.T.astype(MXU_DTYPE)


def _split3(x):
    hi = x.astype(MXU_DTYPE)
    r = x - hi.astype(F32)
    mid = r.astype(MXU_DTYPE)
    lo = (r - mid.astype(F32)).astype(MXU_DTYPE)
    return hi, mid, lo


def _pos_features(pos):
    hi = jnp.floor(pos / POS_SPLIT) * POS_SPLIT
    lo = pos - hi
    cols = jnp.stack([hi, hi, hi, lo, lo, lo], axis=1)
    return jnp.pad(cols, ((0, 0), (0, LANES - 6))).astype(MXU_DTYPE)


def _slope_features(tq):
    sig = jnp.asarray([s * LOG2E for s in ALIBI_SLOPES[DIL_HEADS:]], F32)
    pieces = jnp.stack(_split3(sig), axis=1)
    rows = jnp.concatenate([pieces, pieces], axis=1)
    rows = jnp.pad(rows, ((0, 0), (0, LANES - 6)))
    return jnp.repeat(rows, tq, axis=0)


def _masked_softmax_t(s_t, mask):
    s_t = jnp.where(mask, s_t, NEG_INF)
    m = jnp.max(s_t, axis=0, keepdims=True)
    e = jnp.exp2(s_t - m)
    den = jnp.sum(e, axis=0, keepdims=True)
    return e * jnp.where(m > 0.5 * NEG_INF, 1.0 / den, 0.0)


def _nsa_kernel(q_ref, k_ref, vt_ref, kc_ref, vct_ref, c2st_ref, pf_ref, cpf_ref, sf_ref, wband_ref, g_ref,
                o_ref, m_scr, acc_scr, bias_scr, touched_smem, ocmp_scr, score_scr,
                *, tq, tk, topk, cmp_chunk, sel_chunk):
    i = pl.program_id(1)
    t0 = i * tq
    nh = NSA_HEADS
    cols = nh * tq
    ncp = kc_ref.shape[1]
    ns = c2st_ref.shape[0]
    blocks_per_tile = tk // NSA_SEL_LEN

    q = q_ref[0]
    q4 = jnp.concatenate([q[:, h * LANES:(h + 1) * LANES] for h in range(nh)], axis=0)
    q4 = (q4.astype(F32) * (HEAD_DIM ** -0.5 * LOG2E)).astype(MXU_DTYPE)
    q4 = jnp.concatenate([q4, sf_ref[...]], axis=1)
    col = lax.broadcasted_iota(jnp.int32, (1, cols), 1)
    tpos = (t0 + (col & (tq - 1))).astype(F32)

    def cmp_part(rows):
        cidx = lax.broadcasted_iota(jnp.int32, (rows, 1), 0).astype(F32)
        c_end = cidx * NSA_CMP_STRIDE + (NSA_CMP_LEN - 1)
        kc = jnp.concatenate([kc_ref[0, 0:rows, :], cpf_ref[0:rows, :]], axis=1)
        p_cmp = _masked_softmax_t(_dot_t(kc, q4), c_end <= tpos)
        ocmp_scr[...] = _dot(vct_ref[0, :, 0:rows], p_cmp.astype(MXU_DTYPE))
        p_sum = p_cmp[:, 0:tq]
        for h in range(1, nh):
            p_sum = p_sum + p_cmp[:, h * tq:(h + 1) * tq]
        c2st = c2st_ref[:, 0:rows]
        if MXU_DTYPE == jnp.float32:
            score_scr[...] = _dot(c2st, p_sum)
        else:
            score_scr[...] = sum(_dot(c2st, piece) for piece in _split3(p_sum))

    n_variants = -(-ncp // cmp_chunk)
    n_ending = (t0 + tq) // NSA_CMP_STRIDE - 1
    need = jnp.clip((n_ending + cmp_chunk - 1) // cmp_chunk, 1, n_variants)
    for v in range(1, n_variants + 1):
        pl.when(need == v)(functools.partial(cmp_part, min(v * cmp_chunk, ncp)))
    o_cmp = ocmp_scr[...]

    t1 = t0 + lax.broadcasted_iota(jnp.int32, (1, tq), 1)
    cur = _shr(t1, NSA_SEL_LEN).astype(F32)
    max_forced = 3
    n_forced = 1.0 + jnp.where(cur >= 1.0, 1.0, 0.0) + jnp.where(cur >= 2.0, 1.0, 0.0)

    def select_blocks(rows, may_lack_forced):
        jj = lax.broadcasted_iota(jnp.int32, (rows, 1), 0).astype(F32)
        forced = (jj == 0.0) | (jj == cur) | (jj == cur - 1.0)
        score = jnp.where(jj > cur, -1.0, jnp.where(forced, -2.0, score_scr[0:rows, :]))
        bias = jnp.where(forced, 0.0, NEG_INF)

        def pick(score, bias, wanted):
            best = jnp.max(score, axis=0, keepdims=True)
            idx = jnp.min(jnp.where(score == best, jj, float(rows)), axis=0, keepdims=True)
            hit = jj == idx
            if wanted is not None:
                hit = hit & wanted
            return jnp.where(hit, -2.0, score), jnp.where(hit, 0.0, bias)

        for _ in range(topk - max_forced):
            score, bias = pick(score, bias, None)
        bias_scr[0:rows, :] = bias
        if rows < ns:
            bias_scr[rows:ns, :] = jnp.full((ns - rows, tq), NEG_INF, F32)
        if may_lack_forced:
            score_scr[0:rows, :] = score

            @pl.when(t0 < (max_forced - 1) * NSA_SEL_LEN)
            def _():
                score, bias = score_scr[0:rows, :], bias_scr[0:rows, :]
                for done in range(topk - max_forced, topk - 1):
                    score, bias = pick(score, bias, n_forced + done < float(topk))
                bias_scr[0:rows, :] = bias

    sel_variants = -(-ns // sel_chunk)
    n_started = (t0 + tq) // NSA_SEL_LEN
    need_sel = jnp.clip((n_started + sel_chunk - 1) // sel_chunk, 1, sel_variants)
    assert (max_forced - 1) * NSA_SEL_LEN + tq <= sel_chunk * NSA_SEL_LEN
    for v in range(1, sel_variants + 1):
        pl.when(need_sel == v)(functools.partial(select_blocks, min(v * sel_chunk, ns), v == 1))

    bias = bias_scr[...]
    block_any = jnp.broadcast_to(jnp.max(bias, axis=1, keepdims=True), (ns, LANES))
    tile_any = jnp.max(block_any.reshape(ns // blocks_per_tile, blocks_per_tile, LANES), axis=1)
    for kt in range(ns // blocks_per_tile):
        touched_smem[kt] = tile_any[kt, 0]

    def slc_update(kt, causal):
        k0 = pl.multiple_of(kt * tk, tk)
        b0 = pl.multiple_of(kt * blocks_per_tile, blocks_per_tile)

        def scores():
            keys = jnp.concatenate([k_ref[0, pl.ds(k0, tk), 0:LANES], pf_ref[pl.ds(k0, tk), :]], axis=1)
            s_t = _dot_t(keys, q4)
            rows = [jnp.broadcast_to(bias_scr[pl.ds(b0 + r, 1), :], (NSA_SEL_LEN, tq))
                    for r in range(blocks_per_tile)]
            sel_bias = jnp.concatenate(rows, axis=0)
            s_t = s_t + jnp.concatenate([sel_bias] * nh, axis=1)
            if causal:
                kpos = (k0 + lax.broadcasted_iota(jnp.int32, (tk, 1), 0)).astype(F32)
                s_t = jnp.where(kpos <= tpos, s_t, NEG_INF)
            return s_t

        _softmax_piece_t(scores, _with_ones_rows(vt_ref[0:LANES, pl.ds(k0, tk)]), m_scr, acc_scr,
                         first=causal)

    last = t0 // tk
    slc_update(last, True)

    def slc_tile(back, carry):
        kt = last - 1 - back
        pl.when(touched_smem[kt] == 0.0)(functools.partial(slc_update, kt, False))
        return carry

    lax.fori_loop(0, last, slc_tile, 0)
    o_slc = _softmax_result(acc_scr, HEAD_DIM)

    wlen = NSA_WINDOW + tq
    ws = pl.multiple_of(jnp.maximum(t0 - NSA_WINDOW, 0), tq)
    keys = jnp.concatenate([k_ref[0, pl.ds(ws, wlen), LANES:2 * LANES], pf_ref[pl.ds(ws, wlen), :]], axis=1)
    band = wband_ref[jnp.minimum(i, wband_ref.shape[0] - 1)]
    s_t = _dot_t(keys, q4) + jnp.concatenate([band] * nh, axis=1)
    weights = jnp.exp2(s_t - jnp.max(s_t, axis=0, keepdims=True)).astype(MXU_DTYPE)
    win = _dot(_with_ones_rows(vt_ref[LANES:2 * LANES, pl.ds(ws, wlen)]), weights)
    o_win = win[0:HEAD_DIM, :] * (1.0 / win[HEAD_DIM:HEAD_DIM + 1, :])

    gates = jax.nn.sigmoid(g_ref[0]).T
    for h in range(nh):
        c = slice(h * tq, (h + 1) * tq)
        o = (gates[3 * h:3 * h + 1, :] * o_cmp[:, c] + gates[3 * h + 1:3 * h + 2, :] * o_slc[:, c]
             + gates[3 * h + 2:3 * h + 3, :] * o_win[:, c])
        o_ref[0, :, h * LANES:(h + 1) * LANES] = o.T.astype(o_ref.dtype)


def _nsa(h3, gates3, vt, kc, vct, *, tq=NSA_Q_BLOCK, tk=512):
    b, s, _ = h3.shape
    ncp = kc.shape[1]
    ns = s // NSA_SEL_LEN
    tk = min(tk, s)
    w = NSA_HEADS * HEAD_DIM
    c2st = _cmp_to_sel_t(ncp, ns)
    pf = _pos_features(jnp.arange(s, dtype=F32))
    cpf = _pos_features(jnp.arange(ncp, dtype=F32) * NSA_CMP_STRIDE + 0.5 * (NSA_CMP_LEN - 1))
    sf = _slope_features(tq).astype(MXU_DTYPE)
    wband = _window_band_bias(tq)
    const = lambda a: _resident(a.shape, lambda bi, i: (0,) * a.ndim)
    return pl.pallas_call(
        functools.partial(_nsa_kernel, tq=tq, tk=tk, topk=min(NSA_TOPK, ns), cmp_chunk=min(128, ncp),
                          sel_chunk=min(64, ns)),
        grid=(b, s // tq),
        in_specs=[
            pl.BlockSpec((1, tq, w), lambda bi, i: (bi, i, COL_NSA_Q // w)),
            _resident((1, s, 2 * LANES), lambda bi, i: (bi, 0, COL_NSA_K // (2 * LANES))),
            _resident((2 * LANES, s), lambda bi, i: (0, bi)),
            _resident((1, ncp, LANES), lambda bi, i: (bi, 0, 0)),
            _resident((1, LANES, ncp), lambda bi, i: (bi, 0, 0)),
            const(c2st), const(pf), const(cpf), const(sf), const(wband),
            pl.BlockSpec((1, tq, LANES), lambda bi, i: (bi, i, 0)),
        ],
        out_specs=pl.BlockSpec((1, tq, w), lambda bi, i: (bi, i, 0)),
        out_shape=jax.ShapeDtypeStruct((b, s, w), MXU_DTYPE),
        scratch_shapes=[pltpu.VMEM((1, NSA_HEADS * tq), F32),
                        pltpu.VMEM((HEAD_DIM + DEN_ROWS, NSA_HEADS * tq), F32),
                        pltpu.VMEM((ns, tq), F32), pltpu.SMEM((s // tk,), F32),
                        pltpu.VMEM((HEAD_DIM, NSA_HEADS * tq), F32), pltpu.VMEM((ns, tq), F32)],
        compiler_params=_cparams(("parallel", "arbitrary")),
        name="nsa",
    )(h3, h3, vt, kc, vct, c2st, pf, cpf, sf, wband, gates3)


def _window_band_bias(tq):
    wlen = NSA_WINDOW + tq
    u = np.arange(tq)[None, :]
    w = np.arange(wlen)[:, None]
    offsets = list(range(0, NSA_WINDOW, tq)) + [NSA_WINDOW]
    tiles = []
    for off in offsets:
        dist = off + u - w
        tiles.append(np.where((dist >= 0) & (dist < NSA_WINDOW), 0.0, NEG_INF))
    return jnp.asarray(np.stack(tiles).astype(np.float32))


def _cmp_to_sel_t(ncp, ns):
    nc = ncp - 1
    c = np.arange(ncp)[None, :]
    j = np.arange(ns)[:, None]
    start = c * NSA_CMP_STRIDE
    m = (start < (j + 1) * NSA_SEL_LEN) & (start + NSA_CMP_LEN - 1 >= j * NSA_SEL_LEN) & (c < nc)
    return jnp.asarray(m.astype(np.float32)).astype(MXU_DTYPE)


def _token_rows(ref, dilation, scr):
    if dilation == 1:
        return ref[...]
    tm = ref.shape[0] * dilation
    for r in range(dilation):
        for c in range(2):
            lo = r * 2 * LANES + c * LANES
            scr[c, pl.ds(r, tm // dilation, stride=dilation), :] = ref[:, lo:lo + LANES]
    return jnp.concatenate([scr[0], scr[1]], axis=1)


def _mixout_kernel(x_ref, mla_ref, d0_ref, d1_ref, d2_ref, l0_ref, l1_ref, l2_ref, nsa_ref, w_ref, o_ref,
                   relayout_scr):
    dils = [dil for _, dil in DIL_PAIRS]
    o_dil = [_token_rows(r, dil, relayout_scr.at[2 * g:2 * g + 2])
             for g, (r, dil) in enumerate(zip((d0_ref, d1_ref, d2_ref), dils))]
    lse = [_token_rows(r, dil, relayout_scr.at[6 + 2 * g:8 + 2 * g])
           for g, (r, dil) in enumerate(zip((l0_ref, l1_ref, l2_ref), dils))]
    top = jnp.maximum(jnp.maximum(lse[0], lse[1]), lse[2])
    e = [jnp.exp(l - top) for l in lse]
    inv = 1.0 / (e[0] + e[1] + e[2])
    acc = x_ref[...] + _dot(mla_ref[...], w_ref[0:768, :])
    for g in range(3):
        mixed = (o_dil[g] * (e[g] * inv)).astype(MXU_DTYPE)
        acc += _dot(mixed, w_ref[768 + g * 256:768 + (g + 1) * 256, :])
    acc += _dot(nsa_ref[...], w_ref[1536:2048, :])
    o_ref[...] = acc


def _mixout(x2, o_mla, o_dil, lse_dil, o_nsa, w, *, tm=512):
    n, d = x2.shape
    row = lambda width: pl.BlockSpec((tm, width), lambda i: (i, 0))
    dil_specs = [pl.BlockSpec((tm // dil, dil * 2 * LANES), lambda i: (i, 0)) for _, dil in DIL_PAIRS]
    return pl.pallas_call(
        _mixout_kernel,
        grid=(n // tm,),
        in_specs=[row(d), row(o_mla.shape[1])] + dil_specs + dil_specs + [
            row(o_nsa.shape[1]), pl.BlockSpec(w.shape, lambda i: (0, 0))],
        out_specs=row(d),
        out_shape=jax.ShapeDtypeStruct((n, d), F32),
        scratch_shapes=[pltpu.VMEM((12, tm, LANES), F32)],
        compiler_params=_cparams(("parallel",)),
        name="mix_out",
    )(x2, o_mla, *o_dil, *lse_dil, o_nsa, w)


def kernel(x, ffn1_norm, ffn1_w_in, ffn1_w_out, mix_norm, w_mix_in, mla_q_norm, mla_w_uq, mla_kv_norm,
           mla_w_ukv, nsa_cmp_pos, nsa_phi_k1, nsa_phi_k2, nsa_phi_v1, nsa_phi_v2, w_mix_out, ffn2_norm,
           ffn2_w_in, ffn2_w_out, final_norm):
    b, s, d = x.shape
    depth = ffn1_w_in.shape[0]
    n = b * s
    cos, sin = _rope_tables(s)
    x2 = x.reshape(n, d)
    first = lambda l: (ffn1_norm[l], ffn1_w_in[l], ffn1_w_out[l])
    second = lambda l: (ffn2_norm[l], ffn2_w_in[l], ffn2_w_out[l])
    x2 = _ffn(x2, *_prep_ffn([first(0)]))
    for l in range(depth):

        h2, gates2, cmp_in, nsa_vt, *dil_in = _mixin(x2, mix_norm[l], *_prep_mixin(w_mix_in[l]))
        h3 = h2.reshape(b, s, H_COLS)

        q, k, vt = _mla_proj(h2, mla_q_norm[l].reshape(1, -1), mla_kv_norm[l].reshape(1, -1),
                             *_prep_mla(mla_w_uq[l], mla_w_ukv[l]), cos, sin)
        o_mla = _mla_flash(q.reshape(b, s, -1), k.reshape(b, s, -1), vt)

        dil = [_dilated_group(hg, g, b) for g, hg in enumerate(dil_in)]

        kc, vct = _nsa_compress(cmp_in, b, *_prep_nsa_cmp(nsa_cmp_pos[l], nsa_phi_k1[l], nsa_phi_k2[l],
                                                          nsa_phi_v1[l], nsa_phi_v2[l]))
        o_nsa = _nsa(h3, gates2.reshape(b, s, LANES), nsa_vt, kc, vct)

        x2 = _mixout(x2, o_mla.reshape(n, -1), [o for o, _ in dil], [e for _, e in dil],
                     o_nsa.reshape(n, -1), w_mix_out[l].astype(MXU_DTYPE))

        if l == depth - 1:
            x2 = _ffn(x2, *_prep_ffn([second(l)]), final_norm)
        else:
            x2 = _ffn(x2, *_prep_ffn([second(l), first(l + 1)]))
    return x2.reshape(b, s, d)
```

```python
import functools
import math

import numpy as np
import jax
import jax.numpy as jnp
from jax import lax
from jax.experimental import pallas as pl
from jax.experimental.pallas import tpu as pltpu

F32 = jnp.float32
MXU_DTYPE = jnp.bfloat16
VMEM_LIMIT_BYTES = 56 * 1024 * 1024
LANES = 128
LOG2E = math.log2(math.e)
FFN_TILE = 512
DEN_ROWS = 16
LAZY_MAX_SLACK = 16.0

HEAD_DIM = 128
RMS_EPS = 1e-6
NEG_INF = -1e30

MLA_HEADS = 6
MLA_LORA = 512
MLA_NOPE = 128
MLA_ROPE = 64
MLA_V = 128
MLA_QK_PAD = 256
ROPE_BASE = 10000.0

DIL_PAIRS = ((128, 1), (512, 4), (2048, 16))
DIL_HEADS = 6
DIL_BLOCK = 128

NSA_HEADS = 4
NSA_CMP_LEN = 32
NSA_CMP_STRIDE = 16
NSA_SEL_LEN = 64
NSA_TOPK = 16
NSA_WINDOW = 512
NSA_FORCED_SCORE = 100.0
NSA_Q_BLOCK = 512
POS_SPLIT = 128

N_ALIBI = DIL_HEADS + NSA_HEADS
ALIBI_SLOPES = tuple(float(2.0 ** (-8.0 * i / N_ALIBI)) for i in range(1, N_ALIBI + 1))

COL_NSA_Q = 1024
COL_KROPE = 1536
COL_NSA_K = 1792
H_COLS = 2048
DIL_GROUP_COLS = 768

assert NSA_FORCED_SCORE > 2 * NSA_HEADS


def _cparams(sem):
    return pltpu.CompilerParams(dimension_semantics=sem, vmem_limit_bytes=VMEM_LIMIT_BYTES)


def _rms(x, g):
    ms = jnp.mean(x * x, axis=-1, keepdims=True)
    return (x * lax.rsqrt(ms + RMS_EPS)) * g


def _dot(a, b):
    return jnp.dot(a, b, preferred_element_type=F32)


def _shr(x, pow2):
    return lax.shift_right_logical(x, int(pow2).bit_length() - 1)


def _dot_t(a, b):
    return lax.dot_general(a, b, (((1,), (1,)), ((), ())), preferred_element_type=F32)


def _resident(shape, index_map):
    return pl.BlockSpec(shape, index_map, pipeline_mode=pl.Buffered(1))


def _ffn_kernel(x_ref, g_ref, wg_ref, wu_ref, wo_ref, *rest, final):
    if final:
        gf_ref, o_ref, xn_ref = rest
    else:
        o_ref, xn_ref = rest
    j = pl.program_id(1)

    @pl.when(j == 0)
    def _():
        x = x_ref[...]
        xn_ref[...] = _rms(x, g_ref[...]).astype(xn_ref.dtype)
        o_ref[...] = x

    xn = xn_ref[...]
    half = wg_ref.shape[1] // 2
    acc = None
    for c in range(2):
        cs = slice(c * half, (c + 1) * half)
        gate = _dot(xn, wg_ref[:, cs])
        up = _dot(xn, wu_ref[:, cs])
        h = (0.5 * gate) * jax.nn.sigmoid(gate) * up
        part = _dot(h.astype(MXU_DTYPE), wo_ref[cs, :])
        acc = part if acc is None else acc + part
    o_ref[...] += acc

    if final:
        @pl.when(j == pl.num_programs(1) - 1)
        def _():
            o_ref[...] = _rms(o_ref[...], gf_ref[...])


def _ffn(x2, g, wg, wu, wo, gf=None, *, tm=1024, tf=FFN_TILE):
    n, d = x2.shape
    dffp = wo.shape[0]
    nff = dffp // tf
    tm = min(tm, n)
    final = gf is not None
    in_specs = [
        pl.BlockSpec((tm, d), lambda i, j: (i, 0)),
        pl.BlockSpec((1, d), lambda i, j: (0, 0)),
        pl.BlockSpec((d, tf), lambda i, j: (0, j)),
        pl.BlockSpec((d, tf), lambda i, j: (0, j)),
        pl.BlockSpec((tf, d), lambda i, j: (j, 0)),
    ]
    args = [x2, g.reshape(1, d), wg, wu, wo]
    if final:
        in_specs.append(pl.BlockSpec((1, d), lambda i, j: (0, 0)))
        args.append(gf.reshape(1, d))
    return pl.pallas_call(
        functools.partial(_ffn_kernel, final=final),
        grid=(n // tm, nff),
        in_specs=in_specs,
        out_specs=pl.BlockSpec((tm, d), lambda i, j: (i, 0)),
        out_shape=jax.ShapeDtypeStruct((n, d), F32),
        scratch_shapes=[pltpu.VMEM((tm, d), MXU_DTYPE)],
        compiler_params=_cparams(("parallel", "arbitrary")),
        name="ffn_final" if final else "ffn",
    )(*args)


def _prep_ffn(w_in, w_out, tf=FFN_TILE):
    dff = w_in.shape[1] // 2
    pad = -(-dff // tf) * tf - dff
    wg = jnp.pad(w_in[:, :dff].astype(MXU_DTYPE), ((0, 0), (0, pad)))
    wu = jnp.pad(w_in[:, dff:].astype(MXU_DTYPE), ((0, 0), (0, pad)))
    wo = jnp.pad(w_out.astype(MXU_DTYPE), ((0, pad), (0, 0)))
    return wg, wu, wo


def _mixin_body(x_ref, g_ref, wm_ref, wd_ref, wa_ref, wvt_ref, hm_ref, gate_ref, cmp_ref, vt_ref, dil_refs,
                relayout_scr):
    xn = _rms(x_ref[...], g_ref[...]).astype(MXU_DTYPE)
    hm_ref[...] = _dot(xn, wm_ref[...]).astype(hm_ref.dtype)
    aux = _dot(xn, wa_ref[...])
    gate_ref[...] = aux[:, 2 * LANES:]
    for c in range(2):
        relayout_scr[c] = aux[:, c * LANES:(c + 1) * LANES]
    for r in range(NSA_CMP_STRIDE):
        for c in range(2):
            lo = r * 2 * LANES + c * LANES
            cmp_ref[:, lo:lo + LANES] = relayout_scr[c, pl.ds(r, aux.shape[0] // NSA_CMP_STRIDE,
                                                               stride=NSA_CMP_STRIDE), :]
    vt_ref[...] = _dot_t(wvt_ref[...], xn).astype(vt_ref.dtype)
    hd = _dot(xn, wd_ref[...])
    tm = hd.shape[0]
    for g, out_ref in enumerate(dil_refs):
        dilation = DIL_PAIRS[g][1]
        cols = hd[:, g * DIL_GROUP_COLS:(g + 1) * DIL_GROUP_COLS]
        if dilation == 1:
            out_ref[...] = cols.astype(out_ref.dtype)
            continue
        for c in range(DIL_GROUP_COLS // LANES):
            relayout_scr[c] = cols[:, c * LANES:(c + 1) * LANES]
        for r in range(dilation):
            for c in range(DIL_GROUP_COLS // LANES):
                piece = relayout_scr[c, pl.ds(r, tm // dilation, stride=dilation), :]
                out_ref[:, r * DIL_GROUP_COLS + c * LANES:r * DIL_GROUP_COLS + (c + 1) * LANES] = (
                    piece.astype(out_ref.dtype))


def _mixin_kernel(x_ref, g_ref, wm_ref, wd_ref, wa_ref, wvt_ref, hm_ref, gate_ref, cmp_ref, vt_ref, d0_ref,
                  d1_ref, d2_ref, relayout_scr):
    _mixin_body(x_ref, g_ref, wm_ref, wd_ref, wa_ref, wvt_ref, hm_ref, gate_ref, cmp_ref, vt_ref,
                (d0_ref, d1_ref, d2_ref), relayout_scr)


def _mixin(x2, g, wm, wd, wa, wvt, *, tm=512):
    n, d = x2.shape
    weight = lambda a: _resident(a.shape, lambda i: (0, 0))
    dils = [dil for _, dil in DIL_PAIRS]
    chunk = NSA_CMP_STRIDE
    return pl.pallas_call(
        _mixin_kernel,
        grid=(n // tm,),
        in_specs=[
            pl.BlockSpec((tm, d), lambda i: (i, 0)),
            pl.BlockSpec((1, d), lambda i: (0, 0)),
            weight(wm), weight(wd), weight(wa), weight(wvt),
        ],
        out_specs=[
            pl.BlockSpec((tm, H_COLS), lambda i: (i, 0)),
            pl.BlockSpec((tm, LANES), lambda i: (i, 0)),
            pl.BlockSpec((tm // chunk, chunk * 2 * LANES), lambda i: (i, 0)),
            pl.BlockSpec((2 * LANES, tm), lambda i: (0, i)),
        ] + [pl.BlockSpec((tm // dil, dil * DIL_GROUP_COLS), lambda i: (i, 0)) for dil in dils],
        out_shape=[
            jax.ShapeDtypeStruct((n, H_COLS), MXU_DTYPE),
            jax.ShapeDtypeStruct((n, LANES), F32),
            jax.ShapeDtypeStruct((n // chunk, chunk * 2 * LANES), F32),
            jax.ShapeDtypeStruct((2 * LANES, n), MXU_DTYPE),
        ] + [jax.ShapeDtypeStruct((n // dil, dil * DIL_GROUP_COLS), MXU_DTYPE) for dil in dils],
        scratch_shapes=[pltpu.VMEM((DIL_GROUP_COLS // LANES, tm, LANES), F32)],
        compiler_params=_cparams(("parallel",)),
        name="mix_in",
    )(x2, g.reshape(1, d), wm, wd, wa, wvt)


def _rot_half_cols(w):
    half = w.shape[1] // 2
    return jnp.concatenate([-w[:, half:], w[:, :half]], axis=1)


def _prep_mixin(w):
    d = w.shape[0]
    w = w.astype(MXU_DTYPE)
    parts = {"q_lat": w[:, 0:512], "kv_lat": w[:, 512:1024], "k_rope": w[:, 1024:1088]}
    aligned = (("dq", 768), ("dk", 768), ("dv", 768), ("nq", 512), ("nkc", 128), ("nvc", 128),
               ("nks", 128), ("nvs", 128), ("nkw", 128), ("nvw", 128))
    start = 1088
    stop = start + sum(width for _, width in aligned)
    rest = w[:, start:stop]
    parts["ng"] = w[:, stop:]
    o = 0
    for name, width in aligned:
        parts[name] = rest[:, o:o + width]
        o += width
    z64 = jnp.zeros((d, 64), w.dtype)
    kr = parts["k_rope"]
    wm = jnp.concatenate([
        parts["q_lat"], parts["kv_lat"], parts["nq"],
        kr, z64, _rot_half_cols(kr), z64,
        parts["nks"], parts["nkw"]], axis=1).astype(MXU_DTYPE)
    gw = 2 * HEAD_DIM
    wd = jnp.concatenate([parts[name][:, g * gw:(g + 1) * gw] for g in range(len(DIL_PAIRS))
                          for name in ("dq", "dk", "dv")], axis=1).astype(MXU_DTYPE)
    wa = jnp.concatenate([parts["nkc"], parts["nvc"], parts["ng"],
                          jnp.zeros((d, LANES - 12), w.dtype)], axis=1).astype(MXU_DTYPE)
    wvt = jnp.concatenate([parts["nvs"], parts["nvw"]], axis=1).T.astype(MXU_DTYPE)
    return wm, wd, wa, wvt


def _rope_tables_kernel(invf_ref, cos_ref, sin_ref):
    tm = cos_ref.shape[0]
    pos = (pl.program_id(0) * tm + lax.broadcasted_iota(jnp.int32, (tm, 1), 0)).astype(F32)
    ang = pos * invf_ref[...]
    cos_ref[...] = jnp.cos(ang)
    sin_ref[...] = jnp.sin(ang)


def _rope_tables(seq, *, tm=512):
    tm = min(tm, seq)
    spec = pl.BlockSpec((tm, LANES), lambda i: (i, 0))
    sds = jax.ShapeDtypeStruct((seq, LANES), F32)
    return pl.pallas_call(
        _rope_tables_kernel,
        grid=(seq // tm,),
        in_specs=[pl.BlockSpec((1, LANES), lambda i: (0, 0))],
        out_specs=[spec, spec],
        out_shape=[sds, sds],
        compiler_params=_cparams(("parallel",)),
        name="rope_tables",
    )(_rope_inv_freq_row())


def _mla_proj_kernel(lat_ref, kr_ref, qn_ref, kvn_ref, wq_ref, wk_ref, wvt_ref, cos_ref, sin_ref,
                     q_ref, k_ref, vt_ref):
    cos = cos_ref[...]
    sin = sin_ref[...]
    scale = (MLA_NOPE + MLA_ROPE) ** -0.5 * LOG2E

    lat = lat_ref[...].astype(F32)
    qn = _rms(lat[:, :MLA_LORA], qn_ref[...]).astype(MXU_DTYPE)
    kvn = _rms(lat[:, MLA_LORA:], kvn_ref[...]).astype(MXU_DTYPE)
    qm = _dot(qn, wq_ref[...])
    rot0 = MLA_HEADS * MLA_QK_PAD
    for h in range(MLA_HEADS):
        c = h * MLA_QK_PAD
        nope = qm[:, c:c + LANES]
        pe = qm[:, c + LANES:c + 2 * LANES] * cos + qm[:, rot0 + h * LANES:rot0 + (h + 1) * LANES] * sin
        q_ref[:, c:c + LANES] = (nope * scale).astype(q_ref.dtype)
        q_ref[:, c + LANES:c + 2 * LANES] = (pe * scale).astype(q_ref.dtype)

    kr = kr_ref[...].astype(F32)
    kpe = (kr[:, :LANES] * cos + kr[:, LANES:] * sin).astype(k_ref.dtype)
    kn = _dot(kvn, wk_ref[...])
    for h in range(MLA_HEADS):
        c = h * MLA_QK_PAD
        k_ref[:, c:c + LANES] = kn[:, h * LANES:(h + 1) * LANES].astype(k_ref.dtype)
        k_ref[:, c + LANES:c + 2 * LANES] = kpe
    vt_ref[...] = _dot_t(wvt_ref[...], kvn).astype(vt_ref.dtype)


def _mla_proj(h2, qn, kvn, wq, wk, wvt, cos, sin, *, tm=512):
    n = h2.shape[0]
    tm = min(tm, cos.shape[0])
    tiles_per_seq = cos.shape[0] // tm
    qk_cols = MLA_HEADS * MLA_QK_PAD
    v_rows = MLA_HEADS * MLA_V
    full = lambda a: pl.BlockSpec(a.shape, lambda i: (0,) * a.ndim)
    table = pl.BlockSpec((tm, LANES), lambda i: (i % tiles_per_seq, 0))
    return pl.pallas_call(
        _mla_proj_kernel,
        grid=(n // tm,),
        in_specs=[
            pl.BlockSpec((tm, 2 * MLA_LORA), lambda i: (i, 0)),
            pl.BlockSpec((tm, 2 * LANES), lambda i: (i, COL_KROPE // (2 * LANES))),
            full(qn), full(kvn), full(wq), full(wk), full(wvt), table, table,
        ],
        out_specs=[
            pl.BlockSpec((tm, qk_cols), lambda i: (i, 0)),
            pl.BlockSpec((tm, qk_cols), lambda i: (i, 0)),
            pl.BlockSpec((v_rows, tm), lambda i: (0, i)),
        ],
        out_shape=[
            jax.ShapeDtypeStruct((n, qk_cols), MXU_DTYPE),
            jax.ShapeDtypeStruct((n, qk_cols), MXU_DTYPE),
            jax.ShapeDtypeStruct((v_rows, n), MXU_DTYPE),
        ],
        compiler_params=_cparams(("parallel",)),
        name="mla_proj",
    )(h2, h2, qn, kvn, wq, wk, wvt, cos, sin)


def _prep_mla(w_uq, w_ukv):
    r = MLA_LORA
    wq3 = w_uq.reshape(r, MLA_HEADS, MLA_NOPE + MLA_ROPE)
    nope, pe = wq3[..., :MLA_NOPE], wq3[..., MLA_NOPE:]
    z = jnp.zeros((r, MLA_HEADS, 64), w_uq.dtype)
    main = jnp.concatenate([nope, pe, z], axis=-1).reshape(r, MLA_HEADS * MLA_QK_PAD)
    half = MLA_ROPE // 2
    pe_rot = jnp.concatenate([-pe[..., half:], pe[..., :half]], axis=-1)
    rot = jnp.concatenate([pe_rot, z], axis=-1).reshape(r, MLA_HEADS * LANES)
    wq = jnp.concatenate([main, rot], axis=1).astype(MXU_DTYPE)
    wkv3 = w_ukv.reshape(r, MLA_HEADS, MLA_NOPE + MLA_V)
    wk = wkv3[..., :MLA_NOPE].reshape(r, MLA_HEADS * MLA_NOPE).astype(MXU_DTYPE)
    wvt = wkv3[..., MLA_NOPE:].reshape(r, MLA_HEADS * MLA_V).T.astype(MXU_DTYPE)
    return wq, wk, wvt


def _rope_inv_freq_row():
    half = MLA_ROPE // 2
    f = ROPE_BASE ** (-jnp.arange(half, dtype=F32) / half)
    return jnp.concatenate([f, f, jnp.zeros((LANES - MLA_ROPE,), F32)]).reshape(1, LANES)


def _with_ones_rows(v_t):
    return jnp.concatenate([v_t, jnp.ones((DEN_ROWS, v_t.shape[1]), v_t.dtype)], axis=0)


def _softmax_piece_t(scores, v_aug, m_ref, acc_ref, *, first):
    if first:
        s_t = scores()
        m_new = jnp.max(s_t, axis=0, keepdims=True)
        acc_ref[...] = _dot(v_aug, jnp.exp2(s_t - m_new).astype(MXU_DTYPE))
        m_ref[...] = m_new
        return

    m_old = m_ref[...]
    s_t = scores()
    pv = _dot(v_aug, jnp.exp2(s_t - m_old).astype(MXU_DTYPE))
    fits = jnp.max(jnp.max(s_t, axis=0, keepdims=True) - m_old) <= LAZY_MAX_SLACK

    @pl.when(fits)
    def _():
        acc_ref[...] += pv

    @pl.when(jnp.logical_not(fits))
    def _():
        s_again = scores()
        m_new = jnp.maximum(m_old, jnp.max(s_again, axis=0, keepdims=True))
        weights = jnp.exp2(s_again - m_new).astype(MXU_DTYPE)
        acc_ref[...] = jnp.exp2(m_old - m_new) * acc_ref[...] + _dot(v_aug, weights)
        m_ref[...] = m_new


def _softmax_result(acc_scr, dv):
    return acc_scr[0:dv, :] * (1.0 / acc_scr[dv:dv + 1, :])


def _flash_kernel(qt_ref, kt_ref, q_ref, k_ref, vt_ref, o_ref, m_scr, acc_scr, *, t, dsub):
    step = pl.program_id(2)
    qi = qt_ref[step]
    ki = kt_ref[step]

    def piece(k0, q0, size, diagonal):
        ks = slice(k0, k0 + size)
        qs = slice(q0, q0 + size)

        def scores():
            s_t = _dot_t(k_ref[0, ks, :], q_ref[0, qs, :])
            if diagonal and k0 + size - 1 > q0:
                kpos = k0 + lax.broadcasted_iota(jnp.int32, (size, 1), 0)
                qpos = q0 + lax.broadcasted_iota(jnp.int32, (1, size), 1)
                s_t = jnp.where(kpos <= qpos, s_t, NEG_INF)
            return s_t

        _softmax_piece_t(scores, _with_ones_rows(vt_ref[:, ks]), m_scr.at[:, qs], acc_scr.at[:, qs],
                         first=diagonal and k0 == q0)

    @pl.when(ki == qi)
    def _():
        starts = range(0, t, dsub)
        for p in starts:
            piece(p, p, dsub, True)
        for k0 in starts:
            for q0 in starts:
                if q0 > k0:
                    piece(k0, q0, dsub, True)

    @pl.when(ki < qi)
    def _():
        piece(0, 0, t, False)

    @pl.when(ki == 0)
    def _():
        o_ref[0] = _softmax_result(acc_scr, MLA_V).T.astype(o_ref.dtype)


def _mla_flash(q3, k3, vt, *, t=2048, dsub=1024):
    b, s, _ = q3.shape
    t = min(t, s)
    dsub = min(dsub, t)
    nq = s // t
    pairs = [(qi, ki) for qi in range(nq) for ki in range(qi, -1, -1)]
    qt = jnp.asarray(np.array([p[0] for p in pairs], np.int32))
    kt = jnp.asarray(np.array([p[1] for p in pairs], np.int32))
    grid_spec = pltpu.PrefetchScalarGridSpec(
        num_scalar_prefetch=2,
        grid=(b, MLA_HEADS, len(pairs)),
        in_specs=[
            pl.BlockSpec((1, t, MLA_QK_PAD), lambda bi, h, st, qt, kt: (bi, qt[st], h)),
            pl.BlockSpec((1, t, MLA_QK_PAD), lambda bi, h, st, qt, kt: (bi, kt[st], h)),
            pl.BlockSpec((MLA_V, t), lambda bi, h, st, qt, kt: (h, bi * nq + kt[st])),
        ],
        out_specs=pl.BlockSpec((1, t, MLA_V), lambda bi, h, st, qt, kt: (bi, qt[st], h)),
        scratch_shapes=[pltpu.VMEM((1, t), F32), pltpu.VMEM((MLA_V + DEN_ROWS, t), F32)],
    )
    return pl.pallas_call(
        functools.partial(_flash_kernel, t=t, dsub=dsub),
        grid_spec=grid_spec,
        out_shape=jax.ShapeDtypeStruct((b, s, MLA_HEADS * MLA_V), MXU_DTYPE),
        compiler_params=_cparams(("parallel", "parallel", "arbitrary")),
        name="mla_flash",
    )(qt, kt, q3, k3, vt)


def _dil_kernel(q_ref, kc_ref, kp_ref, vc_ref, vp_ref, o_ref, lse_ref, *, dilation, span, slopes, tl):
    n = pl.program_id(2)
    scale = HEAD_DIM ** -0.5
    blk = DIL_BLOCK
    a = lax.broadcasted_iota(jnp.int32, (blk, 1), 0)
    c = lax.broadcasted_iota(jnp.int32, (1, 2 * blk), 1)
    j = blk + a - c
    in_band = (j >= 0) & (j <= span)
    first_valid = in_band & ((c >= blk) | (n > 0))
    dist = (j * dilation).astype(F32)
    for hg in range(2):
        cols = slice(hg * LANES, (hg + 1) * LANES)
        bias = slopes[hg] * dist
        for sb in range(tl // blk):
            rows = slice(sb * blk, (sb + 1) * blk)
            q = (q_ref[0, rows, cols].astype(F32) * scale).astype(MXU_DTYPE)
            if sb == 0:
                kprev, vprev, valid = kp_ref[0, :, cols], vp_ref[0, :, cols], first_valid
            else:
                prev = slice((sb - 1) * blk, sb * blk)
                kprev, vprev, valid = kc_ref[0, prev, cols], vc_ref[0, prev, cols], in_band
            keys = jnp.concatenate([kprev, kc_ref[0, rows, cols]], axis=0)
            vals = jnp.concatenate([vprev, vc_ref[0, rows, cols]], axis=0)
            s = jnp.where(valid, _dot_t(q, keys) - bias, NEG_INF)
            m = jnp.max(s, axis=1, keepdims=True)
            e = jnp.where(valid, jnp.exp(s - m), 0.0)
            den = jnp.sum(e, axis=1, keepdims=True)
            o_ref[0, rows, cols] = _dot((e / den).astype(MXU_DTYPE), vals)
            lse_ref[0, rows, cols] = jnp.broadcast_to(m + jnp.log(den), (blk, LANES))


def _dilated_group(hg, g, b, *, tl=2048):
    window, dilation = DIL_PAIRS[g]
    span = window // dilation
    seq_l = hg.shape[0] // b
    tl = min(tl, seq_l)
    per_blk = tl // DIL_BLOCK
    w = 2 * LANES
    hv = hg.reshape(b, seq_l, dilation * DIL_GROUP_COLS)
    qc, kc, vc = 0, 1, 2
    stride = DIL_GROUP_COLS // w
    cur = lambda col: pl.BlockSpec((1, tl, w), lambda bi, r, n: (bi, n, r * stride + col))
    prev = lambda col: pl.BlockSpec(
        (1, DIL_BLOCK, w), lambda bi, r, n: (bi, jnp.maximum(n * per_blk - 1, 0), r * stride + col))
    out_spec = pl.BlockSpec((1, tl, w), lambda bi, r, n: (bi, n, r))
    out_sds = jax.ShapeDtypeStruct((b, seq_l, dilation * w), F32)
    o, lse = pl.pallas_call(
        functools.partial(_dil_kernel, dilation=dilation, span=span,
                          slopes=ALIBI_SLOPES[2 * g:2 * g + 2], tl=tl),
        grid=(b, dilation, seq_l // tl),
        in_specs=[cur(qc), cur(kc), prev(kc), cur(vc), prev(vc)],
        out_specs=[out_spec, out_spec],
        out_shape=[out_sds, out_sds],
        compiler_params=_cparams(("parallel", "parallel", "parallel")),
        name=f"dilated_g{g}",
    )(hv, hv, hv, hv, hv)
    return o.reshape(b * seq_l, dilation * w), lse.reshape(b * seq_l, dilation * w)


def _nsa_cmp_kernel(x_ref, pos_ref, wa_ref, wb_ref, w2k_ref, w2vt_ref, kc_ref, vct_ref):
    x = x_ref[0]
    xa = (x + pos_ref[0:1, :]).astype(MXU_DTYPE)
    xb = (x + pos_ref[1:2, :]).astype(MXU_DTYPE)
    first = _dot(xa, wa_ref[...])
    second = _dot(xb, wb_ref[...])
    second_next = pltpu.roll(second, x.shape[0] - 1, 0)
    pre = first + second_next
    hid = (pre * jax.nn.sigmoid(pre)).astype(MXU_DTYPE)
    kc_ref[0] = _dot(hid[:, :LANES], w2k_ref[...]).astype(kc_ref.dtype)
    vct_ref[0] = _dot_t(w2vt_ref[...], hid[:, LANES:]).astype(vct_ref.dtype)


def _nsa_compress(cmp_in, b, pos2, wa, wb, w2k, w2vt):
    nchunk = cmp_in.shape[0] // b
    xk = cmp_in.reshape(b, nchunk, cmp_in.shape[1])
    full = lambda a: pl.BlockSpec(a.shape, lambda bi: (0,) * a.ndim)
    return pl.pallas_call(
        _nsa_cmp_kernel,
        grid=(b,),
        in_specs=[pl.BlockSpec((1, nchunk, xk.shape[2]), lambda bi: (bi, 0, 0)),
                  full(pos2), full(wa), full(wb), full(w2k), full(w2vt)],
        out_specs=[pl.BlockSpec((1, nchunk, LANES), lambda bi: (bi, 0, 0)),
                   pl.BlockSpec((1, LANES, nchunk), lambda bi: (bi, 0, 0))],
        out_shape=[jax.ShapeDtypeStruct((b, nchunk, LANES), MXU_DTYPE),
                   jax.ShapeDtypeStruct((b, LANES, nchunk), MXU_DTYPE)],
        compiler_params=_cparams(("parallel",)),
        name="nsa_cmp",
    )(xk, pos2, wa, wb, w2k, w2vt)


def _prep_nsa_cmp(cmp_pos, phi_k1, phi_k2, phi_v1, phi_v2):
    half = NSA_CMP_LEN // 2
    pk = cmp_pos.reshape(2, half, HEAD_DIM)
    pos2 = jnp.concatenate([pk, pk], axis=-1).reshape(2, half * 2 * LANES)

    def halves(w1, is_v):
        w = w1.reshape(2, half, HEAD_DIM, HEAD_DIM)
        z = jnp.zeros_like(w)
        w = jnp.concatenate([z, w] if is_v else [w, z], axis=2)
        return w.reshape(2, half * 2 * LANES, HEAD_DIM)

    wk, wv = halves(phi_k1, False), halves(phi_v1, True)
    wa = jnp.concatenate([wk[0], wv[0]], axis=1).astype(MXU_DTYPE)
    wb = jnp.concatenate([wk[1], wv[1]], axis=1).astype(MXU_DTYPE)
    return pos2, wa, wb, phi_k2.astype(MXU_DTYPE), phi_v2.T.astype(MXU_DTYPE)


def _split3(x):
    hi = x.astype(MXU_DTYPE)
    r = x - hi.astype(F32)
    mid = r.astype(MXU_DTYPE)
    lo = (r - mid.astype(F32)).astype(MXU_DTYPE)
    return hi, mid, lo


def _pos_features(pos):
    hi = jnp.floor(pos / POS_SPLIT) * POS_SPLIT
    lo = pos - hi
    cols = jnp.stack([hi, hi, hi, lo, lo, lo], axis=1)
    return jnp.pad(cols, ((0, 0), (0, LANES - 6))).astype(MXU_DTYPE)


def _slope_features(tq):
    sig = jnp.asarray([s * LOG2E for s in ALIBI_SLOPES[DIL_HEADS:]], F32)
    pieces = jnp.stack(_split3(sig), axis=1)
    rows = jnp.concatenate([pieces, pieces], axis=1)
    rows = jnp.pad(rows, ((0, 0), (0, LANES - 6)))
    return jnp.repeat(rows, tq, axis=0)


def _masked_softmax_t(s_t, mask):
    s_t = jnp.where(mask, s_t, NEG_INF)
    m = jnp.max(s_t, axis=0, keepdims=True)
    e = jnp.exp2(s_t - m)
    den = jnp.sum(e, axis=0, keepdims=True)
    return e * jnp.where(m > 0.5 * NEG_INF, 1.0 / den, 0.0)


def _nsa_kernel(q_ref, k_ref, vt_ref, kc_ref, vct_ref, c2st_ref, pf_ref, cpf_ref, sf_ref, wband_ref, g_ref,
                o_ref, m_scr, acc_scr, bias_scr, touched_smem, ocmp_scr, score_scr,
                *, tq, tk, topk, cmp_chunk, sel_chunk):
    i = pl.program_id(1)
    t0 = i * tq
    nh = NSA_HEADS
    cols = nh * tq
    ncp = kc_ref.shape[1]
    ns = c2st_ref.shape[0]
    blocks_per_tile = tk // NSA_SEL_LEN

    q = q_ref[0]
    q4 = jnp.concatenate([q[:, h * LANES:(h + 1) * LANES] for h in range(nh)], axis=0)
    q4 = (q4.astype(F32) * (HEAD_DIM ** -0.5 * LOG2E)).astype(MXU_DTYPE)
    q4 = jnp.concatenate([q4, sf_ref[...]], axis=1)
    col = lax.broadcasted_iota(jnp.int32, (1, cols), 1)
    tpos = (t0 + (col & (tq - 1))).astype(F32)

    def cmp_part(rows):
        cidx = lax.broadcasted_iota(jnp.int32, (rows, 1), 0).astype(F32)
        c_end = cidx * NSA_CMP_STRIDE + (NSA_CMP_LEN - 1)
        kc = jnp.concatenate([kc_ref[0, 0:rows, :], cpf_ref[0:rows, :]], axis=1)
        p_cmp = _masked_softmax_t(_dot_t(kc, q4), c_end <= tpos)
        ocmp_scr[...] = _dot(vct_ref[0, :, 0:rows], p_cmp.astype(MXU_DTYPE))
        p_sum = p_cmp[:, 0:tq]
        for h in range(1, nh):
            p_sum = p_sum + p_cmp[:, h * tq:(h + 1) * tq]
        c2st = c2st_ref[:, 0:rows]
        if MXU_DTYPE == jnp.float32:
            score_scr[...] = _dot(c2st, p_sum)
        else:
            score_scr[...] = sum(_dot(c2st, piece) for piece in _split3(p_sum))

    n_variants = -(-ncp // cmp_chunk)
    n_ending = (t0 + tq) // NSA_CMP_STRIDE - 1
    need = jnp.clip((n_ending + cmp_chunk - 1) // cmp_chunk, 1, n_variants)
    for v in range(1, n_variants + 1):
        pl.when(need == v)(functools.partial(cmp_part, min(v * cmp_chunk, ncp)))
    o_cmp = ocmp_scr[...]

    t1 = t0 + lax.broadcasted_iota(jnp.int32, (1, tq), 1)
    cur = _shr(t1, NSA_SEL_LEN).astype(F32)
    max_forced = 3
    n_forced = 1.0 + jnp.where(cur >= 1.0, 1.0, 0.0) + jnp.where(cur >= 2.0, 1.0, 0.0)

    def select_blocks(rows, may_lack_forced):
        jj = lax.broadcasted_iota(jnp.int32, (rows, 1), 0).astype(F32)
        forced = (jj == 0.0) | (jj == cur) | (jj == cur - 1.0)
        score = jnp.where(jj > cur, -1.0, jnp.where(forced, -2.0, score_scr[0:rows, :]))
        bias = jnp.where(forced, 0.0, NEG_INF)

        def pick(score, bias, wanted):
            best = jnp.max(score, axis=0, keepdims=True)
            idx = jnp.min(jnp.where(score == best, jj, float(rows)), axis=0, keepdims=True)
            hit = jj == idx
            if wanted is not None:
                hit = hit & wanted
            return jnp.where(hit, -2.0, score), jnp.where(hit, 0.0, bias)

        for _ in range(topk - max_forced):
            score, bias = pick(score, bias, None)
        bias_scr[0:rows, :] = bias
        if rows < ns:
            bias_scr[rows:ns, :] = jnp.full((ns - rows, tq), NEG_INF, F32)
        if may_lack_forced:
            score_scr[0:rows, :] = score

            @pl.when(t0 < (max_forced - 1) * NSA_SEL_LEN)
            def _():
                score, bias = score_scr[0:rows, :], bias_scr[0:rows, :]
                for done in range(topk - max_forced, topk - 1):
                    score, bias = pick(score, bias, n_forced + done < float(topk))
                bias_scr[0:rows, :] = bias

    sel_variants = -(-ns // sel_chunk)
    n_started = (t0 + tq) // NSA_SEL_LEN
    need_sel = jnp.clip((n_started + sel_chunk - 1) // sel_chunk, 1, sel_variants)
    assert (max_forced - 1) * NSA_SEL_LEN + tq <= sel_chunk * NSA_SEL_LEN
    for v in range(1, sel_variants + 1):
        pl.when(need_sel == v)(functools.partial(select_blocks, min(v * sel_chunk, ns), v == 1))

    bias = bias_scr[...]
    block_any = jnp.broadcast_to(jnp.max(bias, axis=1, keepdims=True), (ns, LANES))
    tile_any = jnp.max(block_any.reshape(ns // blocks_per_tile, blocks_per_tile, LANES), axis=1)
    for kt in range(ns // blocks_per_tile):
        touched_smem[kt] = tile_any[kt, 0]

    def slc_update(kt, causal):
        k0 = pl.multiple_of(kt * tk, tk)
        b0 = pl.multiple_of(kt * blocks_per_tile, blocks_per_tile)

        def scores():
            keys = jnp.concatenate([k_ref[0, pl.ds(k0, tk), 0:LANES], pf_ref[pl.ds(k0, tk), :]], axis=1)
            s_t = _dot_t(keys, q4)
            rows = [jnp.broadcast_to(bias_scr[pl.ds(b0 + r, 1), :], (NSA_SEL_LEN, tq))
                    for r in range(blocks_per_tile)]
            sel_bias = jnp.concatenate(rows, axis=0)
            s_t = s_t + jnp.concatenate([sel_bias] * nh, axis=1)
            if causal:
                kpos = (k0 + lax.broadcasted_iota(jnp.int32, (tk, 1), 0)).astype(F32)
                s_t = jnp.where(kpos <= tpos, s_t, NEG_INF)
            return s_t

        _softmax_piece_t(scores, _with_ones_rows(vt_ref[0:LANES, pl.ds(k0, tk)]), m_scr, acc_scr,
                         first=causal)

    last = t0 // tk
    slc_update(last, True)

    def slc_tile(back, carry):
        kt = last - 1 - back
        pl.when(touched_smem[kt] == 0.0)(functools.partial(slc_update, kt, False))
        return carry

    lax.fori_loop(0, last, slc_tile, 0)
    o_slc = _softmax_result(acc_scr, HEAD_DIM)

    wlen = NSA_WINDOW + tq
    ws = pl.multiple_of(jnp.maximum(t0 - NSA_WINDOW, 0), tq)
    keys = jnp.concatenate([k_ref[0, pl.ds(ws, wlen), LANES:2 * LANES], pf_ref[pl.ds(ws, wlen), :]], axis=1)
    band = wband_ref[jnp.minimum(i, wband_ref.shape[0] - 1)]
    s_t = _dot_t(keys, q4) + jnp.concatenate([band] * nh, axis=1)
    weights = jnp.exp2(s_t - jnp.max(s_t, axis=0, keepdims=True)).astype(MXU_DTYPE)
    win = _dot(_with_ones_rows(vt_ref[LANES:2 * LANES, pl.ds(ws, wlen)]), weights)
    o_win = win[0:HEAD_DIM, :] * (1.0 / win[HEAD_DIM:HEAD_DIM + 1, :])

    gates = jax.nn.sigmoid(g_ref[0]).T
    for h in range(nh):
        c = slice(h * tq, (h + 1) * tq)
        o = (gates[3 * h:3 * h + 1, :] * o_cmp[:, c] + gates[3 * h + 1:3 * h + 2, :] * o_slc[:, c]
             + gates[3 * h + 2:3 * h + 3, :] * o_win[:, c])
        o_ref[0, :, h * LANES:(h + 1) * LANES] = o.T.astype(o_ref.dtype)


def _nsa(h3, gates3, vt, kc, vct, *, tq=NSA_Q_BLOCK, tk=512):
    b, s, _ = h3.shape
    ncp = kc.shape[1]
    ns = s // NSA_SEL_LEN
    tk = min(tk, s)
    w = NSA_HEADS * HEAD_DIM
    c2st = _cmp_to_sel_t(ncp, ns)
    pf = _pos_features(jnp.arange(s, dtype=F32))
    cpf = _pos_features(jnp.arange(ncp, dtype=F32) * NSA_CMP_STRIDE + 0.5 * (NSA_CMP_LEN - 1))
    sf = _slope_features(tq).astype(MXU_DTYPE)
    wband = _window_band_bias(tq)
    const = lambda a: _resident(a.shape, lambda bi, i: (0,) * a.ndim)
    return pl.pallas_call(
        functools.partial(_nsa_kernel, tq=tq, tk=tk, topk=min(NSA_TOPK, ns), cmp_chunk=min(128, ncp),
                          sel_chunk=min(64, ns)),
        grid=(b, s // tq),
        in_specs=[
            pl.BlockSpec((1, tq, w), lambda bi, i: (bi, i, COL_NSA_Q // w)),
            _resident((1, s, 2 * LANES), lambda bi, i: (bi, 0, COL_NSA_K // (2 * LANES))),
            _resident((2 * LANES, s), lambda bi, i: (0, bi)),
            _resident((1, ncp, LANES), lambda bi, i: (bi, 0, 0)),
            _resident((1, LANES, ncp), lambda bi, i: (bi, 0, 0)),
            const(c2st), const(pf), const(cpf), const(sf), const(wband),
            pl.BlockSpec((1, tq, LANES), lambda bi, i: (bi, i, 0)),
        ],
        out_specs=pl.BlockSpec((1, tq, w), lambda bi, i: (bi, i, 0)),
        out_shape=jax.ShapeDtypeStruct((b, s, w), MXU_DTYPE),
        scratch_shapes=[pltpu.VMEM((1, NSA_HEADS * tq), F32),
                        pltpu.VMEM((HEAD_DIM + DEN_ROWS, NSA_HEADS * tq), F32),
                        pltpu.VMEM((ns, tq), F32), pltpu.SMEM((s // tk,), F32),
                        pltpu.VMEM((HEAD_DIM, NSA_HEADS * tq), F32), pltpu.VMEM((ns, tq), F32)],
        compiler_params=_cparams(("parallel", "arbitrary")),
        name="nsa",
    )(h3, h3, vt, kc, vct, c2st, pf, cpf, sf, wband, gates3)


def _window_band_bias(tq):
    wlen = NSA_WINDOW + tq
    u = np.arange(tq)[None, :]
    w = np.arange(wlen)[:, None]
    offsets = list(range(0, NSA_WINDOW, tq)) + [NSA_WINDOW]
    tiles = []
    for off in offsets:
        dist = off + u - w
        tiles.append(np.where((dist >= 0) & (dist < NSA_WINDOW), 0.0, NEG_INF))
    return jnp.asarray(np.stack(tiles).astype(np.float32))


def _cmp_to_sel_t(ncp, ns):
    nc = ncp - 1
    c = np.arange(ncp)[None, :]
    j = np.arange(ns)[:, None]
    start = c * NSA_CMP_STRIDE
    m = (start < (j + 1) * NSA_SEL_LEN) & (start + NSA_CMP_LEN - 1 >= j * NSA_SEL_LEN) & (c < nc)
    return jnp.asarray(m.astype(np.float32)).astype(MXU_DTYPE)


def _token_rows(ref, dilation, scr):
    if dilation == 1:
        return ref[...]
    tm = ref.shape[0] * dilation
    for r in range(dilation):
        for c in range(2):
            lo = r * 2 * LANES + c * LANES
            scr[c, pl.ds(r, tm // dilation, stride=dilation), :] = ref[:, lo:lo + LANES]
    return jnp.concatenate([scr[0], scr[1]], axis=1)


def _mixout_kernel(x_ref, mla_ref, d0_ref, d1_ref, d2_ref, l0_ref, l1_ref, l2_ref, nsa_ref, w_ref, o_ref,
                   relayout_scr):
    dils = [dil for _, dil in DIL_PAIRS]
    o_dil = [_token_rows(r, dil, relayout_scr.at[2 * g:2 * g + 2])
             for g, (r, dil) in enumerate(zip((d0_ref, d1_ref, d2_ref), dils))]
    lse = [_token_rows(r, dil, relayout_scr.at[6 + 2 * g:8 + 2 * g])
           for g, (r, dil) in enumerate(zip((l0_ref, l1_ref, l2_ref), dils))]
    top = jnp.maximum(jnp.maximum(lse[0], lse[1]), lse[2])
    e = [jnp.exp(l - top) for l in lse]
    inv = 1.0 / (e[0] + e[1] + e[2])
    acc = x_ref[...] + _dot(mla_ref[...], w_ref[0:768, :])
    for g in range(3):
        mixed = (o_dil[g] * (e[g] * inv)).astype(MXU_DTYPE)
        acc += _dot(mixed, w_ref[768 + g * 256:768 + (g + 1) * 256, :])
    acc += _dot(nsa_ref[...], w_ref[1536:2048, :])
    o_ref[...] = acc


def _mixout(x2, o_mla, o_dil, lse_dil, o_nsa, w, *, tm=512):
    n, d = x2.shape
    row = lambda width: pl.BlockSpec((tm, width), lambda i: (i, 0))
    dil_specs = [pl.BlockSpec((tm // dil, dil * 2 * LANES), lambda i: (i, 0)) for _, dil in DIL_PAIRS]
    return pl.pallas_call(
        _mixout_kernel,
        grid=(n // tm,),
        in_specs=[row(d), row(o_mla.shape[1])] + dil_specs + dil_specs + [
            row(o_nsa.shape[1]), pl.BlockSpec(w.shape, lambda i: (0, 0))],
        out_specs=row(d),
        out_shape=jax.ShapeDtypeStruct((n, d), F32),
        scratch_shapes=[pltpu.VMEM((12, tm, LANES), F32)],
        compiler_params=_cparams(("parallel",)),
        name="mix_out",
    )(x2, o_mla, *o_dil, *lse_dil, o_nsa, w)


def kernel(x, ffn1_norm, ffn1_w_in, ffn1_w_out, mix_norm, w_mix_in, mla_q_norm, mla_w_uq, mla_kv_norm,
           mla_w_ukv, nsa_cmp_pos, nsa_phi_k1, nsa_phi_k2, nsa_phi_v1, nsa_phi_v2, w_mix_out, ffn2_norm,
           ffn2_w_in, ffn2_w_out, final_norm):
    b, s, d = x.shape
    depth = ffn1_w_in.shape[0]
    n = b * s
    cos, sin = _rope_tables(s)
    x2 = x.reshape(n, d)
    for l in range(depth):
        x2 = _ffn(x2, ffn1_norm[l], *_prep_ffn(ffn1_w_in[l], ffn1_w_out[l]))

        h2, gates2, cmp_in, nsa_vt, *dil_in = _mixin(x2, mix_norm[l], *_prep_mixin(w_mix_in[l]))
        h3 = h2.reshape(b, s, H_COLS)

        q, k, vt = _mla_proj(h2, mla_q_norm[l].reshape(1, -1), mla_kv_norm[l].reshape(1, -1),
                             *_prep_mla(mla_w_uq[l], mla_w_ukv[l]), cos, sin)
        o_mla = _mla_flash(q.reshape(b, s, -1), k.reshape(b, s, -1), vt)

        dil = [_dilated_group(hg, g, b) for g, hg in enumerate(dil_in)]

        kc, vct = _nsa_compress(cmp_in, b, *_prep_nsa_cmp(nsa_cmp_pos[l], nsa_phi_k1[l], nsa_phi_k2[l],
                                                          nsa_phi_v1[l], nsa_phi_v2[l]))
        o_nsa = _nsa(h3, gates2.reshape(b, s, LANES), nsa_vt, kc, vct)

        x2 = _mixout(x2, o_mla.reshape(n, -1), [o for o, _ in dil], [e for _, e in dil],
                     o_nsa.reshape(n, -1), w_mix_out[l].astype(MXU_DTYPE))

        gf = final_norm if l == depth - 1 else None
        x2 = _ffn(x2, ffn2_norm[l], *_prep_ffn(ffn2_w_in[l], ffn2_w_out[l]), gf)
    return x2.reshape(b, s, d)
```
